```python
import math
import jax, jax.numpy as jnp
from jax import lax
import numpy as np

D_MODEL = 1024
BATCH = 4
SEQ = 4096
DEPTH = 1
DEC_BATCH = 128
DEC_SEQ = 8
PAST_LEN = 8192
PAGE_SIZE = 128

N_META = 16
D_FF = 2816
DN_DK = 128
DN_DV = 128
DN_HEADS = D_MODEL // DN_DV
DN_CONV = 4
DN_CHUNK = 64
SWA_HD = 128
SWA_HEADS = D_MODEL // SWA_HD
SWA_KV_HEADS = SWA_HEADS // 4
SWA_GROUP = SWA_HEADS // SWA_KV_HEADS
WINDOW = 128
SWA_BLOCK = 128
RMS_EPS = 1e-6
L2_EPS = 1e-6

DN_QK_W = DN_HEADS * DN_DK
DN_V_W = DN_HEADS * DN_DV
DN_CONV_W = 2 * DN_QK_W + DN_V_W
SWA_Q_W = SWA_HEADS * SWA_HD
SWA_KV_W = SWA_KV_HEADS * SWA_HD
IN_SIZES = (DN_CONV_W, DN_V_W, DN_HEADS, DN_HEADS, SWA_Q_W, SWA_KV_W, SWA_KV_W, D_MODEL, D_MODEL)
D_IN = sum(IN_SIZES)

kernel_name = 'hybrid_gdn_swa_macaron_step'


def rms_norm(x, g):
    xf = x.astype(jnp.float32)
    y = xf * lax.rsqrt(jnp.mean(xf * xf, axis=-1, keepdims=True) + RMS_EPS)
    return (y * g.astype(jnp.float32)).astype(x.dtype)


def l2_normalize(x):
    return x * lax.rsqrt(jnp.sum(x * x, axis=-1, keepdims=True) + L2_EPS)


def swiglu(x, w_gate, w_up, w_down):
    return (jax.nn.silu(x @ w_gate) * (x @ w_up)) @ w_down


def half_ffn(h, g_pre, g_post, w_gate, w_up, w_down):
    return h + 0.5 * rms_norm(swiglu(rms_norm(h, g_pre), w_gate, w_up, w_down), g_post)


def causal_conv(xp, w):
    k_w = w.shape[0]
    t = xp.shape[1] - (k_w - 1)
    out = xp[:, 0:t] * w[0]
    for j in range(1, k_w):
        out = out + xp[:, j:j + t] * w[j]
    return out


def alibi_slopes():
    return jnp.exp2(-8.0 * jnp.arange(1, SWA_HEADS + 1, dtype=jnp.float32) / SWA_HEADS)


def gated_delta_chunked(q, k, v, beta, g):
    n, t, h, dk = q.shape
    dv = v.shape[-1]
    c = DN_CHUNK
    nz = t // c

    def blocks(x):
        return jnp.moveaxis(x.reshape((n, nz, c) + x.shape[2:]), 2, 3)

    q, k, v, beta, g = (blocks(x) for x in (q, k, v, beta, g))
    gc = jnp.cumsum(g, axis=-1)
    causal = jnp.tril(jnp.ones((c, c), bool))
    strict = jnp.tril(jnp.ones((c, c), bool), -1)
    diff = gc[..., :, None] - gc[..., None, :]
    decay = jnp.where(causal, jnp.exp(jnp.where(causal, diff, 0.0)), 0.0)
    kk = jnp.einsum('nzhid,nzhjd->nzhij', k, k)
    a_mat = jnp.where(strict, beta[..., :, None] * kk * decay, 0.0) + jnp.eye(c, dtype=q.dtype)
    rhs = jnp.concatenate([v * beta[..., None], k * (beta * jnp.exp(gc))[..., None]], axis=-1)
    sol = lax.linalg.triangular_solve(a_mat, rhs, left_side=True, lower=True, unit_diagonal=True)
    u_base, w = sol[..., :dv], sol[..., dv:]
    qk = jnp.einsum('nzhid,nzhjd->nzhij', q, k) * decay
    q_dec = q * jnp.exp(gc)[..., None]
    k_dec = k * jnp.exp(gc[..., -1:] - gc)[..., None]
    c_dec = jnp.exp(gc[..., -1])

    def step(s, xs):
        u_b, w_c, qk_c, qd_c, kd_c, cd_c = xs
        u = u_b - jnp.einsum('nhcd,nhde->nhce', w_c, s)
        o = jnp.einsum('nhcd,nhde->nhce', qd_c, s) + jnp.einsum('nhij,nhje->nhie', qk_c, u)
        s = s * cd_c[..., None, None] + jnp.einsum('nhcd,nhce->nhde', kd_c, u)
        return s, o

    xs = tuple(jnp.moveaxis(x, 1, 0) for x in (u_base, w, qk, q_dec, k_dec, c_dec))
    s0 = jnp.zeros((n, h, dk, dv), jnp.float32)
    s, o = lax.scan(step, s0, xs)
    o = jnp.transpose(o, (1, 0, 3, 2, 4)).reshape(n, t, h, dv)
    return o, s


def gated_delta_prompt(q, k, v, beta, g):
    pad = (-q.shape[1]) % DN_CHUNK

    def padf(x):
        return jnp.pad(x, ((0, 0), (pad, 0)) + ((0, 0),) * (x.ndim - 2))

    o, s = gated_delta_chunked(padf(q), padf(k), padf(v), padf(beta), padf(g))
    return o[:, pad:], s


def gated_delta_recurrent(q, k, v, beta, g, s0):
    def step(s, xs):
        q_t, k_t, v_t, b_t, g_t = xs
        s = s * jnp.exp(g_t)[..., None, None]
        u = b_t[..., None] * (v_t - jnp.einsum('nhd,nhde->nhe', k_t, s))
        s = s + k_t[..., :, None] * u[..., None, :]
        return s, jnp.einsum('nhd,nhde->nhe', q_t, s)

    xs = tuple(jnp.moveaxis(x, 1, 0) for x in (q, k, v, beta, g))
    s, o = lax.scan(step, s0.astype(jnp.float32), xs)
    return jnp.moveaxis(o, 0, 1), s


def sink_softmax(scores, dist, mask, sinks, slopes):
    m = slopes.reshape(SWA_KV_HEADS, SWA_GROUP, 1, 1)
    logits = jnp.where(mask, scores - m * jnp.minimum(dist, WINDOW).astype(jnp.float32), -jnp.inf)
    sink = jnp.broadcast_to(sinks.astype(jnp.float32).reshape(SWA_KV_HEADS, SWA_GROUP, 1, 1), logits.shape[:-1] + (1,))
    return jax.nn.softmax(jnp.concatenate([logits, sink], axis=-1), axis=-1)[..., :-1]


def swa_banded(q, k, v, sinks, slopes, n_keep):
    n, t = q.shape[:2]
    pad = (-t) % SWA_BLOCK
    nb = (t + pad) // SWA_BLOCK

    def padf(x):
        return jnp.pad(x, ((0, 0), (pad, 0), (0, 0), (0, 0)))

    def band(x):
        xb = x.reshape(n, nb, SWA_BLOCK, SWA_KV_HEADS, SWA_HD)
        prev = jnp.concatenate([jnp.zeros_like(xb[:, :1]), xb[:, :-1]], axis=1)
        return jnp.concatenate([prev, xb], axis=2)

    qb = padf(q).reshape(n, nb, SWA_BLOCK, SWA_KV_HEADS, SWA_GROUP, SWA_HD).astype(jnp.float32)
    kb, vb = band(padf(k)), band(padf(v))
    k_meta, v_meta = k[:, :N_META], v[:, :N_META]
    pos = (jnp.arange(t + pad) - pad).reshape(nb, SWA_BLOCK)
    kpos = jnp.concatenate([pos - SWA_BLOCK, pos], axis=1)
    dist_meta = pos[:, :, None] - jnp.arange(N_META)[None, None, :]
    dist_band = pos[:, :, None] - kpos[:, None, :]
    dist = jnp.concatenate([dist_meta, dist_band], axis=-1)
    mask = jnp.concatenate([dist_meta >= 0,
                            (dist_band >= 0) & (dist_band <= WINDOW) & (kpos[:, None, :] >= N_META)], axis=-1)
    scale = SWA_HD ** -0.5
    scores = jnp.concatenate([
        jnp.einsum('nbqkgd,nskd->nbkgqs', qb, k_meta.astype(jnp.float32)),
        jnp.einsum('nbqkgd,nbskd->nbkgqs', qb, kb.astype(jnp.float32))], axis=-1) * scale
    p = sink_softmax(scores, dist[:, None, None], mask[:, None, None], sinks, slopes).astype(v.dtype)
    o = (jnp.einsum('nbkgqs,nskd->nbqkgd', p[..., :N_META], v_meta)
         + jnp.einsum('nbkgqs,nbskd->nbqkgd', p[..., N_META:], vb))
    o = o.reshape(n, t + pad, SWA_HEADS, SWA_HD)[:, pad:]
    return o, (k_meta, v_meta, k[:, -n_keep:], v[:, -n_keep:])


def swa_step(q, k, v, k_meta, v_meta, k_buf, v_buf, sinks, slopes):
    n, t = q.shape[:2]
    w = k_buf.shape[1]
    k_all = jnp.concatenate([k_buf.astype(k.dtype), k], axis=1)
    v_all = jnp.concatenate([v_buf.astype(v.dtype), v], axis=1)
    qg = q.reshape(n, t, SWA_KV_HEADS, SWA_GROUP, SWA_HD).astype(jnp.float32)
    qpos = PAST_LEN + jnp.arange(t)
    kpos = PAST_LEN - w + jnp.arange(w + t)
    dist_meta = qpos[:, None] - jnp.arange(N_META)[None, :]
    dist_band = qpos[:, None] - kpos[None, :]
    dist = jnp.concatenate([dist_meta, dist_band], axis=-1)
    mask = jnp.concatenate([dist_meta >= 0,
                            (dist_band >= 0) & (dist_band <= WINDOW) & (kpos[None, :] >= N_META)], axis=-1)
    scale = SWA_HD ** -0.5
    scores = jnp.concatenate([
        jnp.einsum('ntkgd,nskd->nkgts', qg, k_meta.astype(jnp.float32)),
        jnp.einsum('ntkgd,nskd->nkgts', qg, k_all.astype(jnp.float32))], axis=-1) * scale
    p = sink_softmax(scores, dist, mask, sinks, slopes).astype(v.dtype)
    o = (jnp.einsum('nkgts,nskd->ntkgd', p[..., :N_META], v_meta.astype(v.dtype))
         + jnp.einsum('nkgts,nskd->ntkgd', p[..., N_META:], v_all))
    return o.reshape(n, t, SWA_HEADS, SWA_HD), (k_all[:, -w:], v_all[:, -w:])


def token_mixer(u, conv_hist, w_in, conv_w, a_log, dt_bias, dn_norm_w, w_out, dn_core, swa_core):
    n, t, _ = u.shape
    f32 = jnp.float32
    split_at = np.cumsum(IN_SIZES)[:-1].tolist()
    qkv_pre, z, b, a, sq, sk, sv, g_dn, g_swa = jnp.split(u @ w_in, split_at, axis=-1)
    xp = jnp.concatenate([conv_hist.astype(u.dtype), qkv_pre], axis=1)
    new_conv = xp[:, -(DN_CONV - 1):]
    qkv = jax.nn.silu(causal_conv(xp, conv_w)).astype(f32)
    dq, dk, dv = jnp.split(qkv, [DN_QK_W, 2 * DN_QK_W], axis=-1)
    dq = l2_normalize(dq.reshape(n, t, DN_HEADS, DN_DK)) * (DN_DK ** -0.5)
    dk = l2_normalize(dk.reshape(n, t, DN_HEADS, DN_DK))
    dv = dv.reshape(n, t, DN_HEADS, DN_DV)
    beta = jax.nn.sigmoid(b.astype(f32))
    g = -jnp.exp(a_log.astype(f32)) * jax.nn.softplus(a.astype(f32) + dt_bias.astype(f32))
    o_dn, s_new = dn_core(dq, dk, dv, beta, g)
    o_dn = rms_norm(o_dn, dn_norm_w) * jax.nn.silu(z.astype(f32).reshape(n, t, DN_HEADS, DN_DV))
    o_dn = o_dn.reshape(n, t, DN_V_W).astype(u.dtype)
    o_sw, swa_states = swa_core(sq.reshape(n, t, SWA_HEADS, SWA_HD),
                                sk.reshape(n, t, SWA_KV_HEADS, SWA_HD),
                                sv.reshape(n, t, SWA_KV_HEADS, SWA_HD))
    o_sw = o_sw.reshape(n, t, SWA_Q_W)
    y = jax.nn.sigmoid(g_dn) * o_dn + jax.nn.sigmoid(g_swa) * o_sw
    return y @ w_out, (new_conv, s_new.astype(u.dtype)) + swa_states


def setup_inputs(seed: int = 0) -> dict:
    key = jax.random.key(seed)
    ks = jax.random.split(key, 40)
    f32 = jnp.float32
    cnt = [0]

    def nk():
        cnt[0] += 1
        return ks[cnt[0] - 1]

    def nrm(shape, scale):
        return jax.random.normal(nk(), shape, f32) * scale

    def gain(width):
        return 1.0 + 0.05 * jax.random.normal(nk(), (DEPTH, width), f32)

    n_keep = min(WINDOW, PAST_LEN)
    inp = {}
    inp['x_prompt'] = nrm((BATCH, SEQ, D_MODEL), 1.0)
    inp['x_sample'] = nrm((DEC_BATCH, DEC_SEQ, D_MODEL), 1.0)
    inp['state_dn_conv'] = nrm((DEPTH, DEC_BATCH, DN_CONV - 1, DN_CONV_W), 1.0)
    inp['state_dn_ssm'] = nrm((DEPTH, DEC_BATCH, DN_HEADS, DN_DK, DN_DV), 0.05)
    inp['cache_swa_meta_k'] = nrm((DEPTH, DEC_BATCH, N_META, SWA_KV_HEADS, SWA_HD), 1.0)
    inp['cache_swa_meta_v'] = nrm((DEPTH, DEC_BATCH, N_META, SWA_KV_HEADS, SWA_HD), 1.0)
    inp['cache_swa_k'] = nrm((DEPTH, DEC_BATCH, n_keep, SWA_KV_HEADS, SWA_HD), 1.0)
    inp['cache_swa_v'] = nrm((DEPTH, DEC_BATCH, n_keep, SWA_KV_HEADS, SWA_HD), 1.0)
    inp['meta_tokens'] = nrm((N_META, D_MODEL), 1.0)
    inp['ffn1_norm_pre'] = gain(D_MODEL)
    inp['ffn1_norm_post'] = gain(D_MODEL)
    inp['ffn1_w_gate'] = nrm((DEPTH, D_MODEL, D_FF), D_MODEL ** -0.5)
    inp['ffn1_w_up'] = nrm((DEPTH, D_MODEL, D_FF), D_MODEL ** -0.5)
    inp['ffn1_w_down'] = nrm((DEPTH, D_FF, D_MODEL), D_FF ** -0.5)
    inp['mix_norm_pre'] = gain(D_MODEL)
    inp['mix_norm_post'] = gain(D_MODEL)
    inp['w_in'] = nrm((DEPTH, D_MODEL, D_IN), D_MODEL ** -0.5)
    inp['dn_conv_w'] = nrm((DEPTH, DN_CONV, DN_CONV_W), DN_CONV ** -0.5)
    inp['dn_a_log'] = jnp.log(jax.random.uniform(nk(), (DEPTH, DN_HEADS), f32, 1.0, 16.0))
    dt = jnp.exp(jax.random.uniform(nk(), (DEPTH, DN_HEADS), f32, math.log(1e-3), math.log(1e-1)))
    inp['dn_dt_bias'] = dt + jnp.log(-jnp.expm1(-dt))
    inp['dn_norm_w'] = gain(DN_DV)
    inp['swa_sinks'] = nrm((DEPTH, SWA_HEADS), 0.5)
    inp['w_out'] = nrm((DEPTH, D_MODEL, D_MODEL), D_MODEL ** -0.5)
    inp['ffn2_norm_pre'] = gain(D_MODEL)
    inp['ffn2_norm_post'] = gain(D_MODEL)
    inp['ffn2_w_gate'] = nrm((DEPTH, D_MODEL, D_FF), D_MODEL ** -0.5)
    inp['ffn2_w_up'] = nrm((DEPTH, D_MODEL, D_FF), D_MODEL ** -0.5)
    inp['ffn2_w_down'] = nrm((DEPTH, D_FF, D_MODEL), D_FF ** -0.5)
    return inp


def reference(x_prompt, x_sample, state_dn_conv, state_dn_ssm, cache_swa_meta_k, cache_swa_meta_v,
              cache_swa_k, cache_swa_v, meta_tokens, ffn1_norm_pre, ffn1_norm_post, ffn1_w_gate,
              ffn1_w_up, ffn1_w_down, mix_norm_pre, mix_norm_post, w_in, dn_conv_w, dn_a_log,
              dn_dt_bias, dn_norm_w, swa_sinks, w_out, ffn2_norm_pre, ffn2_norm_post, ffn2_w_gate,
              ffn2_w_up, ffn2_w_down):
    slopes = alibi_slopes()
    n_keep = min(WINDOW, PAST_LEN)

    def run_layer(h, l, conv_hist, dn_core, swa_core):
        h = half_ffn(h, ffn1_norm_pre[l], ffn1_norm_post[l], ffn1_w_gate[l], ffn1_w_up[l], ffn1_w_down[l])
        y, st = token_mixer(rms_norm(h, mix_norm_pre[l]), conv_hist, w_in[l], dn_conv_w[l], dn_a_log[l],
                            dn_dt_bias[l], dn_norm_w[l], w_out[l], dn_core, swa_core)
        h = h + rms_norm(y, mix_norm_post[l])
        h = half_ffn(h, ffn2_norm_pre[l], ffn2_norm_post[l], ffn2_w_gate[l], ffn2_w_up[l], ffn2_w_down[l])
        return h, st

    n_p = x_prompt.shape[0]
    meta = jnp.broadcast_to(meta_tokens.astype(x_prompt.dtype)[None], (n_p, N_META, D_MODEL))
    hp = jnp.concatenate([meta, x_prompt], axis=1)
    hs = x_sample
    p_st, s_st = [], []
    for l in range(DEPTH):
        conv0 = jnp.zeros((n_p, DN_CONV - 1, DN_CONV_W), hp.dtype)
        hp, st = run_layer(hp, l, conv0, gated_delta_prompt,
                           lambda q, k, v, l=l: swa_banded(q, k, v, swa_sinks[l], slopes, n_keep))
        p_st.append(st)
        hs, st = run_layer(hs, l, state_dn_conv[l],
                           lambda q, k, v, b, g, l=l: gated_delta_recurrent(q, k, v, b, g, state_dn_ssm[l]),
                           lambda q, k, v, l=l: swa_step(q, k, v, cache_swa_meta_k[l], cache_swa_meta_v[l],
                                                         cache_swa_k[l], cache_swa_v[l], swa_sinks[l], slopes))
        s_st.append(st)
    p_conv, p_ssm, p_meta_k, p_meta_v, p_win_k, p_win_v = [jnp.stack(a) for a in zip(*p_st)]
    s_conv, s_ssm, s_win_k, s_win_v = [jnp.stack(a) for a in zip(*s_st)]
    y_prompt = hp[:, N_META:]
    return (y_prompt, hs, p_conv, p_ssm, p_meta_k, p_meta_v, p_win_k, p_win_v, s_conv, s_ssm, s_win_k, s_win_v)
```

```python
import functools
import math

import jax
import jax.numpy as jnp
from jax import lax
from jax.experimental import pallas as pl
from jax.experimental.pallas import tpu as pltpu

F32 = jnp.float32
BF16 = jnp.bfloat16
HIGHEST = lax.Precision.HIGHEST

RMS_EPS = 1e-6
L2_EPS = 1e-6
WINDOW = 128
PAST_LEN = 8192
HEAD_DIM = 128
SWA_GROUP = 4
DN_CONV = 4
DN_CHUNK = 64
HIST_ROWS = 8

VMEM_LIMIT_BYTES = 56 * 1024 * 1024
ROW_TILE = 512
NEG_BIG = -1e30


def _rms(x, g):
    return x * lax.rsqrt(jnp.mean(x * x, axis=-1, keepdims=True) + RMS_EPS) * g


def _silu(x):
    return x * jax.nn.sigmoid(x)


def _dot(a, b):
    return jnp.dot(a, b, preferred_element_type=F32)


def _dot_hi(a, b):
    return jnp.dot(a, b, preferred_element_type=F32, precision=HIGHEST)


def _dot_nt_hi(a, b):
    return lax.dot_general(a, b, (((1,), (1,)), ((), ())), preferred_element_type=F32, precision=HIGHEST)


def _dot_tn_hi(a, b):
    return lax.dot_general(a, b, (((0,), (0,)), ((), ())), preferred_element_type=F32, precision=HIGHEST)


def _dot_nt(a, b):
    return lax.dot_general(a, b, (((1,), (1,)), ((), ())), preferred_element_type=F32)


def _row_tile(rows):
    return ROW_TILE if rows % ROW_TILE == 0 else rows


def _resident(shape):
    return pl.BlockSpec(shape, lambda *_: (0,) * len(shape), pipeline_mode=pl.Buffered(1))


def _params(*semantics):
    return pltpu.CompilerParams(dimension_semantics=semantics, vmem_limit_bytes=VMEM_LIMIT_BYTES)


def _ffn_tile(h, g_pre, g_post, wg_ref, wu_ref, wd_ref, ff_chunk):
    xn = _rms(h, g_pre).astype(BF16)
    acc = None
    for c0 in range(0, wg_ref.shape[1], ff_chunk):
        gate = _dot(xn, wg_ref[:, c0:c0 + ff_chunk])
        up = _dot(xn, wu_ref[:, c0:c0 + ff_chunk])
        part = _dot((_silu(gate) * up).astype(BF16), wd_ref[c0:c0 + ff_chunk, :])
        acc = part if acc is None else acc + part
    return h + 0.5 * _rms(acc, g_post)


def _ff_chunk(d_ff):
    half = d_ff // 2
    return half if d_ff % 2 == 0 and half % 128 == 0 else d_ff


def _ffn_kernel(h_ref, gpre_ref, gpost_ref, wg_ref, wu_ref, wd_ref, o_ref, *, ff_chunk):
    o_ref[...] = _ffn_tile(h_ref[...], gpre_ref[...], gpost_ref[...], wg_ref, wu_ref, wd_ref, ff_chunk)


def _ffn(h, g_pre, g_post, wg, wu, wd):
    rows, d = h.shape
    d_ff = wg.shape[1]
    tm = _row_tile(rows)
    row_spec = pl.BlockSpec((tm, d), lambda i: (i, 0))
    return pl.pallas_call(
        functools.partial(_ffn_kernel, ff_chunk=_ff_chunk(d_ff)),
        grid=(rows // tm,),
        in_specs=[row_spec, _resident((1, d)), _resident((1, d)),
                  _resident((d, d_ff)), _resident((d, d_ff)), _resident((d_ff, d))],
        out_specs=row_spec,
        out_shape=jax.ShapeDtypeStruct((rows, d), F32),
        compiler_params=_params("parallel"),
        name="ffn",
    )(h, g_pre, g_post, wg, wu, wd)


def _proj_kernel(h_ref, g_ref, w_ref, o_ref):
    o_ref[...] = _dot(_rms(h_ref[...], g_ref[...]).astype(BF16), w_ref[...])


def _proj(h, g, w):
    rows, d = h.shape
    n = w.shape[1]
    tm = _row_tile(rows)
    return pl.pallas_call(
        _proj_kernel,
        grid=(rows // tm,),
        in_specs=[pl.BlockSpec((tm, d), lambda i: (i, 0)), _resident((1, d)), _resident((d, n))],
        out_specs=pl.BlockSpec((tm, n), lambda i: (i, 0)),
        out_shape=jax.ShapeDtypeStruct((rows, n), F32),
        compiler_params=_params("parallel"),
        name="proj",
    )(h, g, w)


def _mixffn_kernel(h_ref, odn_ref, osw_ref, gdn_ref, gsw_ref, wout_ref, gmix_ref,
                   gpre_ref, gpost_ref, wg_ref, wu_ref, wd_ref, o_ref, *, ff_chunk):
    y = jax.nn.sigmoid(gdn_ref[...]) * odn_ref[...] + jax.nn.sigmoid(gsw_ref[...]) * osw_ref[...]
    h2 = h_ref[...] + _rms(_dot(y.astype(BF16), wout_ref[...]), gmix_ref[...])
    o_ref[...] = _ffn_tile(h2, gpre_ref[...], gpost_ref[...], wg_ref, wu_ref, wd_ref, ff_chunk)


def _mixffn(h, o_dn, o_sw, proj_a, proj_b, cols, w_out, g_mix, g_pre, g_post, wg, wu, wd):
    rows, d = h.shape
    d_ff = wg.shape[1]
    tm = _row_tile(rows)
    row_spec = pl.BlockSpec((tm, d), lambda i: (i, 0))
    gdn_blk, gsw_blk = cols["g_dn"] // d, cols["g_swa"] // d
    return pl.pallas_call(
        functools.partial(_mixffn_kernel, ff_chunk=_ff_chunk(d_ff)),
        grid=(rows // tm,),
        in_specs=[row_spec, row_spec, row_spec,
                  pl.BlockSpec((tm, d), lambda i: (i, gdn_blk)),
                  pl.BlockSpec((tm, d), lambda i: (i, gsw_blk)),
                  _resident((d, d)), _resident((1, d)), _resident((1, d)), _resident((1, d)),
                  _resident((d, d_ff)), _resident((d, d_ff)), _resident((d_ff, d))],
        out_specs=row_spec,
        out_shape=jax.ShapeDtypeStruct((rows, d), F32),
        compiler_params=_params("parallel"),
        name="mixffn",
    )(h, o_dn, o_sw, proj_a, proj_b, w_out, g_mix, g_pre, g_post, wg, wu, wd)


def _unit_lower_inverse(a, eye):
    c = a.shape[0]
    t = eye - a
    p = a
    for _ in range(int(math.log2(c)) - 1):
        p = _dot_hi(p, p)
        t = t + _dot_hi(t, p)
    return t


def _gdn_kernel(qkv_ref, z_ref, ba_ref, hist_ref, s0_ref, convw_ref, alog_ref, dtb_ref, nw_ref,
                o_ref, sout_ref, xbuf_ref, s_ref, *, chunk, heads):
    j = pl.program_id(1)
    tb = qkv_ref.shape[0]
    hd = HEAD_DIM
    qk_w = heads * hd

    @pl.when(j == 0)
    def _():
        xbuf_ref[0:HIST_ROWS, :] = hist_ref[0]
        s_ref[...] = s0_ref[0]

    xbuf_ref[HIST_ROWS:HIST_ROWS + tb, :] = qkv_ref[...]

    rows = lax.broadcasted_iota(jnp.int32, (chunk, chunk), 0)
    cols = lax.broadcasted_iota(jnp.int32, (chunk, chunk), 1)
    causal = rows >= cols
    strict = rows > cols
    eye = (rows == cols).astype(F32)
    lower_ones = causal.astype(F32)

    def conv_silu(r0, c0):
        acc = None
        for tap in range(DN_CONV):
            lo = HIST_ROWS + r0 - (DN_CONV - 1) + tap
            term = xbuf_ref[lo:lo + chunk, c0:c0 + hd] * convw_ref[tap:tap + 1, c0:c0 + hd]
            acc = term if acc is None else acc + term
        return _silu(acc)

    for r0 in range(0, tb, chunk):
        ba = ba_ref[r0:r0 + chunk, :]
        beta_all = jax.nn.sigmoid(ba[:, 0:heads])
        a_in = ba[:, heads:2 * heads] + dtb_ref[...]
        softplus = jnp.maximum(a_in, 0.0) + jnp.log(1.0 + jnp.exp(-jnp.abs(a_in)))
        g_all = -jnp.exp(alog_ref[...]) * softplus
        gc_col_all = _dot_hi(lower_ones, g_all)
        gc_row_all = _dot_tn_hi(g_all, (rows <= cols).astype(F32))
        for h in range(heads):
            q = conv_silu(r0, h * hd)
            k = conv_silu(r0, qk_w + h * hd)
            v = conv_silu(r0, 2 * qk_w + h * hd)
            q = q * lax.rsqrt(jnp.sum(q * q, axis=-1, keepdims=True) + L2_EPS) * (hd ** -0.5)
            k = k * lax.rsqrt(jnp.sum(k * k, axis=-1, keepdims=True) + L2_EPS)
            beta = beta_all[:, h:h + 1]
            gc = gc_col_all[:, h:h + 1]
            gc_row = gc_row_all[h:h + 1, :]
            gc_last = gc_row[:, chunk - 1:chunk]
            decay = jnp.where(causal, jnp.exp(jnp.where(causal, gc - gc_row, 0.0)), 0.0)
            a_mat = jnp.where(strict, beta * _dot_nt_hi(k, k) * decay, 0.0)
            t_inv = _unit_lower_inverse(a_mat, eye)
            rhs = jnp.concatenate([v * beta, k * (beta * jnp.exp(gc))], axis=-1)
            sol = _dot_hi(t_inv, rhs)
            u_base, w = sol[:, :hd], sol[:, hd:]
            qk = _dot_nt_hi(q, k) * decay
            s = s_ref[h]
            u = u_base - _dot_hi(w, s)
            o = _dot_hi(q * jnp.exp(gc), s) + _dot_hi(qk, u)
            s_ref[h] = s * jnp.exp(gc_last) + _dot_tn_hi(k * jnp.exp(gc_last - gc), u)
            zh = z_ref[r0:r0 + chunk, h * hd:(h + 1) * hd]
            o_ref[r0:r0 + chunk, h * hd:(h + 1) * hd] = _rms(o, nw_ref[...]) * _silu(zh)

    xbuf_ref[0:HIST_ROWS, :] = xbuf_ref[tb:tb + HIST_ROWS, :]

    @pl.when(j == pl.num_programs(1) - 1)
    def _():
        sout_ref[0] = s_ref[...]


def _gdn(proj_a, cols, hist, s0, conv_w, a_log, dt_bias, norm_w, *, n_seq, seq_len, block, chunk):
    heads = a_log.shape[1]
    conv_width = conv_w.shape[1]
    v_w = heads * HEAD_DIM
    nblk = seq_len // block
    z_blk, ba_blk = cols["z"] // v_w, cols["ba"] // 128
    shared_hist, shared_s0 = hist.shape[0] == 1, s0.shape[0] == 1
    return pl.pallas_call(
        functools.partial(_gdn_kernel, chunk=chunk, heads=heads),
        grid=(n_seq, nblk),
        in_specs=[
            pl.BlockSpec((block, conv_width), lambda n, j: (n * nblk + j, 0)),
            pl.BlockSpec((block, v_w), lambda n, j: (n * nblk + j, z_blk)),
            pl.BlockSpec((block, 128), lambda n, j: (n * nblk + j, ba_blk)),
            pl.BlockSpec((1, HIST_ROWS, conv_width), lambda n, j: (0 if shared_hist else n, 0, 0)),
            pl.BlockSpec((1, heads, HEAD_DIM, HEAD_DIM), lambda n, j: (0 if shared_s0 else n, 0, 0, 0)),
            pl.BlockSpec((DN_CONV, conv_width), lambda n, j: (0, 0)),
            pl.BlockSpec((1, heads), lambda n, j: (0, 0)),
            pl.BlockSpec((1, heads), lambda n, j: (0, 0)),
            pl.BlockSpec((1, HEAD_DIM), lambda n, j: (0, 0)),
        ],
        out_specs=[
            pl.BlockSpec((block, v_w), lambda n, j: (n * nblk + j, 0)),
            pl.BlockSpec((1, heads, HEAD_DIM, HEAD_DIM), lambda n, j: (n, 0, 0, 0)),
        ],
        out_shape=[
            jax.ShapeDtypeStruct((n_seq * seq_len, v_w), F32),
            jax.ShapeDtypeStruct((n_seq, heads, HEAD_DIM, HEAD_DIM), F32),
        ],
        scratch_shapes=[
            pltpu.VMEM((block + HIST_ROWS, conv_width), F32),
            pltpu.VMEM((heads, HEAD_DIM, HEAD_DIM), F32),
        ],
        compiler_params=_params("parallel", "arbitrary"),
        name="gdn",
    )(proj_a, proj_a, proj_a, hist, s0, conv_w, a_log, dt_bias, norm_w)


def _alibi_slope(head, n_heads):
    return 2.0 ** (-8.0 * (head + 1) / n_heads)


def _attend(q4, segments, sink_logits, slopes, n_rows):
    scale = HEAD_DIM ** -0.5
    scores = [_dot_nt(q4, k) * scale for k, _, _, _ in segments]
    outs = []
    for g in range(len(slopes)):
        r = slice(g * n_rows, (g + 1) * n_rows)
        logits = [jnp.where(mask, sc[r] - slopes[g] * jnp.minimum(dist, WINDOW).astype(F32), NEG_BIG)
                  for sc, (_, _, dist, mask) in zip(scores, segments)]
        m = sink_logits[g]
        for lg in logits:
            m = jnp.maximum(m, jnp.max(lg, axis=-1, keepdims=True))
        denom = jnp.exp(sink_logits[g] - m)
        acc = None
        for lg, (_, v, _, _) in zip(logits, segments):
            p = jnp.exp(lg - m)
            denom = denom + jnp.sum(p, axis=-1, keepdims=True)
            pv = _dot(p.astype(BF16), v)
            acc = pv if acc is None else acc + pv
        outs.append(acc / denom)
    return outs


def _swa_prompt_kernel(q_ref, kv_ref, kvprev_ref, kvmeta_ref, sinks_ref, o_ref, *, n_meta, kv_heads):
    j = pl.program_id(1)
    tq = q_ref.shape[0]
    hd = HEAD_DIM
    kv_w = kv_heads * hd
    n_heads = kv_heads * SWA_GROUP
    qi = lax.broadcasted_iota(jnp.int32, (tq, tq), 0)
    ki = lax.broadcasted_iota(jnp.int32, (tq, tq), 1)
    dist_own = qi - ki
    dist_prev = dist_own + tq
    mask_own = dist_own >= 0
    mask_prev = (dist_prev <= WINDOW) & (j > 0)
    qpos = n_meta + j * tq + lax.broadcasted_iota(jnp.int32, (tq, n_meta), 0)
    dist_meta = qpos - lax.broadcasted_iota(jnp.int32, (tq, n_meta), 1)
    mask_meta = dist_meta >= 0
    for kvh in range(kv_heads):
        ks = slice(kvh * hd, (kvh + 1) * hd)
        vs = slice(kv_w + kvh * hd, kv_w + (kvh + 1) * hd)
        segments = [
            (kvmeta_ref[:, ks].astype(BF16), kvmeta_ref[:, vs].astype(BF16), dist_meta, mask_meta),
            (kvprev_ref[:, ks].astype(BF16), kvprev_ref[:, vs].astype(BF16), dist_prev, mask_prev),
            (kv_ref[:, ks].astype(BF16), kv_ref[:, vs].astype(BF16), dist_own, mask_own),
        ]
        heads = [kvh * SWA_GROUP + g for g in range(SWA_GROUP)]
        q4 = jnp.concatenate([q_ref[:, hh * hd:(hh + 1) * hd] for hh in heads], axis=0).astype(BF16)
        sink_logits = [sinks_ref[:, hh:hh + 1] for hh in heads]
        slopes = [_alibi_slope(hh, n_heads) for hh in heads]
        for hh, o in zip(heads, _attend(q4, segments, sink_logits, slopes, tq)):
            o_ref[:, hh * hd:(hh + 1) * hd] = o


def _swa_prompt(proj_b, proj_b_meta, cols, sinks, *, n_seq, seq_len, n_meta):
    n_heads = sinks.shape[1]
    kv_heads = n_heads // SWA_GROUP
    q_w, kv_w2 = n_heads * HEAD_DIM, 2 * kv_heads * HEAD_DIM
    tq = WINDOW
    nblk = seq_len // tq
    kv_blk = cols["skv"] // kv_w2
    return pl.pallas_call(
        functools.partial(_swa_prompt_kernel, n_meta=n_meta, kv_heads=kv_heads),
        grid=(n_seq, nblk),
        in_specs=[
            pl.BlockSpec((tq, q_w), lambda n, j: (n * nblk + j, 0)),
            pl.BlockSpec((tq, kv_w2), lambda n, j: (n * nblk + j, kv_blk)),
            pl.BlockSpec((tq, kv_w2), lambda n, j: (n * nblk + jnp.maximum(j - 1, 0), kv_blk)),
            pl.BlockSpec((n_meta, kv_w2), lambda n, j: (0, kv_blk)),
            pl.BlockSpec((1, n_heads), lambda n, j: (0, 0)),
        ],
        out_specs=pl.BlockSpec((tq, q_w), lambda n, j: (n * nblk + j, 0)),
        out_shape=jax.ShapeDtypeStruct((n_seq * seq_len, q_w), F32),
        compiler_params=_params("parallel", "arbitrary"),
        name="swa_prompt",
    )(proj_b, proj_b, proj_b, proj_b_meta, sinks)


def _swa_step_kernel(q_ref, kvnew_ref, kmeta_ref, vmeta_ref, kbuf_ref, vbuf_ref, sinks_ref,
                     o_ref, kout_ref, vout_ref, *, kv_heads):
    nb, t, _ = q_ref.shape
    n_meta, w = kmeta_ref.shape[1], kbuf_ref.shape[1]
    hd = HEAD_DIM
    kv_w = kv_heads * hd
    n_heads = kv_heads * SWA_GROUP

    def grid2(rows, cols_):
        return (lax.broadcasted_iota(jnp.int32, (rows, cols_), 0), lax.broadcasted_iota(jnp.int32, (rows, cols_), 1))

    ti, mi = grid2(t, n_meta)
    dist_meta = PAST_LEN + ti - mi
    mask_meta = dist_meta >= 0
    ti, bi = grid2(t, w)
    dist_buf = w + ti - bi
    mask_buf = (dist_buf <= WINDOW) & (PAST_LEN - w + bi >= n_meta)
    ti, si = grid2(t, t)
    dist_new = ti - si
    mask_new = dist_new >= 0
    for b in range(nb):
        kv_new = kvnew_ref[b]
        kout_ref[b, 0:w - t, :] = kbuf_ref[b, t:w, :]
        kout_ref[b, w - t:w, :] = kv_new[:, 0:kv_w]
        vout_ref[b, 0:w - t, :] = vbuf_ref[b, t:w, :]
        vout_ref[b, w - t:w, :] = kv_new[:, kv_w:2 * kv_w]
        for kvh in range(kv_heads):
            ks = slice(kvh * hd, (kvh + 1) * hd)
            segments = [
                (kmeta_ref[b, :, ks].astype(BF16), vmeta_ref[b, :, ks].astype(BF16), dist_meta, mask_meta),
                (kbuf_ref[b, :, ks].astype(BF16), vbuf_ref[b, :, ks].astype(BF16), dist_buf, mask_buf),
                (kv_new[:, ks].astype(BF16), kv_new[:, kv_w + kvh * hd:kv_w + (kvh + 1) * hd].astype(BF16),
                 dist_new, mask_new),
            ]
            heads = [kvh * SWA_GROUP + g for g in range(SWA_GROUP)]
            q4 = jnp.concatenate([q_ref[b, :, hh * hd:(hh + 1) * hd] for hh in heads], axis=0).astype(BF16)
            sink_logits = [sinks_ref[:, hh:hh + 1] for hh in heads]
            slopes = [_alibi_slope(hh, n_heads) for hh in heads]
            for hh, o in zip(heads, _attend(q4, segments, sink_logits, slopes, t)):
                o_ref[b, :, hh * hd:(hh + 1) * hd] = o


def _swa_step(proj_b3, cols, k_meta, v_meta, k_buf, v_buf, sinks, *, seqs_per_step):
    n, t, _ = proj_b3.shape
    n_heads = sinks.shape[1]
    kv_heads = n_heads // SWA_GROUP
    q_w, kv_w = n_heads * HEAD_DIM, kv_heads * HEAD_DIM
    n_meta, w = k_meta.shape[1], k_buf.shape[1]
    nb = seqs_per_step
    kv_blk = cols["skv"] // (2 * kv_w)
    seq3 = lambda rows, width, blk=0: pl.BlockSpec((nb, rows, width), lambda i: (i, 0, blk))
    return pl.pallas_call(
        functools.partial(_swa_step_kernel, kv_heads=kv_heads),
        grid=(n // nb,),
        in_specs=[seq3(t, q_w), seq3(t, 2 * kv_w, kv_blk), seq3(n_meta, kv_w), seq3(n_meta, kv_w),
                  seq3(w, kv_w), seq3(w, kv_w), pl.BlockSpec((1, n_heads), lambda i: (0, 0))],
        out_specs=[seq3(t, q_w), seq3(w, kv_w), seq3(w, kv_w)],
        out_shape=[jax.ShapeDtypeStruct((n, t, q_w), F32),
                   jax.ShapeDtypeStruct((n, w, kv_w), F32),
                   jax.ShapeDtypeStruct((n, w, kv_w), F32)],
        compiler_params=_params("parallel"),
        name="swa_step",
    )(proj_b3, proj_b3, k_meta, v_meta, k_buf, v_buf, sinks)


def _split_w_in(w_in, heads, n_heads, kv_heads, d):
    conv_w = 3 * heads * HEAD_DIM
    v_w = heads * HEAD_DIM
    q_w, kv_w = n_heads * HEAD_DIM, kv_heads * HEAD_DIM
    sizes = (conv_w, v_w, heads, heads, q_w, kv_w, kv_w, d, d)
    offs = [0]
    for s in sizes:
        offs.append(offs[-1] + s)
    part = lambda i: w_in[:, offs[i]:offs[i + 1]]
    ba = jnp.concatenate([part(2), part(3), jnp.zeros((w_in.shape[0], 128 - 2 * heads), w_in.dtype)], axis=1)
    w_a = jnp.concatenate([part(0), part(1), part(7), ba], axis=1).astype(BF16)
    w_b = jnp.concatenate([part(4), part(8), part(5), part(6)], axis=1).astype(BF16)
    cols_a = {"qkv": 0, "z": conv_w, "g_dn": conv_w + v_w, "ba": conv_w + v_w + d}
    cols_b = {"q": 0, "g_swa": q_w, "skv": q_w + d}
    return w_a, w_b, cols_a, cols_b


def kernel(x_prompt, x_sample, state_dn_conv, state_dn_ssm, cache_swa_meta_k, cache_swa_meta_v, cache_swa_k, cache_swa_v, meta_tokens, ffn1_norm_pre, ffn1_norm_post, ffn1_w_gate, ffn1_w_up, ffn1_w_down, mix_norm_pre, mix_norm_post, w_in, dn_conv_w, dn_a_log, dn_dt_bias, dn_norm_w, swa_sinks, w_out, ffn2_norm_pre, ffn2_norm_post, ffn2_w_gate, ffn2_w_up, ffn2_w_down):
    assert w_in.shape[0] == 1, "single-layer step"
    n_p, seq, d = x_prompt.shape
    n_s, t_s, _ = x_sample.shape
    n_meta = meta_tokens.shape[0]
    heads = dn_a_log.shape[1]
    n_heads = swa_sinks.shape[1]
    kv_heads = n_heads // SWA_GROUP
    conv_w = dn_conv_w.shape[2]
    kv_w = kv_heads * HEAD_DIM
    w_keep = cache_swa_k.shape[2]
    assert w_keep == WINDOW and seq % WINDOW == 0 and seq % DN_CHUNK == 0 and seq >= WINDOW

    w_a, w_b, cols_a, cols_b = _split_w_in(w_in[0], heads, n_heads, kv_heads, d)
    cols = {**cols_a, **cols_b}
    f1 = (ffn1_norm_pre, ffn1_norm_post, ffn1_w_gate[0].astype(BF16), ffn1_w_up[0].astype(BF16),
          ffn1_w_down[0].astype(BF16))
    f2 = (ffn2_norm_pre, ffn2_norm_post, ffn2_w_gate[0].astype(BF16), ffn2_w_up[0].astype(BF16),
          ffn2_w_down[0].astype(BF16))
    w_o = w_out[0].astype(BF16)
    gdn_w = (dn_conv_w[0], dn_a_log, dn_dt_bias, dn_norm_w)

    def front(h):
        h1 = _ffn(h, *f1)
        return h1, _proj(h1, mix_norm_pre, w_a), _proj(h1, mix_norm_pre, w_b)

    def back(h1, o_dn, o_sw, pa, pb):
        return _mixffn(h1, o_dn, o_sw, pa, pb, cols, w_o, mix_norm_post, *f2)

    def hist_tile(rows3):
        return jnp.pad(rows3, ((0, 0), (HIST_ROWS - (DN_CONV - 1), 0), (0, 0)))

    _, pa_m, pb_m = front(meta_tokens)
    zero_hist = jnp.zeros((1, HIST_ROWS, conv_w), F32)
    zero_state = jnp.zeros((1, heads, HEAD_DIM, HEAD_DIM), F32)
    _, s_meta = _gdn(pa_m, cols, zero_hist, zero_state, *gdn_w, n_seq=1, seq_len=n_meta, block=n_meta,
                     chunk=n_meta)
    hist_meta = hist_tile(pa_m[None, n_meta - (DN_CONV - 1):, :conv_w])

    h1_p, pa_p, pb_p = front(x_prompt.reshape(n_p * seq, d))
    o_dn_p, s_p = _gdn(pa_p, cols, hist_meta, s_meta, *gdn_w, n_seq=n_p, seq_len=seq, block=2 * DN_CHUNK,
                       chunk=DN_CHUNK)
    o_sw_p = _swa_prompt(pb_p, pb_m, cols, swa_sinks, n_seq=n_p, seq_len=seq, n_meta=n_meta)
    y_prompt = back(h1_p, o_dn_p, o_sw_p, pa_p, pb_p).reshape(n_p, seq, d)

    pa_p3 = pa_p.reshape(n_p, seq, -1)
    pb_p3 = pb_p.reshape(n_p, seq, -1)
    p_conv = pa_p3[:, seq - (DN_CONV - 1):, :conv_w]
    kv_meta = pb_m[:, cols["skv"]:]
    p_meta_k = jnp.broadcast_to(kv_meta[None, :, :kv_w], (n_p, n_meta, kv_w))
    p_meta_v = jnp.broadcast_to(kv_meta[None, :, kv_w:], (n_p, n_meta, kv_w))
    p_win_k = pb_p3[:, seq - w_keep:, cols["skv"]:cols["skv"] + kv_w]
    p_win_v = pb_p3[:, seq - w_keep:, cols["skv"] + kv_w:]

    h1_s, pa_s, pb_s = front(x_sample.reshape(n_s * t_s, d))
    o_dn_s, s_s = _gdn(pa_s, cols, hist_tile(state_dn_conv[0]), state_dn_ssm[0], *gdn_w, n_seq=n_s,
                       seq_len=t_s, block=t_s, chunk=t_s)
    o_sw_s, s_win_k, s_win_v = _swa_step(
        pb_s.reshape(n_s, t_s, -1), cols,
        cache_swa_meta_k[0].reshape(n_s, n_meta, kv_w), cache_swa_meta_v[0].reshape(n_s, n_meta, kv_w),
        cache_swa_k[0].reshape(n_s, w_keep, kv_w), cache_swa_v[0].reshape(n_s, w_keep, kv_w),
        swa_sinks, seqs_per_step=4)
    y_sample = back(h1_s, o_dn_s, o_sw_s.reshape(n_s * t_s, -1), pa_s, pb_s).reshape(n_s, t_s, d)
    xp_s = jnp.concatenate([state_dn_conv[0], pa_s.reshape(n_s, t_s, -1)[:, :, :conv_w]], axis=1)
    s_conv = xp_s[:, -(DN_CONV - 1):]

    def kv4(x):
        return x.reshape(1, x.shape[0], x.shape[1], kv_heads, HEAD_DIM)

    return (y_prompt, y_sample, p_conv[None], s_p[None], kv4(p_meta_k), kv4(p_meta_v), kv4(p_win_k),
            kv4(p_win_v), s_conv[None], s_s[None], kv4(s_win_k), kv4(s_win_v))
```

```python
import functools
import math

import jax
import jax.numpy as jnp
from jax import lax
from jax.experimental import pallas as pl
from jax.experimental.pallas import tpu as pltpu

F32 = jnp.float32
BF16 = jnp.bfloat16
HIGHEST = lax.Precision.HIGHEST

RMS_EPS = 1e-6
L2_EPS = 1e-6
WINDOW = 128
PAST_LEN = 8192
HEAD_DIM = 128
SWA_GROUP = 4
DN_CONV = 4
DN_CHUNK = 64
HIST_ROWS = 8

VMEM_LIMIT_BYTES = 56 * 1024 * 1024
ROW_TILE = 512
NEG_BIG = -1e30


def _rms(x, g):
    return x * lax.rsqrt(jnp.mean(x * x, axis=-1, keepdims=True) + RMS_EPS) * g


def _silu(x):
    return x * jax.nn.sigmoid(x)


def _dot(a, b):
    return jnp.dot(a, b, preferred_element_type=F32)


def _dot_hi(a, b):
    return jnp.dot(a, b, preferred_element_type=F32, precision=HIGHEST)


def _dot_nt_hi(a, b):
    return lax.dot_general(a, b, (((1,), (1,)), ((), ())), preferred_element_type=F32, precision=HIGHEST)


def _dot_tn_hi(a, b):
    return lax.dot_general(a, b, (((0,), (0,)), ((), ())), preferred_element_type=F32, precision=HIGHEST)


def _dot_nt(a, b):
    return lax.dot_general(a, b, (((1,), (1,)), ((), ())), preferred_element_type=F32)


def _dot_tn(a, b):
    return lax.dot_general(a, b, (((0,), (0,)), ((), ())), preferred_element_type=F32)


def _split_bf16(a):
    hi = a.astype(BF16)
    return hi, (a - hi.astype(F32)).astype(BF16)


def _dot3(a, b, kind="nn"):
    a_hi, a_lo = _split_bf16(a)
    b_hi, b_lo = _split_bf16(b)
    if kind == "tn":
        m = a.shape[1]
        top = _dot_tn(jnp.concatenate([a_hi, a_lo], axis=1), b_hi)
        return top[:m] + top[m:] + _dot_tn(a_hi, b_lo)
    dot = _dot if kind == "nn" else _dot_nt
    m = a.shape[0]
    top = dot(jnp.concatenate([a_hi, a_lo], axis=0), b_hi)
    return top[:m] + top[m:] + dot(a_hi, b_lo)


def _row_tile(rows):
    return ROW_TILE if rows % ROW_TILE == 0 else rows


def _resident(shape):
    return pl.BlockSpec(shape, lambda *_: (0,) * len(shape), pipeline_mode=pl.Buffered(1))


def _params(*semantics):
    return pltpu.CompilerParams(dimension_semantics=semantics, vmem_limit_bytes=VMEM_LIMIT_BYTES)


def _ffn_tile(h, g_pre, g_post, wg_ref, wu_ref, wd_ref, ff_chunk):
    xn = _rms(h, g_pre).astype(BF16)
    acc = None
    for c0 in range(0, wg_ref.shape[1], ff_chunk):
        gate = _dot(xn, wg_ref[:, c0:c0 + ff_chunk])
        up = _dot(xn, wu_ref[:, c0:c0 + ff_chunk])
        part = _dot((_silu(gate) * up).astype(BF16), wd_ref[c0:c0 + ff_chunk, :])
        acc = part if acc is None else acc + part
    return h + 0.5 * _rms(acc, g_post)


def _ff_chunk(d_ff):
    half = d_ff // 2
    return half if d_ff % 2 == 0 and half % 128 == 0 else d_ff


def _ffn_kernel(h_ref, gpre_ref, gpost_ref, wg_ref, wu_ref, wd_ref, o_ref, *, ff_chunk):
    o_ref[...] = _ffn_tile(h_ref[...], gpre_ref[...], gpost_ref[...], wg_ref, wu_ref, wd_ref, ff_chunk)


def _ffn(h, g_pre, g_post, wg, wu, wd):
    rows, d = h.shape
    d_ff = wg.shape[1]
    tm = _row_tile(rows)
    row_spec = pl.BlockSpec((tm, d), lambda i: (i, 0))
    return pl.pallas_call(
        functools.partial(_ffn_kernel, ff_chunk=_ff_chunk(d_ff)),
        grid=(rows // tm,),
        in_specs=[row_spec, _resident((1, d)), _resident((1, d)),
                  _resident((d, d_ff)), _resident((d, d_ff)), _resident((d_ff, d))],
        out_specs=row_spec,
        out_shape=jax.ShapeDtypeStruct((rows, d), F32),
        compiler_params=_params("parallel"),
        name="ffn",
    )(h, g_pre, g_post, wg, wu, wd)


def _proj_kernel(h_ref, g_ref, w_ref, o_ref):
    o_ref[...] = _dot(_rms(h_ref[...], g_ref[...]).astype(BF16), w_ref[...])


def _proj(h, g, w):
    rows, d = h.shape
    n = w.shape[1]
    tm = _row_tile(rows)
    return pl.pallas_call(
        _proj_kernel,
        grid=(rows // tm,),
        in_specs=[pl.BlockSpec((tm, d), lambda i: (i, 0)), _resident((1, d)), _resident((d, n))],
        out_specs=pl.BlockSpec((tm, n), lambda i: (i, 0)),
        out_shape=jax.ShapeDtypeStruct((rows, n), F32),
        compiler_params=_params("parallel"),
        name="proj",
    )(h, g, w)


def _mixffn_kernel(h_ref, odn_ref, osw_ref, gdn_ref, gsw_ref, wout_ref, gmix_ref,
                   gpre_ref, gpost_ref, wg_ref, wu_ref, wd_ref, o_ref, *, ff_chunk):
    y = jax.nn.sigmoid(gdn_ref[...]) * odn_ref[...] + jax.nn.sigmoid(gsw_ref[...]) * osw_ref[...]
    h2 = h_ref[...] + _rms(_dot(y.astype(BF16), wout_ref[...]), gmix_ref[...])
    o_ref[...] = _ffn_tile(h2, gpre_ref[...], gpost_ref[...], wg_ref, wu_ref, wd_ref, ff_chunk)


def _mixffn(h, o_dn, o_sw, proj_a, proj_b, cols, w_out, g_mix, g_pre, g_post, wg, wu, wd):
    rows, d = h.shape
    d_ff = wg.shape[1]
    tm = _row_tile(rows)
    row_spec = pl.BlockSpec((tm, d), lambda i: (i, 0))
    gdn_blk, gsw_blk = cols["g_dn"] // d, cols["g_swa"] // d
    return pl.pallas_call(
        functools.partial(_mixffn_kernel, ff_chunk=_ff_chunk(d_ff)),
        grid=(rows // tm,),
        in_specs=[row_spec, row_spec, row_spec,
                  pl.BlockSpec((tm, d), lambda i: (i, gdn_blk)),
                  pl.BlockSpec((tm, d), lambda i: (i, gsw_blk)),
                  _resident((d, d)), _resident((1, d)), _resident((1, d)), _resident((1, d)),
                  _resident((d, d_ff)), _resident((d, d_ff)), _resident((d_ff, d))],
        out_specs=row_spec,
        out_shape=jax.ShapeDtypeStruct((rows, d), F32),
        compiler_params=_params("parallel"),
        name="mixffn",
    )(h, o_dn, o_sw, proj_a, proj_b, w_out, g_mix, g_pre, g_post, wg, wu, wd)


def _unit_lower_inverse(a, eye):
    c = a.shape[0]
    t = eye - a
    p = a
    for _ in range(int(math.log2(c)) - 1):
        p = _dot_hi(p, p)
        t = t + _dot_hi(t, p)
    return t


def _gdn_kernel(qkv_ref, z_ref, ba_ref, hist_ref, s0_ref, convw_ref, alog_ref, dtb_ref, nw_ref,
                o_ref, sout_ref, xbuf_ref, s_ref, *, chunk, heads):
    j = pl.program_id(1)
    tb = qkv_ref.shape[0]
    hd = HEAD_DIM
    qk_w = heads * hd

    @pl.when(j == 0)
    def _():
        xbuf_ref[0:HIST_ROWS, :] = hist_ref[0]
        s_ref[...] = s0_ref[0]

    xbuf_ref[HIST_ROWS:HIST_ROWS + tb, :] = qkv_ref[...]

    rows = lax.broadcasted_iota(jnp.int32, (chunk, chunk), 0)
    cols = lax.broadcasted_iota(jnp.int32, (chunk, chunk), 1)
    causal = rows >= cols
    strict = rows > cols
    eye = (rows == cols).astype(F32)
    lower_ones = causal.astype(F32)

    def conv_silu(r0, c0):
        acc = None
        for tap in range(DN_CONV):
            lo = HIST_ROWS + r0 - (DN_CONV - 1) + tap
            term = xbuf_ref[lo:lo + chunk, c0:c0 + hd] * convw_ref[tap:tap + 1, c0:c0 + hd]
            acc = term if acc is None else acc + term
        return _silu(acc)

    for r0 in range(0, tb, chunk):
        ba = ba_ref[r0:r0 + chunk, :]
        beta_all = jax.nn.sigmoid(ba[:, 0:heads])
        a_in = ba[:, heads:2 * heads] + dtb_ref[...]
        softplus = jnp.maximum(a_in, 0.0) + jnp.log(1.0 + jnp.exp(-jnp.abs(a_in)))
        g_all = -jnp.exp(alog_ref[...]) * softplus
        gc_col_all = _dot_hi(lower_ones, g_all)
        gc_row_all = _dot_tn_hi(g_all, (rows <= cols).astype(F32))
        for h in range(heads):
            q = conv_silu(r0, h * hd)
            k = conv_silu(r0, qk_w + h * hd)
            v = conv_silu(r0, 2 * qk_w + h * hd)
            q = q * lax.rsqrt(jnp.sum(q * q, axis=-1, keepdims=True) + L2_EPS) * (hd ** -0.5)
            k = k * lax.rsqrt(jnp.sum(k * k, axis=-1, keepdims=True) + L2_EPS)
            beta = beta_all[:, h:h + 1]
            gc = gc_col_all[:, h:h + 1]
            gc_row = gc_row_all[h:h + 1, :]
            gc_last = gc_row[:, chunk - 1:chunk]
            decay = jnp.where(causal, jnp.exp(jnp.where(causal, gc - gc_row, 0.0)), 0.0)
            a_mat = jnp.where(strict, beta * _dot_nt_hi(k, k) * decay, 0.0)
            t_inv = _unit_lower_inverse(a_mat, eye)
            rhs = jnp.concatenate([v * beta, k * (beta * jnp.exp(gc))], axis=-1)
            sol = _dot_hi(t_inv, rhs)
            u_base, w = sol[:, :hd], sol[:, hd:]
            qk = _dot_nt_hi(q, k) * decay
            s = s_ref[h]
            u = u_base - _dot_hi(w, s)
            o = _dot_hi(q * jnp.exp(gc), s) + _dot_hi(qk, u)
            s_ref[h] = s * jnp.exp(gc_last) + _dot_tn_hi(k * jnp.exp(gc_last - gc), u)
            zh = z_ref[r0:r0 + chunk, h * hd:(h + 1) * hd]
            o_ref[r0:r0 + chunk, h * hd:(h + 1) * hd] = _rms(o, nw_ref[...]) * _silu(zh)

    xbuf_ref[0:HIST_ROWS, :] = xbuf_ref[tb:tb + HIST_ROWS, :]

    @pl.when(j == pl.num_programs(1) - 1)
    def _():
        sout_ref[0] = s_ref[...]


GROUP = 4


def _gdn_packed_kernel(qkv_ref, z_ref, ba_ref, hist_ref, s0_ref, convw_ref, alog_ref, dtb_ref, nw_ref,
                       o_ref, sout_ref, xbuf_ref, s_ref, *, chunk, heads):
    j = pl.program_id(1)
    tb = qkv_ref.shape[0]
    hd = HEAD_DIM
    c = chunk
    gw = GROUP * c
    qk_w = heads * hd

    @pl.when(j == 0)
    def _():
        xbuf_ref[0:HIST_ROWS, :] = hist_ref[0]
        s_ref[...] = s0_ref[0]

    xbuf_ref[HIST_ROWS:HIST_ROWS + tb, :] = qkv_ref[...]

    def iota(shape, dim):
        return lax.broadcasted_iota(jnp.int32, shape, dim)

    row = iota((c, gw), 0)
    col = iota((c, gw), 1) & (c - 1)
    causal4, strict4 = row >= col, row > col
    eye4 = (row == col).astype(F32)
    bd_mask = (iota((gw, gw), 0) // c) == (iota((gw, gw), 1) // c)
    lower_ones = (iota((c, c), 0) >= iota((c, c), 1)).astype(F32)
    upper_dup = (iota((c, 2 * c), 0) <= (iota((c, 2 * c), 1) & (c - 1))).astype(F32)
    first_half = iota((c, 2 * c), 1) < c
    zeros_head = jnp.zeros((c, hd), F32)

    def conv_silu(r0, c0):
        acc = None
        for tap in range(DN_CONV):
            lo = HIST_ROWS + r0 - (DN_CONV - 1) + tap
            term = xbuf_ref[lo:lo + c, c0:c0 + hd] * convw_ref[tap:tap + 1, c0:c0 + hd]
            acc = term if acc is None else acc + term
        return _silu(acc)

    def block_diag(x4):
        return jnp.where(bd_mask, jnp.concatenate([x4] * GROUP, axis=0), 0.0)

    for r0 in range(0, tb, c):
        ba = ba_ref[r0:r0 + c, :]
        beta_all = jax.nn.sigmoid(ba[:, 0:heads])
        a_in = ba[:, heads:2 * heads] + dtb_ref[...]
        softplus = jnp.maximum(a_in, 0.0) + jnp.log(1.0 + jnp.exp(-jnp.abs(a_in)))
        g_all = -jnp.exp(alog_ref[...]) * softplus
        gc_all = _dot_hi(lower_ones, g_all)
        gc_row_dup = _dot_tn_hi(g_all, upper_dup)

        def pack_cols(col_all, hs):
            halves = [jnp.where(first_half,
                                jnp.broadcast_to(col_all[:, hs[2 * p]:hs[2 * p] + 1], (c, 2 * c)),
                                jnp.broadcast_to(col_all[:, hs[2 * p + 1]:hs[2 * p + 1] + 1], (c, 2 * c)))
                      for p in range(GROUP // 2)]
            return jnp.concatenate(halves, axis=1)

        for g0 in range(0, heads, GROUP):
            hs = list(range(g0, g0 + GROUP))
            qs, ks, vs = [], [], []
            for h in hs:
                q = conv_silu(r0, h * hd)
                k = conv_silu(r0, qk_w + h * hd)
                qs.append(q * lax.rsqrt(jnp.sum(q * q, axis=-1, keepdims=True) + L2_EPS) * (hd ** -0.5))
                ks.append(k * lax.rsqrt(jnp.sum(k * k, axis=-1, keepdims=True) + L2_EPS))
                vs.append(conv_silu(r0, 2 * qk_w + h * hd))
            beta4 = pack_cols(beta_all, hs)
            gc_row4 = jnp.concatenate(
                [jnp.where(first_half[0:1], gc_row_dup[hs[2 * p]:hs[2 * p] + 1], gc_row_dup[hs[2 * p + 1]:hs[2 * p + 1] + 1])
                 for p in range(GROUP // 2)], axis=1)
            diff4 = pack_cols(gc_all, hs) - gc_row4
            decay4 = jnp.where(causal4, jnp.exp(jnp.where(causal4, diff4, 0.0)), 0.0)
            lhs = jnp.concatenate([jnp.concatenate(qs, axis=1), jnp.concatenate(ks, axis=1)], axis=0)
            k_bd = jnp.concatenate(
                [jnp.concatenate([ks[i] if ii == i else zeros_head for ii in range(GROUP)], axis=1)
                 for i in range(GROUP)], axis=0)
            qkkk = _dot3(lhs, k_bd, "nt")
            qk4 = qkkk[:c] * decay4
            a4 = jnp.where(strict4, beta4 * qkkk[c:] * decay4, 0.0)
            t4, p4 = eye4 - a4, _dot_hi(a4, block_diag(a4))
            n_sq = int(math.log2(c)) - 1
            for step in range(n_sq):
                bd = block_diag(p4)
                if step < n_sq - 1:
                    both = _dot_hi(jnp.concatenate([p4, t4], axis=0), bd)
                    p4, t4 = both[:c], t4 + both[c:]
                else:
                    t4 = t4 + _dot_hi(t4, bd)
            rhs = jnp.concatenate(
                [jnp.concatenate([vs[i] * beta_all[:, h:h + 1],
                                  ks[i] * (beta_all[:, h:h + 1] * jnp.exp(gc_all[:, h:h + 1]))], axis=1)
                 for i, h in enumerate(hs)], axis=0)
            sol = _dot3(block_diag(t4), rhs)
            us, o_inter = [], []
            for i, h in enumerate(hs):
                sl = slice(i * c, (i + 1) * c)
                ws = _dot3(jnp.concatenate([sol[sl, hd:], qs[i] * jnp.exp(gc_all[:, h:h + 1])], axis=0), s_ref[h])
                us.append(sol[sl, :hd] - ws[:c])
                o_inter.append(ws[c:])
            o_intra = _dot3(block_diag(qk4), jnp.concatenate(us, axis=0))
            for i, h in enumerate(hs):
                gc = gc_all[:, h:h + 1]
                gc_last = gc_row_dup[h:h + 1, c - 1:c]
                s_ref[h] = s_ref[h] * jnp.exp(gc_last) + _dot3(ks[i] * jnp.exp(gc_last - gc), us[i], "tn")
                o = o_inter[i] + o_intra[i * c:(i + 1) * c]
                zh = z_ref[r0:r0 + c, h * hd:(h + 1) * hd]
                o_ref[r0:r0 + c, h * hd:(h + 1) * hd] = _rms(o, nw_ref[...]) * _silu(zh)

    xbuf_ref[0:HIST_ROWS, :] = xbuf_ref[tb:tb + HIST_ROWS, :]

    @pl.when(j == pl.num_programs(1) - 1)
    def _():
        sout_ref[0] = s_ref[...]


def _gdn(proj_a, cols, hist, s0, conv_w, a_log, dt_bias, norm_w, *, n_seq, seq_len, block, chunk):
    heads = a_log.shape[1]
    conv_width = conv_w.shape[1]
    v_w = heads * HEAD_DIM
    nblk = seq_len // block
    z_blk, ba_blk = cols["z"] // v_w, cols["ba"] // 128
    shared_hist, shared_s0 = hist.shape[0] == 1, s0.shape[0] == 1
    packed = 2 * chunk == HEAD_DIM and heads % GROUP == 0
    return pl.pallas_call(
        functools.partial(_gdn_packed_kernel if packed else _gdn_kernel, chunk=chunk, heads=heads),
        grid=(n_seq, nblk),
        in_specs=[
            pl.BlockSpec((block, conv_width), lambda n, j: (n * nblk + j, 0)),
            pl.BlockSpec((block, v_w), lambda n, j: (n * nblk + j, z_blk)),
            pl.BlockSpec((block, 128), lambda n, j: (n * nblk + j, ba_blk)),
            pl.BlockSpec((1, HIST_ROWS, conv_width), lambda n, j: (0 if shared_hist else n, 0, 0)),
            pl.BlockSpec((1, heads, HEAD_DIM, HEAD_DIM), lambda n, j: (0 if shared_s0 else n, 0, 0, 0)),
            pl.BlockSpec((DN_CONV, conv_width), lambda n, j: (0, 0)),
            pl.BlockSpec((1, heads), lambda n, j: (0, 0)),
            pl.BlockSpec((1, heads), lambda n, j: (0, 0)),
            pl.BlockSpec((1, HEAD_DIM), lambda n, j: (0, 0)),
        ],
        out_specs=[
            pl.BlockSpec((block, v_w), lambda n, j: (n * nblk + j, 0)),
            pl.BlockSpec((1, heads, HEAD_DIM, HEAD_DIM), lambda n, j: (n, 0, 0, 0)),
        ],
        out_shape=[
            jax.ShapeDtypeStruct((n_seq * seq_len, v_w), F32),
            jax.ShapeDtypeStruct((n_seq, heads, HEAD_DIM, HEAD_DIM), F32),
        ],
        scratch_shapes=[
            pltpu.VMEM((block + HIST_ROWS, conv_width), F32),
            pltpu.VMEM((heads, HEAD_DIM, HEAD_DIM), F32),
        ],
        compiler_params=_params("parallel", "arbitrary"),
        name="gdn",
    )(proj_a, proj_a, proj_a, hist, s0, conv_w, a_log, dt_bias, norm_w)


def _alibi_slope(head, n_heads):
    return 2.0 ** (-8.0 * (head + 1) / n_heads)


def _attend(q4, segments, sink_logits, slopes, n_rows):
    scale = HEAD_DIM ** -0.5
    scores = [_dot_nt(q4, k) * scale for k, _, _, _ in segments]
    outs = []
    for g in range(len(slopes)):
        r = slice(g * n_rows, (g + 1) * n_rows)
        logits = [jnp.where(mask, sc[r] - slopes[g] * jnp.minimum(dist, WINDOW).astype(F32), NEG_BIG)
                  for sc, (_, _, dist, mask) in zip(scores, segments)]
        m = sink_logits[g]
        for lg in logits:
            m = jnp.maximum(m, jnp.max(lg, axis=-1, keepdims=True))
        denom = jnp.exp(sink_logits[g] - m)
        acc = None
        for lg, (_, v, _, _) in zip(logits, segments):
            p = jnp.exp(lg - m)
            denom = denom + jnp.sum(p, axis=-1, keepdims=True)
            pv = _dot(p.astype(BF16), v)
            acc = pv if acc is None else acc + pv
        outs.append(acc / denom)
    return outs


def _swa_prompt_kernel(q_ref, kv_ref, kvprev_ref, kvmeta_ref, sinks_ref, o_ref, *, n_meta, kv_heads):
    j = pl.program_id(1)
    tq = q_ref.shape[0]
    hd = HEAD_DIM
    kv_w = kv_heads * hd
    n_heads = kv_heads * SWA_GROUP
    qi = lax.broadcasted_iota(jnp.int32, (tq, tq), 0)
    ki = lax.broadcasted_iota(jnp.int32, (tq, tq), 1)
    dist_own = qi - ki
    dist_prev = dist_own + tq
    mask_own = dist_own >= 0
    mask_prev = (dist_prev <= WINDOW) & (j > 0)
    qpos = n_meta + j * tq + lax.broadcasted_iota(jnp.int32, (tq, n_meta), 0)
    dist_meta = qpos - lax.broadcasted_iota(jnp.int32, (tq, n_meta), 1)
    mask_meta = dist_meta >= 0
    for kvh in range(kv_heads):
        ks = slice(kvh * hd, (kvh + 1) * hd)
        vs = slice(kv_w + kvh * hd, kv_w + (kvh + 1) * hd)
        segments = [
            (kvmeta_ref[:, ks].astype(BF16), kvmeta_ref[:, vs].astype(BF16), dist_meta, mask_meta),
            (kvprev_ref[:, ks].astype(BF16), kvprev_ref[:, vs].astype(BF16), dist_prev, mask_prev),
            (kv_ref[:, ks].astype(BF16), kv_ref[:, vs].astype(BF16), dist_own, mask_own),
        ]
        heads = [kvh * SWA_GROUP + g for g in range(SWA_GROUP)]
        q4 = jnp.concatenate([q_ref[:, hh * hd:(hh + 1) * hd] for hh in heads], axis=0).astype(BF16)
        sink_logits = [sinks_ref[:, hh:hh + 1] for hh in heads]
        slopes = [_alibi_slope(hh, n_heads) for hh in heads]
        for hh, o in zip(heads, _attend(q4, segments, sink_logits, slopes, tq)):
            o_ref[:, hh * hd:(hh + 1) * hd] = o


def _swa_prompt(proj_b, proj_b_meta, cols, sinks, *, n_seq, seq_len, n_meta):
    n_heads = sinks.shape[1]
    kv_heads = n_heads // SWA_GROUP
    q_w, kv_w2 = n_heads * HEAD_DIM, 2 * kv_heads * HEAD_DIM
    tq = WINDOW
    nblk = seq_len // tq
    kv_blk = cols["skv"] // kv_w2
    return pl.pallas_call(
        functools.partial(_swa_prompt_kernel, n_meta=n_meta, kv_heads=kv_heads),
        grid=(n_seq, nblk),
        in_specs=[
            pl.BlockSpec((tq, q_w), lambda n, j: (n * nblk + j, 0)),
            pl.BlockSpec((tq, kv_w2), lambda n, j: (n * nblk + j, kv_blk)),
            pl.BlockSpec((tq, kv_w2), lambda n, j: (n * nblk + jnp.maximum(j - 1, 0), kv_blk)),
            pl.BlockSpec((n_meta, kv_w2), lambda n, j: (0, kv_blk)),
            pl.BlockSpec((1, n_heads), lambda n, j: (0, 0)),
        ],
        out_specs=pl.BlockSpec((tq, q_w), lambda n, j: (n * nblk + j, 0)),
        out_shape=jax.ShapeDtypeStruct((n_seq * seq_len, q_w), F32),
        compiler_params=_params("parallel", "arbitrary"),
        name="swa_prompt",
    )(proj_b, proj_b, proj_b, proj_b_meta, sinks)


def _swa_step_kernel(q_ref, kvnew_ref, kmeta_ref, vmeta_ref, kbuf_ref, vbuf_ref, sinks_ref,
                     o_ref, kout_ref, vout_ref, *, kv_heads):
    nb, t, _ = q_ref.shape
    n_meta, w = kmeta_ref.shape[1], kbuf_ref.shape[1]
    hd = HEAD_DIM
    kv_w = kv_heads * hd
    n_heads = kv_heads * SWA_GROUP

    def grid2(rows, cols_):
        return (lax.broadcasted_iota(jnp.int32, (rows, cols_), 0), lax.broadcasted_iota(jnp.int32, (rows, cols_), 1))

    ti, mi = grid2(t, n_meta)
    dist_meta = PAST_LEN + ti - mi
    mask_meta = dist_meta >= 0
    ti, bi = grid2(t, w)
    dist_buf = w + ti - bi
    mask_buf = (dist_buf <= WINDOW) & (PAST_LEN - w + bi >= n_meta)
    ti, si = grid2(t, t)
    dist_new = ti - si
    mask_new = dist_new >= 0
    for b in range(nb):
        kv_new = kvnew_ref[b]
        kout_ref[b, 0:w - t, :] = kbuf_ref[b, t:w, :]
        kout_ref[b, w - t:w, :] = kv_new[:, 0:kv_w]
        vout_ref[b, 0:w - t, :] = vbuf_ref[b, t:w, :]
        vout_ref[b, w - t:w, :] = kv_new[:, kv_w:2 * kv_w]
        for kvh in range(kv_heads):
            ks = slice(kvh * hd, (kvh + 1) * hd)
            segments = [
                (kmeta_ref[b, :, ks].astype(BF16), vmeta_ref[b, :, ks].astype(BF16), dist_meta, mask_meta),
                (kbuf_ref[b, :, ks].astype(BF16), vbuf_ref[b, :, ks].astype(BF16), dist_buf, mask_buf),
                (kv_new[:, ks].astype(BF16), kv_new[:, kv_w + kvh * hd:kv_w + (kvh + 1) * hd].astype(BF16),
                 dist_new, mask_new),
            ]
            heads = [kvh * SWA_GROUP + g for g in range(SWA_GROUP)]
            q4 = jnp.concatenate([q_ref[b, :, hh * hd:(hh + 1) * hd] for hh in heads], axis=0).astype(BF16)
            sink_logits = [sinks_ref[:, hh:hh + 1] for hh in heads]
            slopes = [_alibi_slope(hh, n_heads) for hh in heads]
            for hh, o in zip(heads, _attend(q4, segments, sink_logits, slopes, t)):
                o_ref[b, :, hh * hd:(hh + 1) * hd] = o


def _swa_step(proj_b3, cols, k_meta, v_meta, k_buf, v_buf, sinks, *, seqs_per_step):
    n, t, _ = proj_b3.shape
    n_heads = sinks.shape[1]
    kv_heads = n_heads // SWA_GROUP
    q_w, kv_w = n_heads * HEAD_DIM, kv_heads * HEAD_DIM
    n_meta, w = k_meta.shape[1], k_buf.shape[1]
    nb = seqs_per_step
    kv_blk = cols["skv"] // (2 * kv_w)
    seq3 = lambda rows, width, blk=0: pl.BlockSpec((nb, rows, width), lambda i: (i, 0, blk))
    return pl.pallas_call(
        functools.partial(_swa_step_kernel, kv_heads=kv_heads),
        grid=(n // nb,),
        in_specs=[seq3(t, q_w), seq3(t, 2 * kv_w, kv_blk), seq3(n_meta, kv_w), seq3(n_meta, kv_w),
                  seq3(w, kv_w), seq3(w, kv_w), pl.BlockSpec((1, n_heads), lambda i: (0, 0))],
        out_specs=[seq3(t, q_w), seq3(w, kv_w), seq3(w, kv_w)],
        out_shape=[jax.ShapeDtypeStruct((n, t, q_w), F32),
                   jax.ShapeDtypeStruct((n, w, kv_w), F32),
                   jax.ShapeDtypeStruct((n, w, kv_w), F32)],
        compiler_params=_params("parallel"),
        name="swa_step",
    )(proj_b3, proj_b3, k_meta, v_meta, k_buf, v_buf, sinks)


def _split_w_in(w_in, heads, n_heads, kv_heads, d):
    conv_w = 3 * heads * HEAD_DIM
    v_w = heads * HEAD_DIM
    q_w, kv_w = n_heads * HEAD_DIM, kv_heads * HEAD_DIM
    sizes = (conv_w, v_w, heads, heads, q_w, kv_w, kv_w, d, d)
    offs = [0]
    for s in sizes:
        offs.append(offs[-1] + s)
    part = lambda i: w_in[:, offs[i]:offs[i + 1]]
    ba = jnp.concatenate([part(2), part(3), jnp.zeros((w_in.shape[0], 128 - 2 * heads), w_in.dtype)], axis=1)
    w_a = jnp.concatenate([part(0), part(1), part(7), ba], axis=1).astype(BF16)
    w_b = jnp.concatenate([part(4), part(8), part(5), part(6)], axis=1).astype(BF16)
    cols_a = {"qkv": 0, "z": conv_w, "g_dn": conv_w + v_w, "ba": conv_w + v_w + d}
    cols_b = {"q": 0, "g_swa": q_w, "skv": q_w + d}
    return w_a, w_b, cols_a, cols_b


def kernel(x_prompt, x_sample, state_dn_conv, state_dn_ssm, cache_swa_meta_k, cache_swa_meta_v, cache_swa_k, cache_swa_v, meta_tokens, ffn1_norm_pre, ffn1_norm_post, ffn1_w_gate, ffn1_w_up, ffn1_w_down, mix_norm_pre, mix_norm_post, w_in, dn_conv_w, dn_a_log, dn_dt_bias, dn_norm_w, swa_sinks, w_out, ffn2_norm_pre, ffn2_norm_post, ffn2_w_gate, ffn2_w_up, ffn2_w_down):
    assert w_in.shape[0] == 1, "single-layer step"
    n_p, seq, d = x_prompt.shape
    n_s, t_s, _ = x_sample.shape
    n_meta = meta_tokens.shape[0]
    heads = dn_a_log.shape[1]
    n_heads = swa_sinks.shape[1]
    kv_heads = n_heads // SWA_GROUP
    conv_w = dn_conv_w.shape[2]
    kv_w = kv_heads * HEAD_DIM
    w_keep = cache_swa_k.shape[2]
    assert w_keep == WINDOW and seq % WINDOW == 0 and seq % DN_CHUNK == 0 and seq >= WINDOW

    w_a, w_b, cols_a, cols_b = _split_w_in(w_in[0], heads, n_heads, kv_heads, d)
    cols = {**cols_a, **cols_b}
    f1 = (ffn1_norm_pre, ffn1_norm_post, ffn1_w_gate[0].astype(BF16), ffn1_w_up[0].astype(BF16),
          ffn1_w_down[0].astype(BF16))
    f2 = (ffn2_norm_pre, ffn2_norm_post, ffn2_w_gate[0].astype(BF16), ffn2_w_up[0].astype(BF16),
          ffn2_w_down[0].astype(BF16))
    w_o = w_out[0].astype(BF16)
    gdn_w = (dn_conv_w[0], dn_a_log, dn_dt_bias, dn_norm_w)

    def front(h):
        h1 = _ffn(h, *f1)
        return h1, _proj(h1, mix_norm_pre, w_a), _proj(h1, mix_norm_pre, w_b)

    def back(h1, o_dn, o_sw, pa, pb):
        return _mixffn(h1, o_dn, o_sw, pa, pb, cols, w_o, mix_norm_post, *f2)

    def hist_tile(rows3):
        return jnp.pad(rows3, ((0, 0), (HIST_ROWS - (DN_CONV - 1), 0), (0, 0)))

    _, pa_m, pb_m = front(meta_tokens)
    zero_hist = jnp.zeros((1, HIST_ROWS, conv_w), F32)
    zero_state = jnp.zeros((1, heads, HEAD_DIM, HEAD_DIM), F32)
    _, s_meta = _gdn(pa_m, cols, zero_hist, zero_state, *gdn_w, n_seq=1, seq_len=n_meta, block=n_meta,
                     chunk=n_meta)
    hist_meta = hist_tile(pa_m[None, n_meta - (DN_CONV - 1):, :conv_w])

    h1_p, pa_p, pb_p = front(x_prompt.reshape(n_p * seq, d))
    o_dn_p, s_p = _gdn(pa_p, cols, hist_meta, s_meta, *gdn_w, n_seq=n_p, seq_len=seq, block=2 * DN_CHUNK,
                       chunk=DN_CHUNK)
    o_sw_p = _swa_prompt(pb_p, pb_m, cols, swa_sinks, n_seq=n_p, seq_len=seq, n_meta=n_meta)
    y_prompt = back(h1_p, o_dn_p, o_sw_p, pa_p, pb_p).reshape(n_p, seq, d)

    pa_p3 = pa_p.reshape(n_p, seq, -1)
    pb_p3 = pb_p.reshape(n_p, seq, -1)
    p_conv = pa_p3[:, seq - (DN_CONV - 1):, :conv_w]
    kv_meta = pb_m[:, cols["skv"]:]
    p_meta_k = jnp.broadcast_to(kv_meta[None, :, :kv_w], (n_p, n_meta, kv_w))
    p_meta_v = jnp.broadcast_to(kv_meta[None, :, kv_w:], (n_p, n_meta, kv_w))
    p_win_k = pb_p3[:, seq - w_keep:, cols["skv"]:cols["skv"] + kv_w]
    p_win_v = pb_p3[:, seq - w_keep:, cols["skv"] + kv_w:]

    h1_s, pa_s, pb_s = front(x_sample.reshape(n_s * t_s, d))
    o_dn_s, s_s = _gdn(pa_s, cols, hist_tile(state_dn_conv[0]), state_dn_ssm[0], *gdn_w, n_seq=n_s,
                       seq_len=t_s, block=t_s, chunk=t_s)
    o_sw_s, s_win_k, s_win_v = _swa_step(
        pb_s.reshape(n_s, t_s, -1), cols,
        cache_swa_meta_k[0].reshape(n_s, n_meta, kv_w), cache_swa_meta_v[0].reshape(n_s, n_meta, kv_w),
        cache_swa_k[0].reshape(n_s, w_keep, kv_w), cache_swa_v[0].reshape(n_s, w_keep, kv_w),
        swa_sinks, seqs_per_step=4)
    y_sample = back(h1_s, o_dn_s, o_sw_s.reshape(n_s * t_s, -1), pa_s, pb_s).reshape(n_s, t_s, d)
    xp_s = jnp.concatenate([state_dn_conv[0], pa_s.reshape(n_s, t_s, -1)[:, :, :conv_w]], axis=1)
    s_conv = xp_s[:, -(DN_CONV - 1):]

    def kv4(x):
        return x.reshape(1, x.shape[0], x.shape[1], kv_heads, HEAD_DIM)

    return (y_prompt, y_sample, p_conv[None], s_p[None], kv4(p_meta_k), kv4(p_meta_v), kv4(p_win_k),
            kv4(p_win_v), s_conv[None], s_s[None], kv4(s_win_k), kv4(s_win_v))
```

```python
import functools
import math

import jax
import jax.numpy as jnp
from jax import lax
from jax.experimental import pallas as pl
from jax.experimental.pallas import tpu as pltpu

F32 = jnp.float32
BF16 = jnp.bfloat16
HIGHEST = lax.Precision.HIGHEST

RMS_EPS = 1e-6
L2_EPS = 1e-6
WINDOW = 128
PAST_LEN = 8192
HEAD_DIM = 128
SWA_GROUP = 4
DN_CONV = 4
DN_CHUNK = 64
HIST_ROWS = 8

VMEM_LIMIT_BYTES = 56 * 1024 * 1024
ROW_TILE = 512
NEG_BIG = -1e30


def _rms(x, g):
    return x * lax.rsqrt(jnp.mean(x * x, axis=-1, keepdims=True) + RMS_EPS) * g


def _silu(x):
    return x * jax.nn.sigmoid(x)


def _dot(a, b):
    return jnp.dot(a, b, preferred_element_type=F32)


def _dot_hi(a, b):
    return jnp.dot(a, b, preferred_element_type=F32, precision=HIGHEST)


def _dot_nt_hi(a, b):
    return lax.dot_general(a, b, (((1,), (1,)), ((), ())), preferred_element_type=F32, precision=HIGHEST)


def _dot_tn_hi(a, b):
    return lax.dot_general(a, b, (((0,), (0,)), ((), ())), preferred_element_type=F32, precision=HIGHEST)


def _dot_nt(a, b):
    return lax.dot_general(a, b, (((1,), (1,)), ((), ())), preferred_element_type=F32)


def _dot_tn(a, b):
    return lax.dot_general(a, b, (((0,), (0,)), ((), ())), preferred_element_type=F32)


def _split_bf16(a):
    hi = a.astype(BF16)
    return hi, (a - hi.astype(F32)).astype(BF16)


def _dot3(a, b, kind="nn"):
    a_hi, a_lo = _split_bf16(a)
    b_hi, b_lo = _split_bf16(b)
    if kind == "tn":
        m = a.shape[1]
        top = _dot_tn(jnp.concatenate([a_hi, a_lo], axis=1), b_hi)
        return top[:m] + top[m:] + _dot_tn(a_hi, b_lo)
    dot = _dot if kind == "nn" else _dot_nt
    m = a.shape[0]
    top = dot(jnp.concatenate([a_hi, a_lo], axis=0), b_hi)
    return top[:m] + top[m:] + dot(a_hi, b_lo)


def _row_tile(rows):
    return ROW_TILE if rows % ROW_TILE == 0 else rows


def _resident(shape):
    return pl.BlockSpec(shape, lambda *_: (0,) * len(shape), pipeline_mode=pl.Buffered(1))


def _params(*semantics):
    return pltpu.CompilerParams(dimension_semantics=semantics, vmem_limit_bytes=VMEM_LIMIT_BYTES)


def _ffn_tile(h, g_pre, g_post, wg_ref, wu_ref, wd_ref, ff_chunk):
    xn = _rms(h, g_pre).astype(BF16)
    acc = None
    for c0 in range(0, wg_ref.shape[1], ff_chunk):
        gate = _dot(xn, wg_ref[:, c0:c0 + ff_chunk])
        up = _dot(xn, wu_ref[:, c0:c0 + ff_chunk])
        part = _dot((_silu(gate) * up).astype(BF16), wd_ref[c0:c0 + ff_chunk, :])
        acc = part if acc is None else acc + part
    return h + 0.5 * _rms(acc, g_post)


def _ff_chunk(d_ff):
    half = d_ff // 2
    return half if d_ff % 2 == 0 and half % 128 == 0 else d_ff


def _ffn_kernel(h_ref, gpre_ref, gpost_ref, wg_ref, wu_ref, wd_ref, o_ref, *, ff_chunk):
    o_ref[...] = _ffn_tile(h_ref[...], gpre_ref[...], gpost_ref[...], wg_ref, wu_ref, wd_ref, ff_chunk)


def _ffn(h, g_pre, g_post, wg, wu, wd):
    rows, d = h.shape
    d_ff = wg.shape[1]
    tm = _row_tile(rows)
    row_spec = pl.BlockSpec((tm, d), lambda i: (i, 0))
    return pl.pallas_call(
        functools.partial(_ffn_kernel, ff_chunk=_ff_chunk(d_ff)),
        grid=(rows // tm,),
        in_specs=[row_spec, _resident((1, d)), _resident((1, d)),
                  _resident((d, d_ff)), _resident((d, d_ff)), _resident((d_ff, d))],
        out_specs=row_spec,
        out_shape=jax.ShapeDtypeStruct((rows, d), F32),
        compiler_params=_params("parallel"),
        name="ffn",
    )(h, g_pre, g_post, wg, wu, wd)


def _proj_kernel(h_ref, g_ref, w_ref, o_ref):
    o_ref[...] = _dot(_rms(h_ref[...], g_ref[...]).astype(BF16), w_ref[...])


def _proj(h, g, w):
    rows, d = h.shape
    n = w.shape[1]
    tm = _row_tile(rows)
    return pl.pallas_call(
        _proj_kernel,
        grid=(rows // tm,),
        in_specs=[pl.BlockSpec((tm, d), lambda i: (i, 0)), _resident((1, d)), _resident((d, n))],
        out_specs=pl.BlockSpec((tm, n), lambda i: (i, 0)),
        out_shape=jax.ShapeDtypeStruct((rows, n), F32),
        compiler_params=_params("parallel"),
        name="proj",
    )(h, g, w)


def _mixffn_kernel(h_ref, odn_ref, osw_ref, gdn_ref, gsw_ref, wout_ref, gmix_ref,
                   gpre_ref, gpost_ref, wg_ref, wu_ref, wd_ref, o_ref, *, ff_chunk):
    y = jax.nn.sigmoid(gdn_ref[...]) * odn_ref[...] + jax.nn.sigmoid(gsw_ref[...]) * osw_ref[...]
    h2 = h_ref[...] + _rms(_dot(y.astype(BF16), wout_ref[...]), gmix_ref[...])
    o_ref[...] = _ffn_tile(h2, gpre_ref[...], gpost_ref[...], wg_ref, wu_ref, wd_ref, ff_chunk)


def _mixffn(h, o_dn, o_sw, proj_a, proj_b, cols, w_out, g_mix, g_pre, g_post, wg, wu, wd):
    rows, d = h.shape
    d_ff = wg.shape[1]
    tm = _row_tile(rows)
    row_spec = pl.BlockSpec((tm, d), lambda i: (i, 0))
    gdn_blk, gsw_blk = cols["g_dn"] // d, cols["g_swa"] // d
    return pl.pallas_call(
        functools.partial(_mixffn_kernel, ff_chunk=_ff_chunk(d_ff)),
        grid=(rows // tm,),
        in_specs=[row_spec, row_spec, row_spec,
                  pl.BlockSpec((tm, d), lambda i: (i, gdn_blk)),
                  pl.BlockSpec((tm, d), lambda i: (i, gsw_blk)),
                  _resident((d, d)), _resident((1, d)), _resident((1, d)), _resident((1, d)),
                  _resident((d, d_ff)), _resident((d, d_ff)), _resident((d_ff, d))],
        out_specs=row_spec,
        out_shape=jax.ShapeDtypeStruct((rows, d), F32),
        compiler_params=_params("parallel"),
        name="mixffn",
    )(h, o_dn, o_sw, proj_a, proj_b, w_out, g_mix, g_pre, g_post, wg, wu, wd)


def _unit_lower_inverse(a, eye):
    c = a.shape[0]
    t = eye - a
    p = a
    for _ in range(int(math.log2(c)) - 1):
        p = _dot_hi(p, p)
        t = t + _dot_hi(t, p)
    return t


def _gdn_kernel(qkv_ref, z_ref, ba_ref, hist_ref, s0_ref, convw_ref, alog_ref, dtb_ref, nw_ref,
                o_ref, sout_ref, xbuf_ref, s_ref, *, chunk, heads):
    j = pl.program_id(1)
    tb = qkv_ref.shape[0]
    hd = HEAD_DIM
    qk_w = heads * hd

    @pl.when(j == 0)
    def _():
        xbuf_ref[0:HIST_ROWS, :] = hist_ref[0]
        s_ref[...] = s0_ref[0]

    xbuf_ref[HIST_ROWS:HIST_ROWS + tb, :] = qkv_ref[...]

    rows = lax.broadcasted_iota(jnp.int32, (chunk, chunk), 0)
    cols = lax.broadcasted_iota(jnp.int32, (chunk, chunk), 1)
    causal = rows >= cols
    strict = rows > cols
    eye = (rows == cols).astype(F32)
    lower_ones = causal.astype(F32)

    def conv_silu(r0, c0):
        acc = None
        for tap in range(DN_CONV):
            lo = HIST_ROWS + r0 - (DN_CONV - 1) + tap
            term = xbuf_ref[lo:lo + chunk, c0:c0 + hd] * convw_ref[tap:tap + 1, c0:c0 + hd]
            acc = term if acc is None else acc + term
        return _silu(acc)

    for r0 in range(0, tb, chunk):
        ba = ba_ref[r0:r0 + chunk, :]
        beta_all = jax.nn.sigmoid(ba[:, 0:heads])
        a_in = ba[:, heads:2 * heads] + dtb_ref[...]
        softplus = jnp.maximum(a_in, 0.0) + jnp.log(1.0 + jnp.exp(-jnp.abs(a_in)))
        g_all = -jnp.exp(alog_ref[...]) * softplus
        gc_col_all = _dot_hi(lower_ones, g_all)
        gc_row_all = _dot_tn_hi(g_all, (rows <= cols).astype(F32))
        for h in range(heads):
            q = conv_silu(r0, h * hd)
            k = conv_silu(r0, qk_w + h * hd)
            v = conv_silu(r0, 2 * qk_w + h * hd)
            q = q * lax.rsqrt(jnp.sum(q * q, axis=-1, keepdims=True) + L2_EPS) * (hd ** -0.5)
            k = k * lax.rsqrt(jnp.sum(k * k, axis=-1, keepdims=True) + L2_EPS)
            beta = beta_all[:, h:h + 1]
            gc = gc_col_all[:, h:h + 1]
            gc_row = gc_row_all[h:h + 1, :]
            gc_last = gc_row[:, chunk - 1:chunk]
            decay = jnp.where(causal, jnp.exp(jnp.where(causal, gc - gc_row, 0.0)), 0.0)
            a_mat = jnp.where(strict, beta * _dot_nt_hi(k, k) * decay, 0.0)
            t_inv = _unit_lower_inverse(a_mat, eye)
            rhs = jnp.concatenate([v * beta, k * (beta * jnp.exp(gc))], axis=-1)
            sol = _dot_hi(t_inv, rhs)
            u_base, w = sol[:, :hd], sol[:, hd:]
            qk = _dot_nt_hi(q, k) * decay
            s = s_ref[h]
            u = u_base - _dot_hi(w, s)
            o = _dot_hi(q * jnp.exp(gc), s) + _dot_hi(qk, u)
            s_ref[h] = s * jnp.exp(gc_last) + _dot_tn_hi(k * jnp.exp(gc_last - gc), u)
            zh = z_ref[r0:r0 + chunk, h * hd:(h + 1) * hd]
            o_ref[r0:r0 + chunk, h * hd:(h + 1) * hd] = _rms(o, nw_ref[...]) * _silu(zh)

    xbuf_ref[0:HIST_ROWS, :] = xbuf_ref[tb:tb + HIST_ROWS, :]

    @pl.when(j == pl.num_programs(1) - 1)
    def _():
        sout_ref[0] = s_ref[...]


GROUP = 4


def _gdn_packed_kernel(qkv_ref, z_ref, ba_ref, hist_ref, s0_ref, convw_ref, alog_ref, dtb_ref, nw_ref,
                       o_ref, sout_ref, xbuf_ref, s_ref, *, chunk, heads):
    j = pl.program_id(1)
    tb = qkv_ref.shape[0]
    hd = HEAD_DIM
    c = chunk
    gw = GROUP * c
    qk_w = heads * hd

    @pl.when(j == 0)
    def _():
        xbuf_ref[0:HIST_ROWS, :] = hist_ref[0]
        s_ref[...] = s0_ref[0]

    xbuf_ref[HIST_ROWS:HIST_ROWS + tb, :] = qkv_ref[...]

    def iota(shape, dim):
        return lax.broadcasted_iota(jnp.int32, shape, dim)

    row = iota((c, gw), 0)
    col = iota((c, gw), 1) & (c - 1)
    causal4, strict4 = row >= col, row > col
    eye4 = (row == col).astype(F32)
    bd_mask = (iota((gw, gw), 0) // c) == (iota((gw, gw), 1) // c)
    lower_ones = (iota((c, c), 0) >= iota((c, c), 1)).astype(F32)
    upper_dup = (iota((c, 2 * c), 0) <= (iota((c, 2 * c), 1) & (c - 1))).astype(F32)
    first_half = iota((c, 2 * c), 1) < c
    zeros_head = jnp.zeros((c, hd), F32)

    def conv_silu(r0, c0):
        acc = None
        for tap in range(DN_CONV):
            lo = HIST_ROWS + r0 - (DN_CONV - 1) + tap
            term = xbuf_ref[lo:lo + c, c0:c0 + hd] * convw_ref[tap:tap + 1, c0:c0 + hd]
            acc = term if acc is None else acc + term
        return _silu(acc)

    def block_diag(x4):
        return jnp.where(bd_mask, jnp.concatenate([x4] * GROUP, axis=0), 0.0)

    for r0 in range(0, tb, c):
        ba = ba_ref[r0:r0 + c, :]
        beta_all = jax.nn.sigmoid(ba[:, 0:heads])
        a_in = ba[:, heads:2 * heads] + dtb_ref[...]
        softplus = jnp.maximum(a_in, 0.0) + jnp.log(1.0 + jnp.exp(-jnp.abs(a_in)))
        g_all = -jnp.exp(alog_ref[...]) * softplus
        gc_all = _dot_hi(lower_ones, g_all)
        gc_row_dup = _dot_tn_hi(g_all, upper_dup)

        def pack_cols(col_all, hs):
            halves = [jnp.where(first_half,
                                jnp.broadcast_to(col_all[:, hs[2 * p]:hs[2 * p] + 1], (c, 2 * c)),
                                jnp.broadcast_to(col_all[:, hs[2 * p + 1]:hs[2 * p + 1] + 1], (c, 2 * c)))
                      for p in range(GROUP // 2)]
            return jnp.concatenate(halves, axis=1)

        for g0 in range(0, heads, GROUP):
            hs = list(range(g0, g0 + GROUP))
            qs, ks, vs = [], [], []
            for h in hs:
                q = conv_silu(r0, h * hd)
                k = conv_silu(r0, qk_w + h * hd)
                qs.append(q * lax.rsqrt(jnp.sum(q * q, axis=-1, keepdims=True) + L2_EPS) * (hd ** -0.5))
                ks.append(k * lax.rsqrt(jnp.sum(k * k, axis=-1, keepdims=True) + L2_EPS))
                vs.append(conv_silu(r0, 2 * qk_w + h * hd))
            beta4 = pack_cols(beta_all, hs)
            gc_row4 = jnp.concatenate(
                [jnp.where(first_half[0:1], gc_row_dup[hs[2 * p]:hs[2 * p] + 1], gc_row_dup[hs[2 * p + 1]:hs[2 * p + 1] + 1])
                 for p in range(GROUP // 2)], axis=1)
            diff4 = pack_cols(gc_all, hs) - gc_row4
            decay4 = jnp.where(causal4, jnp.exp(jnp.where(causal4, diff4, 0.0)), 0.0)
            lhs = jnp.concatenate([jnp.concatenate(qs, axis=1), jnp.concatenate(ks, axis=1)], axis=0)
            k_bd = jnp.concatenate(
                [jnp.concatenate([ks[i] if ii == i else zeros_head for ii in range(GROUP)], axis=1)
                 for i in range(GROUP)], axis=0)
            qkkk = _dot3(lhs, k_bd, "nt")
            qk4 = qkkk[:c] * decay4
            a4 = jnp.where(strict4, beta4 * qkkk[c:] * decay4, 0.0)
            t4, p4 = eye4 - a4, _dot_hi(a4, block_diag(a4))
            n_sq = int(math.log2(c)) - 1
            for step in range(n_sq):
                bd = block_diag(p4)
                if step < n_sq - 1:
                    both = _dot_hi(jnp.concatenate([p4, t4], axis=0), bd)
                    p4, t4 = both[:c], t4 + both[c:]
                else:
                    t4 = t4 + _dot_hi(t4, bd)
            rhs = jnp.concatenate(
                [jnp.concatenate([vs[i] * beta_all[:, h:h + 1],
                                  ks[i] * (beta_all[:, h:h + 1] * jnp.exp(gc_all[:, h:h + 1]))], axis=1)
                 for i, h in enumerate(hs)], axis=0)
            sol = _dot3(block_diag(t4), rhs)
            us, o_inter = [], []
            for i, h in enumerate(hs):
                sl = slice(i * c, (i + 1) * c)
                ws = _dot3(jnp.concatenate([sol[sl, hd:], qs[i] * jnp.exp(gc_all[:, h:h + 1])], axis=0), s_ref[h])
                us.append(sol[sl, :hd] - ws[:c])
                o_inter.append(ws[c:])
            o_intra = _dot3(block_diag(qk4), jnp.concatenate(us, axis=0))
            for i, h in enumerate(hs):
                gc = gc_all[:, h:h + 1]
                gc_last = gc_row_dup[h:h + 1, c - 1:c]
                s_ref[h] = s_ref[h] * jnp.exp(gc_last) + _dot3(ks[i] * jnp.exp(gc_last - gc), us[i], "tn")
                o = o_inter[i] + o_intra[i * c:(i + 1) * c]
                zh = z_ref[r0:r0 + c, h * hd:(h + 1) * hd]
                o_ref[r0:r0 + c, h * hd:(h + 1) * hd] = _rms(o, nw_ref[...]) * _silu(zh)

    xbuf_ref[0:HIST_ROWS, :] = xbuf_ref[tb:tb + HIST_ROWS, :]

    @pl.when(j == pl.num_programs(1) - 1)
    def _():
        sout_ref[0] = s_ref[...]


def _gdn(proj_a, cols, hist, s0, conv_w, a_log, dt_bias, norm_w, *, n_seq, seq_len, block, chunk):
    heads = a_log.shape[1]
    conv_width = conv_w.shape[1]
    v_w = heads * HEAD_DIM
    nblk = seq_len // block
    z_blk, ba_blk = cols["z"] // v_w, cols["ba"] // 128
    shared_hist, shared_s0 = hist.shape[0] == 1, s0.shape[0] == 1
    packed = 2 * chunk == HEAD_DIM and heads % GROUP == 0
    return pl.pallas_call(
        functools.partial(_gdn_packed_kernel if packed else _gdn_kernel, chunk=chunk, heads=heads),
        grid=(n_seq, nblk),
        in_specs=[
            pl.BlockSpec((block, conv_width), lambda n, j: (n * nblk + j, 0)),
            pl.BlockSpec((block, v_w), lambda n, j: (n * nblk + j, z_blk)),
            pl.BlockSpec((block, 128), lambda n, j: (n * nblk + j, ba_blk)),
            pl.BlockSpec((1, HIST_ROWS, conv_width), lambda n, j: (0 if shared_hist else n, 0, 0)),
            pl.BlockSpec((1, heads, HEAD_DIM, HEAD_DIM), lambda n, j: (0 if shared_s0 else n, 0, 0, 0)),
            pl.BlockSpec((DN_CONV, conv_width), lambda n, j: (0, 0)),
            pl.BlockSpec((1, heads), lambda n, j: (0, 0)),
            pl.BlockSpec((1, heads), lambda n, j: (0, 0)),
            pl.BlockSpec((1, HEAD_DIM), lambda n, j: (0, 0)),
        ],
        out_specs=[
            pl.BlockSpec((block, v_w), lambda n, j: (n * nblk + j, 0)),
            pl.BlockSpec((1, heads, HEAD_DIM, HEAD_DIM), lambda n, j: (n, 0, 0, 0)),
        ],
        out_shape=[
            jax.ShapeDtypeStruct((n_seq * seq_len, v_w), F32),
            jax.ShapeDtypeStruct((n_seq, heads, HEAD_DIM, HEAD_DIM), F32),
        ],
        scratch_shapes=[
            pltpu.VMEM((block + HIST_ROWS, conv_width), F32),
            pltpu.VMEM((heads, HEAD_DIM, HEAD_DIM), F32),
        ],
        compiler_params=_params("parallel", "arbitrary"),
        name="gdn",
    )(proj_a, proj_a, proj_a, hist, s0, conv_w, a_log, dt_bias, norm_w)


def _gdn_step_kernel(qkv_ref, z_ref, ba_ref, hist_ref, s0_ref, convw_ref, alog_ref, dtb_ref, nw_ref,
                     o_ref, sout_ref, xbuf_ref, *, heads):
    nb, t, _ = qkv_ref.shape
    hd = HEAD_DIM
    qk_w = heads * hd
    assert 2 * t * heads == hd, "q/k rows of all heads fill one square tile for the transpose"

    def per_sequence(b, carry):
        xbuf_ref[0:HIST_ROWS, :] = hist_ref[b]
        xbuf_ref[HIST_ROWS:HIST_ROWS + t, :] = qkv_ref[b]
        acc = None
        for tap in range(DN_CONV):
            lo = HIST_ROWS - (DN_CONV - 1) + tap
            term = xbuf_ref[lo:lo + t, :] * convw_ref[tap:tap + 1, :]
            acc = term if acc is None else acc + term
        x = _silu(acc)
        ba = ba_ref[b]
        beta = jax.nn.sigmoid(ba[:, 0:heads])
        a_in = ba[:, heads:2 * heads] + dtb_ref[...]
        softplus = jnp.maximum(a_in, 0.0) + jnp.log(1.0 + jnp.exp(-jnp.abs(a_in)))
        decay = jnp.exp(-jnp.exp(alog_ref[...]) * softplus)
        rows = []
        for h in range(heads):
            q = x[:, h * hd:(h + 1) * hd]
            k = x[:, qk_w + h * hd:qk_w + (h + 1) * hd]
            rows.append(q * lax.rsqrt(jnp.sum(q * q, axis=-1, keepdims=True) + L2_EPS) * (hd ** -0.5))
            rows.append(k * lax.rsqrt(jnp.sum(k * k, axis=-1, keepdims=True) + L2_EPS))
        qk_cols = jnp.concatenate(rows, axis=0).T
        z = z_ref[b]
        for h in range(heads):
            v = x[:, 2 * qk_w + h * hd:2 * qk_w + (h + 1) * hd]
            s = s0_ref[b, h]
            outs = []
            for tt in range(t):
                q_col = jnp.broadcast_to(qk_cols[:, 2 * t * h + tt:2 * t * h + tt + 1], (hd, hd))
                k_col = jnp.broadcast_to(qk_cols[:, 2 * t * h + t + tt:2 * t * h + t + tt + 1], (hd, hd))
                s = s * decay[tt:tt + 1, h:h + 1]
                u = beta[tt:tt + 1, h:h + 1] * (v[tt:tt + 1, :] - jnp.sum(s * k_col, axis=0, keepdims=True))
                s = s + k_col * u
                outs.append(jnp.sum(s * q_col, axis=0, keepdims=True))
            sout_ref[b, h] = s
            o = jnp.concatenate(outs, axis=0)
            o_ref[b, :, h * hd:(h + 1) * hd] = _rms(o, nw_ref[...]) * _silu(z[:, h * hd:(h + 1) * hd])
        return carry

    lax.fori_loop(0, nb, per_sequence, 0)


def _gdn_step(proj_a3, cols, hist, s0, conv_w, a_log, dt_bias, norm_w, *, seqs_per_step):
    n, t, _ = proj_a3.shape
    heads = a_log.shape[1]
    conv_width = conv_w.shape[1]
    v_w = heads * HEAD_DIM
    nb = seqs_per_step
    z_blk, ba_blk = cols["z"] // v_w, cols["ba"] // 128
    state_spec = pl.BlockSpec((nb, heads, HEAD_DIM, HEAD_DIM), lambda i: (i, 0, 0, 0))
    return pl.pallas_call(
        functools.partial(_gdn_step_kernel, heads=heads),
        grid=(n // nb,),
        in_specs=[
            pl.BlockSpec((nb, t, conv_width), lambda i: (i, 0, 0)),
            pl.BlockSpec((nb, t, v_w), lambda i: (i, 0, z_blk)),
            pl.BlockSpec((nb, t, 128), lambda i: (i, 0, ba_blk)),
            pl.BlockSpec((nb, HIST_ROWS, conv_width), lambda i: (i, 0, 0)),
            state_spec,
            pl.BlockSpec((DN_CONV, conv_width), lambda i: (0, 0)),
            pl.BlockSpec((1, heads), lambda i: (0, 0)),
            pl.BlockSpec((1, heads), lambda i: (0, 0)),
            pl.BlockSpec((1, HEAD_DIM), lambda i: (0, 0)),
        ],
        out_specs=[pl.BlockSpec((nb, t, v_w), lambda i: (i, 0, 0)), state_spec],
        out_shape=[jax.ShapeDtypeStruct((n, t, v_w), F32),
                   jax.ShapeDtypeStruct((n, heads, HEAD_DIM, HEAD_DIM), F32)],
        scratch_shapes=[pltpu.VMEM((HIST_ROWS + t, conv_width), F32)],
        compiler_params=_params("parallel"),
        name="gdn_step",
    )(proj_a3, proj_a3, proj_a3, hist, s0, conv_w, a_log, dt_bias, norm_w)


def _alibi_slope(head, n_heads):
    return 2.0 ** (-8.0 * (head + 1) / n_heads)


def _attend(q4, segments, sink_logits, slopes, n_rows):
    scale = HEAD_DIM ** -0.5
    scores = [_dot_nt(q4, k) * scale for k, _, _, _ in segments]
    outs = []
    for g in range(len(slopes)):
        r = slice(g * n_rows, (g + 1) * n_rows)
        logits = [jnp.where(mask, sc[r] - slopes[g] * jnp.minimum(dist, WINDOW).astype(F32), NEG_BIG)
                  for sc, (_, _, dist, mask) in zip(scores, segments)]
        m = sink_logits[g]
        for lg in logits:
            m = jnp.maximum(m, jnp.max(lg, axis=-1, keepdims=True))
        denom = jnp.exp(sink_logits[g] - m)
        acc = None
        for lg, (_, v, _, _) in zip(logits, segments):
            p = jnp.exp(lg - m)
            denom = denom + jnp.sum(p, axis=-1, keepdims=True)
            pv = _dot(p.astype(BF16), v)
            acc = pv if acc is None else acc + pv
        outs.append(acc / denom)
    return outs


def _swa_prompt_kernel(q_ref, kv_ref, kvprev_ref, kvmeta_ref, sinks_ref, o_ref, *, n_meta, kv_heads):
    j = pl.program_id(1)
    tq = q_ref.shape[0]
    hd = HEAD_DIM
    kv_w = kv_heads * hd
    n_heads = kv_heads * SWA_GROUP
    qi = lax.broadcasted_iota(jnp.int32, (tq, tq), 0)
    ki = lax.broadcasted_iota(jnp.int32, (tq, tq), 1)
    dist_own = qi - ki
    dist_prev = dist_own + tq
    mask_own = dist_own >= 0
    mask_prev = (dist_prev <= WINDOW) & (j > 0)
    qpos = n_meta + j * tq + lax.broadcasted_iota(jnp.int32, (tq, n_meta), 0)
    dist_meta = qpos - lax.broadcasted_iota(jnp.int32, (tq, n_meta), 1)
    mask_meta = dist_meta >= 0
    for kvh in range(kv_heads):
        ks = slice(kvh * hd, (kvh + 1) * hd)
        vs = slice(kv_w + kvh * hd, kv_w + (kvh + 1) * hd)
        segments = [
            (kvmeta_ref[:, ks].astype(BF16), kvmeta_ref[:, vs].astype(BF16), dist_meta, mask_meta),
            (kvprev_ref[:, ks].astype(BF16), kvprev_ref[:, vs].astype(BF16), dist_prev, mask_prev),
            (kv_ref[:, ks].astype(BF16), kv_ref[:, vs].astype(BF16), dist_own, mask_own),
        ]
        heads = [kvh * SWA_GROUP + g for g in range(SWA_GROUP)]
        q4 = jnp.concatenate([q_ref[:, hh * hd:(hh + 1) * hd] for hh in heads], axis=0).astype(BF16)
        sink_logits = [sinks_ref[:, hh:hh + 1] for hh in heads]
        slopes = [_alibi_slope(hh, n_heads) for hh in heads]
        for hh, o in zip(heads, _attend(q4, segments, sink_logits, slopes, tq)):
            o_ref[:, hh * hd:(hh + 1) * hd] = o


def _swa_prompt(proj_b, proj_b_meta, cols, sinks, *, n_seq, seq_len, n_meta):
    n_heads = sinks.shape[1]
    kv_heads = n_heads // SWA_GROUP
    q_w, kv_w2 = n_heads * HEAD_DIM, 2 * kv_heads * HEAD_DIM
    tq = WINDOW
    nblk = seq_len // tq
    kv_blk = cols["skv"] // kv_w2
    return pl.pallas_call(
        functools.partial(_swa_prompt_kernel, n_meta=n_meta, kv_heads=kv_heads),
        grid=(n_seq, nblk),
        in_specs=[
            pl.BlockSpec((tq, q_w), lambda n, j: (n * nblk + j, 0)),
            pl.BlockSpec((tq, kv_w2), lambda n, j: (n * nblk + j, kv_blk)),
            pl.BlockSpec((tq, kv_w2), lambda n, j: (n * nblk + jnp.maximum(j - 1, 0), kv_blk)),
            pl.BlockSpec((n_meta, kv_w2), lambda n, j: (0, kv_blk)),
            pl.BlockSpec((1, n_heads), lambda n, j: (0, 0)),
        ],
        out_specs=pl.BlockSpec((tq, q_w), lambda n, j: (n * nblk + j, 0)),
        out_shape=jax.ShapeDtypeStruct((n_seq * seq_len, q_w), F32),
        compiler_params=_params("parallel", "arbitrary"),
        name="swa_prompt",
    )(proj_b, proj_b, proj_b, proj_b_meta, sinks)


def _swa_step_kernel(q_ref, kvnew_ref, kmeta_ref, vmeta_ref, kbuf_ref, vbuf_ref, sinks_ref,
                     o_ref, kout_ref, vout_ref, *, kv_heads):
    nb, t, _ = q_ref.shape
    n_meta, w = kmeta_ref.shape[1], kbuf_ref.shape[1]
    hd = HEAD_DIM
    kv_w = kv_heads * hd
    n_heads = kv_heads * SWA_GROUP

    def grid2(rows, cols_):
        return (lax.broadcasted_iota(jnp.int32, (rows, cols_), 0), lax.broadcasted_iota(jnp.int32, (rows, cols_), 1))

    ti, mi = grid2(t, n_meta)
    dist_meta = PAST_LEN + ti - mi
    mask_meta = dist_meta >= 0
    ti, bi = grid2(t, w)
    dist_buf = w + ti - bi
    mask_buf = (dist_buf <= WINDOW) & (PAST_LEN - w + bi >= n_meta)
    ti, si = grid2(t, t)
    dist_new = ti - si
    mask_new = dist_new >= 0
    for b in range(nb):
        kv_new = kvnew_ref[b]
        kout_ref[b, 0:w - t, :] = kbuf_ref[b, t:w, :]
        kout_ref[b, w - t:w, :] = kv_new[:, 0:kv_w]
        vout_ref[b, 0:w - t, :] = vbuf_ref[b, t:w, :]
        vout_ref[b, w - t:w, :] = kv_new[:, kv_w:2 * kv_w]
        for kvh in range(kv_heads):
            ks = slice(kvh * hd, (kvh + 1) * hd)
            segments = [
                (kmeta_ref[b, :, ks].astype(BF16), vmeta_ref[b, :, ks].astype(BF16), dist_meta, mask_meta),
                (kbuf_ref[b, :, ks].astype(BF16), vbuf_ref[b, :, ks].astype(BF16), dist_buf, mask_buf),
                (kv_new[:, ks].astype(BF16), kv_new[:, kv_w + kvh * hd:kv_w + (kvh + 1) * hd].astype(BF16),
                 dist_new, mask_new),
            ]
            heads = [kvh * SWA_GROUP + g for g in range(SWA_GROUP)]
            q4 = jnp.concatenate([q_ref[b, :, hh * hd:(hh + 1) * hd] for hh in heads], axis=0).astype(BF16)
            sink_logits = [sinks_ref[:, hh:hh + 1] for hh in heads]
            slopes = [_alibi_slope(hh, n_heads) for hh in heads]
            for hh, o in zip(heads, _attend(q4, segments, sink_logits, slopes, t)):
                o_ref[b, :, hh * hd:(hh + 1) * hd] = o


def _swa_step(proj_b3, cols, k_meta, v_meta, k_buf, v_buf, sinks, *, seqs_per_step):
    n, t, _ = proj_b3.shape
    n_heads = sinks.shape[1]
    kv_heads = n_heads // SWA_GROUP
    q_w, kv_w = n_heads * HEAD_DIM, kv_heads * HEAD_DIM
    n_meta, w = k_meta.shape[1], k_buf.shape[1]
    nb = seqs_per_step
    kv_blk = cols["skv"] // (2 * kv_w)
    seq3 = lambda rows, width, blk=0: pl.BlockSpec((nb, rows, width), lambda i: (i, 0, blk))
    return pl.pallas_call(
        functools.partial(_swa_step_kernel, kv_heads=kv_heads),
        grid=(n // nb,),
        in_specs=[seq3(t, q_w), seq3(t, 2 * kv_w, kv_blk), seq3(n_meta, kv_w), seq3(n_meta, kv_w),
                  seq3(w, kv_w), seq3(w, kv_w), pl.BlockSpec((1, n_heads), lambda i: (0, 0))],
        out_specs=[seq3(t, q_w), seq3(w, kv_w), seq3(w, kv_w)],
        out_shape=[jax.ShapeDtypeStruct((n, t, q_w), F32),
                   jax.ShapeDtypeStruct((n, w, kv_w), F32),
                   jax.ShapeDtypeStruct((n, w, kv_w), F32)],
        compiler_params=_params("parallel"),
        name="swa_step",
    )(proj_b3, proj_b3, k_meta, v_meta, k_buf, v_buf, sinks)


def _split_w_in(w_in, heads, n_heads, kv_heads, d):
    conv_w = 3 * heads * HEAD_DIM
    v_w = heads * HEAD_DIM
    q_w, kv_w = n_heads * HEAD_DIM, kv_heads * HEAD_DIM
    sizes = (conv_w, v_w, heads, heads, q_w, kv_w, kv_w, d, d)
    offs = [0]
    for s in sizes:
        offs.append(offs[-1] + s)
    part = lambda i: w_in[:, offs[i]:offs[i + 1]]
    ba = jnp.concatenate([part(2), part(3), jnp.zeros((w_in.shape[0], 128 - 2 * heads), w_in.dtype)], axis=1)
    w_a = jnp.concatenate([part(0), part(1), part(7), ba], axis=1).astype(BF16)
    w_b = jnp.concatenate([part(4), part(8), part(5), part(6)], axis=1).astype(BF16)
    cols_a = {"qkv": 0, "z": conv_w, "g_dn": conv_w + v_w, "ba": conv_w + v_w + d}
    cols_b = {"q": 0, "g_swa": q_w, "skv": q_w + d}
    return w_a, w_b, cols_a, cols_b


def kernel(x_prompt, x_sample, state_dn_conv, state_dn_ssm, cache_swa_meta_k, cache_swa_meta_v, cache_swa_k, cache_swa_v, meta_tokens, ffn1_norm_pre, ffn1_norm_post, ffn1_w_gate, ffn1_w_up, ffn1_w_down, mix_norm_pre, mix_norm_post, w_in, dn_conv_w, dn_a_log, dn_dt_bias, dn_norm_w, swa_sinks, w_out, ffn2_norm_pre, ffn2_norm_post, ffn2_w_gate, ffn2_w_up, ffn2_w_down):
    assert w_in.shape[0] == 1, "single-layer step"
    n_p, seq, d = x_prompt.shape
    n_s, t_s, _ = x_sample.shape
    n_meta = meta_tokens.shape[0]
    heads = dn_a_log.shape[1]
    n_heads = swa_sinks.shape[1]
    kv_heads = n_heads // SWA_GROUP
    conv_w = dn_conv_w.shape[2]
    kv_w = kv_heads * HEAD_DIM
    w_keep = cache_swa_k.shape[2]
    assert w_keep == WINDOW and seq % WINDOW == 0 and seq % DN_CHUNK == 0 and seq >= WINDOW

    w_a, w_b, cols_a, cols_b = _split_w_in(w_in[0], heads, n_heads, kv_heads, d)
    cols = {**cols_a, **cols_b}
    f1 = (ffn1_norm_pre, ffn1_norm_post, ffn1_w_gate[0].astype(BF16), ffn1_w_up[0].astype(BF16),
          ffn1_w_down[0].astype(BF16))
    f2 = (ffn2_norm_pre, ffn2_norm_post, ffn2_w_gate[0].astype(BF16), ffn2_w_up[0].astype(BF16),
          ffn2_w_down[0].astype(BF16))
    w_o = w_out[0].astype(BF16)
    gdn_w = (dn_conv_w[0], dn_a_log, dn_dt_bias, dn_norm_w)

    def front(h):
        h1 = _ffn(h, *f1)
        return h1, _proj(h1, mix_norm_pre, w_a), _proj(h1, mix_norm_pre, w_b)

    def back(h1, o_dn, o_sw, pa, pb):
        return _mixffn(h1, o_dn, o_sw, pa, pb, cols, w_o, mix_norm_post, *f2)

    def hist_tile(rows3):
        return jnp.pad(rows3, ((0, 0), (HIST_ROWS - (DN_CONV - 1), 0), (0, 0)))

    _, pa_m, pb_m = front(meta_tokens)
    zero_hist = jnp.zeros((1, HIST_ROWS, conv_w), F32)
    zero_state = jnp.zeros((1, heads, HEAD_DIM, HEAD_DIM), F32)
    _, s_meta = _gdn(pa_m, cols, zero_hist, zero_state, *gdn_w, n_seq=1, seq_len=n_meta, block=n_meta,
                     chunk=n_meta)
    hist_meta = hist_tile(pa_m[None, n_meta - (DN_CONV - 1):, :conv_w])

    h1_p, pa_p, pb_p = front(x_prompt.reshape(n_p * seq, d))
    o_dn_p, s_p = _gdn(pa_p, cols, hist_meta, s_meta, *gdn_w, n_seq=n_p, seq_len=seq, block=2 * DN_CHUNK,
                       chunk=DN_CHUNK)
    o_sw_p = _swa_prompt(pb_p, pb_m, cols, swa_sinks, n_seq=n_p, seq_len=seq, n_meta=n_meta)
    y_prompt = back(h1_p, o_dn_p, o_sw_p, pa_p, pb_p).reshape(n_p, seq, d)

    pa_p3 = pa_p.reshape(n_p, seq, -1)
    pb_p3 = pb_p.reshape(n_p, seq, -1)
    p_conv = pa_p3[:, seq - (DN_CONV - 1):, :conv_w]
    kv_meta = pb_m[:, cols["skv"]:]
    p_meta_k = jnp.broadcast_to(kv_meta[None, :, :kv_w], (n_p, n_meta, kv_w))
    p_meta_v = jnp.broadcast_to(kv_meta[None, :, kv_w:], (n_p, n_meta, kv_w))
    p_win_k = pb_p3[:, seq - w_keep:, cols["skv"]:cols["skv"] + kv_w]
    p_win_v = pb_p3[:, seq - w_keep:, cols["skv"] + kv_w:]

    h1_s, pa_s, pb_s = front(x_sample.reshape(n_s * t_s, d))
    o_dn_s, s_s = _gdn_step(pa_s.reshape(n_s, t_s, -1), cols, hist_tile(state_dn_conv[0]), state_dn_ssm[0],
                            *gdn_w, seqs_per_step=4)
    o_dn_s = o_dn_s.reshape(n_s * t_s, -1)
    o_sw_s, s_win_k, s_win_v = _swa_step(
        pb_s.reshape(n_s, t_s, -1), cols,
        cache_swa_meta_k[0].reshape(n_s, n_meta, kv_w), cache_swa_meta_v[0].reshape(n_s, n_meta, kv_w),
        cache_swa_k[0].reshape(n_s, w_keep, kv_w), cache_swa_v[0].reshape(n_s, w_keep, kv_w),
        swa_sinks, seqs_per_step=4)
    y_sample = back(h1_s, o_dn_s, o_sw_s.reshape(n_s * t_s, -1), pa_s, pb_s).reshape(n_s, t_s, d)
    xp_s = jnp.concatenate([state_dn_conv[0], pa_s.reshape(n_s, t_s, -1)[:, :, :conv_w]], axis=1)
    s_conv = xp_s[:, -(DN_CONV - 1):]

    def kv4(x):
        return x.reshape(1, x.shape[0], x.shape[1], kv_heads, HEAD_DIM)

    return (y_prompt, y_sample, p_conv[None], s_p[None], kv4(p_meta_k), kv4(p_meta_v), kv4(p_win_k),
            kv4(p_win_v), s_conv[None], s_s[None], kv4(s_win_k), kv4(s_win_v))
```

```python
import functools
import math

import jax
import jax.numpy as jnp
from jax import lax
from jax.experimental import pallas as pl
from jax.experimental.pallas import tpu as pltpu

F32 = jnp.float32
BF16 = jnp.bfloat16
HIGHEST = lax.Precision.HIGHEST

RMS_EPS = 1e-6
L2_EPS = 1e-6
WINDOW = 128
PAST_LEN = 8192
HEAD_DIM = 128
SWA_GROUP = 4
DN_CONV = 4
DN_CHUNK = 64
HIST_ROWS = 8

VMEM_LIMIT_BYTES = 56 * 1024 * 1024
ROW_TILE = 512
NEG_BIG = -1e30


def _rms(x, g):
    return x * lax.rsqrt(jnp.mean(x * x, axis=-1, keepdims=True) + RMS_EPS) * g


def _silu(x):
    return x * jax.nn.sigmoid(x)


def _dot(a, b):
    return jnp.dot(a, b, preferred_element_type=F32)


def _dot_hi(a, b):
    return jnp.dot(a, b, preferred_element_type=F32, precision=HIGHEST)


def _dot_nt_hi(a, b):
    return lax.dot_general(a, b, (((1,), (1,)), ((), ())), preferred_element_type=F32, precision=HIGHEST)


def _dot_tn_hi(a, b):
    return lax.dot_general(a, b, (((0,), (0,)), ((), ())), preferred_element_type=F32, precision=HIGHEST)


def _dot_nt(a, b):
    return lax.dot_general(a, b, (((1,), (1,)), ((), ())), preferred_element_type=F32)


def _dot_tn(a, b):
    return lax.dot_general(a, b, (((0,), (0,)), ((), ())), preferred_element_type=F32)


def _split_bf16(a):
    hi = a.astype(BF16)
    return hi, (a - hi.astype(F32)).astype(BF16)


_DOTS = {"nn": _dot, "nt": _dot_nt, "tn": _dot_tn}


def _dot3_split(a_pair, b_pair, kind="nn"):
    (a_hi, a_lo), (b_hi, b_lo) = a_pair, b_pair
    dot = _DOTS[kind]
    axis = 1 if kind == "tn" else 0
    m = a_hi.shape[axis]
    top = dot(jnp.concatenate([a_hi, a_lo], axis=axis), b_hi)
    return top[:m] + top[m:] + dot(a_hi, b_lo)


def _mm(a, b, kind="nn", passes=3):
    if passes == 1:
        return _DOTS[kind](a.astype(BF16), b.astype(BF16))
    return _dot3_split(_split_bf16(a), _split_bf16(b), kind)


def _row_tile(rows):
    return ROW_TILE if rows % ROW_TILE == 0 else rows


def _resident(shape):
    return pl.BlockSpec(shape, lambda *_: (0,) * len(shape), pipeline_mode=pl.Buffered(1))


def _params(*semantics):
    return pltpu.CompilerParams(dimension_semantics=semantics, vmem_limit_bytes=VMEM_LIMIT_BYTES)


def _ffn_tile(h, g_pre, g_post, wg_ref, wu_ref, wd_ref, ff_chunk):
    xn = _rms(h, g_pre).astype(BF16)
    acc = None
    for c0 in range(0, wg_ref.shape[1], ff_chunk):
        gate = _dot(xn, wg_ref[:, c0:c0 + ff_chunk])
        up = _dot(xn, wu_ref[:, c0:c0 + ff_chunk])
        part = _dot((_silu(gate) * up).astype(BF16), wd_ref[c0:c0 + ff_chunk, :])
        acc = part if acc is None else acc + part
    return h + 0.5 * _rms(acc, g_post)


def _ff_chunk(d_ff):
    half = d_ff // 2
    return half if d_ff % 2 == 0 and half % 128 == 0 else d_ff


def _ffn_kernel(h_ref, gpre_ref, gpost_ref, wg_ref, wu_ref, wd_ref, o_ref, *, ff_chunk):
    o_ref[...] = _ffn_tile(h_ref[...], gpre_ref[...], gpost_ref[...], wg_ref, wu_ref, wd_ref, ff_chunk)


def _ffn(h, g_pre, g_post, wg, wu, wd):
    rows, d = h.shape
    d_ff = wg.shape[1]
    tm = _row_tile(rows)
    row_spec = pl.BlockSpec((tm, d), lambda i: (i, 0))
    return pl.pallas_call(
        functools.partial(_ffn_kernel, ff_chunk=_ff_chunk(d_ff)),
        grid=(rows // tm,),
        in_specs=[row_spec, _resident((1, d)), _resident((1, d)),
                  _resident((d, d_ff)), _resident((d, d_ff)), _resident((d_ff, d))],
        out_specs=row_spec,
        out_shape=jax.ShapeDtypeStruct((rows, d), F32),
        compiler_params=_params("parallel"),
        name="ffn",
    )(h, g_pre, g_post, wg, wu, wd)


def _proj_kernel(h_ref, g_ref, w_ref, o_ref):
    o_ref[...] = _dot(_rms(h_ref[...], g_ref[...]).astype(BF16), w_ref[...])


def _proj(h, g, w):
    rows, d = h.shape
    n = w.shape[1]
    tm = _row_tile(rows)
    return pl.pallas_call(
        _proj_kernel,
        grid=(rows // tm,),
        in_specs=[pl.BlockSpec((tm, d), lambda i: (i, 0)), _resident((1, d)), _resident((d, n))],
        out_specs=pl.BlockSpec((tm, n), lambda i: (i, 0)),
        out_shape=jax.ShapeDtypeStruct((rows, n), F32),
        compiler_params=_params("parallel"),
        name="proj",
    )(h, g, w)


def _mixffn_kernel(h_ref, odn_ref, osw_ref, gdn_ref, gsw_ref, wout_ref, gmix_ref,
                   gpre_ref, gpost_ref, wg_ref, wu_ref, wd_ref, o_ref, *, ff_chunk):
    y = jax.nn.sigmoid(gdn_ref[...]) * odn_ref[...] + jax.nn.sigmoid(gsw_ref[...]) * osw_ref[...]
    h2 = h_ref[...] + _rms(_dot(y.astype(BF16), wout_ref[...]), gmix_ref[...])
    o_ref[...] = _ffn_tile(h2, gpre_ref[...], gpost_ref[...], wg_ref, wu_ref, wd_ref, ff_chunk)


def _mixffn(h, o_dn, o_sw, proj_a, proj_b, cols, w_out, g_mix, g_pre, g_post, wg, wu, wd):
    rows, d = h.shape
    d_ff = wg.shape[1]
    tm = _row_tile(rows)
    row_spec = pl.BlockSpec((tm, d), lambda i: (i, 0))
    gdn_blk, gsw_blk = cols["g_dn"] // d, cols["g_swa"] // d
    return pl.pallas_call(
        functools.partial(_mixffn_kernel, ff_chunk=_ff_chunk(d_ff)),
        grid=(rows // tm,),
        in_specs=[row_spec, row_spec, row_spec,
                  pl.BlockSpec((tm, d), lambda i: (i, gdn_blk)),
                  pl.BlockSpec((tm, d), lambda i: (i, gsw_blk)),
                  _resident((d, d)), _resident((1, d)), _resident((1, d)), _resident((1, d)),
                  _resident((d, d_ff)), _resident((d, d_ff)), _resident((d_ff, d))],
        out_specs=row_spec,
        out_shape=jax.ShapeDtypeStruct((rows, d), F32),
        compiler_params=_params("parallel"),
        name="mixffn",
    )(h, o_dn, o_sw, proj_a, proj_b, w_out, g_mix, g_pre, g_post, wg, wu, wd)


def _unit_lower_inverse(a, eye):
    c = a.shape[0]
    t = eye - a
    p = a
    for _ in range(int(math.log2(c)) - 1):
        p = _dot_hi(p, p)
        t = t + _dot_hi(t, p)
    return t


def _gdn_kernel(qkv_ref, z_ref, ba_ref, hist_ref, s0_ref, convw_ref, alog_ref, dtb_ref, nw_ref,
                o_ref, sout_ref, xbuf_ref, s_ref, *, chunk, heads):
    j = pl.program_id(1)
    tb = qkv_ref.shape[0]
    hd = HEAD_DIM
    qk_w = heads * hd

    @pl.when(j == 0)
    def _():
        xbuf_ref[0:HIST_ROWS, :] = hist_ref[0]
        s_ref[...] = s0_ref[0]

    xbuf_ref[HIST_ROWS:HIST_ROWS + tb, :] = qkv_ref[...]

    rows = lax.broadcasted_iota(jnp.int32, (chunk, chunk), 0)
    cols = lax.broadcasted_iota(jnp.int32, (chunk, chunk), 1)
    causal = rows >= cols
    strict = rows > cols
    eye = (rows == cols).astype(F32)
    lower_ones = causal.astype(F32)

    def conv_silu(r0, c0):
        acc = None
        for tap in range(DN_CONV):
            lo = HIST_ROWS + r0 - (DN_CONV - 1) + tap
            term = xbuf_ref[lo:lo + chunk, c0:c0 + hd] * convw_ref[tap:tap + 1, c0:c0 + hd]
            acc = term if acc is None else acc + term
        return _silu(acc)

    for r0 in range(0, tb, chunk):
        ba = ba_ref[r0:r0 + chunk, :]
        beta_all = jax.nn.sigmoid(ba[:, 0:heads])
        a_in = ba[:, heads:2 * heads] + dtb_ref[...]
        softplus = jnp.maximum(a_in, 0.0) + jnp.log(1.0 + jnp.exp(-jnp.abs(a_in)))
        g_all = -jnp.exp(alog_ref[...]) * softplus
        gc_col_all = _dot_hi(lower_ones, g_all)
        gc_row_all = _dot_tn_hi(g_all, (rows <= cols).astype(F32))
        for h in range(heads):
            q = conv_silu(r0, h * hd)
            k = conv_silu(r0, qk_w + h * hd)
            v = conv_silu(r0, 2 * qk_w + h * hd)
            q = q * lax.rsqrt(jnp.sum(q * q, axis=-1, keepdims=True) + L2_EPS) * (hd ** -0.5)
            k = k * lax.rsqrt(jnp.sum(k * k, axis=-1, keepdims=True) + L2_EPS)
            beta = beta_all[:, h:h + 1]
            gc = gc_col_all[:, h:h + 1]
            gc_row = gc_row_all[h:h + 1, :]
            gc_last = gc_row[:, chunk - 1:chunk]
            decay = jnp.where(causal, jnp.exp(jnp.where(causal, gc - gc_row, 0.0)), 0.0)
            a_mat = jnp.where(strict, beta * _dot_nt_hi(k, k) * decay, 0.0)
            t_inv = _unit_lower_inverse(a_mat, eye)
            rhs = jnp.concatenate([v * beta, k * (beta * jnp.exp(gc))], axis=-1)
            sol = _dot_hi(t_inv, rhs)
            u_base, w = sol[:, :hd], sol[:, hd:]
            qk = _dot_nt_hi(q, k) * decay
            s = s_ref[h]
            u = u_base - _dot_hi(w, s)
            o = _dot_hi(q * jnp.exp(gc), s) + _dot_hi(qk, u)
            s_ref[h] = s * jnp.exp(gc_last) + _dot_tn_hi(k * jnp.exp(gc_last - gc), u)
            zh = z_ref[r0:r0 + chunk, h * hd:(h + 1) * hd]
            o_ref[r0:r0 + chunk, h * hd:(h + 1) * hd] = _rms(o, nw_ref[...]) * _silu(zh)

    xbuf_ref[0:HIST_ROWS, :] = xbuf_ref[tb:tb + HIST_ROWS, :]

    @pl.when(j == pl.num_programs(1) - 1)
    def _():
        sout_ref[0] = s_ref[...]


GROUP = 4
PASSES_QK = 1
PASSES_STATE = 1
PASSES_INV = 3


def _gdn_packed_kernel(qkv_ref, z_ref, ba_ref, hist_ref, s0_ref, convw_ref, alog_ref, dtb_ref, nw_ref,
                       o_ref, sout_ref, xbuf_ref, s_ref, *, chunk, heads):
    j = pl.program_id(1)
    tb = qkv_ref.shape[0]
    hd = HEAD_DIM
    c = chunk
    gw = GROUP * c
    qk_w = heads * hd

    @pl.when(j == 0)
    def _():
        xbuf_ref[0:HIST_ROWS, :] = hist_ref[0]
        s_ref[...] = s0_ref[0]

    xbuf_ref[HIST_ROWS:HIST_ROWS + tb, :] = qkv_ref[...]

    def iota(shape, dim):
        return lax.broadcasted_iota(jnp.int32, shape, dim)

    row = iota((c, gw), 0)
    col = iota((c, gw), 1) & (c - 1)
    causal4, strict4 = row >= col, row > col
    eye4 = (row == col).astype(F32)
    bd_mask = (iota((gw, gw), 0) // c) == (iota((gw, gw), 1) // c)
    lower_ones = (iota((c, c), 0) >= iota((c, c), 1)).astype(F32)
    upper_dup = (iota((c, 2 * c), 0) <= (iota((c, 2 * c), 1) & (c - 1))).astype(F32)
    first_half = iota((c, 2 * c), 1) < c
    zeros_head = jnp.zeros((c, hd), F32)

    def conv_silu(r0, c0):
        acc = None
        for tap in range(DN_CONV):
            lo = HIST_ROWS + r0 - (DN_CONV - 1) + tap
            term = xbuf_ref[lo:lo + c, c0:c0 + hd] * convw_ref[tap:tap + 1, c0:c0 + hd]
            acc = term if acc is None else acc + term
        return _silu(acc)

    def block_diag(x4):
        return jnp.where(bd_mask, jnp.concatenate([x4] * GROUP, axis=0), jnp.zeros((), x4.dtype))

    def times_block_diag(lhs, x4, passes):
        if passes == 1:
            return _dot(lhs.astype(BF16), block_diag(x4.astype(BF16)))
        hi, lo = _split_bf16(x4)
        return _dot3_split(_split_bf16(lhs), (block_diag(hi), block_diag(lo)))

    def block_diag_times(x4, rhs, passes):
        if passes == 1:
            return _dot(block_diag(x4.astype(BF16)), rhs.astype(BF16))
        hi, lo = _split_bf16(x4)
        return _dot3_split((block_diag(hi), block_diag(lo)), _split_bf16(rhs))

    def pack_cols(col_all, hs):
        halves = [jnp.where(first_half,
                            jnp.broadcast_to(col_all[:, hs[2 * p]:hs[2 * p] + 1], (c, 2 * c)),
                            jnp.broadcast_to(col_all[:, hs[2 * p + 1]:hs[2 * p + 1] + 1], (c, 2 * c)))
                  for p in range(GROUP // 2)]
        return jnp.concatenate(halves, axis=1)

    chunks = []
    for r0 in range(0, tb, c):
        ba = ba_ref[r0:r0 + c, :]
        beta_all = jax.nn.sigmoid(ba[:, 0:heads])
        a_in = ba[:, heads:2 * heads] + dtb_ref[...]
        softplus = jnp.maximum(a_in, 0.0) + jnp.log(1.0 + jnp.exp(-jnp.abs(a_in)))
        g_all = -jnp.exp(alog_ref[...]) * softplus
        gc_all = _dot_hi(lower_ones, g_all)
        gc_row_dup = _dot_tn_hi(g_all, upper_dup)
        chunks.append(dict(r0=r0, beta=beta_all, gc=gc_all, gc_row=gc_row_dup))

    items = []
    for ch in chunks:
        for g0 in range(0, heads, GROUP):
            hs = list(range(g0, g0 + GROUP))
            qs, ks, vs = [], [], []
            for h in hs:
                q = conv_silu(ch["r0"], h * hd)
                k = conv_silu(ch["r0"], qk_w + h * hd)
                qs.append(q * lax.rsqrt(jnp.sum(q * q, axis=-1, keepdims=True) + L2_EPS) * (hd ** -0.5))
                ks.append(k * lax.rsqrt(jnp.sum(k * k, axis=-1, keepdims=True) + L2_EPS))
                vs.append(conv_silu(ch["r0"], 2 * qk_w + h * hd))
            items.append(dict(ch=ch, hs=hs, qs=qs, ks=ks, vs=vs))

    for it in items:
        ch, hs, qs, ks = it["ch"], it["hs"], it["qs"], it["ks"]
        beta4 = pack_cols(ch["beta"], hs)
        gc_row4 = jnp.concatenate(
            [jnp.where(first_half[0:1], ch["gc_row"][hs[2 * p]:hs[2 * p] + 1],
                       ch["gc_row"][hs[2 * p + 1]:hs[2 * p + 1] + 1])
             for p in range(GROUP // 2)], axis=1)
        diff4 = pack_cols(ch["gc"], hs) - gc_row4
        decay4 = jnp.where(causal4, jnp.exp(jnp.where(causal4, diff4, 0.0)), 0.0)
        lhs = jnp.concatenate([jnp.concatenate(qs, axis=1), jnp.concatenate(ks, axis=1)], axis=0)
        k_bd = jnp.concatenate(
            [jnp.concatenate([ks[i] if ii == i else zeros_head for ii in range(GROUP)], axis=1)
             for i in range(GROUP)], axis=0)
        qkkk = _mm(lhs, k_bd, "nt", PASSES_QK)
        it["qk4"] = qkkk[:c] * decay4
        a4 = jnp.where(strict4, beta4 * qkkk[c:] * decay4, 0.0)
        it["t4"] = eye4 - a4
        it["a4"] = a4

    for it in items:
        it["p4"] = times_block_diag(it["a4"], it["a4"], PASSES_INV)
    n_sq = int(math.log2(c)) - 1
    for step in range(n_sq):
        for it in items:
            if step < n_sq - 1:
                both = times_block_diag(jnp.concatenate([it["p4"], it["t4"]], axis=0), it["p4"], PASSES_INV)
                it["p4"], it["t4"] = both[:c], it["t4"] + both[c:]
            else:
                it["t4"] = it["t4"] + times_block_diag(it["t4"], it["p4"], PASSES_INV)

    for it in items:
        ch, hs, ks, vs = it["ch"], it["hs"], it["ks"], it["vs"]
        rhs = jnp.concatenate(
            [jnp.concatenate([vs[i] * ch["beta"][:, h:h + 1],
                              ks[i] * (ch["beta"][:, h:h + 1] * jnp.exp(ch["gc"][:, h:h + 1]))], axis=1)
             for i, h in enumerate(hs)], axis=0)
        it["sol"] = block_diag_times(it["t4"], rhs, PASSES_INV)

    for ch in chunks:
        group_items = [it for it in items if it["ch"] is ch]
        for it in group_items:
            us, o_inter = [], []
            for i, h in enumerate(it["hs"]):
                sl = slice(i * c, (i + 1) * c)
                ws = _mm(jnp.concatenate([it["sol"][sl, hd:], it["qs"][i] * jnp.exp(ch["gc"][:, h:h + 1])], axis=0),
                         s_ref[h], "nn", PASSES_STATE)
                us.append(it["sol"][sl, :hd] - ws[:c])
                o_inter.append(ws[c:])
            it["us"], it["o_inter"] = us, o_inter
        for it in group_items:
            it["o_intra"] = block_diag_times(it["qk4"], jnp.concatenate(it["us"], axis=0), PASSES_STATE)
        for it in group_items:
            for i, h in enumerate(it["hs"]):
                gc = ch["gc"][:, h:h + 1]
                gc_last = ch["gc_row"][h:h + 1, c - 1:c]
                s_ref[h] = s_ref[h] * jnp.exp(gc_last) + _mm(it["ks"][i] * jnp.exp(gc_last - gc), it["us"][i],
                                                             "tn", PASSES_STATE)
        for it in group_items:
            r0 = ch["r0"]
            for i, h in enumerate(it["hs"]):
                o = it["o_inter"][i] + it["o_intra"][i * c:(i + 1) * c]
                zh = z_ref[r0:r0 + c, h * hd:(h + 1) * hd]
                o_ref[r0:r0 + c, h * hd:(h + 1) * hd] = _rms(o, nw_ref[...]) * _silu(zh)

    xbuf_ref[0:HIST_ROWS, :] = xbuf_ref[tb:tb + HIST_ROWS, :]

    @pl.when(j == pl.num_programs(1) - 1)
    def _():
        sout_ref[0] = s_ref[...]


def _gdn(proj_a, cols, hist, s0, conv_w, a_log, dt_bias, norm_w, *, n_seq, seq_len, block, chunk):
    heads = a_log.shape[1]
    conv_width = conv_w.shape[1]
    v_w = heads * HEAD_DIM
    nblk = seq_len // block
    z_blk, ba_blk = cols["z"] // v_w, cols["ba"] // 128
    shared_hist, shared_s0 = hist.shape[0] == 1, s0.shape[0] == 1
    packed = 2 * chunk == HEAD_DIM and heads % GROUP == 0
    return pl.pallas_call(
        functools.partial(_gdn_packed_kernel if packed else _gdn_kernel, chunk=chunk, heads=heads),
        grid=(n_seq, nblk),
        in_specs=[
            pl.BlockSpec((block, conv_width), lambda n, j: (n * nblk + j, 0)),
            pl.BlockSpec((block, v_w), lambda n, j: (n * nblk + j, z_blk)),
            pl.BlockSpec((block, 128), lambda n, j: (n * nblk + j, ba_blk)),
            pl.BlockSpec((1, HIST_ROWS, conv_width), lambda n, j: (0 if shared_hist else n, 0, 0)),
            pl.BlockSpec((1, heads, HEAD_DIM, HEAD_DIM), lambda n, j: (0 if shared_s0 else n, 0, 0, 0)),
            pl.BlockSpec((DN_CONV, conv_width), lambda n, j: (0, 0)),
            pl.BlockSpec((1, heads), lambda n, j: (0, 0)),
            pl.BlockSpec((1, heads), lambda n, j: (0, 0)),
            pl.BlockSpec((1, HEAD_DIM), lambda n, j: (0, 0)),
        ],
        out_specs=[
            pl.BlockSpec((block, v_w), lambda n, j: (n * nblk + j, 0)),
            pl.BlockSpec((1, heads, HEAD_DIM, HEAD_DIM), lambda n, j: (n, 0, 0, 0)),
        ],
        out_shape=[
            jax.ShapeDtypeStruct((n_seq * seq_len, v_w), F32),
            jax.ShapeDtypeStruct((n_seq, heads, HEAD_DIM, HEAD_DIM), F32),
        ],
        scratch_shapes=[
            pltpu.VMEM((block + HIST_ROWS, conv_width), F32),
            pltpu.VMEM((heads, HEAD_DIM, HEAD_DIM), F32),
        ],
        compiler_params=_params("parallel", "arbitrary"),
        name="gdn",
    )(proj_a, proj_a, proj_a, hist, s0, conv_w, a_log, dt_bias, norm_w)


def _gdn_step_kernel(qkv_ref, z_ref, ba_ref, hist_ref, s0_ref, convw_ref, alog_ref, dtb_ref, nw_ref,
                     o_ref, sout_ref, xbuf_ref, *, heads):
    nb, t, _ = qkv_ref.shape
    hd = HEAD_DIM
    qk_w = heads * hd
    assert 2 * t * heads == hd, "q/k rows of all heads fill one square tile for the transpose"

    def per_sequence(b, carry):
        xbuf_ref[0:HIST_ROWS, :] = hist_ref[b]
        xbuf_ref[HIST_ROWS:HIST_ROWS + t, :] = qkv_ref[b]
        acc = None
        for tap in range(DN_CONV):
            lo = HIST_ROWS - (DN_CONV - 1) + tap
            term = xbuf_ref[lo:lo + t, :] * convw_ref[tap:tap + 1, :]
            acc = term if acc is None else acc + term
        x = _silu(acc)
        ba = ba_ref[b]
        beta = jax.nn.sigmoid(ba[:, 0:heads])
        a_in = ba[:, heads:2 * heads] + dtb_ref[...]
        softplus = jnp.maximum(a_in, 0.0) + jnp.log(1.0 + jnp.exp(-jnp.abs(a_in)))
        decay = jnp.exp(-jnp.exp(alog_ref[...]) * softplus)
        rows = []
        for h in range(heads):
            q = x[:, h * hd:(h + 1) * hd]
            k = x[:, qk_w + h * hd:qk_w + (h + 1) * hd]
            rows.append(q * lax.rsqrt(jnp.sum(q * q, axis=-1, keepdims=True) + L2_EPS) * (hd ** -0.5))
            rows.append(k * lax.rsqrt(jnp.sum(k * k, axis=-1, keepdims=True) + L2_EPS))
        qk_cols = jnp.concatenate(rows, axis=0).T
        z = z_ref[b]
        for h in range(heads):
            v = x[:, 2 * qk_w + h * hd:2 * qk_w + (h + 1) * hd]
            s = s0_ref[b, h]
            outs = []
            for tt in range(t):
                q_col = jnp.broadcast_to(qk_cols[:, 2 * t * h + tt:2 * t * h + tt + 1], (hd, hd))
                k_col = jnp.broadcast_to(qk_cols[:, 2 * t * h + t + tt:2 * t * h + t + tt + 1], (hd, hd))
                s = s * decay[tt:tt + 1, h:h + 1]
                u = beta[tt:tt + 1, h:h + 1] * (v[tt:tt + 1, :] - jnp.sum(s * k_col, axis=0, keepdims=True))
                s = s + k_col * u
                outs.append(jnp.sum(s * q_col, axis=0, keepdims=True))
            sout_ref[b, h] = s
            o = jnp.concatenate(outs, axis=0)
            o_ref[b, :, h * hd:(h + 1) * hd] = _rms(o, nw_ref[...]) * _silu(z[:, h * hd:(h + 1) * hd])
        return carry

    lax.fori_loop(0, nb, per_sequence, 0)


def _gdn_step(proj_a3, cols, hist, s0, conv_w, a_log, dt_bias, norm_w, *, seqs_per_step):
    n, t, _ = proj_a3.shape
    heads = a_log.shape[1]
    conv_width = conv_w.shape[1]
    v_w = heads * HEAD_DIM
    nb = seqs_per_step
    z_blk, ba_blk = cols["z"] // v_w, cols["ba"] // 128
    state_spec = pl.BlockSpec((nb, heads, HEAD_DIM, HEAD_DIM), lambda i: (i, 0, 0, 0))
    return pl.pallas_call(
        functools.partial(_gdn_step_kernel, heads=heads),
        grid=(n // nb,),
        in_specs=[
            pl.BlockSpec((nb, t, conv_width), lambda i: (i, 0, 0)),
            pl.BlockSpec((nb, t, v_w), lambda i: (i, 0, z_blk)),
            pl.BlockSpec((nb, t, 128), lambda i: (i, 0, ba_blk)),
            pl.BlockSpec((nb, HIST_ROWS, conv_width), lambda i: (i, 0, 0)),
            state_spec,
            pl.BlockSpec((DN_CONV, conv_width), lambda i: (0, 0)),
            pl.BlockSpec((1, heads), lambda i: (0, 0)),
            pl.BlockSpec((1, heads), lambda i: (0, 0)),
            pl.BlockSpec((1, HEAD_DIM), lambda i: (0, 0)),
        ],
        out_specs=[pl.BlockSpec((nb, t, v_w), lambda i: (i, 0, 0)), state_spec],
        out_shape=[jax.ShapeDtypeStruct((n, t, v_w), F32),
                   jax.ShapeDtypeStruct((n, heads, HEAD_DIM, HEAD_DIM), F32)],
        scratch_shapes=[pltpu.VMEM((HIST_ROWS + t, conv_width), F32)],
        compiler_params=_params("parallel"),
        name="gdn_step",
    )(proj_a3, proj_a3, proj_a3, hist, s0, conv_w, a_log, dt_bias, norm_w)


def _alibi_slope(head, n_heads):
    return 2.0 ** (-8.0 * (head + 1) / n_heads)


def _attend(q4, segments, sink_logits, slopes, n_rows):
    scale = HEAD_DIM ** -0.5
    scores = [_dot_nt(q4, k) * scale for k, _, _, _ in segments]
    outs = []
    for g in range(len(slopes)):
        r = slice(g * n_rows, (g + 1) * n_rows)
        logits = [jnp.where(mask, sc[r] - slopes[g] * jnp.minimum(dist, WINDOW).astype(F32), NEG_BIG)
                  for sc, (_, _, dist, mask) in zip(scores, segments)]
        m = sink_logits[g]
        for lg in logits:
            m = jnp.maximum(m, jnp.max(lg, axis=-1, keepdims=True))
        denom = jnp.exp(sink_logits[g] - m)
        acc = None
        for lg, (_, v, _, _) in zip(logits, segments):
            p = jnp.exp(lg - m)
            denom = denom + jnp.sum(p, axis=-1, keepdims=True)
            pv = _dot(p.astype(BF16), v)
            acc = pv if acc is None else acc + pv
        outs.append(acc / denom)
    return outs


def _swa_prompt_kernel(q_ref, kv_ref, kvprev_ref, kvmeta_ref, sinks_ref, o_ref, *, n_meta, kv_heads):
    j = pl.program_id(1)
    tq = q_ref.shape[0]
    hd = HEAD_DIM
    kv_w = kv_heads * hd
    n_heads = kv_heads * SWA_GROUP
    qi = lax.broadcasted_iota(jnp.int32, (tq, tq), 0)
    ki = lax.broadcasted_iota(jnp.int32, (tq, tq), 1)
    dist_own = qi - ki
    dist_prev = dist_own + tq
    mask_own = dist_own >= 0
    mask_prev = (dist_prev <= WINDOW) & (j > 0)
    qpos = n_meta + j * tq + lax.broadcasted_iota(jnp.int32, (tq, n_meta), 0)
    dist_meta = qpos - lax.broadcasted_iota(jnp.int32, (tq, n_meta), 1)
    mask_meta = dist_meta >= 0
    for kvh in range(kv_heads):
        ks = slice(kvh * hd, (kvh + 1) * hd)
        vs = slice(kv_w + kvh * hd, kv_w + (kvh + 1) * hd)
        segments = [
            (kvmeta_ref[:, ks].astype(BF16), kvmeta_ref[:, vs].astype(BF16), dist_meta, mask_meta),
            (kvprev_ref[:, ks].astype(BF16), kvprev_ref[:, vs].astype(BF16), dist_prev, mask_prev),
            (kv_ref[:, ks].astype(BF16), kv_ref[:, vs].astype(BF16), dist_own, mask_own),
        ]
        heads = [kvh * SWA_GROUP + g for g in range(SWA_GROUP)]
        q4 = jnp.concatenate([q_ref[:, hh * hd:(hh + 1) * hd] for hh in heads], axis=0).astype(BF16)
        sink_logits = [sinks_ref[:, hh:hh + 1] for hh in heads]
        slopes = [_alibi_slope(hh, n_heads) for hh in heads]
        for hh, o in zip(heads, _attend(q4, segments, sink_logits, slopes, tq)):
            o_ref[:, hh * hd:(hh + 1) * hd] = o


def _swa_prompt(proj_b, proj_b_meta, cols, sinks, *, n_seq, seq_len, n_meta):
    n_heads = sinks.shape[1]
    kv_heads = n_heads // SWA_GROUP
    q_w, kv_w2 = n_heads * HEAD_DIM, 2 * kv_heads * HEAD_DIM
    tq = WINDOW
    nblk = seq_len // tq
    kv_blk = cols["skv"] // kv_w2
    return pl.pallas_call(
        functools.partial(_swa_prompt_kernel, n_meta=n_meta, kv_heads=kv_heads),
        grid=(n_seq, nblk),
        in_specs=[
            pl.BlockSpec((tq, q_w), lambda n, j: (n * nblk + j, 0)),
            pl.BlockSpec((tq, kv_w2), lambda n, j: (n * nblk + j, kv_blk)),
            pl.BlockSpec((tq, kv_w2), lambda n, j: (n * nblk + jnp.maximum(j - 1, 0), kv_blk)),
            pl.BlockSpec((n_meta, kv_w2), lambda n, j: (0, kv_blk)),
            pl.BlockSpec((1, n_heads), lambda n, j: (0, 0)),
        ],
        out_specs=pl.BlockSpec((tq, q_w), lambda n, j: (n * nblk + j, 0)),
        out_shape=jax.ShapeDtypeStruct((n_seq * seq_len, q_w), F32),
        compiler_params=_params("parallel", "arbitrary"),
        name="swa_prompt",
    )(proj_b, proj_b, proj_b, proj_b_meta, sinks)


def _swa_step_kernel(q_ref, kvnew_ref, kmeta_ref, vmeta_ref, kbuf_ref, vbuf_ref, sinks_ref,
                     o_ref, kout_ref, vout_ref, *, kv_heads):
    nb, t, _ = q_ref.shape
    n_meta, w = kmeta_ref.shape[1], kbuf_ref.shape[1]
    hd = HEAD_DIM
    kv_w = kv_heads * hd
    n_heads = kv_heads * SWA_GROUP

    def grid2(rows, cols_):
        return (lax.broadcasted_iota(jnp.int32, (rows, cols_), 0), lax.broadcasted_iota(jnp.int32, (rows, cols_), 1))

    ti, mi = grid2(t, n_meta)
    dist_meta = PAST_LEN + ti - mi
    mask_meta = dist_meta >= 0
    ti, bi = grid2(t, w)
    dist_buf = w + ti - bi
    mask_buf = (dist_buf <= WINDOW) & (PAST_LEN - w + bi >= n_meta)
    ti, si = grid2(t, t)
    dist_new = ti - si
    mask_new = dist_new >= 0
    for b in range(nb):
        kv_new = kvnew_ref[b]
        kout_ref[b, 0:w - t, :] = kbuf_ref[b, t:w, :]
        kout_ref[b, w - t:w, :] = kv_new[:, 0:kv_w]
        vout_ref[b, 0:w - t, :] = vbuf_ref[b, t:w, :]
        vout_ref[b, w - t:w, :] = kv_new[:, kv_w:2 * kv_w]
        for kvh in range(kv_heads):
            ks = slice(kvh * hd, (kvh + 1) * hd)
            segments = [
                (kmeta_ref[b, :, ks].astype(BF16), vmeta_ref[b, :, ks].astype(BF16), dist_meta, mask_meta),
                (kbuf_ref[b, :, ks].astype(BF16), vbuf_ref[b, :, ks].astype(BF16), dist_buf, mask_buf),
                (kv_new[:, ks].astype(BF16), kv_new[:, kv_w + kvh * hd:kv_w + (kvh + 1) * hd].astype(BF16),
                 dist_new, mask_new),
            ]
            heads = [kvh * SWA_GROUP + g for g in range(SWA_GROUP)]
            q4 = jnp.concatenate([q_ref[b, :, hh * hd:(hh + 1) * hd] for hh in heads], axis=0).astype(BF16)
            sink_logits = [sinks_ref[:, hh:hh + 1] for hh in heads]
            slopes = [_alibi_slope(hh, n_heads) for hh in heads]
            for hh, o in zip(heads, _attend(q4, segments, sink_logits, slopes, t)):
                o_ref[b, :, hh * hd:(hh + 1) * hd] = o


def _swa_step(proj_b3, cols, k_meta, v_meta, k_buf, v_buf, sinks, *, seqs_per_step):
    n, t, _ = proj_b3.shape
    n_heads = sinks.shape[1]
    kv_heads = n_heads // SWA_GROUP
    q_w, kv_w = n_heads * HEAD_DIM, kv_heads * HEAD_DIM
    n_meta, w = k_meta.shape[1], k_buf.shape[1]
    nb = seqs_per_step
    kv_blk = cols["skv"] // (2 * kv_w)
    seq3 = lambda rows, width, blk=0: pl.BlockSpec((nb, rows, width), lambda i: (i, 0, blk))
    return pl.pallas_call(
        functools.partial(_swa_step_kernel, kv_heads=kv_heads),
        grid=(n // nb,),
        in_specs=[seq3(t, q_w), seq3(t, 2 * kv_w, kv_blk), seq3(n_meta, kv_w), seq3(n_meta, kv_w),
                  seq3(w, kv_w), seq3(w, kv_w), pl.BlockSpec((1, n_heads), lambda i: (0, 0))],
        out_specs=[seq3(t, q_w), seq3(w, kv_w), seq3(w, kv_w)],
        out_shape=[jax.ShapeDtypeStruct((n, t, q_w), F32),
                   jax.ShapeDtypeStruct((n, w, kv_w), F32),
                   jax.ShapeDtypeStruct((n, w, kv_w), F32)],
        compiler_params=_params("parallel"),
        name="swa_step",
    )(proj_b3, proj_b3, k_meta, v_meta, k_buf, v_buf, sinks)


def _split_w_in(w_in, heads, n_heads, kv_heads, d):
    conv_w = 3 * heads * HEAD_DIM
    v_w = heads * HEAD_DIM
    q_w, kv_w = n_heads * HEAD_DIM, kv_heads * HEAD_DIM
    sizes = (conv_w, v_w, heads, heads, q_w, kv_w, kv_w, d, d)
    offs = [0]
    for s in sizes:
        offs.append(offs[-1] + s)
    part = lambda i: w_in[:, offs[i]:offs[i + 1]]
    ba = jnp.concatenate([part(2), part(3), jnp.zeros((w_in.shape[0], 128 - 2 * heads), w_in.dtype)], axis=1)
    w_a = jnp.concatenate([part(0), part(1), part(7), ba], axis=1).astype(BF16)
    w_b = jnp.concatenate([part(4), part(8), part(5), part(6)], axis=1).astype(BF16)
    cols_a = {"qkv": 0, "z": conv_w, "g_dn": conv_w + v_w, "ba": conv_w + v_w + d}
    cols_b = {"q": 0, "g_swa": q_w, "skv": q_w + d}
    return w_a, w_b, cols_a, cols_b


def kernel(x_prompt, x_sample, state_dn_conv, state_dn_ssm, cache_swa_meta_k, cache_swa_meta_v, cache_swa_k, cache_swa_v, meta_tokens, ffn1_norm_pre, ffn1_norm_post, ffn1_w_gate, ffn1_w_up, ffn1_w_down, mix_norm_pre, mix_norm_post, w_in, dn_conv_w, dn_a_log, dn_dt_bias, dn_norm_w, swa_sinks, w_out, ffn2_norm_pre, ffn2_norm_post, ffn2_w_gate, ffn2_w_up, ffn2_w_down):
    assert w_in.shape[0] == 1, "single-layer step"
    n_p, seq, d = x_prompt.shape
    n_s, t_s, _ = x_sample.shape
    n_meta = meta_tokens.shape[0]
    heads = dn_a_log.shape[1]
    n_heads = swa_sinks.shape[1]
    kv_heads = n_heads // SWA_GROUP
    conv_w = dn_conv_w.shape[2]
    kv_w = kv_heads * HEAD_DIM
    w_keep = cache_swa_k.shape[2]
    assert w_keep == WINDOW and seq % WINDOW == 0 and seq % DN_CHUNK == 0 and seq >= WINDOW

    w_a, w_b, cols_a, cols_b = _split_w_in(w_in[0], heads, n_heads, kv_heads, d)
    cols = {**cols_a, **cols_b}
    f1 = (ffn1_norm_pre, ffn1_norm_post, ffn1_w_gate[0].astype(BF16), ffn1_w_up[0].astype(BF16),
          ffn1_w_down[0].astype(BF16))
    f2 = (ffn2_norm_pre, ffn2_norm_post, ffn2_w_gate[0].astype(BF16), ffn2_w_up[0].astype(BF16),
          ffn2_w_down[0].astype(BF16))
    w_o = w_out[0].astype(BF16)
    gdn_w = (dn_conv_w[0], dn_a_log, dn_dt_bias, dn_norm_w)

    def front(h):
        h1 = _ffn(h, *f1)
        return h1, _proj(h1, mix_norm_pre, w_a), _proj(h1, mix_norm_pre, w_b)

    def back(h1, o_dn, o_sw, pa, pb):
        return _mixffn(h1, o_dn, o_sw, pa, pb, cols, w_o, mix_norm_post, *f2)

    def hist_tile(rows3):
        return jnp.pad(rows3, ((0, 0), (HIST_ROWS - (DN_CONV - 1), 0), (0, 0)))

    _, pa_m, pb_m = front(meta_tokens)
    zero_hist = jnp.zeros((1, HIST_ROWS, conv_w), F32)
    zero_state = jnp.zeros((1, heads, HEAD_DIM, HEAD_DIM), F32)
    _, s_meta = _gdn(pa_m, cols, zero_hist, zero_state, *gdn_w, n_seq=1, seq_len=n_meta, block=n_meta,
                     chunk=n_meta)
    hist_meta = hist_tile(pa_m[None, n_meta - (DN_CONV - 1):, :conv_w])

    h1_p, pa_p, pb_p = front(x_prompt.reshape(n_p * seq, d))
    o_dn_p, s_p = _gdn(pa_p, cols, hist_meta, s_meta, *gdn_w, n_seq=n_p, seq_len=seq, block=4 * DN_CHUNK,
                       chunk=DN_CHUNK)
    o_sw_p = _swa_prompt(pb_p, pb_m, cols, swa_sinks, n_seq=n_p, seq_len=seq, n_meta=n_meta)
    y_prompt = back(h1_p, o_dn_p, o_sw_p, pa_p, pb_p).reshape(n_p, seq, d)

    pa_p3 = pa_p.reshape(n_p, seq, -1)
    pb_p3 = pb_p.reshape(n_p, seq, -1)
    p_conv = pa_p3[:, seq - (DN_CONV - 1):, :conv_w]
    kv_meta = pb_m[:, cols["skv"]:]
    p_meta_k = jnp.broadcast_to(kv_meta[None, :, :kv_w], (n_p, n_meta, kv_w))
    p_meta_v = jnp.broadcast_to(kv_meta[None, :, kv_w:], (n_p, n_meta, kv_w))
    p_win_k = pb_p3[:, seq - w_keep:, cols["skv"]:cols["skv"] + kv_w]
    p_win_v = pb_p3[:, seq - w_keep:, cols["skv"] + kv_w:]

    h1_s, pa_s, pb_s = front(x_sample.reshape(n_s * t_s, d))
    o_dn_s, s_s = _gdn_step(pa_s.reshape(n_s, t_s, -1), cols, hist_tile(state_dn_conv[0]), state_dn_ssm[0],
                            *gdn_w, seqs_per_step=4)
    o_dn_s = o_dn_s.reshape(n_s * t_s, -1)
    o_sw_s, s_win_k, s_win_v = _swa_step(
        pb_s.reshape(n_s, t_s, -1), cols,
        cache_swa_meta_k[0].reshape(n_s, n_meta, kv_w), cache_swa_meta_v[0].reshape(n_s, n_meta, kv_w),
        cache_swa_k[0].reshape(n_s, w_keep, kv_w), cache_swa_v[0].reshape(n_s, w_keep, kv_w),
        swa_sinks, seqs_per_step=4)
    y_sample = back(h1_s, o_dn_s, o_sw_s.reshape(n_s * t_s, -1), pa_s, pb_s).reshape(n_s, t_s, d)
    xp_s = jnp.concatenate([state_dn_conv[0], pa_s.reshape(n_s, t_s, -1)[:, :, :conv_w]], axis=1)
    s_conv = xp_s[:, -(DN_CONV - 1):]

    def kv4(x):
        return x.reshape(1, x.shape[0], x.shape[1], kv_heads, HEAD_DIM)

    return (y_prompt, y_sample, p_conv[None], s_p[None], kv4(p_meta_k), kv4(p_meta_v), kv4(p_win_k),
            kv4(p_win_v), s_conv[None], s_s[None], kv4(s_win_k), kv4(s_win_v))
```

```python
import functools
import math

import jax
import jax.numpy as jnp
from jax import lax
from jax.experimental import pallas as pl
from jax.experimental.pallas import tpu as pltpu

F32 = jnp.float32
BF16 = jnp.bfloat16
HIGHEST = lax.Precision.HIGHEST

RMS_EPS = 1e-6
L2_EPS = 1e-6
WINDOW = 128
PAST_LEN = 8192
HEAD_DIM = 128
SWA_GROUP = 4
DN_CONV = 4
DN_CHUNK = 64
HIST_ROWS = 8

VMEM_LIMIT_BYTES = 56 * 1024 * 1024
ROW_TILE = 512
NEG_BIG = -1e30


def _rms(x, g):
    return x * lax.rsqrt(jnp.mean(x * x, axis=-1, keepdims=True) + RMS_EPS) * g


def _silu(x):
    return x * jax.nn.sigmoid(x)


def _dot(a, b):
    return jnp.dot(a, b, preferred_element_type=F32)


def _dot_hi(a, b):
    return jnp.dot(a, b, preferred_element_type=F32, precision=HIGHEST)


def _dot_nt_hi(a, b):
    return lax.dot_general(a, b, (((1,), (1,)), ((), ())), preferred_element_type=F32, precision=HIGHEST)


def _dot_tn_hi(a, b):
    return lax.dot_general(a, b, (((0,), (0,)), ((), ())), preferred_element_type=F32, precision=HIGHEST)


def _dot_nt(a, b):
    return lax.dot_general(a, b, (((1,), (1,)), ((), ())), preferred_element_type=F32)


def _dot_tn(a, b):
    return lax.dot_general(a, b, (((0,), (0,)), ((), ())), preferred_element_type=F32)


def _split_bf16(a):
    hi = a.astype(BF16)
    return hi, (a - hi.astype(F32)).astype(BF16)


_DOTS = {"nn": _dot, "nt": _dot_nt, "tn": _dot_tn}


def _dot3_split(a_pair, b_pair, kind="nn"):
    (a_hi, a_lo), (b_hi, b_lo) = a_pair, b_pair
    dot = _DOTS[kind]
    axis = 1 if kind == "tn" else 0
    m = a_hi.shape[axis]
    top = dot(jnp.concatenate([a_hi, a_lo], axis=axis), b_hi)
    return top[:m] + top[m:] + dot(a_hi, b_lo)


def _mm(a, b, kind="nn", passes=3):
    if passes == 1:
        return _DOTS[kind](a.astype(BF16), b.astype(BF16))
    return _dot3_split(_split_bf16(a), _split_bf16(b), kind)


def _row_tile(rows):
    return ROW_TILE if rows % ROW_TILE == 0 else rows


def _resident(shape):
    return pl.BlockSpec(shape, lambda *_: (0,) * len(shape), pipeline_mode=pl.Buffered(1))


def _params(*semantics):
    return pltpu.CompilerParams(dimension_semantics=semantics, vmem_limit_bytes=VMEM_LIMIT_BYTES)


def _ffn_tile(h, g_pre, g_post, wg_ref, wu_ref, wd_ref, ff_chunk):
    xn = _rms(h, g_pre).astype(BF16)
    acc = None
    for c0 in range(0, wg_ref.shape[1], ff_chunk):
        gate = _dot(xn, wg_ref[:, c0:c0 + ff_chunk])
        up = _dot(xn, wu_ref[:, c0:c0 + ff_chunk])
        part = _dot((_silu(gate) * up).astype(BF16), wd_ref[c0:c0 + ff_chunk, :])
        acc = part if acc is None else acc + part
    return h + 0.5 * _rms(acc, g_post)


def _ff_chunk(d_ff):
    half = d_ff // 2
    return half if d_ff % 2 == 0 and half % 128 == 0 else d_ff


def _ffn_kernel(h_ref, gpre_ref, gpost_ref, wg_ref, wu_ref, wd_ref, o_ref, *, ff_chunk):
    o_ref[...] = _ffn_tile(h_ref[...], gpre_ref[...], gpost_ref[...], wg_ref, wu_ref, wd_ref, ff_chunk)


def _ffn(h, g_pre, g_post, wg, wu, wd):
    rows, d = h.shape
    d_ff = wg.shape[1]
    tm = _row_tile(rows)
    row_spec = pl.BlockSpec((tm, d), lambda i: (i, 0))
    return pl.pallas_call(
        functools.partial(_ffn_kernel, ff_chunk=_ff_chunk(d_ff)),
        grid=(rows // tm,),
        in_specs=[row_spec, _resident((1, d)), _resident((1, d)),
                  _resident((d, d_ff)), _resident((d, d_ff)), _resident((d_ff, d))],
        out_specs=row_spec,
        out_shape=jax.ShapeDtypeStruct((rows, d), F32),
        compiler_params=_params("parallel"),
        name="ffn",
    )(h, g_pre, g_post, wg, wu, wd)


def _proj_kernel(h_ref, g_ref, w_ref, o_ref):
    o_ref[...] = _dot(_rms(h_ref[...], g_ref[...]).astype(BF16), w_ref[...])


def _proj(h, g, w):
    rows, d = h.shape
    n = w.shape[1]
    tm = _row_tile(rows)
    return pl.pallas_call(
        _proj_kernel,
        grid=(rows // tm,),
        in_specs=[pl.BlockSpec((tm, d), lambda i: (i, 0)), _resident((1, d)), _resident((d, n))],
        out_specs=pl.BlockSpec((tm, n), lambda i: (i, 0)),
        out_shape=jax.ShapeDtypeStruct((rows, n), F32),
        compiler_params=_params("parallel"),
        name="proj",
    )(h, g, w)


def _mixffn_kernel(h_ref, odn_ref, osw_ref, gdn_ref, gsw_ref, wout_ref, gmix_ref,
                   gpre_ref, gpost_ref, wg_ref, wu_ref, wd_ref, o_ref, *, ff_chunk):
    y = jax.nn.sigmoid(gdn_ref[...]) * odn_ref[...] + jax.nn.sigmoid(gsw_ref[...]) * osw_ref[...]
    h2 = h_ref[...] + _rms(_dot(y.astype(BF16), wout_ref[...]), gmix_ref[...])
    o_ref[...] = _ffn_tile(h2, gpre_ref[...], gpost_ref[...], wg_ref, wu_ref, wd_ref, ff_chunk)


def _mixffn(h, o_dn, o_sw, proj_a, proj_b, cols, w_out, g_mix, g_pre, g_post, wg, wu, wd):
    rows, d = h.shape
    d_ff = wg.shape[1]
    tm = _row_tile(rows)
    row_spec = pl.BlockSpec((tm, d), lambda i: (i, 0))
    gdn_blk, gsw_blk = cols["g_dn"] // d, cols["g_swa"] // d
    return pl.pallas_call(
        functools.partial(_mixffn_kernel, ff_chunk=_ff_chunk(d_ff)),
        grid=(rows // tm,),
        in_specs=[row_spec, row_spec, row_spec,
                  pl.BlockSpec((tm, d), lambda i: (i, gdn_blk)),
                  pl.BlockSpec((tm, d), lambda i: (i, gsw_blk)),
                  _resident((d, d)), _resident((1, d)), _resident((1, d)), _resident((1, d)),
                  _resident((d, d_ff)), _resident((d, d_ff)), _resident((d_ff, d))],
        out_specs=row_spec,
        out_shape=jax.ShapeDtypeStruct((rows, d), F32),
        compiler_params=_params("parallel"),
        name="mixffn",
    )(h, o_dn, o_sw, proj_a, proj_b, w_out, g_mix, g_pre, g_post, wg, wu, wd)


def _unit_lower_inverse(a, eye):
    c = a.shape[0]
    t = eye - a
    p = a
    for _ in range(int(math.log2(c)) - 1):
        p = _dot_hi(p, p)
        t = t + _dot_hi(t, p)
    return t


def _gdn_kernel(qkv_ref, z_ref, ba_ref, hist_ref, s0_ref, convw_ref, alog_ref, dtb_ref, nw_ref,
                o_ref, sout_ref, xbuf_ref, s_ref, *, chunk, heads):
    j = pl.program_id(1)
    tb = qkv_ref.shape[0]
    hd = HEAD_DIM
    qk_w = heads * hd

    @pl.when(j == 0)
    def _():
        xbuf_ref[0:HIST_ROWS, :] = hist_ref[0]
        s_ref[...] = s0_ref[0]

    xbuf_ref[HIST_ROWS:HIST_ROWS + tb, :] = qkv_ref[...]

    rows = lax.broadcasted_iota(jnp.int32, (chunk, chunk), 0)
    cols = lax.broadcasted_iota(jnp.int32, (chunk, chunk), 1)
    causal = rows >= cols
    strict = rows > cols
    eye = (rows == cols).astype(F32)
    lower_ones = causal.astype(F32)

    def conv_silu(r0, c0):
        acc = None
        for tap in range(DN_CONV):
            lo = HIST_ROWS + r0 - (DN_CONV - 1) + tap
            term = xbuf_ref[lo:lo + chunk, c0:c0 + hd] * convw_ref[tap:tap + 1, c0:c0 + hd]
            acc = term if acc is None else acc + term
        return _silu(acc)

    for r0 in range(0, tb, chunk):
        ba = ba_ref[r0:r0 + chunk, :]
        beta_all = jax.nn.sigmoid(ba[:, 0:heads])
        a_in = ba[:, heads:2 * heads] + dtb_ref[...]
        softplus = jnp.maximum(a_in, 0.0) + jnp.log(1.0 + jnp.exp(-jnp.abs(a_in)))
        g_all = -jnp.exp(alog_ref[...]) * softplus
        gc_col_all = _dot_hi(lower_ones, g_all)
        gc_row_all = _dot_tn_hi(g_all, (rows <= cols).astype(F32))
        for h in range(heads):
            q = conv_silu(r0, h * hd)
            k = conv_silu(r0, qk_w + h * hd)
            v = conv_silu(r0, 2 * qk_w + h * hd)
            q = q * lax.rsqrt(jnp.sum(q * q, axis=-1, keepdims=True) + L2_EPS) * (hd ** -0.5)
            k = k * lax.rsqrt(jnp.sum(k * k, axis=-1, keepdims=True) + L2_EPS)
            beta = beta_all[:, h:h + 1]
            gc = gc_col_all[:, h:h + 1]
            gc_row = gc_row_all[h:h + 1, :]
            gc_last = gc_row[:, chunk - 1:chunk]
            decay = jnp.where(causal, jnp.exp(jnp.where(causal, gc - gc_row, 0.0)), 0.0)
            a_mat = jnp.where(strict, beta * _dot_nt_hi(k, k) * decay, 0.0)
            t_inv = _unit_lower_inverse(a_mat, eye)
            rhs = jnp.concatenate([v * beta, k * (beta * jnp.exp(gc))], axis=-1)
            sol = _dot_hi(t_inv, rhs)
            u_base, w = sol[:, :hd], sol[:, hd:]
            qk = _dot_nt_hi(q, k) * decay
            s = s_ref[h]
            u = u_base - _dot_hi(w, s)
            o = _dot_hi(q * jnp.exp(gc), s) + _dot_hi(qk, u)
            s_ref[h] = s * jnp.exp(gc_last) + _dot_tn_hi(k * jnp.exp(gc_last - gc), u)
            zh = z_ref[r0:r0 + chunk, h * hd:(h + 1) * hd]
            o_ref[r0:r0 + chunk, h * hd:(h + 1) * hd] = _rms(o, nw_ref[...]) * _silu(zh)

    xbuf_ref[0:HIST_ROWS, :] = xbuf_ref[tb:tb + HIST_ROWS, :]

    @pl.when(j == pl.num_programs(1) - 1)
    def _():
        sout_ref[0] = s_ref[...]


GROUP = 4
PASSES_QK = 1
PASSES_STATE = 1
PASSES_INV = 3
STEP_PASSES = 3


def _gdn_packed_kernel(qkv_ref, z_ref, ba_ref, hist_ref, s0_ref, convw_ref, alog_ref, dtb_ref, nw_ref,
                       o_ref, sout_ref, xbuf_ref, s_ref, *, chunk, heads):
    j = pl.program_id(1)
    tb = qkv_ref.shape[0]
    hd = HEAD_DIM
    c = chunk
    gw = GROUP * c
    qk_w = heads * hd

    @pl.when(j == 0)
    def _():
        xbuf_ref[0:HIST_ROWS, :] = hist_ref[0]
        s_ref[...] = s0_ref[0]

    xbuf_ref[HIST_ROWS:HIST_ROWS + tb, :] = qkv_ref[...]

    def iota(shape, dim):
        return lax.broadcasted_iota(jnp.int32, shape, dim)

    row = iota((c, gw), 0)
    col = iota((c, gw), 1) & (c - 1)
    causal4, strict4 = row >= col, row > col
    eye4 = (row == col).astype(F32)
    bd_mask = (iota((gw, gw), 0) // c) == (iota((gw, gw), 1) // c)
    lower_ones = (iota((c, c), 0) >= iota((c, c), 1)).astype(F32)
    upper_dup = (iota((c, 2 * c), 0) <= (iota((c, 2 * c), 1) & (c - 1))).astype(F32)
    first_half = iota((c, 2 * c), 1) < c
    zeros_head = jnp.zeros((c, hd), F32)

    def conv_silu(r0, c0):
        acc = None
        for tap in range(DN_CONV):
            lo = HIST_ROWS + r0 - (DN_CONV - 1) + tap
            term = xbuf_ref[lo:lo + c, c0:c0 + hd] * convw_ref[tap:tap + 1, c0:c0 + hd]
            acc = term if acc is None else acc + term
        return _silu(acc)

    def block_diag(x4):
        return jnp.where(bd_mask, jnp.concatenate([x4] * GROUP, axis=0), jnp.zeros((), x4.dtype))

    def times_block_diag(lhs, x4, passes):
        if passes == 1:
            return _dot(lhs.astype(BF16), block_diag(x4.astype(BF16)))
        hi, lo = _split_bf16(x4)
        return _dot3_split(_split_bf16(lhs), (block_diag(hi), block_diag(lo)))

    def block_diag_times(x4, rhs, passes):
        if passes == 1:
            return _dot(block_diag(x4.astype(BF16)), rhs.astype(BF16))
        hi, lo = _split_bf16(x4)
        return _dot3_split((block_diag(hi), block_diag(lo)), _split_bf16(rhs))

    def pack_cols(col_all, hs):
        halves = [jnp.where(first_half,
                            jnp.broadcast_to(col_all[:, hs[2 * p]:hs[2 * p] + 1], (c, 2 * c)),
                            jnp.broadcast_to(col_all[:, hs[2 * p + 1]:hs[2 * p + 1] + 1], (c, 2 * c)))
                  for p in range(GROUP // 2)]
        return jnp.concatenate(halves, axis=1)

    chunks = []
    for r0 in range(0, tb, c):
        ba = ba_ref[r0:r0 + c, :]
        beta_all = jax.nn.sigmoid(ba[:, 0:heads])
        a_in = ba[:, heads:2 * heads] + dtb_ref[...]
        softplus = jnp.maximum(a_in, 0.0) + jnp.log(1.0 + jnp.exp(-jnp.abs(a_in)))
        g_all = -jnp.exp(alog_ref[...]) * softplus
        gc_all = _dot_hi(lower_ones, g_all)
        gc_row_dup = _dot_tn_hi(g_all, upper_dup)
        chunks.append(dict(r0=r0, beta=beta_all, gc=gc_all, gc_row=gc_row_dup))

    items = []
    for ch in chunks:
        for g0 in range(0, heads, GROUP):
            hs = list(range(g0, g0 + GROUP))
            qs, ks, vs = [], [], []
            for h in hs:
                q = conv_silu(ch["r0"], h * hd)
                k = conv_silu(ch["r0"], qk_w + h * hd)
                qs.append(q * lax.rsqrt(jnp.sum(q * q, axis=-1, keepdims=True) + L2_EPS) * (hd ** -0.5))
                ks.append(k * lax.rsqrt(jnp.sum(k * k, axis=-1, keepdims=True) + L2_EPS))
                vs.append(conv_silu(ch["r0"], 2 * qk_w + h * hd))
            items.append(dict(ch=ch, hs=hs, qs=qs, ks=ks, vs=vs))

    for it in items:
        ch, hs, qs, ks = it["ch"], it["hs"], it["qs"], it["ks"]
        beta4 = pack_cols(ch["beta"], hs)
        gc_row4 = jnp.concatenate(
            [jnp.where(first_half[0:1], ch["gc_row"][hs[2 * p]:hs[2 * p] + 1],
                       ch["gc_row"][hs[2 * p + 1]:hs[2 * p + 1] + 1])
             for p in range(GROUP // 2)], axis=1)
        diff4 = pack_cols(ch["gc"], hs) - gc_row4
        decay4 = jnp.where(causal4, jnp.exp(jnp.where(causal4, diff4, 0.0)), 0.0)
        lhs = jnp.concatenate([jnp.concatenate(qs, axis=1), jnp.concatenate(ks, axis=1)], axis=0)
        k_bd = jnp.concatenate(
            [jnp.concatenate([ks[i] if ii == i else zeros_head for ii in range(GROUP)], axis=1)
             for i in range(GROUP)], axis=0)
        qkkk = _mm(lhs, k_bd, "nt", PASSES_QK)
        it["qk4"] = qkkk[:c] * decay4
        a4 = jnp.where(strict4, beta4 * qkkk[c:] * decay4, 0.0)
        it["t4"] = eye4 - a4
        it["a4"] = a4

    for it in items:
        it["p4"] = times_block_diag(it["a4"], it["a4"], PASSES_INV)
    n_sq = int(math.log2(c)) - 1
    for step in range(n_sq):
        for it in items:
            if step < n_sq - 1:
                both = times_block_diag(jnp.concatenate([it["p4"], it["t4"]], axis=0), it["p4"], PASSES_INV)
                it["p4"], it["t4"] = both[:c], it["t4"] + both[c:]
            else:
                it["t4"] = it["t4"] + times_block_diag(it["t4"], it["p4"], PASSES_INV)

    for it in items:
        ch, hs, ks, vs = it["ch"], it["hs"], it["ks"], it["vs"]
        rhs = jnp.concatenate(
            [jnp.concatenate([vs[i] * ch["beta"][:, h:h + 1],
                              ks[i] * (ch["beta"][:, h:h + 1] * jnp.exp(ch["gc"][:, h:h + 1]))], axis=1)
             for i, h in enumerate(hs)], axis=0)
        it["sol"] = block_diag_times(it["t4"], rhs, PASSES_INV)

    for ch in chunks:
        group_items = [it for it in items if it["ch"] is ch]
        for it in group_items:
            us, o_inter = [], []
            for i, h in enumerate(it["hs"]):
                sl = slice(i * c, (i + 1) * c)
                ws = _mm(jnp.concatenate([it["sol"][sl, hd:], it["qs"][i] * jnp.exp(ch["gc"][:, h:h + 1])], axis=0),
                         s_ref[h], "nn", PASSES_STATE)
                us.append(it["sol"][sl, :hd] - ws[:c])
                o_inter.append(ws[c:])
            it["us"], it["o_inter"] = us, o_inter
        for it in group_items:
            it["o_intra"] = block_diag_times(it["qk4"], jnp.concatenate(it["us"], axis=0), PASSES_STATE)
        for it in group_items:
            for i, h in enumerate(it["hs"]):
                gc = ch["gc"][:, h:h + 1]
                gc_last = ch["gc_row"][h:h + 1, c - 1:c]
                s_ref[h] = s_ref[h] * jnp.exp(gc_last) + _mm(it["ks"][i] * jnp.exp(gc_last - gc), it["us"][i],
                                                             "tn", PASSES_STATE)
        for it in group_items:
            r0 = ch["r0"]
            for i, h in enumerate(it["hs"]):
                o = it["o_inter"][i] + it["o_intra"][i * c:(i + 1) * c]
                zh = z_ref[r0:r0 + c, h * hd:(h + 1) * hd]
                o_ref[r0:r0 + c, h * hd:(h + 1) * hd] = _rms(o, nw_ref[...]) * _silu(zh)

    xbuf_ref[0:HIST_ROWS, :] = xbuf_ref[tb:tb + HIST_ROWS, :]

    @pl.when(j == pl.num_programs(1) - 1)
    def _():
        sout_ref[0] = s_ref[...]


def _gdn(proj_a, cols, hist, s0, conv_w, a_log, dt_bias, norm_w, *, n_seq, seq_len, block, chunk):
    heads = a_log.shape[1]
    conv_width = conv_w.shape[1]
    v_w = heads * HEAD_DIM
    nblk = seq_len // block
    z_blk, ba_blk = cols["z"] // v_w, cols["ba"] // 128
    shared_hist, shared_s0 = hist.shape[0] == 1, s0.shape[0] == 1
    packed = 2 * chunk == HEAD_DIM and heads % GROUP == 0
    return pl.pallas_call(
        functools.partial(_gdn_packed_kernel if packed else _gdn_kernel, chunk=chunk, heads=heads),
        grid=(n_seq, nblk),
        in_specs=[
            pl.BlockSpec((block, conv_width), lambda n, j: (n * nblk + j, 0)),
            pl.BlockSpec((block, v_w), lambda n, j: (n * nblk + j, z_blk)),
            pl.BlockSpec((block, 128), lambda n, j: (n * nblk + j, ba_blk)),
            pl.BlockSpec((1, HIST_ROWS, conv_width), lambda n, j: (0 if shared_hist else n, 0, 0)),
            pl.BlockSpec((1, heads, HEAD_DIM, HEAD_DIM), lambda n, j: (0 if shared_s0 else n, 0, 0, 0)),
            pl.BlockSpec((DN_CONV, conv_width), lambda n, j: (0, 0)),
            pl.BlockSpec((1, heads), lambda n, j: (0, 0)),
            pl.BlockSpec((1, heads), lambda n, j: (0, 0)),
            pl.BlockSpec((1, HEAD_DIM), lambda n, j: (0, 0)),
        ],
        out_specs=[
            pl.BlockSpec((block, v_w), lambda n, j: (n * nblk + j, 0)),
            pl.BlockSpec((1, heads, HEAD_DIM, HEAD_DIM), lambda n, j: (n, 0, 0, 0)),
        ],
        out_shape=[
            jax.ShapeDtypeStruct((n_seq * seq_len, v_w), F32),
            jax.ShapeDtypeStruct((n_seq, heads, HEAD_DIM, HEAD_DIM), F32),
        ],
        scratch_shapes=[
            pltpu.VMEM((block + HIST_ROWS, conv_width), F32),
            pltpu.VMEM((heads, HEAD_DIM, HEAD_DIM), F32),
        ],
        compiler_params=_params("parallel", "arbitrary"),
        name="gdn",
    )(proj_a, proj_a, proj_a, hist, s0, conv_w, a_log, dt_bias, norm_w)


def _gdn_step_kernel(qkv_ref, z_ref, ba_ref, hist_ref, s0_ref, convw_ref, alog_ref, dtb_ref, nw_ref,
                     o_ref, sout_ref, xbuf_ref, *, heads):
    nb, t, _ = qkv_ref.shape
    hd = HEAD_DIM
    qk_w = heads * hd
    ti = lax.broadcasted_iota(jnp.int32, (t, t), 0)
    tj = lax.broadcasted_iota(jnp.int32, (t, t), 1)
    lower_ones = (ti >= tj).astype(F32)
    upper_ones = (ti <= tj).astype(F32)

    def per_sequence(b, carry):
        xbuf_ref[0:HIST_ROWS, :] = hist_ref[b]
        xbuf_ref[HIST_ROWS:HIST_ROWS + t, :] = qkv_ref[b]
        acc = None
        for tap in range(DN_CONV):
            lo = HIST_ROWS - (DN_CONV - 1) + tap
            term = xbuf_ref[lo:lo + t, :] * convw_ref[tap:tap + 1, :]
            acc = term if acc is None else acc + term
        x = _silu(acc)
        ba = ba_ref[b]
        beta_all = jax.nn.sigmoid(ba[:, 0:heads])
        a_in = ba[:, heads:2 * heads] + dtb_ref[...]
        softplus = jnp.maximum(a_in, 0.0) + jnp.log(1.0 + jnp.exp(-jnp.abs(a_in)))
        g_all = -jnp.exp(alog_ref[...]) * softplus
        gc_all = _dot_hi(lower_ones, g_all)
        gc_row_all = _dot_tn_hi(g_all, upper_ones)
        z = z_ref[b]
        hs = range(heads)
        qs, ks, vs = [], [], []
        for h in hs:
            q = x[:, h * hd:(h + 1) * hd]
            k = x[:, qk_w + h * hd:qk_w + (h + 1) * hd]
            qs.append(q * lax.rsqrt(jnp.sum(q * q, axis=-1, keepdims=True) + L2_EPS) * (hd ** -0.5))
            ks.append(k * lax.rsqrt(jnp.sum(k * k, axis=-1, keepdims=True) + L2_EPS))
            vs.append(x[:, 2 * qk_w + h * hd:2 * qk_w + (h + 1) * hd])
        gcs = [gc_all[:, h:h + 1] for h in hs]
        gc_lasts = [gc_row_all[h:h + 1, t - 1:t] for h in hs]
        kq = [jnp.concatenate([ks[h], qs[h]], axis=0) for h in hs]
        gram = [_mm(kq[h], ks[h], "nt", STEP_PASSES) for h in hs]
        from_state = [_mm(kq[h] * jnp.exp(jnp.concatenate([gcs[h], gcs[h]], axis=0)), s0_ref[b, h], "nn",
                          STEP_PASSES) for h in hs]
        us, outs = [], []
        for h in hs:
            decay = jnp.where(ti >= tj, jnp.exp(jnp.where(ti >= tj, gcs[h] - gc_row_all[h:h + 1, :], 0.0)), 0.0)
            beta = beta_all[:, h:h + 1]
            lower = jnp.where(ti > tj, beta * gram[h][:t] * decay, 0.0)
            u = beta * (vs[h] - from_state[h][:t])
            for j in range(t - 1):
                u = u - lower[:, j:j + 1] * u[j:j + 1, :]
            qk = gram[h][t:] * decay
            o = from_state[h][t:]
            for j in range(t):
                o = o + qk[:, j:j + 1] * u[j:j + 1, :]
            us.append(u)
            outs.append(o)
        for h in hs:
            sout_ref[b, h] = s0_ref[b, h] * jnp.exp(gc_lasts[h]) + _mm(ks[h] * jnp.exp(gc_lasts[h] - gcs[h]), us[h],
                                                                       "tn", STEP_PASSES)
        for h in hs:
            o_ref[b, :, h * hd:(h + 1) * hd] = _rms(outs[h], nw_ref[...]) * _silu(z[:, h * hd:(h + 1) * hd])
        return carry

    lax.fori_loop(0, nb, per_sequence, 0)


def _gdn_step(proj_a3, cols, hist, s0, conv_w, a_log, dt_bias, norm_w, *, seqs_per_step):
    n, t, _ = proj_a3.shape
    heads = a_log.shape[1]
    conv_width = conv_w.shape[1]
    v_w = heads * HEAD_DIM
    nb = seqs_per_step
    z_blk, ba_blk = cols["z"] // v_w, cols["ba"] // 128
    state_spec = pl.BlockSpec((nb, heads, HEAD_DIM, HEAD_DIM), lambda i: (i, 0, 0, 0))
    return pl.pallas_call(
        functools.partial(_gdn_step_kernel, heads=heads),
        grid=(n // nb,),
        in_specs=[
            pl.BlockSpec((nb, t, conv_width), lambda i: (i, 0, 0)),
            pl.BlockSpec((nb, t, v_w), lambda i: (i, 0, z_blk)),
            pl.BlockSpec((nb, t, 128), lambda i: (i, 0, ba_blk)),
            pl.BlockSpec((nb, HIST_ROWS, conv_width), lambda i: (i, 0, 0)),
            state_spec,
            pl.BlockSpec((DN_CONV, conv_width), lambda i: (0, 0)),
            pl.BlockSpec((1, heads), lambda i: (0, 0)),
            pl.BlockSpec((1, heads), lambda i: (0, 0)),
            pl.BlockSpec((1, HEAD_DIM), lambda i: (0, 0)),
        ],
        out_specs=[pl.BlockSpec((nb, t, v_w), lambda i: (i, 0, 0)), state_spec],
        out_shape=[jax.ShapeDtypeStruct((n, t, v_w), F32),
                   jax.ShapeDtypeStruct((n, heads, HEAD_DIM, HEAD_DIM), F32)],
        scratch_shapes=[pltpu.VMEM((HIST_ROWS + t, conv_width), F32)],
        compiler_params=_params("parallel"),
        name="gdn_step",
    )(proj_a3, proj_a3, proj_a3, hist, s0, conv_w, a_log, dt_bias, norm_w)


def _alibi_slope(head, n_heads):
    return 2.0 ** (-8.0 * (head + 1) / n_heads)


def _penalty(dist, mask):
    return jnp.where(mask, jnp.minimum(dist, WINDOW).astype(F32), -NEG_BIG)


def _attend(jobs, n_rows):
    scale = HEAD_DIM ** -0.5
    hd = HEAD_DIM
    scores = [[_dot_nt(q4, k) * scale for k, _, _ in segs] for q4, segs, _, _ in jobs]
    v_ones = [[jnp.concatenate([v, jnp.ones_like(v)], axis=1) for _, v, _ in segs] for _, segs, _, _ in jobs]
    heads = [(ji, g) for ji in range(len(jobs)) for g in range(len(jobs[ji][3]))]
    logits, maxes = {}, {}
    for ji, g in heads:
        _, segs, _, slopes = jobs[ji]
        r = slice(g * n_rows, (g + 1) * n_rows)
        logits[ji, g] = [sc[r] - slopes[g] * pen for sc, (_, _, pen) in zip(scores[ji], segs)]
    for ji, g in heads:
        by_width = {}
        for lg in logits[ji, g]:
            w = lg.shape[1]
            by_width[w] = lg if w not in by_width else jnp.maximum(by_width[w], lg)
        m = jobs[ji][2][g]
        for lg in by_width.values():
            m = jnp.maximum(m, jnp.max(lg, axis=-1, keepdims=True))
        maxes[ji, g] = m
    accs = {}
    for ji, g in heads:
        acc = None
        for lg, v1 in zip(logits[ji, g], v_ones[ji]):
            pv = _dot(jnp.exp(lg - maxes[ji, g]).astype(BF16), v1)
            acc = pv if acc is None else acc + pv
        accs[ji, g] = acc
    outs = [[None] * len(job[3]) for job in jobs]
    for ji, g in heads:
        acc = accs[ji, g]
        outs[ji][g] = acc[:, :hd] / (acc[:, hd:] + jnp.exp(jobs[ji][2][g] - maxes[ji, g]))
    return outs


def _swa_prompt_kernel(q_ref, kv_ref, kvprev_ref, kvmeta_ref, sinks_ref, o_ref, *, n_meta, kv_heads):
    j = pl.program_id(1)
    tq = q_ref.shape[0]
    hd = HEAD_DIM
    kv_w = kv_heads * hd
    n_heads = kv_heads * SWA_GROUP
    qi = lax.broadcasted_iota(jnp.int32, (tq, tq), 0)
    ki = lax.broadcasted_iota(jnp.int32, (tq, tq), 1)
    dist_own = qi - ki
    dist_prev = dist_own + tq
    pen_own = _penalty(dist_own, dist_own >= 0)
    pen_prev = _penalty(dist_prev, (dist_prev <= WINDOW) & (j > 0))
    qpos = n_meta + j * tq + lax.broadcasted_iota(jnp.int32, (tq, n_meta), 0)
    dist_meta = qpos - lax.broadcasted_iota(jnp.int32, (tq, n_meta), 1)
    pen_meta = _penalty(dist_meta, dist_meta >= 0)
    jobs, job_heads = [], []
    for kvh in range(kv_heads):
        ks = slice(kvh * hd, (kvh + 1) * hd)
        vs = slice(kv_w + kvh * hd, kv_w + (kvh + 1) * hd)
        segments = [
            (kvmeta_ref[:, ks].astype(BF16), kvmeta_ref[:, vs].astype(BF16), pen_meta),
            (kvprev_ref[:, ks].astype(BF16), kvprev_ref[:, vs].astype(BF16), pen_prev),
            (kv_ref[:, ks].astype(BF16), kv_ref[:, vs].astype(BF16), pen_own),
        ]
        heads = [kvh * SWA_GROUP + g for g in range(SWA_GROUP)]
        q4 = jnp.concatenate([q_ref[:, hh * hd:(hh + 1) * hd] for hh in heads], axis=0).astype(BF16)
        jobs.append((q4, segments, [sinks_ref[:, hh:hh + 1] for hh in heads],
                     [_alibi_slope(hh, n_heads) for hh in heads]))
        job_heads.append(heads)
    for heads, outs in zip(job_heads, _attend(jobs, tq)):
        for hh, o in zip(heads, outs):
            o_ref[:, hh * hd:(hh + 1) * hd] = o


def _swa_prompt(proj_b, proj_b_meta, cols, sinks, *, n_seq, seq_len, n_meta):
    n_heads = sinks.shape[1]
    kv_heads = n_heads // SWA_GROUP
    q_w, kv_w2 = n_heads * HEAD_DIM, 2 * kv_heads * HEAD_DIM
    tq = WINDOW
    nblk = seq_len // tq
    kv_blk = cols["skv"] // kv_w2
    return pl.pallas_call(
        functools.partial(_swa_prompt_kernel, n_meta=n_meta, kv_heads=kv_heads),
        grid=(n_seq, nblk),
        in_specs=[
            pl.BlockSpec((tq, q_w), lambda n, j: (n * nblk + j, 0)),
            pl.BlockSpec((tq, kv_w2), lambda n, j: (n * nblk + j, kv_blk)),
            pl.BlockSpec((tq, kv_w2), lambda n, j: (n * nblk + jnp.maximum(j - 1, 0), kv_blk)),
            pl.BlockSpec((n_meta, kv_w2), lambda n, j: (0, kv_blk)),
            pl.BlockSpec((1, n_heads), lambda n, j: (0, 0)),
        ],
        out_specs=pl.BlockSpec((tq, q_w), lambda n, j: (n * nblk + j, 0)),
        out_shape=jax.ShapeDtypeStruct((n_seq * seq_len, q_w), F32),
        compiler_params=_params("parallel", "arbitrary"),
        name="swa_prompt",
    )(proj_b, proj_b, proj_b, proj_b_meta, sinks)


def _swa_step_kernel(q_ref, kvnew_ref, kmeta_ref, vmeta_ref, kbuf_ref, vbuf_ref, sinks_ref,
                     o_ref, kout_ref, vout_ref, *, kv_heads):
    nb, t, _ = q_ref.shape
    n_meta, w = kmeta_ref.shape[1], kbuf_ref.shape[1]
    hd = HEAD_DIM
    kv_w = kv_heads * hd
    n_heads = kv_heads * SWA_GROUP

    def grid2(rows, cols_):
        return (lax.broadcasted_iota(jnp.int32, (rows, cols_), 0), lax.broadcasted_iota(jnp.int32, (rows, cols_), 1))

    ti, mi = grid2(t, n_meta)
    dist_meta = PAST_LEN + ti - mi
    pen_meta = _penalty(dist_meta, dist_meta >= 0)
    ti, bi = grid2(t, w)
    dist_buf = w + ti - bi
    pen_buf = _penalty(dist_buf, (dist_buf <= WINDOW) & (PAST_LEN - w + bi >= n_meta))
    ti, si = grid2(t, t)
    dist_new = ti - si
    pen_new = _penalty(dist_new, dist_new >= 0)
    jobs, job_dst = [], []
    for b in range(nb):
        kv_new = kvnew_ref[b]
        kout_ref[b, 0:w - t, :] = kbuf_ref[b, t:w, :]
        kout_ref[b, w - t:w, :] = kv_new[:, 0:kv_w]
        vout_ref[b, 0:w - t, :] = vbuf_ref[b, t:w, :]
        vout_ref[b, w - t:w, :] = kv_new[:, kv_w:2 * kv_w]
        for kvh in range(kv_heads):
            ks = slice(kvh * hd, (kvh + 1) * hd)
            segments = [
                (kmeta_ref[b, :, ks].astype(BF16), vmeta_ref[b, :, ks].astype(BF16), pen_meta),
                (kbuf_ref[b, :, ks].astype(BF16), vbuf_ref[b, :, ks].astype(BF16), pen_buf),
                (kv_new[:, ks].astype(BF16), kv_new[:, kv_w + kvh * hd:kv_w + (kvh + 1) * hd].astype(BF16), pen_new),
            ]
            heads = [kvh * SWA_GROUP + g for g in range(SWA_GROUP)]
            q4 = jnp.concatenate([q_ref[b, :, hh * hd:(hh + 1) * hd] for hh in heads], axis=0).astype(BF16)
            jobs.append((q4, segments, [sinks_ref[:, hh:hh + 1] for hh in heads],
                         [_alibi_slope(hh, n_heads) for hh in heads]))
            job_dst.append((b, heads))
    for (b, heads), outs in zip(job_dst, _attend(jobs, t)):
        for hh, o in zip(heads, outs):
            o_ref[b, :, hh * hd:(hh + 1) * hd] = o


def _swa_step(proj_b3, cols, k_meta, v_meta, k_buf, v_buf, sinks, *, seqs_per_step):
    n, t, _ = proj_b3.shape
    n_heads = sinks.shape[1]
    kv_heads = n_heads // SWA_GROUP
    q_w, kv_w = n_heads * HEAD_DIM, kv_heads * HEAD_DIM
    n_meta, w = k_meta.shape[1], k_buf.shape[1]
    nb = seqs_per_step
    kv_blk = cols["skv"] // (2 * kv_w)
    seq3 = lambda rows, width, blk=0: pl.BlockSpec((nb, rows, width), lambda i: (i, 0, blk))
    return pl.pallas_call(
        functools.partial(_swa_step_kernel, kv_heads=kv_heads),
        grid=(n // nb,),
        in_specs=[seq3(t, q_w), seq3(t, 2 * kv_w, kv_blk), seq3(n_meta, kv_w), seq3(n_meta, kv_w),
                  seq3(w, kv_w), seq3(w, kv_w), pl.BlockSpec((1, n_heads), lambda i: (0, 0))],
        out_specs=[seq3(t, q_w), seq3(w, kv_w), seq3(w, kv_w)],
        out_shape=[jax.ShapeDtypeStruct((n, t, q_w), F32),
                   jax.ShapeDtypeStruct((n, w, kv_w), F32),
                   jax.ShapeDtypeStruct((n, w, kv_w), F32)],
        compiler_params=_params("parallel"),
        name="swa_step",
    )(proj_b3, proj_b3, k_meta, v_meta, k_buf, v_buf, sinks)


def _split_w_in(w_in, heads, n_heads, kv_heads, d):
    conv_w = 3 * heads * HEAD_DIM
    v_w = heads * HEAD_DIM
    q_w, kv_w = n_heads * HEAD_DIM, kv_heads * HEAD_DIM
    sizes = (conv_w, v_w, heads, heads, q_w, kv_w, kv_w, d, d)
    offs = [0]
    for s in sizes:
        offs.append(offs[-1] + s)
    part = lambda i: w_in[:, offs[i]:offs[i + 1]]
    ba = jnp.concatenate([part(2), part(3), jnp.zeros((w_in.shape[0], 128 - 2 * heads), w_in.dtype)], axis=1)
    w_a = jnp.concatenate([part(0), part(1), part(7), ba], axis=1).astype(BF16)
    w_b = jnp.concatenate([part(4), part(8), part(5), part(6)], axis=1).astype(BF16)
    cols_a = {"qkv": 0, "z": conv_w, "g_dn": conv_w + v_w, "ba": conv_w + v_w + d}
    cols_b = {"q": 0, "g_swa": q_w, "skv": q_w + d}
    return w_a, w_b, cols_a, cols_b


def kernel(x_prompt, x_sample, state_dn_conv, state_dn_ssm, cache_swa_meta_k, cache_swa_meta_v, cache_swa_k, cache_swa_v, meta_tokens, ffn1_norm_pre, ffn1_norm_post, ffn1_w_gate, ffn1_w_up, ffn1_w_down, mix_norm_pre, mix_norm_post, w_in, dn_conv_w, dn_a_log, dn_dt_bias, dn_norm_w, swa_sinks, w_out, ffn2_norm_pre, ffn2_norm_post, ffn2_w_gate, ffn2_w_up, ffn2_w_down):
    assert w_in.shape[0] == 1, "single-layer step"
    n_p, seq, d = x_prompt.shape
    n_s, t_s, _ = x_sample.shape
    n_meta = meta_tokens.shape[0]
    heads = dn_a_log.shape[1]
    n_heads = swa_sinks.shape[1]
    kv_heads = n_heads // SWA_GROUP
    conv_w = dn_conv_w.shape[2]
    kv_w = kv_heads * HEAD_DIM
    w_keep = cache_swa_k.shape[2]
    assert w_keep == WINDOW and seq % WINDOW == 0 and seq % DN_CHUNK == 0 and seq >= WINDOW

    w_a, w_b, cols_a, cols_b = _split_w_in(w_in[0], heads, n_heads, kv_heads, d)
    cols = {**cols_a, **cols_b}
    f1 = (ffn1_norm_pre, ffn1_norm_post, ffn1_w_gate[0].astype(BF16), ffn1_w_up[0].astype(BF16),
          ffn1_w_down[0].astype(BF16))
    f2 = (ffn2_norm_pre, ffn2_norm_post, ffn2_w_gate[0].astype(BF16), ffn2_w_up[0].astype(BF16),
          ffn2_w_down[0].astype(BF16))
    w_o = w_out[0].astype(BF16)
    gdn_w = (dn_conv_w[0], dn_a_log, dn_dt_bias, dn_norm_w)

    def front(h):
        h1 = _ffn(h, *f1)
        return h1, _proj(h1, mix_norm_pre, w_a), _proj(h1, mix_norm_pre, w_b)

    def back(h1, o_dn, o_sw, pa, pb):
        return _mixffn(h1, o_dn, o_sw, pa, pb, cols, w_o, mix_norm_post, *f2)

    def hist_tile(rows3):
        return jnp.pad(rows3, ((0, 0), (HIST_ROWS - (DN_CONV - 1), 0), (0, 0)))

    _, pa_m, pb_m = front(meta_tokens)
    zero_hist = jnp.zeros((1, HIST_ROWS, conv_w), F32)
    zero_state = jnp.zeros((1, heads, HEAD_DIM, HEAD_DIM), F32)
    _, s_meta = _gdn(pa_m, cols, zero_hist, zero_state, *gdn_w, n_seq=1, seq_len=n_meta, block=n_meta,
                     chunk=n_meta)
    hist_meta = hist_tile(pa_m[None, n_meta - (DN_CONV - 1):, :conv_w])

    h1_p, pa_p, pb_p = front(x_prompt.reshape(n_p * seq, d))
    o_dn_p, s_p = _gdn(pa_p, cols, hist_meta, s_meta, *gdn_w, n_seq=n_p, seq_len=seq, block=4 * DN_CHUNK,
                       chunk=DN_CHUNK)
    o_sw_p = _swa_prompt(pb_p, pb_m, cols, swa_sinks, n_seq=n_p, seq_len=seq, n_meta=n_meta)
    y_prompt = back(h1_p, o_dn_p, o_sw_p, pa_p, pb_p).reshape(n_p, seq, d)

    pa_p3 = pa_p.reshape(n_p, seq, -1)
    pb_p3 = pb_p.reshape(n_p, seq, -1)
    p_conv = pa_p3[:, seq - (DN_CONV - 1):, :conv_w]
    kv_meta = pb_m[:, cols["skv"]:]
    p_meta_k = jnp.broadcast_to(kv_meta[None, :, :kv_w], (n_p, n_meta, kv_w))
    p_meta_v = jnp.broadcast_to(kv_meta[None, :, kv_w:], (n_p, n_meta, kv_w))
    p_win_k = pb_p3[:, seq - w_keep:, cols["skv"]:cols["skv"] + kv_w]
    p_win_v = pb_p3[:, seq - w_keep:, cols["skv"] + kv_w:]

    h1_s, pa_s, pb_s = front(x_sample.reshape(n_s * t_s, d))
    o_dn_s, s_s = _gdn_step(pa_s.reshape(n_s, t_s, -1), cols, hist_tile(state_dn_conv[0]), state_dn_ssm[0],
                            *gdn_w, seqs_per_step=4)
    o_dn_s = o_dn_s.reshape(n_s * t_s, -1)
    o_sw_s, s_win_k, s_win_v = _swa_step(
        pb_s.reshape(n_s, t_s, -1), cols,
        cache_swa_meta_k[0].reshape(n_s, n_meta, kv_w), cache_swa_meta_v[0].reshape(n_s, n_meta, kv_w),
        cache_swa_k[0].reshape(n_s, w_keep, kv_w), cache_swa_v[0].reshape(n_s, w_keep, kv_w),
        swa_sinks, seqs_per_step=4)
    y_sample = back(h1_s, o_dn_s, o_sw_s.reshape(n_s * t_s, -1), pa_s, pb_s).reshape(n_s, t_s, d)
    xp_s = jnp.concatenate([state_dn_conv[0], pa_s.reshape(n_s, t_s, -1)[:, :, :conv_w]], axis=1)
    s_conv = xp_s[:, -(DN_CONV - 1):]

    def kv4(x):
        return x.reshape(1, x.shape[0], x.shape[1], kv_heads, HEAD_DIM)

    return (y_prompt, y_sample, p_conv[None], s_p[None], kv4(p_meta_k), kv4(p_meta_v), kv4(p_win_k),
            kv4(p_win_v), s_conv[None], s_s[None], kv4(s_win_k), kv4(s_win_v))
```

```python
import functools
import math

import jax
import jax.numpy as jnp
from jax import lax
from jax.experimental import pallas as pl
from jax.experimental.pallas import tpu as pltpu

F32 = jnp.float32
BF16 = jnp.bfloat16
HIGHEST = lax.Precision.HIGHEST

RMS_EPS = 1e-6
L2_EPS = 1e-6
WINDOW = 128
PAST_LEN = 8192
HEAD_DIM = 128
SWA_GROUP = 4
DN_CONV = 4
DN_CHUNK = 64
HIST_ROWS = 8

VMEM_LIMIT_BYTES = 56 * 1024 * 1024
ROW_TILE = 512
NEG_BIG = -1e30


def _rms(x, g):
    return x * lax.rsqrt(jnp.mean(x * x, axis=-1, keepdims=True) + RMS_EPS) * g


def _silu(x):
    return x * jax.nn.sigmoid(x)


def _dot(a, b):
    return jnp.dot(a, b, preferred_element_type=F32)


def _dot_hi(a, b):
    return jnp.dot(a, b, preferred_element_type=F32, precision=HIGHEST)


def _dot_nt_hi(a, b):
    return lax.dot_general(a, b, (((1,), (1,)), ((), ())), preferred_element_type=F32, precision=HIGHEST)


def _dot_tn_hi(a, b):
    return lax.dot_general(a, b, (((0,), (0,)), ((), ())), preferred_element_type=F32, precision=HIGHEST)


def _dot_nt(a, b):
    return lax.dot_general(a, b, (((1,), (1,)), ((), ())), preferred_element_type=F32)


def _dot_tn(a, b):
    return lax.dot_general(a, b, (((0,), (0,)), ((), ())), preferred_element_type=F32)


def _split_bf16(a):
    hi = a.astype(BF16)
    return hi, (a - hi.astype(F32)).astype(BF16)


_DOTS = {"nn": _dot, "nt": _dot_nt, "tn": _dot_tn}


def _dot3_split(a_pair, b_pair, kind="nn"):
    (a_hi, a_lo), (b_hi, b_lo) = a_pair, b_pair
    dot = _DOTS[kind]
    axis = 1 if kind == "tn" else 0
    m = a_hi.shape[axis]
    top = dot(jnp.concatenate([a_hi, a_lo], axis=axis), b_hi)
    return top[:m] + top[m:] + dot(a_hi, b_lo)


def _mm(a, b, kind="nn", passes=3):
    if passes == 1:
        return _DOTS[kind](a.astype(BF16), b.astype(BF16))
    return _dot3_split(_split_bf16(a), _split_bf16(b), kind)


def _row_tile(rows):
    return ROW_TILE if rows % ROW_TILE == 0 else rows


def _resident(shape):
    return pl.BlockSpec(shape, lambda *_: (0,) * len(shape), pipeline_mode=pl.Buffered(1))


def _params(*semantics):
    return pltpu.CompilerParams(dimension_semantics=semantics, vmem_limit_bytes=VMEM_LIMIT_BYTES)


def _ffn_tile(h, g_pre, g_post, wg_ref, wu_ref, wd_ref, ff_chunk):
    xn = _rms(h, g_pre).astype(BF16)
    acc = None
    for c0 in range(0, wg_ref.shape[1], ff_chunk):
        gate = _dot(xn, wg_ref[:, c0:c0 + ff_chunk])
        up = _dot(xn, wu_ref[:, c0:c0 + ff_chunk])
        part = _dot((_silu(gate) * up).astype(BF16), wd_ref[c0:c0 + ff_chunk, :])
        acc = part if acc is None else acc + part
    return h + 0.5 * _rms(acc, g_post)


def _ff_chunk(d_ff):
    half = d_ff // 2
    return half if d_ff % 2 == 0 and half % 128 == 0 else d_ff


def _ffn_kernel(h_ref, gpre_ref, gpost_ref, wg_ref, wu_ref, wd_ref, o_ref, *, ff_chunk):
    o_ref[...] = _ffn_tile(h_ref[...], gpre_ref[...], gpost_ref[...], wg_ref, wu_ref, wd_ref, ff_chunk)


def _ffn(h, g_pre, g_post, wg, wu, wd):
    rows, d = h.shape
    d_ff = wg.shape[1]
    tm = _row_tile(rows)
    row_spec = pl.BlockSpec((tm, d), lambda i: (i, 0))
    return pl.pallas_call(
        functools.partial(_ffn_kernel, ff_chunk=_ff_chunk(d_ff)),
        grid=(rows // tm,),
        in_specs=[row_spec, _resident((1, d)), _resident((1, d)),
                  _resident((d, d_ff)), _resident((d, d_ff)), _resident((d_ff, d))],
        out_specs=row_spec,
        out_shape=jax.ShapeDtypeStruct((rows, d), F32),
        compiler_params=_params("parallel"),
        name="ffn",
    )(h, g_pre, g_post, wg, wu, wd)


def _proj_kernel(h_ref, g_ref, w_ref, o_ref):
    o_ref[...] = _dot(_rms(h_ref[...], g_ref[...]).astype(BF16), w_ref[...])


def _proj(h, g, w):
    rows, d = h.shape
    n = w.shape[1]
    tm = _row_tile(rows)
    return pl.pallas_call(
        _proj_kernel,
        grid=(rows // tm,),
        in_specs=[pl.BlockSpec((tm, d), lambda i: (i, 0)), _resident((1, d)), _resident((d, n))],
        out_specs=pl.BlockSpec((tm, n), lambda i: (i, 0)),
        out_shape=jax.ShapeDtypeStruct((rows, n), F32),
        compiler_params=_params("parallel"),
        name="proj",
    )(h, g, w)


def _mixffn_kernel(h_ref, odn_ref, osw_ref, gdn_ref, gsw_ref, wout_ref, gmix_ref,
                   gpre_ref, gpost_ref, wg_ref, wu_ref, wd_ref, o_ref, *, ff_chunk):
    y = jax.nn.sigmoid(gdn_ref[...]) * odn_ref[...] + jax.nn.sigmoid(gsw_ref[...]) * osw_ref[...]
    h2 = h_ref[...] + _rms(_dot(y.astype(BF16), wout_ref[...]), gmix_ref[...])
    o_ref[...] = _ffn_tile(h2, gpre_ref[...], gpost_ref[...], wg_ref, wu_ref, wd_ref, ff_chunk)


def _mixffn(h, o_dn, o_sw, proj_a, proj_b, cols, w_out, g_mix, g_pre, g_post, wg, wu, wd):
    rows, d = h.shape
    d_ff = wg.shape[1]
    tm = _row_tile(rows)
    row_spec = pl.BlockSpec((tm, d), lambda i: (i, 0))
    gdn_blk, gsw_blk = cols["g_dn"] // d, cols["g_swa"] // d
    return pl.pallas_call(
        functools.partial(_mixffn_kernel, ff_chunk=_ff_chunk(d_ff)),
        grid=(rows // tm,),
        in_specs=[row_spec, row_spec, row_spec,
                  pl.BlockSpec((tm, d), lambda i: (i, gdn_blk)),
                  pl.BlockSpec((tm, d), lambda i: (i, gsw_blk)),
                  _resident((d, d)), _resident((1, d)), _resident((1, d)), _resident((1, d)),
                  _resident((d, d_ff)), _resident((d, d_ff)), _resident((d_ff, d))],
        out_specs=row_spec,
        out_shape=jax.ShapeDtypeStruct((rows, d), F32),
        compiler_params=_params("parallel"),
        name="mixffn",
    )(h, o_dn, o_sw, proj_a, proj_b, w_out, g_mix, g_pre, g_post, wg, wu, wd)


def _unit_lower_inverse(a, eye):
    c = a.shape[0]
    t = eye - a
    p = a
    for _ in range(int(math.log2(c)) - 1):
        p = _dot_hi(p, p)
        t = t + _dot_hi(t, p)
    return t


def _gdn_kernel(qkv_ref, z_ref, ba_ref, hist_ref, s0_ref, convw_ref, alog_ref, dtb_ref, nw_ref,
                o_ref, sout_ref, xbuf_ref, s_ref, *, chunk, heads):
    j = pl.program_id(1)
    tb = qkv_ref.shape[0]
    hd = HEAD_DIM
    qk_w = heads * hd

    @pl.when(j == 0)
    def _():
        xbuf_ref[0:HIST_ROWS, :] = hist_ref[0]
        s_ref[...] = s0_ref[0]

    xbuf_ref[HIST_ROWS:HIST_ROWS + tb, :] = qkv_ref[...]

    rows = lax.broadcasted_iota(jnp.int32, (chunk, chunk), 0)
    cols = lax.broadcasted_iota(jnp.int32, (chunk, chunk), 1)
    causal = rows >= cols
    strict = rows > cols
    eye = (rows == cols).astype(F32)
    lower_ones = causal.astype(F32)

    def conv_silu(r0, c0):
        acc = None
        for tap in range(DN_CONV):
            lo = HIST_ROWS + r0 - (DN_CONV - 1) + tap
            term = xbuf_ref[lo:lo + chunk, c0:c0 + hd] * convw_ref[tap:tap + 1, c0:c0 + hd]
            acc = term if acc is None else acc + term
        return _silu(acc)

    for r0 in range(0, tb, chunk):
        ba = ba_ref[r0:r0 + chunk, :]
        beta_all = jax.nn.sigmoid(ba[:, 0:heads])
        a_in = ba[:, heads:2 * heads] + dtb_ref[...]
        softplus = jnp.maximum(a_in, 0.0) + jnp.log(1.0 + jnp.exp(-jnp.abs(a_in)))
        g_all = -jnp.exp(alog_ref[...]) * softplus
        gc_col_all = _dot_hi(lower_ones, g_all)
        gc_row_all = _dot_tn_hi(g_all, (rows <= cols).astype(F32))
        for h in range(heads):
            q = conv_silu(r0, h * hd)
            k = conv_silu(r0, qk_w + h * hd)
            v = conv_silu(r0, 2 * qk_w + h * hd)
            q = q * lax.rsqrt(jnp.sum(q * q, axis=-1, keepdims=True) + L2_EPS) * (hd ** -0.5)
            k = k * lax.rsqrt(jnp.sum(k * k, axis=-1, keepdims=True) + L2_EPS)
            beta = beta_all[:, h:h + 1]
            gc = gc_col_all[:, h:h + 1]
            gc_row = gc_row_all[h:h + 1, :]
            gc_last = gc_row[:, chunk - 1:chunk]
            decay = jnp.where(causal, jnp.exp(jnp.where(causal, gc - gc_row, 0.0)), 0.0)
            a_mat = jnp.where(strict, beta * _dot_nt_hi(k, k) * decay, 0.0)
            t_inv = _unit_lower_inverse(a_mat, eye)
            rhs = jnp.concatenate([v * beta, k * (beta * jnp.exp(gc))], axis=-1)
            sol = _dot_hi(t_inv, rhs)
            u_base, w = sol[:, :hd], sol[:, hd:]
            qk = _dot_nt_hi(q, k) * decay
            s = s_ref[h]
            u = u_base - _dot_hi(w, s)
            o = _dot_hi(q * jnp.exp(gc), s) + _dot_hi(qk, u)
            s_ref[h] = s * jnp.exp(gc_last) + _dot_tn_hi(k * jnp.exp(gc_last - gc), u)
            zh = z_ref[r0:r0 + chunk, h * hd:(h + 1) * hd]
            o_ref[r0:r0 + chunk, h * hd:(h + 1) * hd] = _rms(o, nw_ref[...]) * _silu(zh)

    xbuf_ref[0:HIST_ROWS, :] = xbuf_ref[tb:tb + HIST_ROWS, :]

    @pl.when(j == pl.num_programs(1) - 1)
    def _():
        sout_ref[0] = s_ref[...]


GROUP = 4
PASSES_QK = 1
PASSES_STATE = 1
PASSES_INV = 3
STEP_PASSES = 3


def _gdn_packed_kernel(qkv_ref, z_ref, ba_ref, hist_ref, s0_ref, convw_ref, alog_ref, dtb_ref, nw_ref,
                       o_ref, sout_ref, xbuf_ref, s_ref, *, chunk, heads):
    j = pl.program_id(1)
    tb = qkv_ref.shape[0]
    hd = HEAD_DIM
    c = chunk
    gw = GROUP * c
    qk_w = heads * hd

    @pl.when(j == 0)
    def _():
        xbuf_ref[0:HIST_ROWS, :] = hist_ref[0]
        s_ref[...] = s0_ref[0]

    xbuf_ref[HIST_ROWS:HIST_ROWS + tb, :] = qkv_ref[...]

    def iota(shape, dim):
        return lax.broadcasted_iota(jnp.int32, shape, dim)

    row = iota((c, gw), 0)
    col = iota((c, gw), 1) & (c - 1)
    causal4, strict4 = row >= col, row > col
    eye4 = (row == col).astype(F32)
    bd_mask = (iota((gw, gw), 0) // c) == (iota((gw, gw), 1) // c)
    lower_ones = (iota((c, c), 0) >= iota((c, c), 1)).astype(F32)
    upper_dup = (iota((c, 2 * c), 0) <= (iota((c, 2 * c), 1) & (c - 1))).astype(F32)
    first_half = iota((c, 2 * c), 1) < c
    zeros_head = jnp.zeros((c, hd), F32)

    def conv_silu(r0, c0):
        x = xbuf_ref[r0:r0 + HIST_ROWS + c, c0:c0 + hd]
        acc = x[HIST_ROWS:] * convw_ref[DN_CONV - 1:DN_CONV, c0:c0 + hd]
        for back in range(1, DN_CONV):
            shifted = pltpu.roll(x, back, 0)[HIST_ROWS:]
            acc = acc + shifted * convw_ref[DN_CONV - 1 - back:DN_CONV - back, c0:c0 + hd]
        return _silu(acc)

    def block_diag(x4):
        return jnp.where(bd_mask, jnp.concatenate([x4] * GROUP, axis=0), jnp.zeros((), x4.dtype))

    def times_block_diag(lhs, x4, passes):
        if passes == 1:
            return _dot(lhs.astype(BF16), block_diag(x4.astype(BF16)))
        hi, lo = _split_bf16(x4)
        return _dot3_split(_split_bf16(lhs), (block_diag(hi), block_diag(lo)))

    def block_diag_times(x4, rhs, passes):
        if passes == 1:
            return _dot(block_diag(x4.astype(BF16)), rhs.astype(BF16))
        hi, lo = _split_bf16(x4)
        return _dot3_split((block_diag(hi), block_diag(lo)), _split_bf16(rhs))

    def pack_cols(col_all, hs):
        halves = [jnp.where(first_half,
                            jnp.broadcast_to(col_all[:, hs[2 * p]:hs[2 * p] + 1], (c, 2 * c)),
                            jnp.broadcast_to(col_all[:, hs[2 * p + 1]:hs[2 * p + 1] + 1], (c, 2 * c)))
                  for p in range(GROUP // 2)]
        return jnp.concatenate(halves, axis=1)

    chunks = []
    for r0 in range(0, tb, c):
        ba = ba_ref[r0:r0 + c, :]
        beta_all = jax.nn.sigmoid(ba[:, 0:heads])
        a_in = ba[:, heads:2 * heads] + dtb_ref[...]
        softplus = jnp.maximum(a_in, 0.0) + jnp.log(1.0 + jnp.exp(-jnp.abs(a_in)))
        g_all = -jnp.exp(alog_ref[...]) * softplus
        gc_all = _dot_hi(lower_ones, g_all)
        gc_row_dup = _dot_tn_hi(g_all, upper_dup)
        chunks.append(dict(r0=r0, beta=beta_all, gc=gc_all, gc_row=gc_row_dup))

    items = []
    for ch in chunks:
        for g0 in range(0, heads, GROUP):
            hs = list(range(g0, g0 + GROUP))
            qs, ks, vs = [], [], []
            for h in hs:
                q = conv_silu(ch["r0"], h * hd)
                k = conv_silu(ch["r0"], qk_w + h * hd)
                qs.append(q * lax.rsqrt(jnp.sum(q * q, axis=-1, keepdims=True) + L2_EPS) * (hd ** -0.5))
                ks.append(k * lax.rsqrt(jnp.sum(k * k, axis=-1, keepdims=True) + L2_EPS))
                vs.append(conv_silu(ch["r0"], 2 * qk_w + h * hd))
            items.append(dict(ch=ch, hs=hs, qs=qs, ks=ks, vs=vs))

    for it in items:
        ch, hs, qs, ks = it["ch"], it["hs"], it["qs"], it["ks"]
        beta4 = pack_cols(ch["beta"], hs)
        gc_row4 = jnp.concatenate(
            [jnp.where(first_half[0:1], ch["gc_row"][hs[2 * p]:hs[2 * p] + 1],
                       ch["gc_row"][hs[2 * p + 1]:hs[2 * p + 1] + 1])
             for p in range(GROUP // 2)], axis=1)
        diff4 = pack_cols(ch["gc"], hs) - gc_row4
        decay4 = jnp.where(causal4, jnp.exp(jnp.where(causal4, diff4, 0.0)), 0.0)
        lhs = jnp.concatenate([jnp.concatenate(qs, axis=1), jnp.concatenate(ks, axis=1)], axis=0)
        k_bd = jnp.concatenate(
            [jnp.concatenate([ks[i] if ii == i else zeros_head for ii in range(GROUP)], axis=1)
             for i in range(GROUP)], axis=0)
        qkkk = _mm(lhs, k_bd, "nt", PASSES_QK)
        it["qk4"] = qkkk[:c] * decay4
        a4 = jnp.where(strict4, beta4 * qkkk[c:] * decay4, 0.0)
        it["t4"] = eye4 - a4
        it["a4"] = a4

    for it in items:
        it["p4"] = times_block_diag(it["a4"], it["a4"], PASSES_INV)
    n_sq = int(math.log2(c)) - 1
    for step in range(n_sq):
        for it in items:
            if step < n_sq - 1:
                both = times_block_diag(jnp.concatenate([it["p4"], it["t4"]], axis=0), it["p4"], PASSES_INV)
                it["p4"], it["t4"] = both[:c], it["t4"] + both[c:]
            else:
                it["t4"] = it["t4"] + times_block_diag(it["t4"], it["p4"], PASSES_INV)

    for it in items:
        ch, hs, ks, vs = it["ch"], it["hs"], it["ks"], it["vs"]
        rhs = jnp.concatenate(
            [jnp.concatenate([vs[i] * ch["beta"][:, h:h + 1],
                              ks[i] * (ch["beta"][:, h:h + 1] * jnp.exp(ch["gc"][:, h:h + 1]))], axis=1)
             for i, h in enumerate(hs)], axis=0)
        it["sol"] = block_diag_times(it["t4"], rhs, PASSES_INV)

    for ch in chunks:
        group_items = [it for it in items if it["ch"] is ch]
        for it in group_items:
            us, o_inter = [], []
            for i, h in enumerate(it["hs"]):
                sl = slice(i * c, (i + 1) * c)
                ws = _mm(jnp.concatenate([it["sol"][sl, hd:], it["qs"][i] * jnp.exp(ch["gc"][:, h:h + 1])], axis=0),
                         s_ref[h], "nn", PASSES_STATE)
                us.append(it["sol"][sl, :hd] - ws[:c])
                o_inter.append(ws[c:])
            it["us"], it["o_inter"] = us, o_inter
        for it in group_items:
            it["o_intra"] = block_diag_times(it["qk4"], jnp.concatenate(it["us"], axis=0), PASSES_STATE)
        for it in group_items:
            for i, h in enumerate(it["hs"]):
                gc = ch["gc"][:, h:h + 1]
                gc_last = ch["gc_row"][h:h + 1, c - 1:c]
                s_ref[h] = s_ref[h] * jnp.exp(gc_last) + _mm(it["ks"][i] * jnp.exp(gc_last - gc), it["us"][i],
                                                             "tn", PASSES_STATE)
        for it in group_items:
            r0 = ch["r0"]
            for i, h in enumerate(it["hs"]):
                o = it["o_inter"][i] + it["o_intra"][i * c:(i + 1) * c]
                zh = z_ref[r0:r0 + c, h * hd:(h + 1) * hd]
                o_ref[r0:r0 + c, h * hd:(h + 1) * hd] = _rms(o, nw_ref[...]) * _silu(zh)

    xbuf_ref[0:HIST_ROWS, :] = xbuf_ref[tb:tb + HIST_ROWS, :]

    @pl.when(j == pl.num_programs(1) - 1)
    def _():
        sout_ref[0] = s_ref[...]


def _gdn(proj_a, cols, hist, s0, conv_w, a_log, dt_bias, norm_w, *, n_seq, seq_len, block, chunk):
    heads = a_log.shape[1]
    conv_width = conv_w.shape[1]
    v_w = heads * HEAD_DIM
    nblk = seq_len // block
    z_blk, ba_blk = cols["z"] // v_w, cols["ba"] // 128
    shared_hist, shared_s0 = hist.shape[0] == 1, s0.shape[0] == 1
    packed = 2 * chunk == HEAD_DIM and heads % GROUP == 0
    return pl.pallas_call(
        functools.partial(_gdn_packed_kernel if packed else _gdn_kernel, chunk=chunk, heads=heads),
        grid=(n_seq, nblk),
        in_specs=[
            pl.BlockSpec((block, conv_width), lambda n, j: (n * nblk + j, 0)),
            pl.BlockSpec((block, v_w), lambda n, j: (n * nblk + j, z_blk)),
            pl.BlockSpec((block, 128), lambda n, j: (n * nblk + j, ba_blk)),
            pl.BlockSpec((1, HIST_ROWS, conv_width), lambda n, j: (0 if shared_hist else n, 0, 0)),
            pl.BlockSpec((1, heads, HEAD_DIM, HEAD_DIM), lambda n, j: (0 if shared_s0 else n, 0, 0, 0)),
            pl.BlockSpec((DN_CONV, conv_width), lambda n, j: (0, 0)),
            pl.BlockSpec((1, heads), lambda n, j: (0, 0)),
            pl.BlockSpec((1, heads), lambda n, j: (0, 0)),
            pl.BlockSpec((1, HEAD_DIM), lambda n, j: (0, 0)),
        ],
        out_specs=[
            pl.BlockSpec((block, v_w), lambda n, j: (n * nblk + j, 0)),
            pl.BlockSpec((1, heads, HEAD_DIM, HEAD_DIM), lambda n, j: (n, 0, 0, 0)),
        ],
        out_shape=[
            jax.ShapeDtypeStruct((n_seq * seq_len, v_w), F32),
            jax.ShapeDtypeStruct((n_seq, heads, HEAD_DIM, HEAD_DIM), F32),
        ],
        scratch_shapes=[
            pltpu.VMEM((block + HIST_ROWS, conv_width), F32),
            pltpu.VMEM((heads, HEAD_DIM, HEAD_DIM), F32),
        ],
        compiler_params=_params("parallel", "arbitrary"),
        name="gdn",
    )(proj_a, proj_a, proj_a, hist, s0, conv_w, a_log, dt_bias, norm_w)


def _gdn_step_kernel(qkv_ref, z_ref, ba_ref, hist_ref, s0_ref, convw_ref, alog_ref, dtb_ref, nw_ref,
                     o_ref, sout_ref, xbuf_ref, *, heads):
    nb, t, _ = qkv_ref.shape
    hd = HEAD_DIM
    qk_w = heads * hd
    ti = lax.broadcasted_iota(jnp.int32, (t, t), 0)
    tj = lax.broadcasted_iota(jnp.int32, (t, t), 1)
    lower_ones = (ti >= tj).astype(F32)
    upper_ones = (ti <= tj).astype(F32)

    def per_sequence(b, carry):
        xbuf_ref[0:HIST_ROWS, :] = hist_ref[b]
        xbuf_ref[HIST_ROWS:HIST_ROWS + t, :] = qkv_ref[b]
        acc = None
        for tap in range(DN_CONV):
            lo = HIST_ROWS - (DN_CONV - 1) + tap
            term = xbuf_ref[lo:lo + t, :] * convw_ref[tap:tap + 1, :]
            acc = term if acc is None else acc + term
        x = _silu(acc)
        ba = ba_ref[b]
        beta_all = jax.nn.sigmoid(ba[:, 0:heads])
        a_in = ba[:, heads:2 * heads] + dtb_ref[...]
        softplus = jnp.maximum(a_in, 0.0) + jnp.log(1.0 + jnp.exp(-jnp.abs(a_in)))
        g_all = -jnp.exp(alog_ref[...]) * softplus
        gc_all = _dot_hi(lower_ones, g_all)
        gc_row_all = _dot_tn_hi(g_all, upper_ones)
        z = z_ref[b]
        hs = range(heads)
        qs, ks, vs = [], [], []
        for h in hs:
            q = x[:, h * hd:(h + 1) * hd]
            k = x[:, qk_w + h * hd:qk_w + (h + 1) * hd]
            qs.append(q * lax.rsqrt(jnp.sum(q * q, axis=-1, keepdims=True) + L2_EPS) * (hd ** -0.5))
            ks.append(k * lax.rsqrt(jnp.sum(k * k, axis=-1, keepdims=True) + L2_EPS))
            vs.append(x[:, 2 * qk_w + h * hd:2 * qk_w + (h + 1) * hd])
        gcs = [gc_all[:, h:h + 1] for h in hs]
        gc_lasts = [gc_row_all[h:h + 1, t - 1:t] for h in hs]
        kq = [jnp.concatenate([ks[h], qs[h]], axis=0) for h in hs]
        gram = [_mm(kq[h], ks[h], "nt", STEP_PASSES) for h in hs]
        from_state = [_mm(kq[h] * jnp.exp(jnp.concatenate([gcs[h], gcs[h]], axis=0)), s0_ref[b, h], "nn",
                          STEP_PASSES) for h in hs]
        us, outs = [], []
        for h in hs:
            decay = jnp.where(ti >= tj, jnp.exp(jnp.where(ti >= tj, gcs[h] - gc_row_all[h:h + 1, :], 0.0)), 0.0)
            beta = beta_all[:, h:h + 1]
            lower = jnp.where(ti > tj, beta * gram[h][:t] * decay, 0.0)
            u = beta * (vs[h] - from_state[h][:t])
            for j in range(t - 1):
                u = u - lower[:, j:j + 1] * u[j:j + 1, :]
            qk = gram[h][t:] * decay
            o = from_state[h][t:]
            for j in range(t):
                o = o + qk[:, j:j + 1] * u[j:j + 1, :]
            us.append(u)
            outs.append(o)
        for h in hs:
            sout_ref[b, h] = s0_ref[b, h] * jnp.exp(gc_lasts[h]) + _mm(ks[h] * jnp.exp(gc_lasts[h] - gcs[h]), us[h],
                                                                       "tn", STEP_PASSES)
        for h in hs:
            o_ref[b, :, h * hd:(h + 1) * hd] = _rms(outs[h], nw_ref[...]) * _silu(z[:, h * hd:(h + 1) * hd])
        return carry

    lax.fori_loop(0, nb, per_sequence, 0)


def _gdn_step(proj_a3, cols, hist, s0, conv_w, a_log, dt_bias, norm_w, *, seqs_per_step):
    n, t, _ = proj_a3.shape
    heads = a_log.shape[1]
    conv_width = conv_w.shape[1]
    v_w = heads * HEAD_DIM
    nb = seqs_per_step
    z_blk, ba_blk = cols["z"] // v_w, cols["ba"] // 128
    state_spec = pl.BlockSpec((nb, heads, HEAD_DIM, HEAD_DIM), lambda i: (i, 0, 0, 0))
    return pl.pallas_call(
        functools.partial(_gdn_step_kernel, heads=heads),
        grid=(n // nb,),
        in_specs=[
            pl.BlockSpec((nb, t, conv_width), lambda i: (i, 0, 0)),
            pl.BlockSpec((nb, t, v_w), lambda i: (i, 0, z_blk)),
            pl.BlockSpec((nb, t, 128), lambda i: (i, 0, ba_blk)),
            pl.BlockSpec((nb, HIST_ROWS, conv_width), lambda i: (i, 0, 0)),
            state_spec,
            pl.BlockSpec((DN_CONV, conv_width), lambda i: (0, 0)),
            pl.BlockSpec((1, heads), lambda i: (0, 0)),
            pl.BlockSpec((1, heads), lambda i: (0, 0)),
            pl.BlockSpec((1, HEAD_DIM), lambda i: (0, 0)),
        ],
        out_specs=[pl.BlockSpec((nb, t, v_w), lambda i: (i, 0, 0)), state_spec],
        out_shape=[jax.ShapeDtypeStruct((n, t, v_w), F32),
                   jax.ShapeDtypeStruct((n, heads, HEAD_DIM, HEAD_DIM), F32)],
        scratch_shapes=[pltpu.VMEM((HIST_ROWS + t, conv_width), F32)],
        compiler_params=_params("parallel"),
        name="gdn_step",
    )(proj_a3, proj_a3, proj_a3, hist, s0, conv_w, a_log, dt_bias, norm_w)


def _alibi_slope(head, n_heads):
    return 2.0 ** (-8.0 * (head + 1) / n_heads)


def _penalty(dist, mask):
    return jnp.where(mask, jnp.minimum(dist, WINDOW).astype(F32), -NEG_BIG)


def _attend(jobs, n_rows):
    scale = HEAD_DIM ** -0.5
    hd = HEAD_DIM
    scores = [[_dot_nt(q4, k) * scale for k, _, _ in segs] for q4, segs, _, _ in jobs]
    v_ones = [[jnp.concatenate([v, jnp.ones_like(v)], axis=1) for _, v, _ in segs] for _, segs, _, _ in jobs]
    heads = [(ji, g) for ji in range(len(jobs)) for g in range(len(jobs[ji][3]))]
    logits, maxes = {}, {}
    for ji, g in heads:
        _, segs, _, slopes = jobs[ji]
        r = slice(g * n_rows, (g + 1) * n_rows)
        logits[ji, g] = [sc[r] - slopes[g] * pen for sc, (_, _, pen) in zip(scores[ji], segs)]
    for ji, g in heads:
        by_width = {}
        for lg in logits[ji, g]:
            w = lg.shape[1]
            by_width[w] = lg if w not in by_width else jnp.maximum(by_width[w], lg)
        m = jobs[ji][2][g]
        for lg in by_width.values():
            m = jnp.maximum(m, jnp.max(lg, axis=-1, keepdims=True))
        maxes[ji, g] = m
    accs = {}
    for ji, g in heads:
        acc = None
        for lg, v1 in zip(logits[ji, g], v_ones[ji]):
            pv = _dot(jnp.exp(lg - maxes[ji, g]).astype(BF16), v1)
            acc = pv if acc is None else acc + pv
        accs[ji, g] = acc
    outs = [[None] * len(job[3]) for job in jobs]
    for ji, g in heads:
        acc = accs[ji, g]
        outs[ji][g] = acc[:, :hd] / (acc[:, hd:] + jnp.exp(jobs[ji][2][g] - maxes[ji, g]))
    return outs


def _swa_prompt_kernel(q_ref, kv_ref, kvprev_ref, kvmeta_ref, sinks_ref, o_ref, *, n_meta, kv_heads):
    j = pl.program_id(1)
    tq = q_ref.shape[0]
    hd = HEAD_DIM
    kv_w = kv_heads * hd
    n_heads = kv_heads * SWA_GROUP
    qi = lax.broadcasted_iota(jnp.int32, (tq, tq), 0)
    ki = lax.broadcasted_iota(jnp.int32, (tq, tq), 1)
    dist_own = qi - ki
    dist_prev = dist_own + tq
    pen_own = _penalty(dist_own, dist_own >= 0)
    pen_prev = _penalty(dist_prev, (dist_prev <= WINDOW) & (j > 0))
    qpos = n_meta + j * tq + lax.broadcasted_iota(jnp.int32, (tq, n_meta), 0)
    dist_meta = qpos - lax.broadcasted_iota(jnp.int32, (tq, n_meta), 1)
    pen_meta = _penalty(dist_meta, dist_meta >= 0)
    jobs, job_heads = [], []
    for kvh in range(kv_heads):
        ks = slice(kvh * hd, (kvh + 1) * hd)
        vs = slice(kv_w + kvh * hd, kv_w + (kvh + 1) * hd)
        segments = [
            (kvmeta_ref[:, ks].astype(BF16), kvmeta_ref[:, vs].astype(BF16), pen_meta),
            (kvprev_ref[:, ks].astype(BF16), kvprev_ref[:, vs].astype(BF16), pen_prev),
            (kv_ref[:, ks].astype(BF16), kv_ref[:, vs].astype(BF16), pen_own),
        ]
        heads = [kvh * SWA_GROUP + g for g in range(SWA_GROUP)]
        q4 = jnp.concatenate([q_ref[:, hh * hd:(hh + 1) * hd] for hh in heads], axis=0).astype(BF16)
        jobs.append((q4, segments, [sinks_ref[:, hh:hh + 1] for hh in heads],
                     [_alibi_slope(hh, n_heads) for hh in heads]))
        job_heads.append(heads)
    for heads, outs in zip(job_heads, _attend(jobs, tq)):
        for hh, o in zip(heads, outs):
            o_ref[:, hh * hd:(hh + 1) * hd] = o


def _swa_prompt(proj_b, proj_b_meta, cols, sinks, *, n_seq, seq_len, n_meta):
    n_heads = sinks.shape[1]
    kv_heads = n_heads // SWA_GROUP
    q_w, kv_w2 = n_heads * HEAD_DIM, 2 * kv_heads * HEAD_DIM
    tq = WINDOW
    nblk = seq_len // tq
    kv_blk = cols["skv"] // kv_w2
    return pl.pallas_call(
        functools.partial(_swa_prompt_kernel, n_meta=n_meta, kv_heads=kv_heads),
        grid=(n_seq, nblk),
        in_specs=[
            pl.BlockSpec((tq, q_w), lambda n, j: (n * nblk + j, 0)),
            pl.BlockSpec((tq, kv_w2), lambda n, j: (n * nblk + j, kv_blk)),
            pl.BlockSpec((tq, kv_w2), lambda n, j: (n * nblk + jnp.maximum(j - 1, 0), kv_blk)),
            pl.BlockSpec((n_meta, kv_w2), lambda n, j: (0, kv_blk)),
            pl.BlockSpec((1, n_heads), lambda n, j: (0, 0)),
        ],
        out_specs=pl.BlockSpec((tq, q_w), lambda n, j: (n * nblk + j, 0)),
        out_shape=jax.ShapeDtypeStruct((n_seq * seq_len, q_w), F32),
        compiler_params=_params("parallel", "arbitrary"),
        name="swa_prompt",
    )(proj_b, proj_b, proj_b, proj_b_meta, sinks)


def _swa_step_kernel(q_ref, kvnew_ref, kmeta_ref, vmeta_ref, kbuf_ref, vbuf_ref, sinks_ref,
                     o_ref, kout_ref, vout_ref, *, kv_heads):
    nb, t, _ = q_ref.shape
    n_meta, w = kmeta_ref.shape[1], kbuf_ref.shape[1]
    hd = HEAD_DIM
    kv_w = kv_heads * hd
    n_heads = kv_heads * SWA_GROUP

    def grid2(rows, cols_):
        return (lax.broadcasted_iota(jnp.int32, (rows, cols_), 0), lax.broadcasted_iota(jnp.int32, (rows, cols_), 1))

    ti, mi = grid2(t, n_meta)
    dist_meta = PAST_LEN + ti - mi
    pen_meta = _penalty(dist_meta, dist_meta >= 0)
    ti, bi = grid2(t, w)
    dist_buf = w + ti - bi
    pen_buf = _penalty(dist_buf, (dist_buf <= WINDOW) & (PAST_LEN - w + bi >= n_meta))
    ti, si = grid2(t, t)
    dist_new = ti - si
    pen_new = _penalty(dist_new, dist_new >= 0)
    jobs, job_dst = [], []
    for b in range(nb):
        kv_new = kvnew_ref[b]
        for kvh in range(kv_heads):
            ks = slice(kvh * hd, (kvh + 1) * hd)
            k_old, v_old = kbuf_ref[b, :, kvh, :], vbuf_ref[b, :, kvh, :]
            k_new, v_new = kv_new[:, ks], kv_new[:, kv_w + kvh * hd:kv_w + (kvh + 1) * hd]
            kout_ref[b, 0:w - t, kvh, :] = k_old[t:w]
            kout_ref[b, w - t:w, kvh, :] = k_new
            vout_ref[b, 0:w - t, kvh, :] = v_old[t:w]
            vout_ref[b, w - t:w, kvh, :] = v_new
            segments = [
                (kmeta_ref[b, :, ks].astype(BF16), vmeta_ref[b, :, ks].astype(BF16), pen_meta),
                (k_old.astype(BF16), v_old.astype(BF16), pen_buf),
                (k_new.astype(BF16), v_new.astype(BF16), pen_new),
            ]
            heads = [kvh * SWA_GROUP + g for g in range(SWA_GROUP)]
            q4 = jnp.concatenate([q_ref[b, :, hh * hd:(hh + 1) * hd] for hh in heads], axis=0).astype(BF16)
            jobs.append((q4, segments, [sinks_ref[:, hh:hh + 1] for hh in heads],
                         [_alibi_slope(hh, n_heads) for hh in heads]))
            job_dst.append((b, heads))
    for (b, heads), outs in zip(job_dst, _attend(jobs, t)):
        for hh, o in zip(heads, outs):
            o_ref[b, :, hh * hd:(hh + 1) * hd] = o


def _swa_step(proj_b3, cols, k_meta, v_meta, k_buf, v_buf, sinks, *, seqs_per_step):
    n, t, _ = proj_b3.shape
    n_heads = sinks.shape[1]
    kv_heads = n_heads // SWA_GROUP
    q_w, kv_w = n_heads * HEAD_DIM, kv_heads * HEAD_DIM
    n_meta, w = k_meta.shape[1], k_buf.shape[1]
    nb = seqs_per_step
    kv_blk = cols["skv"] // (2 * kv_w)
    seq3 = lambda rows, width, blk=0: pl.BlockSpec((nb, rows, width), lambda i: (i, 0, blk))
    cache_spec = pl.BlockSpec((nb, w, kv_heads, HEAD_DIM), lambda i: (i, 0, 0, 0))
    return pl.pallas_call(
        functools.partial(_swa_step_kernel, kv_heads=kv_heads),
        grid=(n // nb,),
        in_specs=[seq3(t, q_w), seq3(t, 2 * kv_w, kv_blk), seq3(n_meta, kv_w), seq3(n_meta, kv_w),
                  cache_spec, cache_spec, pl.BlockSpec((1, n_heads), lambda i: (0, 0))],
        out_specs=[seq3(t, q_w), cache_spec, cache_spec],
        out_shape=[jax.ShapeDtypeStruct((n, t, q_w), F32),
                   jax.ShapeDtypeStruct((n, w, kv_heads, HEAD_DIM), F32),
                   jax.ShapeDtypeStruct((n, w, kv_heads, HEAD_DIM), F32)],
        compiler_params=_params("parallel"),
        name="swa_step",
    )(proj_b3, proj_b3, k_meta, v_meta, k_buf, v_buf, sinks)


def _split_w_in(w_in, heads, n_heads, kv_heads, d):
    conv_w = 3 * heads * HEAD_DIM
    v_w = heads * HEAD_DIM
    q_w, kv_w = n_heads * HEAD_DIM, kv_heads * HEAD_DIM
    sizes = (conv_w, v_w, heads, heads, q_w, kv_w, kv_w, d, d)
    offs = [0]
    for s in sizes:
        offs.append(offs[-1] + s)
    part = lambda i: w_in[:, offs[i]:offs[i + 1]].astype(BF16)
    w_a = jnp.concatenate([part(0), part(1), part(7), part(2), part(3),
                           jnp.zeros((w_in.shape[0], 128 - 2 * heads), BF16)], axis=1)
    w_b = jnp.concatenate([part(4), part(8), part(5), part(6)], axis=1)
    cols_a = {"qkv": 0, "z": conv_w, "g_dn": conv_w + v_w, "ba": conv_w + v_w + d}
    cols_b = {"q": 0, "g_swa": q_w, "skv": q_w + d}
    return w_a, w_b, cols_a, cols_b


def kernel(x_prompt, x_sample, state_dn_conv, state_dn_ssm, cache_swa_meta_k, cache_swa_meta_v, cache_swa_k, cache_swa_v, meta_tokens, ffn1_norm_pre, ffn1_norm_post, ffn1_w_gate, ffn1_w_up, ffn1_w_down, mix_norm_pre, mix_norm_post, w_in, dn_conv_w, dn_a_log, dn_dt_bias, dn_norm_w, swa_sinks, w_out, ffn2_norm_pre, ffn2_norm_post, ffn2_w_gate, ffn2_w_up, ffn2_w_down):
    assert w_in.shape[0] == 1, "single-layer step"
    n_p, seq, d = x_prompt.shape
    n_s, t_s, _ = x_sample.shape
    n_meta = meta_tokens.shape[0]
    heads = dn_a_log.shape[1]
    n_heads = swa_sinks.shape[1]
    kv_heads = n_heads // SWA_GROUP
    conv_w = dn_conv_w.shape[2]
    kv_w = kv_heads * HEAD_DIM
    w_keep = cache_swa_k.shape[2]
    assert w_keep == WINDOW and seq % WINDOW == 0 and seq % DN_CHUNK == 0 and seq >= WINDOW

    w_a, w_b, cols_a, cols_b = _split_w_in(w_in[0], heads, n_heads, kv_heads, d)
    cols = {**cols_a, **cols_b}
    f1 = (ffn1_norm_pre, ffn1_norm_post, ffn1_w_gate[0].astype(BF16), ffn1_w_up[0].astype(BF16),
          ffn1_w_down[0].astype(BF16))
    f2 = (ffn2_norm_pre, ffn2_norm_post, ffn2_w_gate[0].astype(BF16), ffn2_w_up[0].astype(BF16),
          ffn2_w_down[0].astype(BF16))
    w_o = w_out[0].astype(BF16)
    gdn_w = (dn_conv_w[0], dn_a_log, dn_dt_bias, dn_norm_w)

    def front(h):
        h1 = _ffn(h, *f1)
        return h1, _proj(h1, mix_norm_pre, w_a), _proj(h1, mix_norm_pre, w_b)

    def back(h1, o_dn, o_sw, pa, pb):
        return _mixffn(h1, o_dn, o_sw, pa, pb, cols, w_o, mix_norm_post, *f2)

    def hist_tile(rows3):
        return jnp.pad(rows3, ((0, 0), (HIST_ROWS - (DN_CONV - 1), 0), (0, 0)))

    _, pa_m, pb_m = front(meta_tokens)
    zero_hist = jnp.zeros((1, HIST_ROWS, conv_w), F32)
    zero_state = jnp.zeros((1, heads, HEAD_DIM, HEAD_DIM), F32)
    _, s_meta = _gdn(pa_m, cols, zero_hist, zero_state, *gdn_w, n_seq=1, seq_len=n_meta, block=n_meta,
                     chunk=n_meta)
    hist_meta = hist_tile(pa_m[None, n_meta - (DN_CONV - 1):, :conv_w])

    h1_p, pa_p, pb_p = front(x_prompt.reshape(n_p * seq, d))
    o_dn_p, s_p = _gdn(pa_p, cols, hist_meta, s_meta, *gdn_w, n_seq=n_p, seq_len=seq, block=4 * DN_CHUNK,
                       chunk=DN_CHUNK)
    o_sw_p = _swa_prompt(pb_p, pb_m, cols, swa_sinks, n_seq=n_p, seq_len=seq, n_meta=n_meta)
    y_prompt = back(h1_p, o_dn_p, o_sw_p, pa_p, pb_p).reshape(n_p, seq, d)

    pa_p3 = pa_p.reshape(n_p, seq, -1)
    pb_p3 = pb_p.reshape(n_p, seq, -1)
    p_conv = pa_p3[:, seq - (DN_CONV - 1):, :conv_w]
    kv_meta = pb_m[:, cols["skv"]:]
    p_meta_k = jnp.broadcast_to(kv_meta[None, :, :kv_w], (n_p, n_meta, kv_w))
    p_meta_v = jnp.broadcast_to(kv_meta[None, :, kv_w:], (n_p, n_meta, kv_w))
    p_win_k = pb_p3[:, seq - w_keep:, cols["skv"]:cols["skv"] + kv_w]
    p_win_v = pb_p3[:, seq - w_keep:, cols["skv"] + kv_w:]

    h1_s, pa_s, pb_s = front(x_sample.reshape(n_s * t_s, d))
    o_dn_s, s_s = _gdn_step(pa_s.reshape(n_s, t_s, -1), cols, hist_tile(state_dn_conv[0]), state_dn_ssm[0],
                            *gdn_w, seqs_per_step=4)
    o_dn_s = o_dn_s.reshape(n_s * t_s, -1)
    o_sw_s, s_win_k, s_win_v = _swa_step(
        pb_s.reshape(n_s, t_s, -1), cols,
        cache_swa_meta_k[0].reshape(n_s, n_meta, kv_w), cache_swa_meta_v[0].reshape(n_s, n_meta, kv_w),
        cache_swa_k[0], cache_swa_v[0], swa_sinks, seqs_per_step=4)
    y_sample = back(h1_s, o_dn_s, o_sw_s.reshape(n_s * t_s, -1), pa_s, pb_s).reshape(n_s, t_s, d)
    assert t_s >= DN_CONV - 1, "new conv state is taken from the new rows alone"
    s_conv = pa_s.reshape(n_s, t_s, -1)[:, t_s - (DN_CONV - 1):, :conv_w]

    def kv4(x):
        return x.reshape(1, x.shape[0], x.shape[1], kv_heads, HEAD_DIM)

    return (y_prompt, y_sample, p_conv[None], s_p[None], kv4(p_meta_k), kv4(p_meta_v), kv4(p_win_k),
            kv4(p_win_v), s_conv[None], s_s[None], s_win_k[None], s_win_v[None])
```

```python
import functools
import math

import jax
import jax.numpy as jnp
from jax import lax
from jax.experimental import pallas as pl
from jax.experimental.pallas import tpu as pltpu

F32 = jnp.float32
BF16 = jnp.bfloat16
HIGHEST = lax.Precision.HIGHEST

RMS_EPS = 1e-6
L2_EPS = 1e-6
WINDOW = 128
PAST_LEN = 8192
HEAD_DIM = 128
SWA_GROUP = 4
DN_CONV = 4
DN_CHUNK = 64
HIST_ROWS = 8

VMEM_LIMIT_BYTES = 56 * 1024 * 1024
ROW_TILE = 512
MXU_WIDTH = 256
NEG_BIG = -1e30


def _rms(x, g):
    return x * lax.rsqrt(jnp.mean(x * x, axis=-1, keepdims=True) + RMS_EPS) * g


def _silu(x):
    return x * jax.nn.sigmoid(x)


def _dot(a, b):
    return jnp.dot(a, b, preferred_element_type=F32)


def _dot_hi(a, b):
    return jnp.dot(a, b, preferred_element_type=F32, precision=HIGHEST)


def _dot_nt_hi(a, b):
    return lax.dot_general(a, b, (((1,), (1,)), ((), ())), preferred_element_type=F32, precision=HIGHEST)


def _dot_tn_hi(a, b):
    return lax.dot_general(a, b, (((0,), (0,)), ((), ())), preferred_element_type=F32, precision=HIGHEST)


def _dot_nt(a, b):
    return lax.dot_general(a, b, (((1,), (1,)), ((), ())), preferred_element_type=F32)


def _dot_tn(a, b):
    return lax.dot_general(a, b, (((0,), (0,)), ((), ())), preferred_element_type=F32)


def _split_bf16(a):
    hi = a.astype(BF16)
    return hi, (a - hi.astype(F32)).astype(BF16)


_DOTS = {"nn": _dot, "nt": _dot_nt, "tn": _dot_tn}


def _dot3_split(a_pair, b_pair, kind="nn"):
    (a_hi, a_lo), (b_hi, b_lo) = a_pair, b_pair
    dot = _DOTS[kind]
    axis = 1 if kind == "tn" else 0
    m = a_hi.shape[axis]
    top = dot(jnp.concatenate([a_hi, a_lo], axis=axis), b_hi)
    return top[:m] + top[m:] + dot(a_hi, b_lo)


def _mm(a, b, kind="nn", passes=3):
    if passes == 1:
        return _DOTS[kind](a.astype(BF16), b.astype(BF16))
    return _dot3_split(_split_bf16(a), _split_bf16(b), kind)


def _row_tile(rows):
    return ROW_TILE if rows % ROW_TILE == 0 else rows


def _resident(shape):
    return pl.BlockSpec(shape, lambda *_: (0,) * len(shape), pipeline_mode=pl.Buffered(1))


def _params(*semantics):
    return pltpu.CompilerParams(dimension_semantics=semantics, vmem_limit_bytes=VMEM_LIMIT_BYTES)


def _ffn_tile(h, g_pre, g_post, wg_ref, wu_ref, wd_ref, ff_chunk):
    xn = _rms(h, g_pre).astype(BF16)
    acc = None
    c0 = 0
    for width in ff_chunk:
        gate = _dot(xn, wg_ref[:, c0:c0 + width])
        up = _dot(xn, wu_ref[:, c0:c0 + width])
        part = _dot((_silu(gate) * up).astype(BF16), wd_ref[c0:c0 + width, :])
        acc = part if acc is None else acc + part
        c0 += width
    return h + 0.5 * _rms(acc, g_post)


def _ff_chunk(d_ff):
    tiles, rem = divmod(d_ff, MXU_WIDTH)
    if rem or tiles < 2:
        return (d_ff,)
    first = (tiles + 1) // 2
    return (first * MXU_WIDTH, (tiles - first) * MXU_WIDTH)


def _ffn_kernel(h_ref, gpre_ref, gpost_ref, wg_ref, wu_ref, wd_ref, o_ref, *, ff_chunk):
    o_ref[...] = _ffn_tile(h_ref[...], gpre_ref[...], gpost_ref[...], wg_ref, wu_ref, wd_ref, ff_chunk)


def _ffn(h, g_pre, g_post, wg, wu, wd):
    rows, d = h.shape
    d_ff = wg.shape[1]
    tm = _row_tile(rows)
    row_spec = pl.BlockSpec((tm, d), lambda i: (i, 0))
    return pl.pallas_call(
        functools.partial(_ffn_kernel, ff_chunk=_ff_chunk(d_ff)),
        grid=(rows // tm,),
        in_specs=[row_spec, _resident((1, d)), _resident((1, d)),
                  _resident((d, d_ff)), _resident((d, d_ff)), _resident((d_ff, d))],
        out_specs=row_spec,
        out_shape=jax.ShapeDtypeStruct((rows, d), F32),
        compiler_params=_params("parallel"),
        name="ffn",
    )(h, g_pre, g_post, wg, wu, wd)


def _proj_kernel(h_ref, g_ref, w_ref, o_ref):
    o_ref[...] = _dot(_rms(h_ref[...], g_ref[...]).astype(BF16), w_ref[...])


def _proj(h, g, w):
    rows, d = h.shape
    n = w.shape[1]
    tm = _row_tile(rows)
    return pl.pallas_call(
        _proj_kernel,
        grid=(rows // tm,),
        in_specs=[pl.BlockSpec((tm, d), lambda i: (i, 0)), _resident((1, d)), _resident((d, n))],
        out_specs=pl.BlockSpec((tm, n), lambda i: (i, 0)),
        out_shape=jax.ShapeDtypeStruct((rows, n), F32),
        compiler_params=_params("parallel"),
        name="proj",
    )(h, g, w)


def _proj_conv_kernel(h_ref, g_ref, w_ref, hist_ref, convw_ref, o_ref, tail_ref, ybuf_ref, *, conv_width, qk_width):
    j = pl.program_id(1)
    tm = h_ref.shape[0]
    hd = HEAD_DIM
    xn = _rms(h_ref[...], g_ref[...]).astype(BF16)

    @pl.when(j == 0)
    def _():
        ybuf_ref[0:HIST_ROWS, :] = hist_ref[0]

    @pl.when(j > 0)
    def _():
        ybuf_ref[0:HIST_ROWS, :] = tail_ref[0]

    def project_raw(c0, width):
        ybuf_ref[HIST_ROWS:HIST_ROWS + tm, c0:c0 + width] = _dot(xn, w_ref[:, c0:c0 + width])

    def conv_group(g0, width):
        tail_ref[0, :, g0:g0 + width] = ybuf_ref[tm:tm + HIST_ROWS, g0:g0 + width]
        for c0 in range(g0, g0 + width, hd):
            x = ybuf_ref[:, c0:c0 + hd]
            acc = x[HIST_ROWS:] * convw_ref[DN_CONV - 1:DN_CONV, c0:c0 + hd]
            for back in range(1, DN_CONV):
                shifted = pltpu.roll(x, back, 0)[HIST_ROWS:]
                acc = acc + shifted * convw_ref[DN_CONV - 1 - back:DN_CONV - back, c0:c0 + hd]
            y = _silu(acc)
            if c0 < 2 * qk_width:
                y = y * lax.rsqrt(jnp.sum(y * y, axis=-1, keepdims=True) + L2_EPS)
                if c0 < qk_width:
                    y = y * (hd ** -0.5)
            o_ref[:, c0:c0 + hd] = y

    group = 2 * MXU_WIDTH
    starts = list(range(0, conv_width, group))
    n_rest = w_ref.shape[1] - conv_width
    rest_cuts = ([conv_width + (n_rest * i // len(starts)) // MXU_WIDTH * MXU_WIDTH for i in range(len(starts))]
                 + [w_ref.shape[1]])
    project_raw(starts[0], group)
    for gi, g0 in enumerate(starts):
        if gi + 1 < len(starts):
            project_raw(starts[gi + 1], group)
        r0, r1 = rest_cuts[gi], rest_cuts[gi + 1]
        if r1 > r0:
            o_ref[:, r0:r1] = _dot(xn, w_ref[:, r0:r1])
        conv_group(g0, group)


def _proj_conv(h, g, w, hist, conv_w, *, n_seq, seq_len, qk_width):
    rows, d = h.shape
    n = w.shape[1]
    conv_width = conv_w.shape[1]
    tm = _row_tile(seq_len)
    tiles = seq_len // tm
    shared_hist = hist.shape[0] == 1
    return pl.pallas_call(
        functools.partial(_proj_conv_kernel, conv_width=conv_width, qk_width=qk_width),
        grid=(n_seq, tiles),
        in_specs=[pl.BlockSpec((tm, d), lambda s, j: (s * tiles + j, 0)), _resident((1, d)), _resident((d, n)),
                  pl.BlockSpec((1, HIST_ROWS, conv_width), lambda s, j: (0 if shared_hist else s, 0, 0)),
                  _resident((DN_CONV, conv_width))],
        out_specs=[pl.BlockSpec((tm, n), lambda s, j: (s * tiles + j, 0)),
                   pl.BlockSpec((1, HIST_ROWS, conv_width), lambda s, j: (s, 0, 0))],
        out_shape=[jax.ShapeDtypeStruct((rows, n), F32),
                   jax.ShapeDtypeStruct((n_seq, HIST_ROWS, conv_width), F32)],
        scratch_shapes=[pltpu.VMEM((HIST_ROWS + tm, conv_width), F32)],
        compiler_params=_params("parallel", "arbitrary"),
        name="proj_conv",
    )(h, g, w, hist, conv_w)


def _mixffn_kernel(h_ref, odn_ref, osw_ref, gdn_ref, gsw_ref, wout_ref, gmix_ref,
                   gpre_ref, gpost_ref, wg_ref, wu_ref, wd_ref, o_ref, *, ff_chunk):
    y = jax.nn.sigmoid(gdn_ref[...]) * odn_ref[...] + jax.nn.sigmoid(gsw_ref[...]) * osw_ref[...]
    h2 = h_ref[...] + _rms(_dot(y.astype(BF16), wout_ref[...]), gmix_ref[...])
    o_ref[...] = _ffn_tile(h2, gpre_ref[...], gpost_ref[...], wg_ref, wu_ref, wd_ref, ff_chunk)


def _mixffn(h, o_dn, o_sw, proj_a, proj_b, cols, w_out, g_mix, g_pre, g_post, wg, wu, wd):
    rows, d = h.shape
    d_ff = wg.shape[1]
    tm = _row_tile(rows)
    row_spec = pl.BlockSpec((tm, d), lambda i: (i, 0))
    gdn_blk, gsw_blk = cols["g_dn"] // d, cols["g_swa"] // d
    return pl.pallas_call(
        functools.partial(_mixffn_kernel, ff_chunk=_ff_chunk(d_ff)),
        grid=(rows // tm,),
        in_specs=[row_spec, row_spec, row_spec,
                  pl.BlockSpec((tm, d), lambda i: (i, gdn_blk)),
                  pl.BlockSpec((tm, d), lambda i: (i, gsw_blk)),
                  _resident((d, d)), _resident((1, d)), _resident((1, d)), _resident((1, d)),
                  _resident((d, d_ff)), _resident((d, d_ff)), _resident((d_ff, d))],
        out_specs=row_spec,
        out_shape=jax.ShapeDtypeStruct((rows, d), F32),
        compiler_params=_params("parallel"),
        name="mixffn",
    )(h, o_dn, o_sw, proj_a, proj_b, w_out, g_mix, g_pre, g_post, wg, wu, wd)


def _unit_lower_inverse(a, eye):
    c = a.shape[0]
    t = eye - a
    p = a
    for _ in range(int(math.log2(c)) - 1):
        p = _dot_hi(p, p)
        t = t + _dot_hi(t, p)
    return t


def _gdn_kernel(qkv_ref, z_ref, ba_ref, hist_ref, s0_ref, convw_ref, alog_ref, dtb_ref, nw_ref,
                o_ref, sout_ref, xbuf_ref, s_ref, *, chunk, heads):
    j = pl.program_id(1)
    tb = qkv_ref.shape[0]
    hd = HEAD_DIM
    qk_w = heads * hd

    @pl.when(j == 0)
    def _():
        xbuf_ref[0:HIST_ROWS, :] = hist_ref[0]
        s_ref[...] = s0_ref[0]

    xbuf_ref[HIST_ROWS:HIST_ROWS + tb, :] = qkv_ref[...]

    rows = lax.broadcasted_iota(jnp.int32, (chunk, chunk), 0)
    cols = lax.broadcasted_iota(jnp.int32, (chunk, chunk), 1)
    causal = rows >= cols
    strict = rows > cols
    eye = (rows == cols).astype(F32)
    lower_ones = causal.astype(F32)

    def conv_silu(r0, c0):
        acc = None
        for tap in range(DN_CONV):
            lo = HIST_ROWS + r0 - (DN_CONV - 1) + tap
            term = xbuf_ref[lo:lo + chunk, c0:c0 + hd] * convw_ref[tap:tap + 1, c0:c0 + hd]
            acc = term if acc is None else acc + term
        return _silu(acc)

    for r0 in range(0, tb, chunk):
        ba = ba_ref[r0:r0 + chunk, :]
        beta_all = jax.nn.sigmoid(ba[:, 0:heads])
        a_in = ba[:, heads:2 * heads] + dtb_ref[...]
        softplus = jnp.maximum(a_in, 0.0) + jnp.log(1.0 + jnp.exp(-jnp.abs(a_in)))
        g_all = -jnp.exp(alog_ref[...]) * softplus
        gc_col_all = _dot_hi(lower_ones, g_all)
        gc_row_all = _dot_tn_hi(g_all, (rows <= cols).astype(F32))
        for h in range(heads):
            q = conv_silu(r0, h * hd)
            k = conv_silu(r0, qk_w + h * hd)
            v = conv_silu(r0, 2 * qk_w + h * hd)
            q = q * lax.rsqrt(jnp.sum(q * q, axis=-1, keepdims=True) + L2_EPS) * (hd ** -0.5)
            k = k * lax.rsqrt(jnp.sum(k * k, axis=-1, keepdims=True) + L2_EPS)
            beta = beta_all[:, h:h + 1]
            gc = gc_col_all[:, h:h + 1]
            gc_row = gc_row_all[h:h + 1, :]
            gc_last = gc_row[:, chunk - 1:chunk]
            decay = jnp.where(causal, jnp.exp(jnp.where(causal, gc - gc_row, 0.0)), 0.0)
            a_mat = jnp.where(strict, beta * _dot_nt_hi(k, k) * decay, 0.0)
            t_inv = _unit_lower_inverse(a_mat, eye)
            rhs = jnp.concatenate([v * beta, k * (beta * jnp.exp(gc))], axis=-1)
            sol = _dot_hi(t_inv, rhs)
            u_base, w = sol[:, :hd], sol[:, hd:]
            qk = _dot_nt_hi(q, k) * decay
            s = s_ref[h]
            u = u_base - _dot_hi(w, s)
            o = _dot_hi(q * jnp.exp(gc), s) + _dot_hi(qk, u)
            s_ref[h] = s * jnp.exp(gc_last) + _dot_tn_hi(k * jnp.exp(gc_last - gc), u)
            zh = z_ref[r0:r0 + chunk, h * hd:(h + 1) * hd]
            o_ref[r0:r0 + chunk, h * hd:(h + 1) * hd] = _rms(o, nw_ref[...]) * _silu(zh)

    xbuf_ref[0:HIST_ROWS, :] = xbuf_ref[tb:tb + HIST_ROWS, :]

    @pl.when(j == pl.num_programs(1) - 1)
    def _():
        sout_ref[0] = s_ref[...]


GROUP = 4
PASSES_QK = 1
PASSES_STATE = 1
PASSES_INV = 3
STEP_PASSES = 3


def _gdn_packed_kernel(qkv_ref, z_ref, ba_ref, s0_ref, alog_ref, dtb_ref, nw_ref,
                       o_ref, sout_ref, s_ref, *, chunk, heads):
    j = pl.program_id(1)
    tb = qkv_ref.shape[0]
    hd = HEAD_DIM
    c = chunk
    gw = GROUP * c
    qk_w = heads * hd

    @pl.when(j == 0)
    def _():
        s_ref[...] = s0_ref[0]

    def iota(shape, dim):
        return lax.broadcasted_iota(jnp.int32, shape, dim)

    row = iota((c, gw), 0)
    col = iota((c, gw), 1) & (c - 1)
    causal4, strict4 = row >= col, row > col
    eye4 = (row == col).astype(F32)
    bd_mask = (iota((gw, gw), 0) // c) == (iota((gw, gw), 1) // c)
    lower_ones = (iota((c, c), 0) >= iota((c, c), 1)).astype(F32)
    upper_dup = (iota((c, 2 * c), 0) <= (iota((c, 2 * c), 1) & (c - 1))).astype(F32)
    first_half = iota((c, 2 * c), 1) < c
    zeros_head = jnp.zeros((c, hd), F32)

    def block_diag(x4):
        return jnp.where(bd_mask, jnp.concatenate([x4] * GROUP, axis=0), jnp.zeros((), x4.dtype))

    def times_block_diag(lhs, x4, passes):
        if passes == 1:
            return _dot(lhs.astype(BF16), block_diag(x4.astype(BF16)))
        hi, lo = _split_bf16(x4)
        return _dot3_split(_split_bf16(lhs), (block_diag(hi), block_diag(lo)))

    def block_diag_times(x4, rhs, passes):
        if passes == 1:
            return _dot(block_diag(x4.astype(BF16)), rhs.astype(BF16))
        hi, lo = _split_bf16(x4)
        return _dot3_split((block_diag(hi), block_diag(lo)), _split_bf16(rhs))

    def pack_cols(col_all, hs):
        halves = [jnp.where(first_half,
                            jnp.broadcast_to(col_all[:, hs[2 * p]:hs[2 * p] + 1], (c, 2 * c)),
                            jnp.broadcast_to(col_all[:, hs[2 * p + 1]:hs[2 * p + 1] + 1], (c, 2 * c)))
                  for p in range(GROUP // 2)]
        return jnp.concatenate(halves, axis=1)

    chunks = []
    for r0 in range(0, tb, c):
        ba = ba_ref[r0:r0 + c, :]
        beta_all = jax.nn.sigmoid(ba[:, 0:heads])
        a_in = ba[:, heads:2 * heads] + dtb_ref[...]
        softplus = jnp.maximum(a_in, 0.0) + jnp.log(1.0 + jnp.exp(-jnp.abs(a_in)))
        g_all = -jnp.exp(alog_ref[...]) * softplus
        gc_all = _dot_hi(lower_ones, g_all)
        gc_row_dup = _dot_tn_hi(g_all, upper_dup)
        chunks.append(dict(r0=r0, beta=beta_all, gc=gc_all, gc_row=gc_row_dup))

    items = []
    for ch in chunks:
        for g0 in range(0, heads, GROUP):
            hs = list(range(g0, g0 + GROUP))
            qs, ks, vs = [], [], []
            rows = slice(ch["r0"], ch["r0"] + c)
            for h in hs:
                qs.append(qkv_ref[rows, h * hd:(h + 1) * hd])
                ks.append(qkv_ref[rows, qk_w + h * hd:qk_w + (h + 1) * hd])
                vs.append(qkv_ref[rows, 2 * qk_w + h * hd:2 * qk_w + (h + 1) * hd])
            items.append(dict(ch=ch, hs=hs, qs=qs, ks=ks, vs=vs))

    for it in items:
        ch, hs, qs, ks = it["ch"], it["hs"], it["qs"], it["ks"]
        beta4 = pack_cols(ch["beta"], hs)
        gc_row4 = jnp.concatenate(
            [jnp.where(first_half[0:1], ch["gc_row"][hs[2 * p]:hs[2 * p] + 1],
                       ch["gc_row"][hs[2 * p + 1]:hs[2 * p + 1] + 1])
             for p in range(GROUP // 2)], axis=1)
        diff4 = pack_cols(ch["gc"], hs) - gc_row4
        decay4 = jnp.where(causal4, jnp.exp(jnp.where(causal4, diff4, 0.0)), 0.0)
        lhs = jnp.concatenate([jnp.concatenate(qs, axis=1), jnp.concatenate(ks, axis=1)], axis=0)
        k_bd = jnp.concatenate(
            [jnp.concatenate([ks[i] if ii == i else zeros_head for ii in range(GROUP)], axis=1)
             for i in range(GROUP)], axis=0)
        qkkk = _mm(lhs, k_bd, "nt", PASSES_QK)
        it["qk4"] = qkkk[:c] * decay4
        a4 = jnp.where(strict4, beta4 * qkkk[c:] * decay4, 0.0)
        it["t4"] = eye4 - a4
        it["a4"] = a4

    for it in items:
        it["p4"] = times_block_diag(it["a4"], it["a4"], PASSES_INV)
    n_sq = int(math.log2(c)) - 1
    for step in range(n_sq):
        for it in items:
            if step < n_sq - 1:
                both = times_block_diag(jnp.concatenate([it["p4"], it["t4"]], axis=0), it["p4"], PASSES_INV)
                it["p4"], it["t4"] = both[:c], it["t4"] + both[c:]
            else:
                it["t4"] = it["t4"] + times_block_diag(it["t4"], it["p4"], PASSES_INV)

    for it in items:
        ch, hs, ks, vs = it["ch"], it["hs"], it["ks"], it["vs"]
        rhs = jnp.concatenate(
            [jnp.concatenate([vs[i] * ch["beta"][:, h:h + 1],
                              ks[i] * (ch["beta"][:, h:h + 1] * jnp.exp(ch["gc"][:, h:h + 1]))], axis=1)
             for i, h in enumerate(hs)], axis=0)
        it["sol"] = block_diag_times(it["t4"], rhs, PASSES_INV)

    for ch in chunks:
        group_items = [it for it in items if it["ch"] is ch]
        for it in group_items:
            us, o_inter = [], []
            for i, h in enumerate(it["hs"]):
                sl = slice(i * c, (i + 1) * c)
                ws = _mm(jnp.concatenate([it["sol"][sl, hd:], it["qs"][i] * jnp.exp(ch["gc"][:, h:h + 1])], axis=0),
                         s_ref[h], "nn", PASSES_STATE)
                us.append(it["sol"][sl, :hd] - ws[:c])
                o_inter.append(ws[c:])
            it["us"], it["o_inter"] = us, o_inter
        for it in group_items:
            it["o_intra"] = block_diag_times(it["qk4"], jnp.concatenate(it["us"], axis=0), PASSES_STATE)
        for it in group_items:
            for i, h in enumerate(it["hs"]):
                gc = ch["gc"][:, h:h + 1]
                gc_last = ch["gc_row"][h:h + 1, c - 1:c]
                s_ref[h] = s_ref[h] * jnp.exp(gc_last) + _mm(it["ks"][i] * jnp.exp(gc_last - gc), it["us"][i],
                                                             "tn", PASSES_STATE)
        for it in group_items:
            r0 = ch["r0"]
            for i, h in enumerate(it["hs"]):
                o = it["o_inter"][i] + it["o_intra"][i * c:(i + 1) * c]
                zh = z_ref[r0:r0 + c, h * hd:(h + 1) * hd]
                o_ref[r0:r0 + c, h * hd:(h + 1) * hd] = _rms(o, nw_ref[...]) * _silu(zh)

    @pl.when(j == pl.num_programs(1) - 1)
    def _():
        sout_ref[0] = s_ref[...]


def _gdn_packed(proj_a, cols, s0, a_log, dt_bias, norm_w, *, n_seq, seq_len, block):
    heads = a_log.shape[1]
    v_w = heads * HEAD_DIM
    conv_width = 3 * v_w
    nblk = seq_len // block
    z_blk, ba_blk = cols["z"] // v_w, cols["ba"] // 128
    shared_s0 = s0.shape[0] == 1
    assert 2 * DN_CHUNK == HEAD_DIM and heads % GROUP == 0 and block % DN_CHUNK == 0
    return pl.pallas_call(
        functools.partial(_gdn_packed_kernel, chunk=DN_CHUNK, heads=heads),
        grid=(n_seq, nblk),
        in_specs=[
            pl.BlockSpec((block, conv_width), lambda n, j: (n * nblk + j, 0)),
            pl.BlockSpec((block, v_w), lambda n, j: (n * nblk + j, z_blk)),
            pl.BlockSpec((block, 128), lambda n, j: (n * nblk + j, ba_blk)),
            pl.BlockSpec((1, heads, HEAD_DIM, HEAD_DIM), lambda n, j: (0 if shared_s0 else n, 0, 0, 0)),
            pl.BlockSpec((1, heads), lambda n, j: (0, 0)),
            pl.BlockSpec((1, heads), lambda n, j: (0, 0)),
            pl.BlockSpec((1, HEAD_DIM), lambda n, j: (0, 0)),
        ],
        out_specs=[
            pl.BlockSpec((block, v_w), lambda n, j: (n * nblk + j, 0)),
            pl.BlockSpec((1, heads, HEAD_DIM, HEAD_DIM), lambda n, j: (n, 0, 0, 0)),
        ],
        out_shape=[
            jax.ShapeDtypeStruct((n_seq * seq_len, v_w), F32),
            jax.ShapeDtypeStruct((n_seq, heads, HEAD_DIM, HEAD_DIM), F32),
        ],
        scratch_shapes=[pltpu.VMEM((heads, HEAD_DIM, HEAD_DIM), F32)],
        compiler_params=_params("parallel", "arbitrary"),
        name="gdn_packed",
    )(proj_a, proj_a, proj_a, s0, a_log, dt_bias, norm_w)


def _gdn(proj_a, cols, hist, s0, conv_w, a_log, dt_bias, norm_w, *, n_seq, seq_len, block, chunk):
    heads = a_log.shape[1]
    conv_width = conv_w.shape[1]
    v_w = heads * HEAD_DIM
    nblk = seq_len // block
    z_blk, ba_blk = cols["z"] // v_w, cols["ba"] // 128
    shared_hist, shared_s0 = hist.shape[0] == 1, s0.shape[0] == 1
    return pl.pallas_call(
        functools.partial(_gdn_kernel, chunk=chunk, heads=heads),
        grid=(n_seq, nblk),
        in_specs=[
            pl.BlockSpec((block, conv_width), lambda n, j: (n * nblk + j, 0)),
            pl.BlockSpec((block, v_w), lambda n, j: (n * nblk + j, z_blk)),
            pl.BlockSpec((block, 128), lambda n, j: (n * nblk + j, ba_blk)),
            pl.BlockSpec((1, HIST_ROWS, conv_width), lambda n, j: (0 if shared_hist else n, 0, 0)),
            pl.BlockSpec((1, heads, HEAD_DIM, HEAD_DIM), lambda n, j: (0 if shared_s0 else n, 0, 0, 0)),
            pl.BlockSpec((DN_CONV, conv_width), lambda n, j: (0, 0)),
            pl.BlockSpec((1, heads), lambda n, j: (0, 0)),
            pl.BlockSpec((1, heads), lambda n, j: (0, 0)),
            pl.BlockSpec((1, HEAD_DIM), lambda n, j: (0, 0)),
        ],
        out_specs=[
            pl.BlockSpec((block, v_w), lambda n, j: (n * nblk + j, 0)),
            pl.BlockSpec((1, heads, HEAD_DIM, HEAD_DIM), lambda n, j: (n, 0, 0, 0)),
        ],
        out_shape=[
            jax.ShapeDtypeStruct((n_seq * seq_len, v_w), F32),
            jax.ShapeDtypeStruct((n_seq, heads, HEAD_DIM, HEAD_DIM), F32),
        ],
        scratch_shapes=[
            pltpu.VMEM((block + HIST_ROWS, conv_width), F32),
            pltpu.VMEM((heads, HEAD_DIM, HEAD_DIM), F32),
        ],
        compiler_params=_params("parallel", "arbitrary"),
        name="gdn",
    )(proj_a, proj_a, proj_a, hist, s0, conv_w, a_log, dt_bias, norm_w)


def _gdn_step_kernel(qkv_ref, z_ref, ba_ref, hist_ref, s0_ref, convw_ref, alog_ref, dtb_ref, nw_ref,
                     o_ref, sout_ref, xbuf_ref, *, heads):
    nb, t, _ = qkv_ref.shape
    hd = HEAD_DIM
    qk_w = heads * hd
    ti = lax.broadcasted_iota(jnp.int32, (t, t), 0)
    tj = lax.broadcasted_iota(jnp.int32, (t, t), 1)
    lower_ones = (ti >= tj).astype(F32)
    upper_ones = (ti <= tj).astype(F32)

    def per_sequence(b, carry):
        xbuf_ref[0:HIST_ROWS, :] = hist_ref[b]
        xbuf_ref[HIST_ROWS:HIST_ROWS + t, :] = qkv_ref[b]
        acc = None
        for tap in range(DN_CONV):
            lo = HIST_ROWS - (DN_CONV - 1) + tap
            term = xbuf_ref[lo:lo + t, :] * convw_ref[tap:tap + 1, :]
            acc = term if acc is None else acc + term
        x = _silu(acc)
        ba = ba_ref[b]
        beta_all = jax.nn.sigmoid(ba[:, 0:heads])
        a_in = ba[:, heads:2 * heads] + dtb_ref[...]
        softplus = jnp.maximum(a_in, 0.0) + jnp.log(1.0 + jnp.exp(-jnp.abs(a_in)))
        g_all = -jnp.exp(alog_ref[...]) * softplus
        gc_all = _dot_hi(lower_ones, g_all)
        gc_row_all = _dot_tn_hi(g_all, upper_ones)
        z = z_ref[b]
        hs = range(heads)
        qs, ks, vs = [], [], []
        for h in hs:
            q = x[:, h * hd:(h + 1) * hd]
            k = x[:, qk_w + h * hd:qk_w + (h + 1) * hd]
            qs.append(q * lax.rsqrt(jnp.sum(q * q, axis=-1, keepdims=True) + L2_EPS) * (hd ** -0.5))
            ks.append(k * lax.rsqrt(jnp.sum(k * k, axis=-1, keepdims=True) + L2_EPS))
            vs.append(x[:, 2 * qk_w + h * hd:2 * qk_w + (h + 1) * hd])
        gcs = [gc_all[:, h:h + 1] for h in hs]
        gc_lasts = [gc_row_all[h:h + 1, t - 1:t] for h in hs]
        kq = [jnp.concatenate([ks[h], qs[h]], axis=0) for h in hs]
        gram = [_mm(kq[h], ks[h], "nt", STEP_PASSES) for h in hs]
        from_state = [_mm(kq[h] * jnp.exp(jnp.concatenate([gcs[h], gcs[h]], axis=0)), s0_ref[b, h], "nn",
                          STEP_PASSES) for h in hs]
        us, outs = [], []
        for h in hs:
            decay = jnp.where(ti >= tj, jnp.exp(jnp.where(ti >= tj, gcs[h] - gc_row_all[h:h + 1, :], 0.0)), 0.0)
            beta = beta_all[:, h:h + 1]
            lower = jnp.where(ti > tj, beta * gram[h][:t] * decay, 0.0)
            u = beta * (vs[h] - from_state[h][:t])
            for j in range(t - 1):
                u = u - lower[:, j:j + 1] * u[j:j + 1, :]
            qk = gram[h][t:] * decay
            o = from_state[h][t:]
            for j in range(t):
                o = o + qk[:, j:j + 1] * u[j:j + 1, :]
            us.append(u)
            outs.append(o)
        for h in hs:
            sout_ref[b, h] = s0_ref[b, h] * jnp.exp(gc_lasts[h]) + _mm(ks[h] * jnp.exp(gc_lasts[h] - gcs[h]), us[h],
                                                                       "tn", STEP_PASSES)
        for h in hs:
            o_ref[b, :, h * hd:(h + 1) * hd] = _rms(outs[h], nw_ref[...]) * _silu(z[:, h * hd:(h + 1) * hd])
        return carry

    lax.fori_loop(0, nb, per_sequence, 0)


def _gdn_step(proj_a3, cols, hist, s0, conv_w, a_log, dt_bias, norm_w, *, seqs_per_step):
    n, t, _ = proj_a3.shape
    heads = a_log.shape[1]
    conv_width = conv_w.shape[1]
    v_w = heads * HEAD_DIM
    nb = seqs_per_step
    z_blk, ba_blk = cols["z"] // v_w, cols["ba"] // 128
    state_spec = pl.BlockSpec((nb, heads, HEAD_DIM, HEAD_DIM), lambda i: (i, 0, 0, 0))
    return pl.pallas_call(
        functools.partial(_gdn_step_kernel, heads=heads),
        grid=(n // nb,),
        in_specs=[
            pl.BlockSpec((nb, t, conv_width), lambda i: (i, 0, 0)),
            pl.BlockSpec((nb, t, v_w), lambda i: (i, 0, z_blk)),
            pl.BlockSpec((nb, t, 128), lambda i: (i, 0, ba_blk)),
            pl.BlockSpec((nb, HIST_ROWS, conv_width), lambda i: (i, 0, 0)),
            state_spec,
            pl.BlockSpec((DN_CONV, conv_width), lambda i: (0, 0)),
            pl.BlockSpec((1, heads), lambda i: (0, 0)),
            pl.BlockSpec((1, heads), lambda i: (0, 0)),
            pl.BlockSpec((1, HEAD_DIM), lambda i: (0, 0)),
        ],
        out_specs=[pl.BlockSpec((nb, t, v_w), lambda i: (i, 0, 0)), state_spec],
        out_shape=[jax.ShapeDtypeStruct((n, t, v_w), F32),
                   jax.ShapeDtypeStruct((n, heads, HEAD_DIM, HEAD_DIM), F32)],
        scratch_shapes=[pltpu.VMEM((HIST_ROWS + t, conv_width), F32)],
        compiler_params=_params("parallel"),
        name="gdn_step",
    )(proj_a3, proj_a3, proj_a3, hist, s0, conv_w, a_log, dt_bias, norm_w)


def _alibi_slope(head, n_heads):
    return 2.0 ** (-8.0 * (head + 1) / n_heads)


def _penalty(dist, mask):
    return jnp.where(mask, jnp.minimum(dist, WINDOW).astype(F32), -NEG_BIG)


def _attend(jobs, n_rows):
    scale = HEAD_DIM ** -0.5
    hd = HEAD_DIM
    scores = [[_dot_nt(q4, k) * scale for k, _, _ in segs] for q4, segs, _, _ in jobs]
    v_ones = [[jnp.concatenate([v, jnp.ones_like(v)], axis=1) for _, v, _ in segs] for _, segs, _, _ in jobs]
    heads = [(ji, g) for ji in range(len(jobs)) for g in range(len(jobs[ji][3]))]
    logits, maxes = {}, {}
    for ji, g in heads:
        _, segs, _, slopes = jobs[ji]
        r = slice(g * n_rows, (g + 1) * n_rows)
        logits[ji, g] = [sc[r] - slopes[g] * pen for sc, (_, _, pen) in zip(scores[ji], segs)]
    for ji, g in heads:
        by_width = {}
        for lg in logits[ji, g]:
            w = lg.shape[1]
            by_width[w] = lg if w not in by_width else jnp.maximum(by_width[w], lg)
        m = jobs[ji][2][g]
        for lg in by_width.values():
            m = jnp.maximum(m, jnp.max(lg, axis=-1, keepdims=True))
        maxes[ji, g] = m
    accs = {}
    for ji, g in heads:
        acc = None
        for lg, v1 in zip(logits[ji, g], v_ones[ji]):
            pv = _dot(jnp.exp(lg - maxes[ji, g]).astype(BF16), v1)
            acc = pv if acc is None else acc + pv
        accs[ji, g] = acc
    outs = [[None] * len(job[3]) for job in jobs]
    for ji, g in heads:
        acc = accs[ji, g]
        outs[ji][g] = acc[:, :hd] / (acc[:, hd:] + jnp.exp(jobs[ji][2][g] - maxes[ji, g]))
    return outs


def _swa_prompt_kernel(q_ref, kv_ref, kvprev_ref, kvmeta_ref, sinks_ref, o_ref, *, n_meta, kv_heads):
    j = pl.program_id(1)
    tq = q_ref.shape[0]
    hd = HEAD_DIM
    kv_w = kv_heads * hd
    n_heads = kv_heads * SWA_GROUP
    qi = lax.broadcasted_iota(jnp.int32, (tq, tq), 0)
    ki = lax.broadcasted_iota(jnp.int32, (tq, tq), 1)
    dist_own = qi - ki
    dist_prev = dist_own + tq
    pen_own = _penalty(dist_own, dist_own >= 0)
    pen_prev = _penalty(dist_prev, (dist_prev <= WINDOW) & (j > 0))
    qpos = n_meta + j * tq + lax.broadcasted_iota(jnp.int32, (tq, n_meta), 0)
    dist_meta = qpos - lax.broadcasted_iota(jnp.int32, (tq, n_meta), 1)
    pen_meta = _penalty(dist_meta, dist_meta >= 0)
    jobs, job_heads = [], []
    for kvh in range(kv_heads):
        ks = slice(kvh * hd, (kvh + 1) * hd)
        vs = slice(kv_w + kvh * hd, kv_w + (kvh + 1) * hd)
        segments = [
            (kvmeta_ref[:, ks].astype(BF16), kvmeta_ref[:, vs].astype(BF16), pen_meta),
            (kvprev_ref[:, ks].astype(BF16), kvprev_ref[:, vs].astype(BF16), pen_prev),
            (kv_ref[:, ks].astype(BF16), kv_ref[:, vs].astype(BF16), pen_own),
        ]
        heads = [kvh * SWA_GROUP + g for g in range(SWA_GROUP)]
        q4 = jnp.concatenate([q_ref[:, hh * hd:(hh + 1) * hd] for hh in heads], axis=0).astype(BF16)
        jobs.append((q4, segments, [sinks_ref[:, hh:hh + 1] for hh in heads],
                     [_alibi_slope(hh, n_heads) for hh in heads]))
        job_heads.append(heads)
    for heads, outs in zip(job_heads, _attend(jobs, tq)):
        for hh, o in zip(heads, outs):
            o_ref[:, hh * hd:(hh + 1) * hd] = o


def _swa_prompt(proj_b, proj_b_meta, cols, sinks, *, n_seq, seq_len, n_meta):
    n_heads = sinks.shape[1]
    kv_heads = n_heads // SWA_GROUP
    q_w, kv_w2 = n_heads * HEAD_DIM, 2 * kv_heads * HEAD_DIM
    tq = WINDOW
    nblk = seq_len // tq
    kv_blk = cols["skv"] // kv_w2
    return pl.pallas_call(
        functools.partial(_swa_prompt_kernel, n_meta=n_meta, kv_heads=kv_heads),
        grid=(n_seq, nblk),
        in_specs=[
            pl.BlockSpec((tq, q_w), lambda n, j: (n * nblk + j, 0)),
            pl.BlockSpec((tq, kv_w2), lambda n, j: (n * nblk + j, kv_blk)),
            pl.BlockSpec((tq, kv_w2), lambda n, j: (n * nblk + jnp.maximum(j - 1, 0), kv_blk)),
            pl.BlockSpec((n_meta, kv_w2), lambda n, j: (0, kv_blk)),
            pl.BlockSpec((1, n_heads), lambda n, j: (0, 0)),
        ],
        out_specs=pl.BlockSpec((tq, q_w), lambda n, j: (n * nblk + j, 0)),
        out_shape=jax.ShapeDtypeStruct((n_seq * seq_len, q_w), F32),
        compiler_params=_params("parallel", "arbitrary"),
        name="swa_prompt",
    )(proj_b, proj_b, proj_b, proj_b_meta, sinks)


def _swa_step_kernel(q_ref, kvnew_ref, kmeta_ref, vmeta_ref, kbuf_ref, vbuf_ref, sinks_ref,
                     o_ref, kout_ref, vout_ref, *, kv_heads):
    nb, t, _ = q_ref.shape
    n_meta, w = kmeta_ref.shape[1], kbuf_ref.shape[1]
    hd = HEAD_DIM
    kv_w = kv_heads * hd
    n_heads = kv_heads * SWA_GROUP

    def grid2(rows, cols_):
        return (lax.broadcasted_iota(jnp.int32, (rows, cols_), 0), lax.broadcasted_iota(jnp.int32, (rows, cols_), 1))

    ti, mi = grid2(t, n_meta)
    dist_meta = PAST_LEN + ti - mi
    pen_meta = _penalty(dist_meta, dist_meta >= 0)
    ti, bi = grid2(t, w)
    dist_buf = w + ti - bi
    pen_buf = _penalty(dist_buf, (dist_buf <= WINDOW) & (PAST_LEN - w + bi >= n_meta))
    ti, si = grid2(t, t)
    dist_new = ti - si
    pen_new = _penalty(dist_new, dist_new >= 0)
    jobs, job_dst = [], []
    for b in range(nb):
        kv_new = kvnew_ref[b]
        for kvh in range(kv_heads):
            ks = slice(kvh * hd, (kvh + 1) * hd)
            k_old, v_old = kbuf_ref[b, :, kvh, :], vbuf_ref[b, :, kvh, :]
            k_new, v_new = kv_new[:, ks], kv_new[:, kv_w + kvh * hd:kv_w + (kvh + 1) * hd]
            kout_ref[b, 0:w - t, kvh, :] = k_old[t:w]
            kout_ref[b, w - t:w, kvh, :] = k_new
            vout_ref[b, 0:w - t, kvh, :] = v_old[t:w]
            vout_ref[b, w - t:w, kvh, :] = v_new
            segments = [
                (kmeta_ref[b, :, ks].astype(BF16), vmeta_ref[b, :, ks].astype(BF16), pen_meta),
                (k_old.astype(BF16), v_old.astype(BF16), pen_buf),
                (k_new.astype(BF16), v_new.astype(BF16), pen_new),
            ]
            heads = [kvh * SWA_GROUP + g for g in range(SWA_GROUP)]
            q4 = jnp.concatenate([q_ref[b, :, hh * hd:(hh + 1) * hd] for hh in heads], axis=0).astype(BF16)
            jobs.append((q4, segments, [sinks_ref[:, hh:hh + 1] for hh in heads],
                         [_alibi_slope(hh, n_heads) for hh in heads]))
            job_dst.append((b, heads))
    for (b, heads), outs in zip(job_dst, _attend(jobs, t)):
        for hh, o in zip(heads, outs):
            o_ref[b, :, hh * hd:(hh + 1) * hd] = o


def _swa_step(proj_b3, cols, k_meta, v_meta, k_buf, v_buf, sinks, *, seqs_per_step):
    n, t, _ = proj_b3.shape
    n_heads = sinks.shape[1]
    kv_heads = n_heads // SWA_GROUP
    q_w, kv_w = n_heads * HEAD_DIM, kv_heads * HEAD_DIM
    n_meta, w = k_meta.shape[1], k_buf.shape[1]
    nb = seqs_per_step
    kv_blk = cols["skv"] // (2 * kv_w)
    seq3 = lambda rows, width, blk=0: pl.BlockSpec((nb, rows, width), lambda i: (i, 0, blk))
    cache_spec = pl.BlockSpec((nb, w, kv_heads, HEAD_DIM), lambda i: (i, 0, 0, 0))
    return pl.pallas_call(
        functools.partial(_swa_step_kernel, kv_heads=kv_heads),
        grid=(n // nb,),
        in_specs=[seq3(t, q_w), seq3(t, 2 * kv_w, kv_blk), seq3(n_meta, kv_w), seq3(n_meta, kv_w),
                  cache_spec, cache_spec, pl.BlockSpec((1, n_heads), lambda i: (0, 0))],
        out_specs=[seq3(t, q_w), cache_spec, cache_spec],
        out_shape=[jax.ShapeDtypeStruct((n, t, q_w), F32),
                   jax.ShapeDtypeStruct((n, w, kv_heads, HEAD_DIM), F32),
                   jax.ShapeDtypeStruct((n, w, kv_heads, HEAD_DIM), F32)],
        compiler_params=_params("parallel"),
        name="swa_step",
    )(proj_b3, proj_b3, k_meta, v_meta, k_buf, v_buf, sinks)


def _split_w_in(w_in, heads, n_heads, kv_heads, d):
    conv_w = 3 * heads * HEAD_DIM
    v_w = heads * HEAD_DIM
    q_w, kv_w = n_heads * HEAD_DIM, kv_heads * HEAD_DIM
    sizes = (conv_w, v_w, heads, heads, q_w, kv_w, kv_w, d, d)
    offs = [0]
    for s in sizes:
        offs.append(offs[-1] + s)
    part = lambda i: w_in[:, offs[i]:offs[i + 1]].astype(BF16)
    w_a = jnp.concatenate([part(0), part(1), part(7), part(2), part(3),
                           jnp.zeros((w_in.shape[0], 128 - 2 * heads), BF16)], axis=1)
    w_b = jnp.concatenate([part(4), part(8), part(5), part(6)], axis=1)
    cols_a = {"qkv": 0, "z": conv_w, "g_dn": conv_w + v_w, "ba": conv_w + v_w + d}
    cols_b = {"q": 0, "g_swa": q_w, "skv": q_w + d}
    return w_a, w_b, cols_a, cols_b


def kernel(x_prompt, x_sample, state_dn_conv, state_dn_ssm, cache_swa_meta_k, cache_swa_meta_v, cache_swa_k, cache_swa_v, meta_tokens, ffn1_norm_pre, ffn1_norm_post, ffn1_w_gate, ffn1_w_up, ffn1_w_down, mix_norm_pre, mix_norm_post, w_in, dn_conv_w, dn_a_log, dn_dt_bias, dn_norm_w, swa_sinks, w_out, ffn2_norm_pre, ffn2_norm_post, ffn2_w_gate, ffn2_w_up, ffn2_w_down):
    assert w_in.shape[0] == 1, "single-layer step"
    n_p, seq, d = x_prompt.shape
    n_s, t_s, _ = x_sample.shape
    n_meta = meta_tokens.shape[0]
    heads = dn_a_log.shape[1]
    n_heads = swa_sinks.shape[1]
    kv_heads = n_heads // SWA_GROUP
    conv_w = dn_conv_w.shape[2]
    kv_w = kv_heads * HEAD_DIM
    w_keep = cache_swa_k.shape[2]
    assert w_keep == WINDOW and seq % WINDOW == 0 and seq % DN_CHUNK == 0 and seq >= WINDOW

    w_a, w_b, cols_a, cols_b = _split_w_in(w_in[0], heads, n_heads, kv_heads, d)
    cols = {**cols_a, **cols_b}
    f1 = (ffn1_norm_pre, ffn1_norm_post, ffn1_w_gate[0].astype(BF16), ffn1_w_up[0].astype(BF16),
          ffn1_w_down[0].astype(BF16))
    f2 = (ffn2_norm_pre, ffn2_norm_post, ffn2_w_gate[0].astype(BF16), ffn2_w_up[0].astype(BF16),
          ffn2_w_down[0].astype(BF16))
    w_o = w_out[0].astype(BF16)
    gdn_w = (dn_conv_w[0], dn_a_log, dn_dt_bias, dn_norm_w)

    def front(h):
        h1 = _ffn(h, *f1)
        return h1, _proj(h1, mix_norm_pre, w_a), _proj(h1, mix_norm_pre, w_b)

    def back(h1, o_dn, o_sw, pa, pb):
        return _mixffn(h1, o_dn, o_sw, pa, pb, cols, w_o, mix_norm_post, *f2)

    def hist_tile(rows3):
        return jnp.pad(rows3, ((0, 0), (HIST_ROWS - (DN_CONV - 1), 0), (0, 0)))

    _, pa_m, pb_m = front(meta_tokens)
    zero_hist = jnp.zeros((1, HIST_ROWS, conv_w), F32)
    zero_state = jnp.zeros((1, heads, HEAD_DIM, HEAD_DIM), F32)
    _, s_meta = _gdn(pa_m, cols, zero_hist, zero_state, *gdn_w, n_seq=1, seq_len=n_meta, block=n_meta,
                     chunk=n_meta)
    hist_meta = hist_tile(pa_m[None, n_meta - (DN_CONV - 1):, :conv_w])

    h1_p = _ffn(x_prompt.reshape(n_p * seq, d), *f1)
    pa_p, tail_p = _proj_conv(h1_p, mix_norm_pre, w_a, hist_meta, dn_conv_w[0], n_seq=n_p, seq_len=seq,
                              qk_width=heads * HEAD_DIM)
    pb_p = _proj(h1_p, mix_norm_pre, w_b)
    o_dn_p, s_p = _gdn_packed(pa_p, cols, s_meta, dn_a_log, dn_dt_bias, dn_norm_w, n_seq=n_p, seq_len=seq,
                              block=4 * DN_CHUNK)
    o_sw_p = _swa_prompt(pb_p, pb_m, cols, swa_sinks, n_seq=n_p, seq_len=seq, n_meta=n_meta)
    y_prompt = back(h1_p, o_dn_p, o_sw_p, pa_p, pb_p).reshape(n_p, seq, d)

    pb_p3 = pb_p.reshape(n_p, seq, -1)
    p_conv = tail_p[:, HIST_ROWS - (DN_CONV - 1):]
    kv_meta = pb_m[:, cols["skv"]:]
    p_meta_k = jnp.broadcast_to(kv_meta[None, :, :kv_w], (n_p, n_meta, kv_w))
    p_meta_v = jnp.broadcast_to(kv_meta[None, :, kv_w:], (n_p, n_meta, kv_w))
    p_win_k = pb_p3[:, seq - w_keep:, cols["skv"]:cols["skv"] + kv_w]
    p_win_v = pb_p3[:, seq - w_keep:, cols["skv"] + kv_w:]

    h1_s, pa_s, pb_s = front(x_sample.reshape(n_s * t_s, d))
    o_dn_s, s_s = _gdn_step(pa_s.reshape(n_s, t_s, -1), cols, hist_tile(state_dn_conv[0]), state_dn_ssm[0],
                            *gdn_w, seqs_per_step=4)
    o_dn_s = o_dn_s.reshape(n_s * t_s, -1)
    o_sw_s, s_win_k, s_win_v = _swa_step(
        pb_s.reshape(n_s, t_s, -1), cols,
        cache_swa_meta_k[0].reshape(n_s, n_meta, kv_w), cache_swa_meta_v[0].reshape(n_s, n_meta, kv_w),
        cache_swa_k[0], cache_swa_v[0], swa_sinks, seqs_per_step=4)
    y_sample = back(h1_s, o_dn_s, o_sw_s.reshape(n_s * t_s, -1), pa_s, pb_s).reshape(n_s, t_s, d)
    assert t_s >= DN_CONV - 1, "new conv state is taken from the new rows alone"
    s_conv = pa_s.reshape(n_s, t_s, -1)[:, t_s - (DN_CONV - 1):, :conv_w]

    def kv4(x):
        return x.reshape(1, x.shape[0], x.shape[1], kv_heads, HEAD_DIM)

    return (y_prompt, y_sample, p_conv[None], s_p[None], kv4(p_meta_k), kv4(p_meta_v), kv4(p_win_k),
            kv4(p_win_v), s_conv[None], s_s[None], s_win_k[None], s_win_v[None])
```

```python
import functools
import math

import jax
import jax.numpy as jnp
from jax import lax
from jax.experimental import pallas as pl
from jax.experimental.pallas import tpu as pltpu

F32 = jnp.float32
BF16 = jnp.bfloat16
HIGHEST = lax.Precision.HIGHEST

RMS_EPS = 1e-6
L2_EPS = 1e-6
WINDOW = 128
PAST_LEN = 8192
HEAD_DIM = 128
SWA_GROUP = 4
DN_CONV = 4
DN_CHUNK = 64
HIST_ROWS = 8

VMEM_LIMIT_BYTES = 56 * 1024 * 1024
ROW_TILE = 512
MXU_WIDTH = 256
NEG_BIG = -1e30
SWA_SUBBLOCKS = 2


def _rms(x, g):
    return x * lax.rsqrt(jnp.mean(x * x, axis=-1, keepdims=True) + RMS_EPS) * g


def _silu(x):
    return x * jax.nn.sigmoid(x)


def _dot(a, b):
    return jnp.dot(a, b, preferred_element_type=F32)


def _dot_hi(a, b):
    return jnp.dot(a, b, preferred_element_type=F32, precision=HIGHEST)


def _dot_nt_hi(a, b):
    return lax.dot_general(a, b, (((1,), (1,)), ((), ())), preferred_element_type=F32, precision=HIGHEST)


def _dot_tn_hi(a, b):
    return lax.dot_general(a, b, (((0,), (0,)), ((), ())), preferred_element_type=F32, precision=HIGHEST)


def _dot_nt(a, b):
    return lax.dot_general(a, b, (((1,), (1,)), ((), ())), preferred_element_type=F32)


def _dot_tn(a, b):
    return lax.dot_general(a, b, (((0,), (0,)), ((), ())), preferred_element_type=F32)


def _split_bf16(a):
    hi = a.astype(BF16)
    return hi, (a - hi.astype(F32)).astype(BF16)


_DOTS = {"nn": _dot, "nt": _dot_nt, "tn": _dot_tn}


def _dot3_split(a_pair, b_pair, kind="nn"):
    (a_hi, a_lo), (b_hi, b_lo) = a_pair, b_pair
    dot = _DOTS[kind]
    axis = 1 if kind == "tn" else 0
    m = a_hi.shape[axis]
    top = dot(jnp.concatenate([a_hi, a_lo], axis=axis), b_hi)
    return top[:m] + top[m:] + dot(a_hi, b_lo)


def _mm(a, b, kind="nn", passes=3):
    if passes == 1:
        return _DOTS[kind](a.astype(BF16), b.astype(BF16))
    return _dot3_split(_split_bf16(a), _split_bf16(b), kind)


def _row_tile(rows):
    return ROW_TILE if rows % ROW_TILE == 0 else rows


def _resident(shape):
    return pl.BlockSpec(shape, lambda *_: (0,) * len(shape), pipeline_mode=pl.Buffered(1))


def _params(*semantics):
    return pltpu.CompilerParams(dimension_semantics=semantics, vmem_limit_bytes=VMEM_LIMIT_BYTES)


def _ffn_tile(h, g_pre, g_post, wg_ref, wu_ref, wd_ref, ff_chunk):
    xn = _rms(h, g_pre).astype(BF16)
    acc = None
    c0 = 0
    for width in ff_chunk:
        gate = _dot(xn, wg_ref[:, c0:c0 + width])
        up = _dot(xn, wu_ref[:, c0:c0 + width])
        part = _dot((_silu(gate) * up).astype(BF16), wd_ref[c0:c0 + width, :])
        acc = part if acc is None else acc + part
        c0 += width
    return h + 0.5 * _rms(acc, g_post)


def _ff_chunk(d_ff):
    tiles, rem = divmod(d_ff, MXU_WIDTH)
    if rem or tiles < 2:
        return (d_ff,)
    first = (tiles + 1) // 2
    return (first * MXU_WIDTH, (tiles - first) * MXU_WIDTH)


def _ffn_kernel(h_ref, gpre_ref, gpost_ref, wg_ref, wu_ref, wd_ref, o_ref, *, ff_chunk):
    o_ref[...] = _ffn_tile(h_ref[...], gpre_ref[...], gpost_ref[...], wg_ref, wu_ref, wd_ref, ff_chunk)


def _ffn(h, g_pre, g_post, wg, wu, wd):
    rows, d = h.shape
    d_ff = wg.shape[1]
    tm = _row_tile(rows)
    row_spec = pl.BlockSpec((tm, d), lambda i: (i, 0))
    return pl.pallas_call(
        functools.partial(_ffn_kernel, ff_chunk=_ff_chunk(d_ff)),
        grid=(rows // tm,),
        in_specs=[row_spec, _resident((1, d)), _resident((1, d)),
                  _resident((d, d_ff)), _resident((d, d_ff)), _resident((d_ff, d))],
        out_specs=row_spec,
        out_shape=jax.ShapeDtypeStruct((rows, d), F32),
        compiler_params=_params("parallel"),
        name="ffn",
    )(h, g_pre, g_post, wg, wu, wd)


def _proj_kernel(h_ref, g_ref, w_ref, o_ref):
    o_ref[...] = _dot(_rms(h_ref[...], g_ref[...]).astype(BF16), w_ref[...])


def _proj(h, g, w):
    rows, d = h.shape
    n = w.shape[1]
    tm = _row_tile(rows)
    return pl.pallas_call(
        _proj_kernel,
        grid=(rows // tm,),
        in_specs=[pl.BlockSpec((tm, d), lambda i: (i, 0)), _resident((1, d)), _resident((d, n))],
        out_specs=pl.BlockSpec((tm, n), lambda i: (i, 0)),
        out_shape=jax.ShapeDtypeStruct((rows, n), F32),
        compiler_params=_params("parallel"),
        name="proj",
    )(h, g, w)


def _proj_conv_kernel(h_ref, g_ref, w_ref, hist_ref, convw_ref, o_ref, tail_ref, ybuf_ref, *, conv_width, qk_width):
    j = pl.program_id(1)
    tm = h_ref.shape[0]
    hd = HEAD_DIM
    xn = _rms(h_ref[...], g_ref[...]).astype(BF16)

    @pl.when(j == 0)
    def _():
        ybuf_ref[0:HIST_ROWS, :] = hist_ref[0]

    @pl.when(j > 0)
    def _():
        ybuf_ref[0:HIST_ROWS, :] = tail_ref[0]

    def project_raw(c0, width):
        ybuf_ref[HIST_ROWS:HIST_ROWS + tm, c0:c0 + width] = _dot(xn, w_ref[:, c0:c0 + width])

    def conv_group(g0, width):
        tail_ref[0, :, g0:g0 + width] = ybuf_ref[tm:tm + HIST_ROWS, g0:g0 + width]
        for c0 in range(g0, g0 + width, hd):
            x = ybuf_ref[:, c0:c0 + hd]
            acc = x[HIST_ROWS:] * convw_ref[DN_CONV - 1:DN_CONV, c0:c0 + hd]
            for back in range(1, DN_CONV):
                shifted = pltpu.roll(x, back, 0)[HIST_ROWS:]
                acc = acc + shifted * convw_ref[DN_CONV - 1 - back:DN_CONV - back, c0:c0 + hd]
            y = _silu(acc)
            if c0 < 2 * qk_width:
                y = y * lax.rsqrt(jnp.sum(y * y, axis=-1, keepdims=True) + L2_EPS)
                if c0 < qk_width:
                    y = y * (hd ** -0.5)
            o_ref[:, c0:c0 + hd] = y

    group = 2 * MXU_WIDTH
    starts = list(range(0, conv_width, group))
    n_rest = w_ref.shape[1] - conv_width
    rest_cuts = ([conv_width + (n_rest * i // len(starts)) // MXU_WIDTH * MXU_WIDTH for i in range(len(starts))]
                 + [w_ref.shape[1]])
    project_raw(starts[0], group)
    for gi, g0 in enumerate(starts):
        if gi + 1 < len(starts):
            project_raw(starts[gi + 1], group)
        r0, r1 = rest_cuts[gi], rest_cuts[gi + 1]
        if r1 > r0:
            o_ref[:, r0:r1] = _dot(xn, w_ref[:, r0:r1])
        conv_group(g0, group)


def _proj_conv(h, g, w, hist, conv_w, *, n_seq, seq_len, qk_width):
    rows, d = h.shape
    n = w.shape[1]
    conv_width = conv_w.shape[1]
    tm = _row_tile(seq_len)
    tiles = seq_len // tm
    shared_hist = hist.shape[0] == 1
    return pl.pallas_call(
        functools.partial(_proj_conv_kernel, conv_width=conv_width, qk_width=qk_width),
        grid=(n_seq, tiles),
        in_specs=[pl.BlockSpec((tm, d), lambda s, j: (s * tiles + j, 0)), _resident((1, d)), _resident((d, n)),
                  pl.BlockSpec((1, HIST_ROWS, conv_width), lambda s, j: (0 if shared_hist else s, 0, 0)),
                  _resident((DN_CONV, conv_width))],
        out_specs=[pl.BlockSpec((tm, n), lambda s, j: (s * tiles + j, 0)),
                   pl.BlockSpec((1, HIST_ROWS, conv_width), lambda s, j: (s, 0, 0))],
        out_shape=[jax.ShapeDtypeStruct((rows, n), F32),
                   jax.ShapeDtypeStruct((n_seq, HIST_ROWS, conv_width), F32)],
        scratch_shapes=[pltpu.VMEM((HIST_ROWS + tm, conv_width), F32)],
        compiler_params=_params("parallel", "arbitrary"),
        name="proj_conv",
    )(h, g, w, hist, conv_w)


def _mixffn_kernel(h_ref, odn_ref, osw_ref, gdn_ref, gsw_ref, wout_ref, gmix_ref,
                   gpre_ref, gpost_ref, wg_ref, wu_ref, wd_ref, o_ref, *, ff_chunk):
    y = jax.nn.sigmoid(gdn_ref[...]) * odn_ref[...] + jax.nn.sigmoid(gsw_ref[...]) * osw_ref[...]
    h2 = h_ref[...] + _rms(_dot(y.astype(BF16), wout_ref[...]), gmix_ref[...])
    o_ref[...] = _ffn_tile(h2, gpre_ref[...], gpost_ref[...], wg_ref, wu_ref, wd_ref, ff_chunk)


def _mixffn(h, o_dn, o_sw, proj_a, proj_b, cols, w_out, g_mix, g_pre, g_post, wg, wu, wd):
    rows, d = h.shape
    d_ff = wg.shape[1]
    tm = _row_tile(rows)
    row_spec = pl.BlockSpec((tm, d), lambda i: (i, 0))
    gdn_blk, gsw_blk = cols["g_dn"] // d, cols["g_swa"] // d
    return pl.pallas_call(
        functools.partial(_mixffn_kernel, ff_chunk=_ff_chunk(d_ff)),
        grid=(rows // tm,),
        in_specs=[row_spec, row_spec, row_spec,
                  pl.BlockSpec((tm, d), lambda i: (i, gdn_blk)),
                  pl.BlockSpec((tm, d), lambda i: (i, gsw_blk)),
                  _resident((d, d)), _resident((1, d)), _resident((1, d)), _resident((1, d)),
                  _resident((d, d_ff)), _resident((d, d_ff)), _resident((d_ff, d))],
        out_specs=row_spec,
        out_shape=jax.ShapeDtypeStruct((rows, d), F32),
        compiler_params=_params("parallel"),
        name="mixffn",
    )(h, o_dn, o_sw, proj_a, proj_b, w_out, g_mix, g_pre, g_post, wg, wu, wd)


def _unit_lower_inverse(a, eye):
    c = a.shape[0]
    t = eye - a
    p = a
    for _ in range(int(math.log2(c)) - 1):
        p = _dot_hi(p, p)
        t = t + _dot_hi(t, p)
    return t


def _gdn_kernel(qkv_ref, z_ref, ba_ref, hist_ref, s0_ref, convw_ref, alog_ref, dtb_ref, nw_ref,
                o_ref, sout_ref, xbuf_ref, s_ref, *, chunk, heads):
    j = pl.program_id(1)
    tb = qkv_ref.shape[0]
    hd = HEAD_DIM
    qk_w = heads * hd

    @pl.when(j == 0)
    def _():
        xbuf_ref[0:HIST_ROWS, :] = hist_ref[0]
        s_ref[...] = s0_ref[0]

    xbuf_ref[HIST_ROWS:HIST_ROWS + tb, :] = qkv_ref[...]

    rows = lax.broadcasted_iota(jnp.int32, (chunk, chunk), 0)
    cols = lax.broadcasted_iota(jnp.int32, (chunk, chunk), 1)
    causal = rows >= cols
    strict = rows > cols
    eye = (rows == cols).astype(F32)
    lower_ones = causal.astype(F32)

    def conv_silu(r0, c0):
        acc = None
        for tap in range(DN_CONV):
            lo = HIST_ROWS + r0 - (DN_CONV - 1) + tap
            term = xbuf_ref[lo:lo + chunk, c0:c0 + hd] * convw_ref[tap:tap + 1, c0:c0 + hd]
            acc = term if acc is None else acc + term
        return _silu(acc)

    for r0 in range(0, tb, chunk):
        ba = ba_ref[r0:r0 + chunk, :]
        beta_all = jax.nn.sigmoid(ba[:, 0:heads])
        a_in = ba[:, heads:2 * heads] + dtb_ref[...]
        softplus = jnp.maximum(a_in, 0.0) + jnp.log(1.0 + jnp.exp(-jnp.abs(a_in)))
        g_all = -jnp.exp(alog_ref[...]) * softplus
        gc_col_all = _dot_hi(lower_ones, g_all)
        gc_row_all = _dot_tn_hi(g_all, (rows <= cols).astype(F32))
        for h in range(heads):
            q = conv_silu(r0, h * hd)
            k = conv_silu(r0, qk_w + h * hd)
            v = conv_silu(r0, 2 * qk_w + h * hd)
            q = q * lax.rsqrt(jnp.sum(q * q, axis=-1, keepdims=True) + L2_EPS) * (hd ** -0.5)
            k = k * lax.rsqrt(jnp.sum(k * k, axis=-1, keepdims=True) + L2_EPS)
            beta = beta_all[:, h:h + 1]
            gc = gc_col_all[:, h:h + 1]
            gc_row = gc_row_all[h:h + 1, :]
            gc_last = gc_row[:, chunk - 1:chunk]
            decay = jnp.where(causal, jnp.exp(jnp.where(causal, gc - gc_row, 0.0)), 0.0)
            a_mat = jnp.where(strict, beta * _dot_nt_hi(k, k) * decay, 0.0)
            t_inv = _unit_lower_inverse(a_mat, eye)
            rhs = jnp.concatenate([v * beta, k * (beta * jnp.exp(gc))], axis=-1)
            sol = _dot_hi(t_inv, rhs)
            u_base, w = sol[:, :hd], sol[:, hd:]
            qk = _dot_nt_hi(q, k) * decay
            s = s_ref[h]
            u = u_base - _dot_hi(w, s)
            o = _dot_hi(q * jnp.exp(gc), s) + _dot_hi(qk, u)
            s_ref[h] = s * jnp.exp(gc_last) + _dot_tn_hi(k * jnp.exp(gc_last - gc), u)
            zh = z_ref[r0:r0 + chunk, h * hd:(h + 1) * hd]
            o_ref[r0:r0 + chunk, h * hd:(h + 1) * hd] = _rms(o, nw_ref[...]) * _silu(zh)

    xbuf_ref[0:HIST_ROWS, :] = xbuf_ref[tb:tb + HIST_ROWS, :]

    @pl.when(j == pl.num_programs(1) - 1)
    def _():
        sout_ref[0] = s_ref[...]


GROUP = 4
PASSES_QK = 1
PASSES_STATE = 1
PASSES_INV = 3
INV_BLOCK = 16
STEP_PASSES = 3


def _gdn_packed_kernel(qkv_ref, z_ref, ba_ref, s0_ref, alog_ref, dtb_ref, nw_ref,
                       o_ref, sout_ref, s_ref, *, chunk, heads):
    j = pl.program_id(1)
    tb = qkv_ref.shape[0]
    hd = HEAD_DIM
    c = chunk
    gw = GROUP * c
    qk_w = heads * hd

    @pl.when(j == 0)
    def _():
        s_ref[...] = s0_ref[0]

    def iota(shape, dim):
        return lax.broadcasted_iota(jnp.int32, shape, dim)

    row = iota((c, gw), 0)
    col = iota((c, gw), 1) & (c - 1)
    causal4, strict4 = row >= col, row > col
    diag_blocks4 = (row // INV_BLOCK) == (col // INV_BLOCK)
    eye4 = (row == col).astype(F32)
    bd_mask = (iota((gw, gw), 0) // c) == (iota((gw, gw), 1) // c)
    lower_ones = (iota((c, c), 0) >= iota((c, c), 1)).astype(F32)
    upper_dup = (iota((c, 2 * c), 0) <= (iota((c, 2 * c), 1) & (c - 1))).astype(F32)
    first_half = iota((c, 2 * c), 1) < c
    zeros_head = jnp.zeros((c, hd), F32)

    def block_diag(x4):
        return jnp.where(bd_mask, jnp.concatenate([x4] * GROUP, axis=0), jnp.zeros((), x4.dtype))

    splits = {}

    def split(x):
        if id(x) not in splits:
            splits[id(x)] = (x, _split_bf16(x))
        return splits[id(x)][1]

    def times_block_diag(lhs, x4, passes):
        parts = lhs if isinstance(lhs, list) else [lhs]
        if passes == 1:
            return _dot(jnp.concatenate(parts, axis=0).astype(BF16), block_diag(x4.astype(BF16)))
        hi, lo = split(x4)
        lhs_split = tuple(jnp.concatenate(pieces, axis=0) for pieces in zip(*[split(p) for p in parts]))
        return _dot3_split(lhs_split, (block_diag(hi), block_diag(lo)))

    def block_diag_times(x4, rhs, passes):
        if passes == 1:
            return _dot(block_diag(x4.astype(BF16)), rhs.astype(BF16))
        hi, lo = _split_bf16(x4)
        return _dot3_split((block_diag(hi), block_diag(lo)), _split_bf16(rhs))

    def pack_cols(col_all, hs):
        halves = [jnp.where(first_half,
                            jnp.broadcast_to(col_all[:, hs[2 * p]:hs[2 * p] + 1], (c, 2 * c)),
                            jnp.broadcast_to(col_all[:, hs[2 * p + 1]:hs[2 * p + 1] + 1], (c, 2 * c)))
                  for p in range(GROUP // 2)]
        return jnp.concatenate(halves, axis=1)

    chunks = []
    for r0 in range(0, tb, c):
        ba = ba_ref[r0:r0 + c, :]
        beta_all = jax.nn.sigmoid(ba[:, 0:heads])
        a_in = ba[:, heads:2 * heads] + dtb_ref[...]
        softplus = jnp.maximum(a_in, 0.0) + jnp.log(1.0 + jnp.exp(-jnp.abs(a_in)))
        g_all = -jnp.exp(alog_ref[...]) * softplus
        gc_all = _dot_hi(lower_ones, g_all)
        gc_row_dup = _dot_tn_hi(g_all, upper_dup)
        chunks.append(dict(r0=r0, beta=beta_all, gc=gc_all, gc_row=gc_row_dup))

    items = []
    for ch in chunks:
        for g0 in range(0, heads, GROUP):
            hs = list(range(g0, g0 + GROUP))
            qs, ks, vs = [], [], []
            rows = slice(ch["r0"], ch["r0"] + c)
            for h in hs:
                qs.append(qkv_ref[rows, h * hd:(h + 1) * hd])
                ks.append(qkv_ref[rows, qk_w + h * hd:qk_w + (h + 1) * hd])
                vs.append(qkv_ref[rows, 2 * qk_w + h * hd:2 * qk_w + (h + 1) * hd])
            items.append(dict(ch=ch, hs=hs, qs=qs, ks=ks, vs=vs))

    for it in items:
        ch, hs, qs, ks = it["ch"], it["hs"], it["qs"], it["ks"]
        beta4 = pack_cols(ch["beta"], hs)
        gc_row4 = jnp.concatenate(
            [jnp.where(first_half[0:1], ch["gc_row"][hs[2 * p]:hs[2 * p] + 1],
                       ch["gc_row"][hs[2 * p + 1]:hs[2 * p + 1] + 1])
             for p in range(GROUP // 2)], axis=1)
        diff4 = pack_cols(ch["gc"], hs) - gc_row4
        decay4 = jnp.where(causal4, jnp.exp(jnp.where(causal4, diff4, 0.0)), 0.0)
        lhs = jnp.concatenate([jnp.concatenate(qs, axis=1), jnp.concatenate(ks, axis=1)], axis=0)
        k_bd = jnp.concatenate(
            [jnp.concatenate([ks[i] if ii == i else zeros_head for ii in range(GROUP)], axis=1)
             for i in range(GROUP)], axis=0)
        qkkk = _mm(lhs, k_bd, "nt", PASSES_QK)
        it["qk4"] = qkkk[:c] * decay4
        a4 = jnp.where(strict4, beta4 * qkkk[c:] * decay4, 0.0)
        a_diag = jnp.where(diag_blocks4, a4, 0.0)
        it["a_diag"], it["a_off"] = a_diag, a4 - a_diag
        it["t4"] = eye4 - a_diag

    assert c // INV_BLOCK == 4
    for it in items:
        it["p4"] = times_block_diag(it["a_diag"], it["a_diag"], PASSES_INV)
    n_sq = int(math.log2(INV_BLOCK)) - 1
    for step in range(n_sq):
        for it in items:
            if step < n_sq - 1:
                both = times_block_diag([it["p4"], it["t4"]], it["p4"], PASSES_INV)
                it["p4"], it["t4"] = both[:c], it["t4"] + both[c:]
            else:
                it["t4"] = it["t4"] + times_block_diag(it["t4"], it["p4"], PASSES_INV)
    for it in items:
        it["m4"] = times_block_diag(it["t4"], it["a_off"], PASSES_INV)
    for it in items:
        it["m2"] = times_block_diag(it["m4"], it["m4"], PASSES_INV)
    for it in items:
        i_minus_m = eye4 - it["m4"]
        it["s4"] = i_minus_m + times_block_diag(i_minus_m, it["m2"], PASSES_INV)
    for it in items:
        it["t4"] = times_block_diag(it["s4"], it["t4"], PASSES_INV)

    for it in items:
        ch, hs, ks, vs = it["ch"], it["hs"], it["ks"], it["vs"]
        rhs = jnp.concatenate(
            [jnp.concatenate([vs[i] * ch["beta"][:, h:h + 1],
                              ks[i] * (ch["beta"][:, h:h + 1] * jnp.exp(ch["gc"][:, h:h + 1]))], axis=1)
             for i, h in enumerate(hs)], axis=0)
        it["sol"] = block_diag_times(it["t4"], rhs, PASSES_INV)

    for ch in chunks:
        group_items = [it for it in items if it["ch"] is ch]
        for it in group_items:
            us, o_inter = [], []
            for i, h in enumerate(it["hs"]):
                sl = slice(i * c, (i + 1) * c)
                ws = _mm(jnp.concatenate([it["sol"][sl, hd:], it["qs"][i] * jnp.exp(ch["gc"][:, h:h + 1])], axis=0),
                         s_ref[h], "nn", PASSES_STATE)
                us.append(it["sol"][sl, :hd] - ws[:c])
                o_inter.append(ws[c:])
            it["us"], it["o_inter"] = us, o_inter
        for it in group_items:
            it["o_intra"] = block_diag_times(it["qk4"], jnp.concatenate(it["us"], axis=0), PASSES_STATE)
        for it in group_items:
            for i, h in enumerate(it["hs"]):
                gc = ch["gc"][:, h:h + 1]
                gc_last = ch["gc_row"][h:h + 1, c - 1:c]
                s_ref[h] = s_ref[h] * jnp.exp(gc_last) + _mm(it["ks"][i] * jnp.exp(gc_last - gc), it["us"][i],
                                                             "tn", PASSES_STATE)
        for it in group_items:
            r0 = ch["r0"]
            for i, h in enumerate(it["hs"]):
                o = it["o_inter"][i] + it["o_intra"][i * c:(i + 1) * c]
                zh = z_ref[r0:r0 + c, h * hd:(h + 1) * hd]
                o_ref[r0:r0 + c, h * hd:(h + 1) * hd] = _rms(o, nw_ref[...]) * _silu(zh)

    @pl.when(j == pl.num_programs(1) - 1)
    def _():
        sout_ref[0] = s_ref[...]


def _gdn_packed(proj_a, cols, s0, a_log, dt_bias, norm_w, *, n_seq, seq_len, block):
    heads = a_log.shape[1]
    v_w = heads * HEAD_DIM
    conv_width = 3 * v_w
    nblk = seq_len // block
    z_blk, ba_blk = cols["z"] // v_w, cols["ba"] // 128
    shared_s0 = s0.shape[0] == 1
    assert 2 * DN_CHUNK == HEAD_DIM and heads % GROUP == 0 and block % DN_CHUNK == 0
    return pl.pallas_call(
        functools.partial(_gdn_packed_kernel, chunk=DN_CHUNK, heads=heads),
        grid=(n_seq, nblk),
        in_specs=[
            pl.BlockSpec((block, conv_width), lambda n, j: (n * nblk + j, 0)),
            pl.BlockSpec((block, v_w), lambda n, j: (n * nblk + j, z_blk)),
            pl.BlockSpec((block, 128), lambda n, j: (n * nblk + j, ba_blk)),
            pl.BlockSpec((1, heads, HEAD_DIM, HEAD_DIM), lambda n, j: (0 if shared_s0 else n, 0, 0, 0)),
            pl.BlockSpec((1, heads), lambda n, j: (0, 0)),
            pl.BlockSpec((1, heads), lambda n, j: (0, 0)),
            pl.BlockSpec((1, HEAD_DIM), lambda n, j: (0, 0)),
        ],
        out_specs=[
            pl.BlockSpec((block, v_w), lambda n, j: (n * nblk + j, 0)),
            pl.BlockSpec((1, heads, HEAD_DIM, HEAD_DIM), lambda n, j: (n, 0, 0, 0)),
        ],
        out_shape=[
            jax.ShapeDtypeStruct((n_seq * seq_len, v_w), F32),
            jax.ShapeDtypeStruct((n_seq, heads, HEAD_DIM, HEAD_DIM), F32),
        ],
        scratch_shapes=[pltpu.VMEM((heads, HEAD_DIM, HEAD_DIM), F32)],
        compiler_params=_params("parallel", "arbitrary"),
        name="gdn_packed",
    )(proj_a, proj_a, proj_a, s0, a_log, dt_bias, norm_w)


def _gdn(proj_a, cols, hist, s0, conv_w, a_log, dt_bias, norm_w, *, n_seq, seq_len, block, chunk):
    heads = a_log.shape[1]
    conv_width = conv_w.shape[1]
    v_w = heads * HEAD_DIM
    nblk = seq_len // block
    z_blk, ba_blk = cols["z"] // v_w, cols["ba"] // 128
    shared_hist, shared_s0 = hist.shape[0] == 1, s0.shape[0] == 1
    return pl.pallas_call(
        functools.partial(_gdn_kernel, chunk=chunk, heads=heads),
        grid=(n_seq, nblk),
        in_specs=[
            pl.BlockSpec((block, conv_width), lambda n, j: (n * nblk + j, 0)),
            pl.BlockSpec((block, v_w), lambda n, j: (n * nblk + j, z_blk)),
            pl.BlockSpec((block, 128), lambda n, j: (n * nblk + j, ba_blk)),
            pl.BlockSpec((1, HIST_ROWS, conv_width), lambda n, j: (0 if shared_hist else n, 0, 0)),
            pl.BlockSpec((1, heads, HEAD_DIM, HEAD_DIM), lambda n, j: (0 if shared_s0 else n, 0, 0, 0)),
            pl.BlockSpec((DN_CONV, conv_width), lambda n, j: (0, 0)),
            pl.BlockSpec((1, heads), lambda n, j: (0, 0)),
            pl.BlockSpec((1, heads), lambda n, j: (0, 0)),
            pl.BlockSpec((1, HEAD_DIM), lambda n, j: (0, 0)),
        ],
        out_specs=[
            pl.BlockSpec((block, v_w), lambda n, j: (n * nblk + j, 0)),
            pl.BlockSpec((1, heads, HEAD_DIM, HEAD_DIM), lambda n, j: (n, 0, 0, 0)),
        ],
        out_shape=[
            jax.ShapeDtypeStruct((n_seq * seq_len, v_w), F32),
            jax.ShapeDtypeStruct((n_seq, heads, HEAD_DIM, HEAD_DIM), F32),
        ],
        scratch_shapes=[
            pltpu.VMEM((block + HIST_ROWS, conv_width), F32),
            pltpu.VMEM((heads, HEAD_DIM, HEAD_DIM), F32),
        ],
        compiler_params=_params("parallel", "arbitrary"),
        name="gdn",
    )(proj_a, proj_a, proj_a, hist, s0, conv_w, a_log, dt_bias, norm_w)


def _gdn_step_kernel(qkv_ref, z_ref, ba_ref, hist_ref, s0_ref, convw_ref, alog_ref, dtb_ref, nw_ref,
                     o_ref, sout_ref, *, heads):
    nb, t, _ = qkv_ref.shape
    hd = HEAD_DIM
    qk_w = heads * hd
    ti = lax.broadcasted_iota(jnp.int32, (t, t), 0)
    tj = lax.broadcasted_iota(jnp.int32, (t, t), 1)
    lower_ones = (ti >= tj).astype(F32)
    upper_ones = (ti <= tj).astype(F32)

    def per_sequence(b, carry):
        raw = jnp.concatenate([hist_ref[b], qkv_ref[b]], axis=0)
        acc = raw[HIST_ROWS:] * convw_ref[DN_CONV - 1:DN_CONV, :]
        for back in range(1, DN_CONV):
            acc = acc + pltpu.roll(raw, back, 0)[HIST_ROWS:] * convw_ref[DN_CONV - 1 - back:DN_CONV - back, :]
        x = _silu(acc)
        ba = ba_ref[b]
        beta_all = jax.nn.sigmoid(ba[:, 0:heads])
        a_in = ba[:, heads:2 * heads] + dtb_ref[...]
        softplus = jnp.maximum(a_in, 0.0) + jnp.log(1.0 + jnp.exp(-jnp.abs(a_in)))
        g_all = -jnp.exp(alog_ref[...]) * softplus
        gc_all = _dot_hi(lower_ones, g_all)
        gc_row_all = _dot_tn_hi(g_all, upper_ones)
        z = z_ref[b]
        hs = range(heads)
        qs, ks, vs = [], [], []
        for h in hs:
            q = x[:, h * hd:(h + 1) * hd]
            k = x[:, qk_w + h * hd:qk_w + (h + 1) * hd]
            qs.append(q * lax.rsqrt(jnp.sum(q * q, axis=-1, keepdims=True) + L2_EPS) * (hd ** -0.5))
            ks.append(k * lax.rsqrt(jnp.sum(k * k, axis=-1, keepdims=True) + L2_EPS))
            vs.append(x[:, 2 * qk_w + h * hd:2 * qk_w + (h + 1) * hd])
        gcs = [gc_all[:, h:h + 1] for h in hs]
        gc_lasts = [gc_row_all[h:h + 1, t - 1:t] for h in hs]
        kq = [jnp.concatenate([ks[h], qs[h]], axis=0) for h in hs]
        gram = [_mm(kq[h], ks[h], "nt", STEP_PASSES) for h in hs]
        from_state = [_mm(kq[h] * jnp.exp(jnp.concatenate([gcs[h], gcs[h]], axis=0)), s0_ref[b, h], "nn",
                          STEP_PASSES) for h in hs]
        us, outs = [], []
        for h in hs:
            decay = jnp.where(ti >= tj, jnp.exp(jnp.where(ti >= tj, gcs[h] - gc_row_all[h:h + 1, :], 0.0)), 0.0)
            beta = beta_all[:, h:h + 1]
            lower = jnp.where(ti > tj, beta * gram[h][:t] * decay, 0.0)
            u = beta * (vs[h] - from_state[h][:t])
            for j in range(t - 1):
                u = u - lower[:, j:j + 1] * u[j:j + 1, :]
            qk = gram[h][t:] * decay
            o = from_state[h][t:]
            for j in range(t):
                o = o + qk[:, j:j + 1] * u[j:j + 1, :]
            us.append(u)
            outs.append(o)
        for h in hs:
            sout_ref[b, h] = s0_ref[b, h] * jnp.exp(gc_lasts[h]) + _mm(ks[h] * jnp.exp(gc_lasts[h] - gcs[h]), us[h],
                                                                       "tn", STEP_PASSES)
        for h in hs:
            o_ref[b, :, h * hd:(h + 1) * hd] = _rms(outs[h], nw_ref[...]) * _silu(z[:, h * hd:(h + 1) * hd])
        return carry

    lax.fori_loop(0, nb, per_sequence, 0, unroll=2)


def _gdn_step(proj_a3, cols, hist, s0, conv_w, a_log, dt_bias, norm_w, *, seqs_per_step):
    n, t, _ = proj_a3.shape
    heads = a_log.shape[1]
    conv_width = conv_w.shape[1]
    v_w = heads * HEAD_DIM
    nb = seqs_per_step
    z_blk, ba_blk = cols["z"] // v_w, cols["ba"] // 128
    state_spec = pl.BlockSpec((nb, heads, HEAD_DIM, HEAD_DIM), lambda i: (i, 0, 0, 0))
    return pl.pallas_call(
        functools.partial(_gdn_step_kernel, heads=heads),
        grid=(n // nb,),
        in_specs=[
            pl.BlockSpec((nb, t, conv_width), lambda i: (i, 0, 0)),
            pl.BlockSpec((nb, t, v_w), lambda i: (i, 0, z_blk)),
            pl.BlockSpec((nb, t, 128), lambda i: (i, 0, ba_blk)),
            pl.BlockSpec((nb, HIST_ROWS, conv_width), lambda i: (i, 0, 0)),
            state_spec,
            pl.BlockSpec((DN_CONV, conv_width), lambda i: (0, 0)),
            pl.BlockSpec((1, heads), lambda i: (0, 0)),
            pl.BlockSpec((1, heads), lambda i: (0, 0)),
            pl.BlockSpec((1, HEAD_DIM), lambda i: (0, 0)),
        ],
        out_specs=[pl.BlockSpec((nb, t, v_w), lambda i: (i, 0, 0)), state_spec],
        out_shape=[jax.ShapeDtypeStruct((n, t, v_w), F32),
                   jax.ShapeDtypeStruct((n, heads, HEAD_DIM, HEAD_DIM), F32)],
        compiler_params=_params("parallel"),
        name="gdn_step",
    )(proj_a3, proj_a3, proj_a3, hist, s0, conv_w, a_log, dt_bias, norm_w)


def _alibi_slope(head, n_heads):
    return 2.0 ** (-8.0 * (head + 1) / n_heads)


def _penalty(dist, mask):
    return jnp.where(mask, jnp.minimum(dist, WINDOW).astype(F32), -NEG_BIG)


def _attend(jobs, n_rows):
    scale = HEAD_DIM ** -0.5
    hd = HEAD_DIM
    scores = [[_dot_nt(q4, k) * scale for k, _, _ in segs] for q4, segs, _, _ in jobs]
    v_ones = [[jnp.concatenate([v, jnp.ones_like(v)], axis=1) for _, v, _ in segs] for _, segs, _, _ in jobs]
    heads = [(ji, g) for ji in range(len(jobs)) for g in range(len(jobs[ji][3]))]
    logits, maxes = {}, {}
    for ji, g in heads:
        _, segs, _, slopes = jobs[ji]
        r = slice(g * n_rows, (g + 1) * n_rows)
        logits[ji, g] = [sc[r] - slopes[g] * pen for sc, (_, _, pen) in zip(scores[ji], segs)]
    for ji, g in heads:
        by_width = {}
        for lg in logits[ji, g]:
            w = lg.shape[1]
            by_width[w] = lg if w not in by_width else jnp.maximum(by_width[w], lg)
        m = jobs[ji][2][g]
        for lg in by_width.values():
            m = jnp.maximum(m, jnp.max(lg, axis=-1, keepdims=True))
        maxes[ji, g] = m
    accs = {}
    for ji, g in heads:
        acc = None
        for lg, v1 in zip(logits[ji, g], v_ones[ji]):
            pv = _dot(jnp.exp(lg - maxes[ji, g]).astype(BF16), v1)
            acc = pv if acc is None else acc + pv
        accs[ji, g] = acc
    outs = [[None] * len(job[3]) for job in jobs]
    for ji, g in heads:
        acc = accs[ji, g]
        outs[ji][g] = acc[:, :hd] / (acc[:, hd:] + jnp.exp(jobs[ji][2][g] - maxes[ji, g]))
    return outs


def _swa_prompt_kernel(q_ref, kv_ref, kvprev_ref, kvmeta_ref, sinks_ref, o_ref, *, n_meta, kv_heads):
    j = pl.program_id(1)
    tq = WINDOW
    n_sub = q_ref.shape[0] // tq
    hd = HEAD_DIM
    kv_w = kv_heads * hd
    n_heads = kv_heads * SWA_GROUP
    qi = lax.broadcasted_iota(jnp.int32, (tq, tq), 0)
    ki = lax.broadcasted_iota(jnp.int32, (tq, tq), 1)
    dist_own = qi - ki
    dist_prev = dist_own + tq
    pen_own = _penalty(dist_own, dist_own >= 0)
    pen_prev = _penalty(dist_prev, dist_prev <= WINDOW)
    pen_prev_first = _penalty(dist_prev, (dist_prev <= WINDOW) & (j > 0))
    jobs, job_dst = [], []
    for sub in range(n_sub):
        rows = slice(sub * tq, (sub + 1) * tq)
        prev_ref, prev_rows = (kvprev_ref, slice(0, tq)) if sub == 0 else (kv_ref, slice((sub - 1) * tq, sub * tq))
        qpos = n_meta + (j * n_sub + sub) * tq + lax.broadcasted_iota(jnp.int32, (tq, n_meta), 0)
        dist_meta = qpos - lax.broadcasted_iota(jnp.int32, (tq, n_meta), 1)
        pen_meta = _penalty(dist_meta, dist_meta >= 0)
        for kvh in range(kv_heads):
            ks = slice(kvh * hd, (kvh + 1) * hd)
            vs = slice(kv_w + kvh * hd, kv_w + (kvh + 1) * hd)
            segments = [
                (kvmeta_ref[:, ks].astype(BF16), kvmeta_ref[:, vs].astype(BF16), pen_meta),
                (prev_ref[prev_rows, ks].astype(BF16), prev_ref[prev_rows, vs].astype(BF16),
                 pen_prev_first if sub == 0 else pen_prev),
                (kv_ref[rows, ks].astype(BF16), kv_ref[rows, vs].astype(BF16), pen_own),
            ]
            heads = [kvh * SWA_GROUP + g for g in range(SWA_GROUP)]
            q4 = jnp.concatenate([q_ref[rows, hh * hd:(hh + 1) * hd] for hh in heads], axis=0).astype(BF16)
            jobs.append((q4, segments, [sinks_ref[:, hh:hh + 1] for hh in heads],
                         [_alibi_slope(hh, n_heads) for hh in heads]))
            job_dst.append((rows, heads))
    for (rows, heads), outs in zip(job_dst, _attend(jobs, tq)):
        for hh, o in zip(heads, outs):
            o_ref[rows, hh * hd:(hh + 1) * hd] = o


def _swa_prompt(proj_b, proj_b_meta, cols, sinks, *, n_seq, seq_len, n_meta):
    n_heads = sinks.shape[1]
    kv_heads = n_heads // SWA_GROUP
    q_w, kv_w2 = n_heads * HEAD_DIM, 2 * kv_heads * HEAD_DIM
    n_sub = SWA_SUBBLOCKS if seq_len % (SWA_SUBBLOCKS * WINDOW) == 0 else 1
    tq = n_sub * WINDOW
    nblk = seq_len // tq
    kv_blk = cols["skv"] // kv_w2
    return pl.pallas_call(
        functools.partial(_swa_prompt_kernel, n_meta=n_meta, kv_heads=kv_heads),
        grid=(n_seq, nblk),
        in_specs=[
            pl.BlockSpec((tq, q_w), lambda n, j: (n * nblk + j, 0)),
            pl.BlockSpec((tq, kv_w2), lambda n, j: (n * nblk + j, kv_blk)),
            pl.BlockSpec((WINDOW, kv_w2), lambda n, j: (jnp.maximum((n * nblk + j) * n_sub - 1, 0), kv_blk)),
            pl.BlockSpec((n_meta, kv_w2), lambda n, j: (0, kv_blk)),
            pl.BlockSpec((1, n_heads), lambda n, j: (0, 0)),
        ],
        out_specs=pl.BlockSpec((tq, q_w), lambda n, j: (n * nblk + j, 0)),
        out_shape=jax.ShapeDtypeStruct((n_seq * seq_len, q_w), F32),
        compiler_params=_params("parallel", "arbitrary"),
        name="swa_prompt",
    )(proj_b, proj_b, proj_b, proj_b_meta, sinks)


def _swa_step_kernel(q_ref, kvnew_ref, kmeta_ref, vmeta_ref, kbuf_ref, vbuf_ref, sinks_ref,
                     o_ref, kout_ref, vout_ref, *, kv_heads):
    nb, t, _ = q_ref.shape
    n_meta, w = kmeta_ref.shape[1], kbuf_ref.shape[1]
    hd = HEAD_DIM
    kv_w = kv_heads * hd
    n_heads = kv_heads * SWA_GROUP

    def grid2(rows, cols_):
        return (lax.broadcasted_iota(jnp.int32, (rows, cols_), 0), lax.broadcasted_iota(jnp.int32, (rows, cols_), 1))

    ti, mi = grid2(t, n_meta)
    dist_meta = PAST_LEN + ti - mi
    pen_meta = _penalty(dist_meta, dist_meta >= 0)
    ti, bi = grid2(t, w)
    dist_buf = w + ti - bi
    pen_buf = _penalty(dist_buf, (dist_buf <= WINDOW) & (PAST_LEN - w + bi >= n_meta))
    ti, si = grid2(t, t)
    dist_new = ti - si
    pen_new = _penalty(dist_new, dist_new >= 0)
    jobs, job_dst = [], []
    for b in range(nb):
        kv_new = kvnew_ref[b]
        for kvh in range(kv_heads):
            ks = slice(kvh * hd, (kvh + 1) * hd)
            k_old, v_old = kbuf_ref[b, :, kvh, :], vbuf_ref[b, :, kvh, :]
            k_new, v_new = kv_new[:, ks], kv_new[:, kv_w + kvh * hd:kv_w + (kvh + 1) * hd]
            kout_ref[b, 0:w - t, kvh, :] = k_old[t:w]
            kout_ref[b, w - t:w, kvh, :] = k_new
            vout_ref[b, 0:w - t, kvh, :] = v_old[t:w]
            vout_ref[b, w - t:w, kvh, :] = v_new
            segments = [
                (kmeta_ref[b, :, ks].astype(BF16), vmeta_ref[b, :, ks].astype(BF16), pen_meta),
                (k_old.astype(BF16), v_old.astype(BF16), pen_buf),
                (k_new.astype(BF16), v_new.astype(BF16), pen_new),
            ]
            heads = [kvh * SWA_GROUP + g for g in range(SWA_GROUP)]
            q4 = jnp.concatenate([q_ref[b, :, hh * hd:(hh + 1) * hd] for hh in heads], axis=0).astype(BF16)
            jobs.append((q4, segments, [sinks_ref[:, hh:hh + 1] for hh in heads],
                         [_alibi_slope(hh, n_heads) for hh in heads]))
            job_dst.append((b, heads))
    for (b, heads), outs in zip(job_dst, _attend(jobs, t)):
        for hh, o in zip(heads, outs):
            o_ref[b, :, hh * hd:(hh + 1) * hd] = o


def _swa_step(proj_b3, cols, k_meta, v_meta, k_buf, v_buf, sinks, *, seqs_per_step):
    n, t, _ = proj_b3.shape
    n_heads = sinks.shape[1]
    kv_heads = n_heads // SWA_GROUP
    q_w, kv_w = n_heads * HEAD_DIM, kv_heads * HEAD_DIM
    n_meta, w = k_meta.shape[1], k_buf.shape[1]
    nb = seqs_per_step
    kv_blk = cols["skv"] // (2 * kv_w)
    seq3 = lambda rows, width, blk=0: pl.BlockSpec((nb, rows, width), lambda i: (i, 0, blk))
    cache_spec = pl.BlockSpec((nb, w, kv_heads, HEAD_DIM), lambda i: (i, 0, 0, 0))
    return pl.pallas_call(
        functools.partial(_swa_step_kernel, kv_heads=kv_heads),
        grid=(n // nb,),
        in_specs=[seq3(t, q_w), seq3(t, 2 * kv_w, kv_blk), seq3(n_meta, kv_w), seq3(n_meta, kv_w),
                  cache_spec, cache_spec, pl.BlockSpec((1, n_heads), lambda i: (0, 0))],
        out_specs=[seq3(t, q_w), cache_spec, cache_spec],
        out_shape=[jax.ShapeDtypeStruct((n, t, q_w), F32),
                   jax.ShapeDtypeStruct((n, w, kv_heads, HEAD_DIM), F32),
                   jax.ShapeDtypeStruct((n, w, kv_heads, HEAD_DIM), F32)],
        compiler_params=_params("parallel"),
        name="swa_step",
    )(proj_b3, proj_b3, k_meta, v_meta, k_buf, v_buf, sinks)


def _split_w_in(w_in, heads, n_heads, kv_heads, d):
    conv_w = 3 * heads * HEAD_DIM
    v_w = heads * HEAD_DIM
    q_w, kv_w = n_heads * HEAD_DIM, kv_heads * HEAD_DIM
    sizes = (conv_w, v_w, heads, heads, q_w, kv_w, kv_w, d, d)
    offs = [0]
    for s in sizes:
        offs.append(offs[-1] + s)
    part = lambda i: w_in[:, offs[i]:offs[i + 1]].astype(BF16)
    w_a = jnp.concatenate([part(0), part(1), part(7), part(2), part(3),
                           jnp.zeros((w_in.shape[0], 128 - 2 * heads), BF16)], axis=1)
    w_b = jnp.concatenate([part(4), part(8), part(5), part(6)], axis=1)
    cols_a = {"qkv": 0, "z": conv_w, "g_dn": conv_w + v_w, "ba": conv_w + v_w + d}
    cols_b = {"q": 0, "g_swa": q_w, "skv": q_w + d}
    return w_a, w_b, cols_a, cols_b


def kernel(x_prompt, x_sample, state_dn_conv, state_dn_ssm, cache_swa_meta_k, cache_swa_meta_v, cache_swa_k, cache_swa_v, meta_tokens, ffn1_norm_pre, ffn1_norm_post, ffn1_w_gate, ffn1_w_up, ffn1_w_down, mix_norm_pre, mix_norm_post, w_in, dn_conv_w, dn_a_log, dn_dt_bias, dn_norm_w, swa_sinks, w_out, ffn2_norm_pre, ffn2_norm_post, ffn2_w_gate, ffn2_w_up, ffn2_w_down):
    assert w_in.shape[0] == 1, "single-layer step"
    n_p, seq, d = x_prompt.shape
    n_s, t_s, _ = x_sample.shape
    n_meta = meta_tokens.shape[0]
    heads = dn_a_log.shape[1]
    n_heads = swa_sinks.shape[1]
    kv_heads = n_heads // SWA_GROUP
    conv_w = dn_conv_w.shape[2]
    kv_w = kv_heads * HEAD_DIM
    w_keep = cache_swa_k.shape[2]
    assert w_keep == WINDOW and seq % WINDOW == 0 and seq % DN_CHUNK == 0 and seq >= WINDOW

    w_a, w_b, cols_a, cols_b = _split_w_in(w_in[0], heads, n_heads, kv_heads, d)
    cols = {**cols_a, **cols_b}
    f1 = (ffn1_norm_pre, ffn1_norm_post, ffn1_w_gate[0].astype(BF16), ffn1_w_up[0].astype(BF16),
          ffn1_w_down[0].astype(BF16))
    f2 = (ffn2_norm_pre, ffn2_norm_post, ffn2_w_gate[0].astype(BF16), ffn2_w_up[0].astype(BF16),
          ffn2_w_down[0].astype(BF16))
    w_o = w_out[0].astype(BF16)
    gdn_w = (dn_conv_w[0], dn_a_log, dn_dt_bias, dn_norm_w)

    def front(h):
        h1 = _ffn(h, *f1)
        return h1, _proj(h1, mix_norm_pre, w_a), _proj(h1, mix_norm_pre, w_b)

    def back(h1, o_dn, o_sw, pa, pb):
        return _mixffn(h1, o_dn, o_sw, pa, pb, cols, w_o, mix_norm_post, *f2)

    def hist_tile(rows3):
        return jnp.pad(rows3, ((0, 0), (HIST_ROWS - (DN_CONV - 1), 0), (0, 0)))

    _, pa_m, pb_m = front(meta_tokens)
    zero_hist = jnp.zeros((1, HIST_ROWS, conv_w), F32)
    zero_state = jnp.zeros((1, heads, HEAD_DIM, HEAD_DIM), F32)
    _, s_meta = _gdn(pa_m, cols, zero_hist, zero_state, *gdn_w, n_seq=1, seq_len=n_meta, block=n_meta,
                     chunk=n_meta)
    hist_meta = hist_tile(pa_m[None, n_meta - (DN_CONV - 1):, :conv_w])

    h1_p = _ffn(x_prompt.reshape(n_p * seq, d), *f1)
    pa_p, tail_p = _proj_conv(h1_p, mix_norm_pre, w_a, hist_meta, dn_conv_w[0], n_seq=n_p, seq_len=seq,
                              qk_width=heads * HEAD_DIM)
    pb_p = _proj(h1_p, mix_norm_pre, w_b)
    o_dn_p, s_p = _gdn_packed(pa_p, cols, s_meta, dn_a_log, dn_dt_bias, dn_norm_w, n_seq=n_p, seq_len=seq,
                              block=4 * DN_CHUNK)
    o_sw_p = _swa_prompt(pb_p, pb_m, cols, swa_sinks, n_seq=n_p, seq_len=seq, n_meta=n_meta)
    y_prompt = back(h1_p, o_dn_p, o_sw_p, pa_p, pb_p).reshape(n_p, seq, d)

    pb_p3 = pb_p.reshape(n_p, seq, -1)
    p_conv = tail_p[:, HIST_ROWS - (DN_CONV - 1):]
    kv_meta = pb_m[:, cols["skv"]:]
    p_meta_k = jnp.broadcast_to(kv_meta[None, :, :kv_w], (n_p, n_meta, kv_w))
    p_meta_v = jnp.broadcast_to(kv_meta[None, :, kv_w:], (n_p, n_meta, kv_w))
    p_win_k = pb_p3[:, seq - w_keep:, cols["skv"]:cols["skv"] + kv_w]
    p_win_v = pb_p3[:, seq - w_keep:, cols["skv"] + kv_w:]

    h1_s, pa_s, pb_s = front(x_sample.reshape(n_s * t_s, d))
    o_dn_s, s_s = _gdn_step(pa_s.reshape(n_s, t_s, -1), cols, hist_tile(state_dn_conv[0]), state_dn_ssm[0],
                            *gdn_w, seqs_per_step=4)
    o_dn_s = o_dn_s.reshape(n_s * t_s, -1)
    o_sw_s, s_win_k, s_win_v = _swa_step(
        pb_s.reshape(n_s, t_s, -1), cols,
        cache_swa_meta_k[0].reshape(n_s, n_meta, kv_w), cache_swa_meta_v[0].reshape(n_s, n_meta, kv_w),
        cache_swa_k[0], cache_swa_v[0], swa_sinks, seqs_per_step=4)
    y_sample = back(h1_s, o_dn_s, o_sw_s.reshape(n_s * t_s, -1), pa_s, pb_s).reshape(n_s, t_s, d)
    assert t_s >= DN_CONV - 1, "new conv state is taken from the new rows alone"
    s_conv = pa_s.reshape(n_s, t_s, -1)[:, t_s - (DN_CONV - 1):, :conv_w]

    def kv4(x):
        return x.reshape(1, x.shape[0], x.shape[1], kv_heads, HEAD_DIM)

    return (y_prompt, y_sample, p_conv[None], s_p[None], kv4(p_meta_k), kv4(p_meta_v), kv4(p_win_k),
            kv4(p_win_v), s_conv[None], s_s[None], s_win_k[None], s_win_v[None])
```

```python
import functools
import math

import jax
import jax.numpy as jnp
from jax import lax
from jax.experimental import pallas as pl
from jax.experimental.pallas import tpu as pltpu

F32 = jnp.float32
BF16 = jnp.bfloat16
HIGHEST = lax.Precision.HIGHEST

RMS_EPS = 1e-6
L2_EPS = 1e-6
WINDOW = 128
PAST_LEN = 8192
HEAD_DIM = 128
SWA_GROUP = 4
DN_CONV = 4
DN_CHUNK = 64
HIST_ROWS = 8

VMEM_LIMIT_BYTES = 56 * 1024 * 1024
ROW_TILE_MAX = 576
SUBLANES = 8
LANES = 128
MXU_WIDTH = 256
NEG_BIG = -1e30
SWA_SUBBLOCKS = 2


def _rms(x, g):
    return x * lax.rsqrt(jnp.mean(x * x, axis=-1, keepdims=True) + RMS_EPS) * g


def _silu(x):
    return x * jax.nn.sigmoid(x)


def _dot(a, b):
    return jnp.dot(a, b, preferred_element_type=F32)


def _dot_hi(a, b):
    return jnp.dot(a, b, preferred_element_type=F32, precision=HIGHEST)


def _dot_nt_hi(a, b):
    return lax.dot_general(a, b, (((1,), (1,)), ((), ())), preferred_element_type=F32, precision=HIGHEST)


def _dot_tn_hi(a, b):
    return lax.dot_general(a, b, (((0,), (0,)), ((), ())), preferred_element_type=F32, precision=HIGHEST)


def _dot_nt(a, b):
    return lax.dot_general(a, b, (((1,), (1,)), ((), ())), preferred_element_type=F32)


def _dot_tn(a, b):
    return lax.dot_general(a, b, (((0,), (0,)), ((), ())), preferred_element_type=F32)


def _split_bf16(a):
    hi = a.astype(BF16)
    return hi, (a - hi.astype(F32)).astype(BF16)


_DOTS = {"nn": _dot, "nt": _dot_nt, "tn": _dot_tn}


def _dot3_split(a_pair, b_pair, kind="nn"):
    (a_hi, a_lo), (b_hi, b_lo) = a_pair, b_pair
    dot = _DOTS[kind]
    axis = 1 if kind == "tn" else 0
    m = a_hi.shape[axis]
    top = dot(jnp.concatenate([a_hi, a_lo], axis=axis), b_hi)
    return top[:m] + top[m:] + dot(a_hi, b_lo)


def _mm(a, b, kind="nn", passes=3):
    if passes == 1:
        return _DOTS[kind](a.astype(BF16), b.astype(BF16))
    return _dot3_split(_split_bf16(a), _split_bf16(b), kind)


def _row_tile(rows):
    start = min(ROW_TILE_MAX, rows) // SUBLANES * SUBLANES
    for tile in range(start, 0, -SUBLANES):
        if rows % tile == 0:
            return tile
    raise ValueError(f"no sublane-aligned row tile divides {rows}")


def _resident(shape):
    return pl.BlockSpec(shape, lambda *_: (0,) * len(shape), pipeline_mode=pl.Buffered(1))


def _params(*semantics):
    return pltpu.CompilerParams(dimension_semantics=semantics, vmem_limit_bytes=VMEM_LIMIT_BYTES)


def _ffn_tile(h, g_pre, g_post, wg_ref, wu_ref, wd_ref, ff_chunk):
    xn = _rms(h, g_pre).astype(BF16)
    acc = None
    c0 = 0
    for width in ff_chunk:
        gate = _dot(xn, wg_ref[:, c0:c0 + width])
        up = _dot(xn, wu_ref[:, c0:c0 + width])
        part = _dot((_silu(gate) * up).astype(BF16), wd_ref[c0:c0 + width, :])
        acc = part if acc is None else acc + part
        c0 += width
    return h + 0.5 * _rms(acc, g_post)


def _ff_chunk(d_ff):
    tiles, rem = divmod(d_ff, MXU_WIDTH)
    if rem or tiles < 2:
        return (d_ff,)
    first = (tiles + 1) // 2
    return (first * MXU_WIDTH, (tiles - first) * MXU_WIDTH)


def _ffn_kernel(h_ref, gpre_ref, gpost_ref, wg_ref, wu_ref, wd_ref, o_ref, *, ff_chunk):
    o_ref[...] = _ffn_tile(h_ref[...], gpre_ref[...], gpost_ref[...], wg_ref, wu_ref, wd_ref, ff_chunk)


def _ffn(h, g_pre, g_post, wg, wu, wd):
    rows, d = h.shape
    d_ff = wg.shape[1]
    tm = _row_tile(rows)
    row_spec = pl.BlockSpec((tm, d), lambda i: (i, 0))
    return pl.pallas_call(
        functools.partial(_ffn_kernel, ff_chunk=_ff_chunk(d_ff)),
        grid=(rows // tm,),
        in_specs=[row_spec, _resident((1, d)), _resident((1, d)),
                  _resident((d, d_ff)), _resident((d, d_ff)), _resident((d_ff, d))],
        out_specs=row_spec,
        out_shape=jax.ShapeDtypeStruct((rows, d), F32),
        compiler_params=_params("parallel"),
        name="ffn",
    )(h, g_pre, g_post, wg, wu, wd)


def _proj_kernel(h_ref, g_ref, w_ref, o_ref):
    o_ref[...] = _dot(_rms(h_ref[...], g_ref[...]).astype(BF16), w_ref[...])


def _proj(h, g, w, n_cols=None):
    rows, d = h.shape
    n = w.shape[1] if n_cols is None else n_cols
    tm = _row_tile(rows)
    return pl.pallas_call(
        _proj_kernel,
        grid=(rows // tm,),
        in_specs=[pl.BlockSpec((tm, d), lambda i: (i, 0)), _resident((1, d)), _resident((d, n))],
        out_specs=pl.BlockSpec((tm, n), lambda i: (i, 0)),
        out_shape=jax.ShapeDtypeStruct((rows, n), F32),
        compiler_params=_params("parallel"),
        name="proj",
    )(h, g, w)


def _proj_conv_kernel(h_ref, g_ref, w_ref, hist_ref, convw_ref, o_ref, tail_ref, ybuf_ref, *, conv_width, qk_width):
    j = pl.program_id(1)
    tm = h_ref.shape[0]
    hd = HEAD_DIM
    xn = _rms(h_ref[...], g_ref[...]).astype(BF16)

    @pl.when(j == 0)
    def _():
        ybuf_ref[0:HIST_ROWS, :] = hist_ref[0]

    @pl.when(j > 0)
    def _():
        ybuf_ref[0:HIST_ROWS, :] = tail_ref[0]

    def project_raw(c0, width):
        ybuf_ref[HIST_ROWS:HIST_ROWS + tm, c0:c0 + width] = _dot(xn, w_ref[:, c0:c0 + width])

    def conv_group(g0, width):
        tail_ref[0, :, g0:g0 + width] = ybuf_ref[tm:tm + HIST_ROWS, g0:g0 + width]
        for c0 in range(g0, g0 + width, hd):
            x = ybuf_ref[:, c0:c0 + hd]
            acc = x[HIST_ROWS:] * convw_ref[DN_CONV - 1:DN_CONV, c0:c0 + hd]
            for back in range(1, DN_CONV):
                shifted = pltpu.roll(x, back, 0)[HIST_ROWS:]
                acc = acc + shifted * convw_ref[DN_CONV - 1 - back:DN_CONV - back, c0:c0 + hd]
            y = _silu(acc)
            if c0 < 2 * qk_width:
                y = y * lax.rsqrt(jnp.sum(y * y, axis=-1, keepdims=True) + L2_EPS)
                if c0 < qk_width:
                    y = y * (hd ** -0.5)
            o_ref[:, c0:c0 + hd] = y

    group = 2 * MXU_WIDTH
    starts = list(range(0, conv_width, group))
    n_rest = w_ref.shape[1] - conv_width
    rest_cuts = ([conv_width + (n_rest * i // len(starts)) // MXU_WIDTH * MXU_WIDTH for i in range(len(starts))]
                 + [w_ref.shape[1]])
    project_raw(starts[0], group)
    for gi, g0 in enumerate(starts):
        if gi + 1 < len(starts):
            project_raw(starts[gi + 1], group)
        r0, r1 = rest_cuts[gi], rest_cuts[gi + 1]
        if r1 > r0:
            o_ref[:, r0:r1] = _dot(xn, w_ref[:, r0:r1])
        conv_group(g0, group)


def _proj_conv(h, g, w, n_cols, hist, conv_w, *, n_seq, seq_len, qk_width):
    rows, d = h.shape
    n = n_cols
    conv_width = conv_w.shape[1]
    tm = _row_tile(seq_len)
    tiles = seq_len // tm
    shared_hist = hist.shape[0] == 1
    return pl.pallas_call(
        functools.partial(_proj_conv_kernel, conv_width=conv_width, qk_width=qk_width),
        grid=(n_seq, tiles),
        in_specs=[pl.BlockSpec((tm, d), lambda s, j: (s * tiles + j, 0)), _resident((1, d)), _resident((d, n)),
                  pl.BlockSpec((1, HIST_ROWS, conv_width), lambda s, j: (0 if shared_hist else s, 0, 0)),
                  _resident((DN_CONV, conv_width))],
        out_specs=[pl.BlockSpec((tm, n), lambda s, j: (s * tiles + j, 0)),
                   pl.BlockSpec((1, HIST_ROWS, conv_width), lambda s, j: (s, 0, 0))],
        out_shape=[jax.ShapeDtypeStruct((rows, n), F32),
                   jax.ShapeDtypeStruct((n_seq, HIST_ROWS, conv_width), F32)],
        scratch_shapes=[pltpu.VMEM((HIST_ROWS + tm, conv_width), F32)],
        compiler_params=_params("parallel", "arbitrary"),
        name="proj_conv",
    )(h, g, w, hist, conv_w)


def _mixffn_kernel(h_ref, odn_ref, osw_ref, gdn_ref, gsw_ref, wout_ref, gmix_ref,
                   gpre_ref, gpost_ref, wg_ref, wu_ref, wd_ref, o_ref, *, ff_chunk):
    y = jax.nn.sigmoid(gdn_ref[...]) * odn_ref[...] + jax.nn.sigmoid(gsw_ref[...]) * osw_ref[...]
    h2 = h_ref[...] + _rms(_dot(y.astype(BF16), wout_ref[...]), gmix_ref[...])
    o_ref[...] = _ffn_tile(h2, gpre_ref[...], gpost_ref[...], wg_ref, wu_ref, wd_ref, ff_chunk)


def _mixffn(h, o_dn, o_sw, proj_b, cols, w_out, g_mix, g_pre, g_post, wg, wu, wd, rows=None):
    d = h.shape[1]
    rows = h.shape[0] if rows is None else rows
    d_ff = wg.shape[1]
    tm = _row_tile(rows)
    row_spec = pl.BlockSpec((tm, d), lambda i: (i, 0))
    gdn_blk, gsw_blk = cols["g_dn"] // d, cols["g_swa"] // d
    return pl.pallas_call(
        functools.partial(_mixffn_kernel, ff_chunk=_ff_chunk(d_ff)),
        grid=(rows // tm,),
        in_specs=[row_spec, row_spec, row_spec,
                  pl.BlockSpec((tm, d), lambda i: (i, gdn_blk)),
                  pl.BlockSpec((tm, d), lambda i: (i, gsw_blk)),
                  _resident((d, d)), _resident((1, d)), _resident((1, d)), _resident((1, d)),
                  _resident((d, d_ff)), _resident((d, d_ff)), _resident((d_ff, d))],
        out_specs=row_spec,
        out_shape=jax.ShapeDtypeStruct((rows, d), F32),
        compiler_params=_params("parallel"),
        name="mixffn",
    )(h, o_dn, o_sw, proj_b, proj_b, w_out, g_mix, g_pre, g_post, wg, wu, wd)


GROUP = 4
PASSES_QK = 1
PASSES_STATE = 1
PASSES_INV = 3
INV_BLOCK = 16
STEP_PASSES = 3


def _gdn_packed_kernel(qkv_ref, z_ref, ba_ref, s0_ref, alog_ref, dtb_ref, nw_ref,
                       o_ref, sout_ref, s_ref, *, chunk, heads):
    j = pl.program_id(1)
    tb = qkv_ref.shape[0]
    hd = HEAD_DIM
    c = chunk
    gw = GROUP * c
    qk_w = heads * hd

    @pl.when(j == 0)
    def _():
        s_ref[...] = s0_ref[0]

    def iota(shape, dim):
        return lax.broadcasted_iota(jnp.int32, shape, dim)

    row = iota((c, gw), 0)
    col = iota((c, gw), 1) & (c - 1)
    causal4, strict4 = row >= col, row > col
    diag_blocks4 = (row // INV_BLOCK) == (col // INV_BLOCK)
    eye4 = (row == col).astype(F32)
    bd_mask = (iota((gw, gw), 0) // c) == (iota((gw, gw), 1) // c)
    lower_ones = (iota((c, c), 0) >= iota((c, c), 1)).astype(F32)
    upper_dup = (iota((c, 2 * c), 0) <= (iota((c, 2 * c), 1) & (c - 1))).astype(F32)
    first_half = iota((c, 2 * c), 1) < c
    zeros_head = jnp.zeros((c, hd), F32)

    def block_diag(x4):
        return jnp.where(bd_mask, jnp.concatenate([x4] * GROUP, axis=0), jnp.zeros((), x4.dtype))

    splits = {}

    def split(x):
        if id(x) not in splits:
            splits[id(x)] = (x, _split_bf16(x))
        return splits[id(x)][1]

    def times_block_diag(lhs, x4, passes):
        parts = lhs if isinstance(lhs, list) else [lhs]
        if passes == 1:
            return _dot(jnp.concatenate(parts, axis=0).astype(BF16), block_diag(x4.astype(BF16)))
        hi, lo = split(x4)
        lhs_split = tuple(jnp.concatenate(pieces, axis=0) for pieces in zip(*[split(p) for p in parts]))
        return _dot3_split(lhs_split, (block_diag(hi), block_diag(lo)))

    def block_diag_times(x4, rhs, passes):
        if passes == 1:
            return _dot(block_diag(x4.astype(BF16)), rhs.astype(BF16))
        hi, lo = _split_bf16(x4)
        return _dot3_split((block_diag(hi), block_diag(lo)), _split_bf16(rhs))

    def pack_cols(col_all, hs):
        halves = [jnp.where(first_half,
                            jnp.broadcast_to(col_all[:, hs[2 * p]:hs[2 * p] + 1], (c, 2 * c)),
                            jnp.broadcast_to(col_all[:, hs[2 * p + 1]:hs[2 * p + 1] + 1], (c, 2 * c)))
                  for p in range(GROUP // 2)]
        return jnp.concatenate(halves, axis=1)

    chunks = []
    for r0 in range(0, tb, c):
        ba = ba_ref[r0:r0 + c, :]
        beta_all = jax.nn.sigmoid(ba[:, 0:heads])
        a_in = ba[:, heads:2 * heads] + dtb_ref[...]
        softplus = jnp.maximum(a_in, 0.0) + jnp.log(1.0 + jnp.exp(-jnp.abs(a_in)))
        g_all = -jnp.exp(alog_ref[...]) * softplus
        gc_all = _dot_hi(lower_ones, g_all)
        gc_row_dup = _dot_tn_hi(g_all, upper_dup)
        chunks.append(dict(r0=r0, beta=beta_all, gc=gc_all, gc_row=gc_row_dup))

    items = []
    for ch in chunks:
        for g0 in range(0, heads, GROUP):
            hs = list(range(g0, g0 + GROUP))
            qs, ks, vs = [], [], []
            rows = slice(ch["r0"], ch["r0"] + c)
            for h in hs:
                qs.append(qkv_ref[rows, h * hd:(h + 1) * hd])
                ks.append(qkv_ref[rows, qk_w + h * hd:qk_w + (h + 1) * hd])
                vs.append(qkv_ref[rows, 2 * qk_w + h * hd:2 * qk_w + (h + 1) * hd])
            items.append(dict(ch=ch, hs=hs, qs=qs, ks=ks, vs=vs))

    for it in items:
        ch, hs, qs, ks = it["ch"], it["hs"], it["qs"], it["ks"]
        beta4 = pack_cols(ch["beta"], hs)
        gc_row4 = jnp.concatenate(
            [jnp.where(first_half[0:1], ch["gc_row"][hs[2 * p]:hs[2 * p] + 1],
                       ch["gc_row"][hs[2 * p + 1]:hs[2 * p + 1] + 1])
             for p in range(GROUP // 2)], axis=1)
        diff4 = pack_cols(ch["gc"], hs) - gc_row4
        decay4 = jnp.where(causal4, jnp.exp(jnp.where(causal4, diff4, 0.0)), 0.0)
        lhs = jnp.concatenate([jnp.concatenate(qs, axis=1), jnp.concatenate(ks, axis=1)], axis=0)
        k_bd = jnp.concatenate(
            [jnp.concatenate([ks[i] if ii == i else zeros_head for ii in range(GROUP)], axis=1)
             for i in range(GROUP)], axis=0)
        qkkk = _mm(lhs, k_bd, "nt", PASSES_QK)
        it["qk4"] = qkkk[:c] * decay4
        a4 = jnp.where(strict4, beta4 * qkkk[c:] * decay4, 0.0)
        a_diag = jnp.where(diag_blocks4, a4, 0.0)
        it["a_diag"], it["a_off"] = a_diag, a4 - a_diag
        it["t4"] = eye4 - a_diag

    assert c // INV_BLOCK == 4
    for it in items:
        it["p4"] = times_block_diag(it["a_diag"], it["a_diag"], PASSES_INV)
    n_sq = int(math.log2(INV_BLOCK)) - 1
    for step in range(n_sq):
        for it in items:
            if step < n_sq - 1:
                both = times_block_diag([it["p4"], it["t4"]], it["p4"], PASSES_INV)
                it["p4"], it["t4"] = both[:c], it["t4"] + both[c:]
            else:
                it["t4"] = it["t4"] + times_block_diag(it["t4"], it["p4"], PASSES_INV)
    for it in items:
        it["m4"] = times_block_diag(it["t4"], it["a_off"], PASSES_INV)
    for it in items:
        it["m2"] = times_block_diag(it["m4"], it["m4"], PASSES_INV)
    for it in items:
        i_minus_m = eye4 - it["m4"]
        it["s4"] = i_minus_m + times_block_diag(i_minus_m, it["m2"], PASSES_INV)
    for it in items:
        it["t4"] = times_block_diag(it["s4"], it["t4"], PASSES_INV)

    for it in items:
        ch, hs, ks, vs = it["ch"], it["hs"], it["ks"], it["vs"]
        rhs = jnp.concatenate(
            [jnp.concatenate([vs[i] * ch["beta"][:, h:h + 1],
                              ks[i] * (ch["beta"][:, h:h + 1] * jnp.exp(ch["gc"][:, h:h + 1]))], axis=1)
             for i, h in enumerate(hs)], axis=0)
        it["sol"] = block_diag_times(it["t4"], rhs, PASSES_INV)

    for ch in chunks:
        group_items = [it for it in items if it["ch"] is ch]
        for it in group_items:
            us, o_inter = [], []
            for i, h in enumerate(it["hs"]):
                sl = slice(i * c, (i + 1) * c)
                ws = _mm(jnp.concatenate([it["sol"][sl, hd:], it["qs"][i] * jnp.exp(ch["gc"][:, h:h + 1])], axis=0),
                         s_ref[h], "nn", PASSES_STATE)
                us.append(it["sol"][sl, :hd] - ws[:c])
                o_inter.append(ws[c:])
            it["us"], it["o_inter"] = us, o_inter
        for it in group_items:
            it["o_intra"] = block_diag_times(it["qk4"], jnp.concatenate(it["us"], axis=0), PASSES_STATE)
        for it in group_items:
            for i, h in enumerate(it["hs"]):
                gc = ch["gc"][:, h:h + 1]
                gc_last = ch["gc_row"][h:h + 1, c - 1:c]
                s_ref[h] = s_ref[h] * jnp.exp(gc_last) + _mm(it["ks"][i] * jnp.exp(gc_last - gc), it["us"][i],
                                                             "tn", PASSES_STATE)
        for it in group_items:
            r0 = ch["r0"]
            for i, h in enumerate(it["hs"]):
                o = it["o_inter"][i] + it["o_intra"][i * c:(i + 1) * c]
                zh = z_ref[r0:r0 + c, h * hd:(h + 1) * hd]
                o_ref[r0:r0 + c, h * hd:(h + 1) * hd] = _rms(o, nw_ref[...]) * _silu(zh)

    @pl.when(j == pl.num_programs(1) - 1)
    def _():
        sout_ref[0] = s_ref[...]


def _gdn_packed(proj_a, cols, s0, a_log, dt_bias, norm_w, *, n_seq, seq_len, block):
    heads = a_log.shape[1]
    v_w = heads * HEAD_DIM
    conv_width = 3 * v_w
    nblk = seq_len // block
    z_blk, ba_blk = cols["z"] // v_w, cols["ba"] // 128
    shared_s0 = s0.shape[0] == 1
    assert 2 * DN_CHUNK == HEAD_DIM and heads % GROUP == 0 and block % DN_CHUNK == 0
    return pl.pallas_call(
        functools.partial(_gdn_packed_kernel, chunk=DN_CHUNK, heads=heads),
        grid=(n_seq, nblk),
        in_specs=[
            pl.BlockSpec((block, conv_width), lambda n, j: (n * nblk + j, 0)),
            pl.BlockSpec((block, v_w), lambda n, j: (n * nblk + j, z_blk)),
            pl.BlockSpec((block, 128), lambda n, j: (n * nblk + j, ba_blk)),
            pl.BlockSpec((1, heads, HEAD_DIM, HEAD_DIM), lambda n, j: (0 if shared_s0 else n, 0, 0, 0)),
            pl.BlockSpec((1, heads), lambda n, j: (0, 0)),
            pl.BlockSpec((1, heads), lambda n, j: (0, 0)),
            pl.BlockSpec((1, HEAD_DIM), lambda n, j: (0, 0)),
        ],
        out_specs=[
            pl.BlockSpec((block, v_w), lambda n, j: (n * nblk + j, 0)),
            pl.BlockSpec((1, heads, HEAD_DIM, HEAD_DIM), lambda n, j: (n, 0, 0, 0)),
        ],
        out_shape=[
            jax.ShapeDtypeStruct((n_seq * seq_len, v_w), F32),
            jax.ShapeDtypeStruct((n_seq, heads, HEAD_DIM, HEAD_DIM), F32),
        ],
        scratch_shapes=[pltpu.VMEM((heads, HEAD_DIM, HEAD_DIM), F32)],
        compiler_params=_params("parallel", "arbitrary"),
        name="gdn_packed",
    )(proj_a, proj_a, proj_a, s0, a_log, dt_bias, norm_w)


def _gdn_step_kernel(qkv_ref, z_ref, ba_ref, hist_ref, s0_ref, convw_ref, alog_ref, dtb_ref, nw_ref,
                     o_ref, sout_ref, *, heads):
    nb, t, _ = qkv_ref.shape
    hd = HEAD_DIM
    qk_w = heads * hd
    ti = lax.broadcasted_iota(jnp.int32, (t, t), 0)
    tj = lax.broadcasted_iota(jnp.int32, (t, t), 1)
    lower_ones = (ti >= tj).astype(F32)
    upper_ones = (ti <= tj).astype(F32)

    def per_sequence(b, carry):
        raw = jnp.concatenate([hist_ref[b], qkv_ref[b]], axis=0)
        acc = raw[HIST_ROWS:] * convw_ref[DN_CONV - 1:DN_CONV, :]
        for back in range(1, DN_CONV):
            acc = acc + pltpu.roll(raw, back, 0)[HIST_ROWS:] * convw_ref[DN_CONV - 1 - back:DN_CONV - back, :]
        x = _silu(acc)
        ba = ba_ref[b]
        beta_all = jax.nn.sigmoid(ba[:, 0:heads])
        a_in = ba[:, heads:2 * heads] + dtb_ref[...]
        softplus = jnp.maximum(a_in, 0.0) + jnp.log(1.0 + jnp.exp(-jnp.abs(a_in)))
        g_all = -jnp.exp(alog_ref[...]) * softplus
        gc_all = _dot_hi(lower_ones, g_all)
        gc_row_all = _dot_tn_hi(g_all, upper_ones)
        z = z_ref[b]
        hs = range(heads)
        qs, ks, vs = [], [], []
        for h in hs:
            q = x[:, h * hd:(h + 1) * hd]
            k = x[:, qk_w + h * hd:qk_w + (h + 1) * hd]
            qs.append(q * lax.rsqrt(jnp.sum(q * q, axis=-1, keepdims=True) + L2_EPS) * (hd ** -0.5))
            ks.append(k * lax.rsqrt(jnp.sum(k * k, axis=-1, keepdims=True) + L2_EPS))
            vs.append(x[:, 2 * qk_w + h * hd:2 * qk_w + (h + 1) * hd])
        gcs = [gc_all[:, h:h + 1] for h in hs]
        gc_lasts = [gc_row_all[h:h + 1, t - 1:t] for h in hs]
        kq = [jnp.concatenate([ks[h], qs[h]], axis=0) for h in hs]
        gram = [_mm(kq[h], ks[h], "nt", STEP_PASSES) for h in hs]
        from_state = [_mm(kq[h] * jnp.exp(jnp.concatenate([gcs[h], gcs[h]], axis=0)), s0_ref[b, h], "nn",
                          STEP_PASSES) for h in hs]
        us, outs = [], []
        for h in hs:
            decay = jnp.where(ti >= tj, jnp.exp(jnp.where(ti >= tj, gcs[h] - gc_row_all[h:h + 1, :], 0.0)), 0.0)
            beta = beta_all[:, h:h + 1]
            lower = jnp.where(ti > tj, beta * gram[h][:t] * decay, 0.0)
            u = beta * (vs[h] - from_state[h][:t])
            for j in range(t - 1):
                u = u - lower[:, j:j + 1] * u[j:j + 1, :]
            qk = gram[h][t:] * decay
            o = from_state[h][t:]
            for j in range(t):
                o = o + qk[:, j:j + 1] * u[j:j + 1, :]
            us.append(u)
            outs.append(o)
        for h in hs:
            sout_ref[b, h] = s0_ref[b, h] * jnp.exp(gc_lasts[h]) + _mm(ks[h] * jnp.exp(gc_lasts[h] - gcs[h]), us[h],
                                                                       "tn", STEP_PASSES)
        for h in hs:
            o_ref[b, :, h * hd:(h + 1) * hd] = _rms(outs[h], nw_ref[...]) * _silu(z[:, h * hd:(h + 1) * hd])
        return carry

    lax.fori_loop(0, nb, per_sequence, 0, unroll=2 if nb % 2 == 0 else 1)


def _gdn_step(proj_a3, cols, hist, s0, conv_w, a_log, dt_bias, norm_w, *, seqs_per_step):
    n, t = s0.shape[0], proj_a3.shape[1]
    heads = a_log.shape[1]
    conv_width = conv_w.shape[1]
    v_w = heads * HEAD_DIM
    nb = seqs_per_step
    z_blk, ba_blk = cols["z"] // v_w, cols["ba"] // 128
    state_spec = pl.BlockSpec((nb, heads, HEAD_DIM, HEAD_DIM), lambda i: (i, 0, 0, 0))
    return pl.pallas_call(
        functools.partial(_gdn_step_kernel, heads=heads),
        grid=(n // nb,),
        in_specs=[
            pl.BlockSpec((nb, t, conv_width), lambda i: (i, 0, 0)),
            pl.BlockSpec((nb, t, v_w), lambda i: (i, 0, z_blk)),
            pl.BlockSpec((nb, t, 128), lambda i: (i, 0, ba_blk)),
            pl.BlockSpec((nb, HIST_ROWS, conv_width), lambda i: (i, 0, 0)),
            state_spec,
            pl.BlockSpec((DN_CONV, conv_width), lambda i: (0, 0)),
            pl.BlockSpec((1, heads), lambda i: (0, 0)),
            pl.BlockSpec((1, heads), lambda i: (0, 0)),
            pl.BlockSpec((1, HEAD_DIM), lambda i: (0, 0)),
        ],
        out_specs=[pl.BlockSpec((nb, t, v_w), lambda i: (i, 0, 0)), state_spec],
        out_shape=[jax.ShapeDtypeStruct((n, t, v_w), F32),
                   jax.ShapeDtypeStruct((n, heads, HEAD_DIM, HEAD_DIM), F32)],
        compiler_params=_params("parallel"),
        name="gdn_step",
    )(proj_a3, proj_a3, proj_a3, hist, s0, conv_w, a_log, dt_bias, norm_w)


def _alibi_slope(head, n_heads):
    return 2.0 ** (-8.0 * (head + 1) / n_heads)


def _penalty(dist, mask):
    return jnp.where(mask, jnp.minimum(dist, WINDOW).astype(F32), -NEG_BIG)


def _attend(jobs, n_rows):
    scale = HEAD_DIM ** -0.5
    hd = HEAD_DIM
    scores = [[_dot_nt(q4, k) * scale for k, _, _ in segs] for q4, segs, _, _ in jobs]
    v_ones = [[jnp.concatenate([v, jnp.ones_like(v)], axis=1) for _, v, _ in segs] for _, segs, _, _ in jobs]
    heads = [(ji, g) for ji in range(len(jobs)) for g in range(len(jobs[ji][3]))]
    logits, maxes = {}, {}
    for ji, g in heads:
        _, segs, _, slopes = jobs[ji]
        r = slice(g * n_rows, (g + 1) * n_rows)
        logits[ji, g] = [sc[r] - slopes[g] * pen for sc, (_, _, pen) in zip(scores[ji], segs)]
    for ji, g in heads:
        by_width = {}
        for lg in logits[ji, g]:
            w = lg.shape[1]
            by_width[w] = lg if w not in by_width else jnp.maximum(by_width[w], lg)
        m = jobs[ji][2][g]
        for lg in by_width.values():
            m = jnp.maximum(m, jnp.max(lg, axis=-1, keepdims=True))
        maxes[ji, g] = m
    accs = {}
    for ji, g in heads:
        acc = None
        for lg, v1 in zip(logits[ji, g], v_ones[ji]):
            pv = _dot(jnp.exp(lg - maxes[ji, g]).astype(BF16), v1)
            acc = pv if acc is None else acc + pv
        accs[ji, g] = acc
    outs = [[None] * len(job[3]) for job in jobs]
    for ji, g in heads:
        acc = accs[ji, g]
        outs[ji][g] = acc[:, :hd] / (acc[:, hd:] + jnp.exp(jobs[ji][2][g] - maxes[ji, g]))
    return outs


def _swa_prompt_kernel(q_ref, kv_ref, kvprev_ref, kvmeta_ref, sinks_ref, o_ref, *, n_meta, kv_heads):
    j = pl.program_id(1)
    tq = WINDOW
    n_sub = q_ref.shape[0] // tq
    hd = HEAD_DIM
    kv_w = kv_heads * hd
    n_heads = kv_heads * SWA_GROUP
    qi = lax.broadcasted_iota(jnp.int32, (tq, tq), 0)
    ki = lax.broadcasted_iota(jnp.int32, (tq, tq), 1)
    dist_own = qi - ki
    dist_prev = dist_own + tq
    pen_own = _penalty(dist_own, dist_own >= 0)
    pen_prev = _penalty(dist_prev, dist_prev <= WINDOW)
    pen_prev_first = _penalty(dist_prev, (dist_prev <= WINDOW) & (j > 0))
    jobs, job_dst = [], []
    for sub in range(n_sub):
        rows = slice(sub * tq, (sub + 1) * tq)
        prev_ref, prev_rows = (kvprev_ref, slice(0, tq)) if sub == 0 else (kv_ref, slice((sub - 1) * tq, sub * tq))
        qpos = n_meta + (j * n_sub + sub) * tq + lax.broadcasted_iota(jnp.int32, (tq, n_meta), 0)
        dist_meta = qpos - lax.broadcasted_iota(jnp.int32, (tq, n_meta), 1)
        pen_meta = _penalty(dist_meta, dist_meta >= 0)
        for kvh in range(kv_heads):
            ks = slice(kvh * hd, (kvh + 1) * hd)
            vs = slice(kv_w + kvh * hd, kv_w + (kvh + 1) * hd)
            segments = [
                (kvmeta_ref[:, ks].astype(BF16), kvmeta_ref[:, vs].astype(BF16), pen_meta),
                (prev_ref[prev_rows, ks].astype(BF16), prev_ref[prev_rows, vs].astype(BF16),
                 pen_prev_first if sub == 0 else pen_prev),
                (kv_ref[rows, ks].astype(BF16), kv_ref[rows, vs].astype(BF16), pen_own),
            ]
            heads = [kvh * SWA_GROUP + g for g in range(SWA_GROUP)]
            q4 = jnp.concatenate([q_ref[rows, hh * hd:(hh + 1) * hd] for hh in heads], axis=0).astype(BF16)
            jobs.append((q4, segments, [sinks_ref[:, hh:hh + 1] for hh in heads],
                         [_alibi_slope(hh, n_heads) for hh in heads]))
            job_dst.append((rows, heads))
    for (rows, heads), outs in zip(job_dst, _attend(jobs, tq)):
        for hh, o in zip(heads, outs):
            o_ref[rows, hh * hd:(hh + 1) * hd] = o


def _swa_prompt(proj_b, proj_b_meta, cols, sinks, *, n_seq, seq_len, n_meta):
    n_heads = sinks.shape[1]
    kv_heads = n_heads // SWA_GROUP
    q_w, kv_w2 = n_heads * HEAD_DIM, 2 * kv_heads * HEAD_DIM
    n_sub = SWA_SUBBLOCKS if seq_len % (SWA_SUBBLOCKS * WINDOW) == 0 else 1
    tq = n_sub * WINDOW
    nblk = seq_len // tq
    kv_blk = cols["skv"] // kv_w2
    return pl.pallas_call(
        functools.partial(_swa_prompt_kernel, n_meta=n_meta, kv_heads=kv_heads),
        grid=(n_seq, nblk),
        in_specs=[
            pl.BlockSpec((tq, q_w), lambda n, j: (n * nblk + j, 0)),
            pl.BlockSpec((tq, kv_w2), lambda n, j: (n * nblk + j, kv_blk)),
            pl.BlockSpec((WINDOW, kv_w2), lambda n, j: (jnp.maximum((n * nblk + j) * n_sub - 1, 0), kv_blk)),
            pl.BlockSpec((n_meta, kv_w2), lambda n, j: (0, kv_blk)),
            pl.BlockSpec((1, n_heads), lambda n, j: (0, 0)),
        ],
        out_specs=pl.BlockSpec((tq, q_w), lambda n, j: (n * nblk + j, 0)),
        out_shape=jax.ShapeDtypeStruct((n_seq * seq_len, q_w), F32),
        compiler_params=_params("parallel", "arbitrary"),
        name="swa_prompt",
    )(proj_b, proj_b, proj_b, proj_b_meta, sinks)


def _swa_step_kernel(q_ref, kvnew_ref, kmeta_ref, vmeta_ref, kbuf_ref, vbuf_ref, sinks_ref,
                     o_ref, kout_ref, vout_ref, *, kv_heads):
    nb, t, _ = q_ref.shape
    n_meta, w = kmeta_ref.shape[1], kbuf_ref.shape[1]
    hd = HEAD_DIM
    kv_w = kv_heads * hd
    n_heads = kv_heads * SWA_GROUP

    def grid2(rows, cols_):
        return (lax.broadcasted_iota(jnp.int32, (rows, cols_), 0), lax.broadcasted_iota(jnp.int32, (rows, cols_), 1))

    ti, mi = grid2(t, n_meta)
    dist_meta = PAST_LEN + ti - mi
    pen_meta = _penalty(dist_meta, dist_meta >= 0)
    ti, bi = grid2(t, w)
    dist_buf = w + ti - bi
    pen_buf = _penalty(dist_buf, (dist_buf <= WINDOW) & (PAST_LEN - w + bi >= n_meta))
    ti, si = grid2(t, t)
    dist_new = ti - si
    pen_new = _penalty(dist_new, dist_new >= 0)
    jobs, job_dst = [], []
    for b in range(nb):
        kv_new = kvnew_ref[b]
        for kvh in range(kv_heads):
            ks = slice(kvh * hd, (kvh + 1) * hd)
            k_old, v_old = kbuf_ref[b, :, kvh, :], vbuf_ref[b, :, kvh, :]
            k_new, v_new = kv_new[:, ks], kv_new[:, kv_w + kvh * hd:kv_w + (kvh + 1) * hd]
            kout_ref[b, 0:w - t, kvh, :] = k_old[t:w]
            kout_ref[b, w - t:w, kvh, :] = k_new
            vout_ref[b, 0:w - t, kvh, :] = v_old[t:w]
            vout_ref[b, w - t:w, kvh, :] = v_new
            segments = [
                (kmeta_ref[b, :, ks].astype(BF16), vmeta_ref[b, :, ks].astype(BF16), pen_meta),
                (k_old.astype(BF16), v_old.astype(BF16), pen_buf),
                (k_new.astype(BF16), v_new.astype(BF16), pen_new),
            ]
            heads = [kvh * SWA_GROUP + g for g in range(SWA_GROUP)]
            q4 = jnp.concatenate([q_ref[b, :, hh * hd:(hh + 1) * hd] for hh in heads], axis=0).astype(BF16)
            jobs.append((q4, segments, [sinks_ref[:, hh:hh + 1] for hh in heads],
                         [_alibi_slope(hh, n_heads) for hh in heads]))
            job_dst.append((b, heads))
    for (b, heads), outs in zip(job_dst, _attend(jobs, t)):
        for hh, o in zip(heads, outs):
            o_ref[b, :, hh * hd:(hh + 1) * hd] = o


def _swa_step(proj_b3, cols, k_meta, v_meta, k_buf, v_buf, sinks, *, seqs_per_step):
    n, t = k_buf.shape[0], proj_b3.shape[1]
    n_heads = sinks.shape[1]
    kv_heads = n_heads // SWA_GROUP
    q_w, kv_w = n_heads * HEAD_DIM, kv_heads * HEAD_DIM
    n_meta, w = k_meta.shape[1], k_buf.shape[1]
    nb = seqs_per_step
    kv_blk = cols["skv"] // (2 * kv_w)
    seq3 = lambda rows, width, blk=0: pl.BlockSpec((nb, rows, width), lambda i: (i, 0, blk))
    cache_spec = pl.BlockSpec((nb, w, kv_heads, HEAD_DIM), lambda i: (i, 0, 0, 0))
    return pl.pallas_call(
        functools.partial(_swa_step_kernel, kv_heads=kv_heads),
        grid=(n // nb,),
        in_specs=[seq3(t, q_w), seq3(t, 2 * kv_w, kv_blk), seq3(n_meta, kv_w), seq3(n_meta, kv_w),
                  cache_spec, cache_spec, pl.BlockSpec((1, n_heads), lambda i: (0, 0))],
        out_specs=[seq3(t, q_w), cache_spec, cache_spec],
        out_shape=[jax.ShapeDtypeStruct((n, t, q_w), F32),
                   jax.ShapeDtypeStruct((n, w, kv_heads, HEAD_DIM), F32),
                   jax.ShapeDtypeStruct((n, w, kv_heads, HEAD_DIM), F32)],
        compiler_params=_params("parallel"),
        name="swa_step",
    )(proj_b3, proj_b3, k_meta, v_meta, k_buf, v_buf, sinks)


def _split_w_in(w_in, heads, n_heads, kv_heads, d):
    conv_w = 3 * heads * HEAD_DIM
    v_w = heads * HEAD_DIM
    q_w, kv_w = n_heads * HEAD_DIM, kv_heads * HEAD_DIM
    sizes = (conv_w, v_w, heads, heads, q_w, kv_w, kv_w, d, d)
    offs = [0]
    for s in sizes:
        offs.append(offs[-1] + s)
    assert offs[2] % LANES == 0 and 2 * heads <= LANES and offs[2] + LANES <= w_in.shape[1]
    w_bf = w_in.astype(BF16)
    part = lambda i: w_bf[:, offs[i]:offs[i + 1]]
    w_b = jnp.concatenate([part(4), part(8), part(7), part(5), part(6)], axis=1)
    cols_a = {"qkv": 0, "z": conv_w, "ba": offs[2]}
    cols_b = {"q": 0, "g_swa": q_w, "g_dn": q_w + d, "skv": q_w + 2 * d}
    return w_bf, offs[2] + LANES, w_b, cols_a, cols_b


def kernel(x_prompt, x_sample, state_dn_conv, state_dn_ssm, cache_swa_meta_k, cache_swa_meta_v, cache_swa_k, cache_swa_v, meta_tokens, ffn1_norm_pre, ffn1_norm_post, ffn1_w_gate, ffn1_w_up, ffn1_w_down, mix_norm_pre, mix_norm_post, w_in, dn_conv_w, dn_a_log, dn_dt_bias, dn_norm_w, swa_sinks, w_out, ffn2_norm_pre, ffn2_norm_post, ffn2_w_gate, ffn2_w_up, ffn2_w_down):
    assert w_in.shape[0] == 1, "single-layer step"
    n_p, seq, d = x_prompt.shape
    n_s, t_s, _ = x_sample.shape
    n_meta = meta_tokens.shape[0]
    heads = dn_a_log.shape[1]
    n_heads = swa_sinks.shape[1]
    kv_heads = n_heads // SWA_GROUP
    conv_w = dn_conv_w.shape[2]
    kv_w = kv_heads * HEAD_DIM
    w_keep = cache_swa_k.shape[2]
    assert w_keep == WINDOW and seq % WINDOW == 0 and seq % DN_CHUNK == 0 and seq >= WINDOW

    w_bf, a_cols, w_b, cols_a, cols_b = _split_w_in(w_in[0], heads, n_heads, kv_heads, d)
    cols = {**cols_a, **cols_b}
    f1 = (ffn1_norm_pre, ffn1_norm_post, ffn1_w_gate[0].astype(BF16), ffn1_w_up[0].astype(BF16),
          ffn1_w_down[0].astype(BF16))
    f2 = (ffn2_norm_pre, ffn2_norm_post, ffn2_w_gate[0].astype(BF16), ffn2_w_up[0].astype(BF16),
          ffn2_w_down[0].astype(BF16))
    w_o = w_out[0].astype(BF16)
    gdn_w = (dn_conv_w[0], dn_a_log, dn_dt_bias, dn_norm_w)

    def front(h):
        h1 = _ffn(h, *f1)
        return h1, _proj(h1, mix_norm_pre, w_bf, a_cols), _proj(h1, mix_norm_pre, w_b)

    def hist_tile(rows3):
        return jnp.pad(rows3, ((0, 0), (HIST_ROWS - (DN_CONV - 1), 0), (0, 0)))

    rows_s = n_s * t_s
    assert n_meta % t_s == 0 and n_meta >= DN_CONV - 1
    h1_sm, pa_sm, pb_sm = front(jnp.concatenate([x_sample.reshape(rows_s, d), meta_tokens], axis=0))
    pa_m, pb_m = pa_sm[rows_s:], pb_sm[rows_s:]
    zero_hist = jnp.zeros((1, HIST_ROWS, conv_w), F32)
    zero_state = jnp.zeros((1, heads, HEAD_DIM, HEAD_DIM), F32)
    _, s_meta = _gdn_step(pa_m[None], cols, zero_hist, zero_state, *gdn_w, seqs_per_step=1)
    hist_meta = hist_tile(pa_m[None, n_meta - (DN_CONV - 1):, :conv_w])

    h1_p = _ffn(x_prompt.reshape(n_p * seq, d), *f1)
    pa_p, tail_p = _proj_conv(h1_p, mix_norm_pre, w_bf, a_cols, hist_meta, dn_conv_w[0], n_seq=n_p, seq_len=seq,
                              qk_width=heads * HEAD_DIM)
    pb_p = _proj(h1_p, mix_norm_pre, w_b)
    o_dn_p, s_p = _gdn_packed(pa_p, cols, s_meta, dn_a_log, dn_dt_bias, dn_norm_w, n_seq=n_p, seq_len=seq,
                              block=4 * DN_CHUNK)
    o_sw_p = _swa_prompt(pb_p, pb_m, cols, swa_sinks, n_seq=n_p, seq_len=seq, n_meta=n_meta)
    y_prompt = _mixffn(h1_p, o_dn_p, o_sw_p, pb_p, cols, w_o, mix_norm_post, *f2).reshape(n_p, seq, d)

    pb_p3 = pb_p.reshape(n_p, seq, -1)
    p_conv = tail_p[:, HIST_ROWS - (DN_CONV - 1):]
    kv_meta = pb_m[:, cols["skv"]:]
    p_meta_k = jnp.broadcast_to(kv_meta[None, :, :kv_w], (n_p, n_meta, kv_w))
    p_meta_v = jnp.broadcast_to(kv_meta[None, :, kv_w:], (n_p, n_meta, kv_w))
    p_win_k = pb_p3[:, seq - w_keep:, cols["skv"]:cols["skv"] + kv_w]
    p_win_v = pb_p3[:, seq - w_keep:, cols["skv"] + kv_w:]

    pa_s3 = pa_sm.reshape(-1, t_s, pa_sm.shape[1])
    pb_s3 = pb_sm.reshape(-1, t_s, pb_sm.shape[1])
    o_dn_s, s_s = _gdn_step(pa_s3, cols, hist_tile(state_dn_conv[0]), state_dn_ssm[0], *gdn_w, seqs_per_step=4)
    o_sw_s, s_win_k, s_win_v = _swa_step(
        pb_s3, cols,
        cache_swa_meta_k[0].reshape(n_s, n_meta, kv_w), cache_swa_meta_v[0].reshape(n_s, n_meta, kv_w),
        cache_swa_k[0], cache_swa_v[0], swa_sinks, seqs_per_step=4)
    y_sample = _mixffn(h1_sm, o_dn_s.reshape(rows_s, -1), o_sw_s.reshape(rows_s, -1), pb_sm, cols, w_o,
                       mix_norm_post, *f2, rows=rows_s).reshape(n_s, t_s, d)
    assert t_s >= DN_CONV - 1, "new conv state is taken from the new rows alone"
    s_conv = pa_s3[:n_s, t_s - (DN_CONV - 1):, :conv_w]

    def kv4(x):
        return x.reshape(1, x.shape[0], x.shape[1], kv_heads, HEAD_DIM)

    return (y_prompt, y_sample, p_conv[None], s_p[None], kv4(p_meta_k), kv4(p_meta_v), kv4(p_win_k),
            kv4(p_win_v), s_conv[None], s_s[None], s_win_k[None], s_win_v[None])
```

```python
import functools
import math

import jax
import jax.numpy as jnp
from jax import lax
from jax.experimental import pallas as pl
from jax.experimental.pallas import tpu as pltpu

F32 = jnp.float32
BF16 = jnp.bfloat16
HIGHEST = lax.Precision.HIGHEST

RMS_EPS = 1e-6
L2_EPS = 1e-6
WINDOW = 128
PAST_LEN = 8192
HEAD_DIM = 128
SWA_GROUP = 4
DN_CONV = 4
DN_CHUNK = 64
HIST_ROWS = 8

VMEM_LIMIT_BYTES = 56 * 1024 * 1024
ROW_TILE_MAX = 576
SUBLANES = 8
LANES = 128
MXU_WIDTH = 256
NEG_BIG = -1e30
SWA_SUBBLOCKS = 2
GDN_CHUNKS_PER_STEP = 4
STEP_SEQS = 4


def _rms(x, g):
    return x * lax.rsqrt(jnp.mean(x * x, axis=-1, keepdims=True) + RMS_EPS) * g


def _silu(x):
    return x * jax.nn.sigmoid(x)


def _dot(a, b):
    return jnp.dot(a, b, preferred_element_type=F32)


def _dot_hi(a, b):
    return jnp.dot(a, b, preferred_element_type=F32, precision=HIGHEST)


def _dot_tn_hi(a, b):
    return lax.dot_general(a, b, (((0,), (0,)), ((), ())), preferred_element_type=F32, precision=HIGHEST)


def _dot_nt(a, b):
    return lax.dot_general(a, b, (((1,), (1,)), ((), ())), preferred_element_type=F32)


def _dot_tn(a, b):
    return lax.dot_general(a, b, (((0,), (0,)), ((), ())), preferred_element_type=F32)


def _split_bf16(a):
    hi = a.astype(BF16)
    return hi, (a - hi.astype(F32)).astype(BF16)


_DOTS = {"nn": _dot, "nt": _dot_nt, "tn": _dot_tn}


def _dot3_split(a_pair, b_pair, kind="nn"):
    (a_hi, a_lo), (b_hi, b_lo) = a_pair, b_pair
    dot = _DOTS[kind]
    axis = 1 if kind == "tn" else 0
    m = a_hi.shape[axis]
    top = dot(jnp.concatenate([a_hi, a_lo], axis=axis), b_hi)
    return top[:m] + top[m:] + dot(a_hi, b_lo)


def _mm(a, b, kind="nn", passes=3):
    if passes == 1:
        return _DOTS[kind](a.astype(BF16), b.astype(BF16))
    return _dot3_split(_split_bf16(a), _split_bf16(b), kind)


def _row_tile(rows):
    start = min(ROW_TILE_MAX, rows) // SUBLANES * SUBLANES
    for tile in range(start, 0, -SUBLANES):
        if rows % tile == 0:
            return tile
    raise ValueError(f"no sublane-aligned row tile divides {rows}")


def _resident(shape):
    return pl.BlockSpec(shape, lambda *_: (0,) * len(shape), pipeline_mode=pl.Buffered(1))


def _params(*semantics):
    return pltpu.CompilerParams(dimension_semantics=semantics, vmem_limit_bytes=VMEM_LIMIT_BYTES)


def _ffn_tiles(hs, g_pre, g_post, wg_ref, wu_ref, wd_ref, ff_chunk):
    xns = [_rms(h, g_pre).astype(BF16) for h in hs]
    accs = [None] * len(hs)
    c0 = 0
    for width in ff_chunk:
        gates = [_dot(xn, wg_ref[:, c0:c0 + width]) for xn in xns]
        ups = [_dot(xn, wu_ref[:, c0:c0 + width]) for xn in xns]
        for i, (gate, up) in enumerate(zip(gates, ups)):
            part = _dot((_silu(gate) * up).astype(BF16), wd_ref[c0:c0 + width, :])
            accs[i] = part if accs[i] is None else accs[i] + part
        c0 += width
    return [h + 0.5 * _rms(acc, g_post) for h, acc in zip(hs, accs)]


def _row_halves(tm):
    return [slice(0, tm // 2), slice(tm // 2, tm)] if tm % (2 * SUBLANES) == 0 else [slice(0, tm)]


def _ff_chunk(d_ff):
    tiles, rem = divmod(d_ff, MXU_WIDTH)
    if rem or tiles < 2:
        return (d_ff,)
    first = (tiles + 1) // 2
    return (first * MXU_WIDTH, (tiles - first) * MXU_WIDTH)


def _ffn_kernel(h_ref, gpre_ref, gpost_ref, wg_ref, wu_ref, wd_ref, o_ref, *, ff_chunk):
    halves = _row_halves(h_ref.shape[0])
    outs = _ffn_tiles([h_ref[r, :] for r in halves], gpre_ref[...], gpost_ref[...], wg_ref, wu_ref, wd_ref, ff_chunk)
    for r, out in zip(halves, outs):
        o_ref[r, :] = out


def _ffn(h, g_pre, g_post, wg, wu, wd):
    rows, d = h.shape
    d_ff = wg.shape[1]
    tm = _row_tile(rows)
    row_spec = pl.BlockSpec((tm, d), lambda i: (i, 0))
    return pl.pallas_call(
        functools.partial(_ffn_kernel, ff_chunk=_ff_chunk(d_ff)),
        grid=(rows // tm,),
        in_specs=[row_spec, _resident((1, d)), _resident((1, d)),
                  _resident((d, d_ff)), _resident((d, d_ff)), _resident((d_ff, d))],
        out_specs=row_spec,
        out_shape=jax.ShapeDtypeStruct((rows, d), F32),
        compiler_params=_params("parallel"),
        name="ffn",
    )(h, g_pre, g_post, wg, wu, wd)


def _proj_kernel(h_ref, g_ref, w_ref, o_ref):
    o_ref[...] = _dot(_rms(h_ref[...], g_ref[...]).astype(BF16), w_ref[...])


def _proj(h, g, w, n_cols=None):
    rows, d = h.shape
    n = w.shape[1] if n_cols is None else n_cols
    tm = _row_tile(rows)
    return pl.pallas_call(
        _proj_kernel,
        grid=(rows // tm,),
        in_specs=[pl.BlockSpec((tm, d), lambda i: (i, 0)), _resident((1, d)), _resident((d, n))],
        out_specs=pl.BlockSpec((tm, n), lambda i: (i, 0)),
        out_shape=jax.ShapeDtypeStruct((rows, n), F32),
        compiler_params=_params("parallel"),
        name="proj",
    )(h, g, w)


def _proj_conv_kernel(h_ref, g_ref, w_ref, hist_ref, convw_ref, o_ref, tail_ref, ybuf_ref, *, conv_width, qk_width):
    j = pl.program_id(1)
    tm = h_ref.shape[0]
    hd = HEAD_DIM
    xn = _rms(h_ref[...], g_ref[...]).astype(BF16)

    @pl.when(j == 0)
    def _():
        ybuf_ref[0:HIST_ROWS, :] = hist_ref[0]

    @pl.when(j > 0)
    def _():
        ybuf_ref[0:HIST_ROWS, :] = tail_ref[0]

    def project_raw(c0, width):
        ybuf_ref[HIST_ROWS:HIST_ROWS + tm, c0:c0 + width] = _dot(xn, w_ref[:, c0:c0 + width])

    def conv_group(g0, width):
        tail_ref[0, :, g0:g0 + width] = ybuf_ref[tm:tm + HIST_ROWS, g0:g0 + width]
        for c0 in range(g0, g0 + width, hd):
            x = ybuf_ref[:, c0:c0 + hd]
            acc = x[HIST_ROWS:] * convw_ref[DN_CONV - 1:DN_CONV, c0:c0 + hd]
            for back in range(1, DN_CONV):
                shifted = pltpu.roll(x, back, 0)[HIST_ROWS:]
                acc = acc + shifted * convw_ref[DN_CONV - 1 - back:DN_CONV - back, c0:c0 + hd]
            y = _silu(acc)
            if c0 < 2 * qk_width:
                y = y * lax.rsqrt(jnp.sum(y * y, axis=-1, keepdims=True) + L2_EPS)
                if c0 < qk_width:
                    y = y * (hd ** -0.5)
            o_ref[:, c0:c0 + hd] = y

    group = 2 * MXU_WIDTH
    starts = list(range(0, conv_width, group))
    n_rest = w_ref.shape[1] - conv_width
    rest_cuts = ([conv_width + (n_rest * i // len(starts)) // MXU_WIDTH * MXU_WIDTH for i in range(len(starts))]
                 + [w_ref.shape[1]])
    project_raw(starts[0], group)
    for gi, g0 in enumerate(starts):
        if gi + 1 < len(starts):
            project_raw(starts[gi + 1], group)
        r0, r1 = rest_cuts[gi], rest_cuts[gi + 1]
        if r1 > r0:
            o_ref[:, r0:r1] = _dot(xn, w_ref[:, r0:r1])
        conv_group(g0, group)


def _proj_conv(h, g, w, n_cols, hist, conv_w, *, n_seq, seq_len, qk_width):
    rows, d = h.shape
    n = n_cols
    conv_width = conv_w.shape[1]
    tm = _row_tile(seq_len)
    tiles = seq_len // tm
    shared_hist = hist.shape[0] == 1
    return pl.pallas_call(
        functools.partial(_proj_conv_kernel, conv_width=conv_width, qk_width=qk_width),
        grid=(n_seq, tiles),
        in_specs=[pl.BlockSpec((tm, d), lambda s, j: (s * tiles + j, 0)), _resident((1, d)), _resident((d, n)),
                  pl.BlockSpec((1, HIST_ROWS, conv_width), lambda s, j: (0 if shared_hist else s, 0, 0)),
                  _resident((DN_CONV, conv_width))],
        out_specs=[pl.BlockSpec((tm, n), lambda s, j: (s * tiles + j, 0)),
                   pl.BlockSpec((1, HIST_ROWS, conv_width), lambda s, j: (s, 0, 0))],
        out_shape=[jax.ShapeDtypeStruct((rows, n), F32),
                   jax.ShapeDtypeStruct((n_seq, HIST_ROWS, conv_width), F32)],
        scratch_shapes=[pltpu.VMEM((HIST_ROWS + tm, conv_width), F32)],
        compiler_params=_params("parallel", "arbitrary"),
        name="proj_conv",
    )(h, g, w, hist, conv_w)


def _mixffn_kernel(h_ref, odn_ref, osw_ref, gdn_ref, gsw_ref, wout_ref, gmix_ref,
                   gpre_ref, gpost_ref, wg_ref, wu_ref, wd_ref, o_ref, *, ff_chunk):
    halves = _row_halves(h_ref.shape[0])
    ys = [(jax.nn.sigmoid(gdn_ref[r, :]) * odn_ref[r, :] + jax.nn.sigmoid(gsw_ref[r, :]) * osw_ref[r, :]).astype(BF16)
          for r in halves]
    mixed = [_dot(y, wout_ref[...]) for y in ys]
    h2s = [h_ref[r, :] + _rms(m, gmix_ref[...]) for r, m in zip(halves, mixed)]
    outs = _ffn_tiles(h2s, gpre_ref[...], gpost_ref[...], wg_ref, wu_ref, wd_ref, ff_chunk)
    for r, out in zip(halves, outs):
        o_ref[r, :] = out


def _mixffn(h, o_dn, o_sw, proj_b, cols, w_out, g_mix, g_pre, g_post, wg, wu, wd, rows=None):
    d = h.shape[1]
    rows = h.shape[0] if rows is None else rows
    d_ff = wg.shape[1]
    tm = _row_tile(rows)
    row_spec = pl.BlockSpec((tm, d), lambda i: (i, 0))
    gdn_blk, gsw_blk = cols["g_dn"] // d, cols["g_swa"] // d
    return pl.pallas_call(
        functools.partial(_mixffn_kernel, ff_chunk=_ff_chunk(d_ff)),
        grid=(rows // tm,),
        in_specs=[row_spec, row_spec, row_spec,
                  pl.BlockSpec((tm, d), lambda i: (i, gdn_blk)),
                  pl.BlockSpec((tm, d), lambda i: (i, gsw_blk)),
                  _resident((d, d)), _resident((1, d)), _resident((1, d)), _resident((1, d)),
                  _resident((d, d_ff)), _resident((d, d_ff)), _resident((d_ff, d))],
        out_specs=row_spec,
        out_shape=jax.ShapeDtypeStruct((rows, d), F32),
        compiler_params=_params("parallel"),
        name="mixffn",
    )(h, o_dn, o_sw, proj_b, proj_b, w_out, g_mix, g_pre, g_post, wg, wu, wd)


GROUP = 4
PASSES_QK = 1
PASSES_STATE = 1
PASSES_INV = 3
INV_BLOCK = 16
STEP_PASSES = 1


def _gdn_packed_kernel(qkv_ref, z_ref, ba_ref, s0_ref, alog_ref, dtb_ref, nw_ref,
                       o_ref, sout_ref, s_ref, *, chunk, heads):
    j = pl.program_id(1)
    tb = qkv_ref.shape[0]
    hd = HEAD_DIM
    c = chunk
    gw = GROUP * c
    qk_w = heads * hd

    @pl.when(j == 0)
    def _():
        s_ref[...] = s0_ref[0]

    def iota(shape, dim):
        return lax.broadcasted_iota(jnp.int32, shape, dim)

    row = iota((c, gw), 0)
    col = iota((c, gw), 1) & (c - 1)
    causal4, strict4 = row >= col, row > col
    diag_blocks4 = (row // INV_BLOCK) == (col // INV_BLOCK)
    eye4 = (row == col).astype(F32)
    bd_mask = (iota((gw, gw), 0) // c) == (iota((gw, gw), 1) // c)
    lower_ones = (iota((c, c), 0) >= iota((c, c), 1)).astype(F32)
    upper_dup = (iota((c, 2 * c), 0) <= (iota((c, 2 * c), 1) & (c - 1))).astype(F32)
    first_half = iota((c, 2 * c), 1) < c
    zeros_head = jnp.zeros((c, hd), F32)

    def block_diag(x4):
        return jnp.where(bd_mask, jnp.concatenate([x4] * GROUP, axis=0), jnp.zeros((), x4.dtype))

    splits = {}

    def split(x):
        if id(x) not in splits:
            splits[id(x)] = (x, _split_bf16(x))
        return splits[id(x)][1]

    def times_block_diag(lhs, x4, passes):
        parts = lhs if isinstance(lhs, list) else [lhs]
        if passes == 1:
            return _dot(jnp.concatenate(parts, axis=0).astype(BF16), block_diag(x4.astype(BF16)))
        hi, lo = split(x4)
        lhs_split = tuple(jnp.concatenate(pieces, axis=0) for pieces in zip(*[split(p) for p in parts]))
        return _dot3_split(lhs_split, (block_diag(hi), block_diag(lo)))

    def block_diag_times(x4, rhs, passes):
        if passes == 1:
            return _dot(block_diag(x4.astype(BF16)), rhs.astype(BF16))
        hi, lo = _split_bf16(x4)
        return _dot3_split((block_diag(hi), block_diag(lo)), _split_bf16(rhs))

    def pack_cols(col_all, hs):
        halves = [jnp.where(first_half,
                            jnp.broadcast_to(col_all[:, hs[2 * p]:hs[2 * p] + 1], (c, 2 * c)),
                            jnp.broadcast_to(col_all[:, hs[2 * p + 1]:hs[2 * p + 1] + 1], (c, 2 * c)))
                  for p in range(GROUP // 2)]
        return jnp.concatenate(halves, axis=1)

    chunks = []
    for r0 in range(0, tb, c):
        ba = ba_ref[r0:r0 + c, :]
        beta_all = jax.nn.sigmoid(ba[:, 0:heads])
        a_in = ba[:, heads:2 * heads] + dtb_ref[...]
        softplus = jnp.maximum(a_in, 0.0) + jnp.log(1.0 + jnp.exp(-jnp.abs(a_in)))
        g_all = -jnp.exp(alog_ref[...]) * softplus
        gc_all = _dot_hi(lower_ones, g_all)
        gc_row_dup = _dot_tn_hi(g_all, upper_dup)
        chunks.append(dict(r0=r0, beta=beta_all, gc=gc_all, gc_row=gc_row_dup))

    items = []
    for ch in chunks:
        for g0 in range(0, heads, GROUP):
            hs = list(range(g0, g0 + GROUP))
            qs, ks, vs = [], [], []
            rows = slice(ch["r0"], ch["r0"] + c)
            for h in hs:
                qs.append(qkv_ref[rows, h * hd:(h + 1) * hd])
                ks.append(qkv_ref[rows, qk_w + h * hd:qk_w + (h + 1) * hd])
                vs.append(qkv_ref[rows, 2 * qk_w + h * hd:2 * qk_w + (h + 1) * hd])
            items.append(dict(ch=ch, hs=hs, qs=qs, ks=ks, vs=vs))

    for it in items:
        ch, hs, qs, ks = it["ch"], it["hs"], it["qs"], it["ks"]
        beta4 = pack_cols(ch["beta"], hs)
        gc_row4 = jnp.concatenate(
            [jnp.where(first_half[0:1], ch["gc_row"][hs[2 * p]:hs[2 * p] + 1],
                       ch["gc_row"][hs[2 * p + 1]:hs[2 * p + 1] + 1])
             for p in range(GROUP // 2)], axis=1)
        diff4 = pack_cols(ch["gc"], hs) - gc_row4
        decay4 = jnp.where(causal4, jnp.exp(jnp.where(causal4, diff4, 0.0)), 0.0)
        lhs = jnp.concatenate([jnp.concatenate(qs, axis=1), jnp.concatenate(ks, axis=1)], axis=0)
        k_bd = jnp.concatenate(
            [jnp.concatenate([ks[i] if ii == i else zeros_head for ii in range(GROUP)], axis=1)
             for i in range(GROUP)], axis=0)
        qkkk = _mm(lhs, k_bd, "nt", PASSES_QK)
        it["qk4"] = qkkk[:c] * decay4
        a4 = jnp.where(strict4, beta4 * qkkk[c:] * decay4, 0.0)
        a_diag = jnp.where(diag_blocks4, a4, 0.0)
        it["a_diag"], it["a_off"] = a_diag, a4 - a_diag
        it["t4"] = eye4 - a_diag

    assert c // INV_BLOCK == 4
    for it in items:
        it["p4"] = times_block_diag(it["a_diag"], it["a_diag"], PASSES_INV)
    n_sq = int(math.log2(INV_BLOCK)) - 1
    for step in range(n_sq):
        for it in items:
            if step < n_sq - 1:
                both = times_block_diag([it["p4"], it["t4"]], it["p4"], PASSES_INV)
                it["p4"], it["t4"] = both[:c], it["t4"] + both[c:]
            else:
                it["t4"] = it["t4"] + times_block_diag(it["t4"], it["p4"], PASSES_INV)
    for it in items:
        it["m4"] = times_block_diag(it["t4"], it["a_off"], PASSES_INV)
    for it in items:
        it["m2"] = times_block_diag(it["m4"], it["m4"], PASSES_INV)
    for it in items:
        i_minus_m = eye4 - it["m4"]
        it["s4"] = i_minus_m + times_block_diag(i_minus_m, it["m2"], PASSES_INV)
    for it in items:
        it["t4"] = times_block_diag(it["s4"], it["t4"], PASSES_INV)

    for it in items:
        ch, hs, ks, vs = it["ch"], it["hs"], it["ks"], it["vs"]
        rhs = jnp.concatenate(
            [jnp.concatenate([vs[i] * ch["beta"][:, h:h + 1],
                              ks[i] * (ch["beta"][:, h:h + 1] * jnp.exp(ch["gc"][:, h:h + 1]))], axis=1)
             for i, h in enumerate(hs)], axis=0)
        it["sol"] = block_diag_times(it["t4"], rhs, PASSES_INV)

    for ch in chunks:
        group_items = [it for it in items if it["ch"] is ch]
        for it in group_items:
            us, o_inter = [], []
            for i, h in enumerate(it["hs"]):
                sl = slice(i * c, (i + 1) * c)
                ws = _mm(jnp.concatenate([it["sol"][sl, hd:], it["qs"][i] * jnp.exp(ch["gc"][:, h:h + 1])], axis=0),
                         s_ref[h], "nn", PASSES_STATE)
                us.append(it["sol"][sl, :hd] - ws[:c])
                o_inter.append(ws[c:])
            it["us"], it["o_inter"] = us, o_inter
        for it in group_items:
            it["o_intra"] = block_diag_times(it["qk4"], jnp.concatenate(it["us"], axis=0), PASSES_STATE)
        for it in group_items:
            for i, h in enumerate(it["hs"]):
                gc = ch["gc"][:, h:h + 1]
                gc_last = ch["gc_row"][h:h + 1, c - 1:c]
                s_ref[h] = s_ref[h] * jnp.exp(gc_last) + _mm(it["ks"][i] * jnp.exp(gc_last - gc), it["us"][i],
                                                             "tn", PASSES_STATE)
        for it in group_items:
            r0 = ch["r0"]
            for i, h in enumerate(it["hs"]):
                o = it["o_inter"][i] + it["o_intra"][i * c:(i + 1) * c]
                zh = z_ref[r0:r0 + c, h * hd:(h + 1) * hd]
                o_ref[r0:r0 + c, h * hd:(h + 1) * hd] = _rms(o, nw_ref[...]) * _silu(zh)

    @pl.when(j == pl.num_programs(1) - 1)
    def _():
        sout_ref[0] = s_ref[...]


def _gdn_packed(proj_a, cols, s0, a_log, dt_bias, norm_w, *, n_seq, seq_len, block):
    heads = a_log.shape[1]
    v_w = heads * HEAD_DIM
    conv_width = 3 * v_w
    nblk = seq_len // block
    z_blk, ba_blk = cols["z"] // v_w, cols["ba"] // LANES
    shared_s0 = s0.shape[0] == 1
    assert 2 * DN_CHUNK == HEAD_DIM and heads % GROUP == 0 and block % DN_CHUNK == 0
    return pl.pallas_call(
        functools.partial(_gdn_packed_kernel, chunk=DN_CHUNK, heads=heads),
        grid=(n_seq, nblk),
        in_specs=[
            pl.BlockSpec((block, conv_width), lambda n, j: (n * nblk + j, 0)),
            pl.BlockSpec((block, v_w), lambda n, j: (n * nblk + j, z_blk)),
            pl.BlockSpec((block, LANES), lambda n, j: (n * nblk + j, ba_blk)),
            pl.BlockSpec((1, heads, HEAD_DIM, HEAD_DIM), lambda n, j: (0 if shared_s0 else n, 0, 0, 0)),
            pl.BlockSpec((1, heads), lambda n, j: (0, 0)),
            pl.BlockSpec((1, heads), lambda n, j: (0, 0)),
            pl.BlockSpec((1, HEAD_DIM), lambda n, j: (0, 0)),
        ],
        out_specs=[
            pl.BlockSpec((block, v_w), lambda n, j: (n * nblk + j, 0)),
            pl.BlockSpec((1, heads, HEAD_DIM, HEAD_DIM), lambda n, j: (n, 0, 0, 0)),
        ],
        out_shape=[
            jax.ShapeDtypeStruct((n_seq * seq_len, v_w), F32),
            jax.ShapeDtypeStruct((n_seq, heads, HEAD_DIM, HEAD_DIM), F32),
        ],
        scratch_shapes=[pltpu.VMEM((heads, HEAD_DIM, HEAD_DIM), F32)],
        compiler_params=_params("parallel", "arbitrary"),
        name="gdn_packed",
    )(proj_a, proj_a, proj_a, s0, a_log, dt_bias, norm_w)


def _gdn_step_kernel(qkv_ref, z_ref, ba_ref, hist_ref, s0_ref, convw_ref, alog_ref, dtb_ref, nw_ref,
                     o_ref, sout_ref, *, heads):
    nb, t, _ = qkv_ref.shape
    hd = HEAD_DIM
    qk_w = heads * hd
    ti = lax.broadcasted_iota(jnp.int32, (t, t), 0)
    tj = lax.broadcasted_iota(jnp.int32, (t, t), 1)
    lower_ones = (ti >= tj).astype(F32)
    upper_ones = (ti <= tj).astype(F32)

    def per_sequence(b, carry):
        raw = jnp.concatenate([hist_ref[b], qkv_ref[b]], axis=0)
        acc = raw[HIST_ROWS:] * convw_ref[DN_CONV - 1:DN_CONV, :]
        for back in range(1, DN_CONV):
            acc = acc + pltpu.roll(raw, back, 0)[HIST_ROWS:] * convw_ref[DN_CONV - 1 - back:DN_CONV - back, :]
        x = _silu(acc)
        ba = ba_ref[b]
        beta_all = jax.nn.sigmoid(ba[:, 0:heads])
        a_in = ba[:, heads:2 * heads] + dtb_ref[...]
        softplus = jnp.maximum(a_in, 0.0) + jnp.log(1.0 + jnp.exp(-jnp.abs(a_in)))
        g_all = -jnp.exp(alog_ref[...]) * softplus
        gc_all = _dot_hi(lower_ones, g_all)
        gc_row_all = _dot_tn_hi(g_all, upper_ones)
        z = z_ref[b]
        hs = range(heads)
        qs, ks, vs = [], [], []
        for h in hs:
            q = x[:, h * hd:(h + 1) * hd]
            k = x[:, qk_w + h * hd:qk_w + (h + 1) * hd]
            qs.append(q * lax.rsqrt(jnp.sum(q * q, axis=-1, keepdims=True) + L2_EPS) * (hd ** -0.5))
            ks.append(k * lax.rsqrt(jnp.sum(k * k, axis=-1, keepdims=True) + L2_EPS))
            vs.append(x[:, 2 * qk_w + h * hd:2 * qk_w + (h + 1) * hd])
        gcs = [gc_all[:, h:h + 1] for h in hs]
        gc_lasts = [gc_row_all[h:h + 1, t - 1:t] for h in hs]
        kq = [jnp.concatenate([ks[h], qs[h]], axis=0) for h in hs]
        gram = [_mm(kq[h], ks[h], "nt", STEP_PASSES) for h in hs]
        from_state = [_mm(kq[h] * jnp.exp(jnp.concatenate([gcs[h], gcs[h]], axis=0)), s0_ref[b, h], "nn",
                          STEP_PASSES) for h in hs]
        us, outs = [], []
        for h in hs:
            decay = jnp.where(ti >= tj, jnp.exp(jnp.where(ti >= tj, gcs[h] - gc_row_all[h:h + 1, :], 0.0)), 0.0)
            beta = beta_all[:, h:h + 1]
            lower = jnp.where(ti > tj, beta * gram[h][:t] * decay, 0.0)
            u = beta * (vs[h] - from_state[h][:t])
            for j in range(t - 1):
                u = u - lower[:, j:j + 1] * u[j:j + 1, :]
            qk = gram[h][t:] * decay
            o = from_state[h][t:]
            for j in range(t):
                o = o + qk[:, j:j + 1] * u[j:j + 1, :]
            us.append(u)
            outs.append(o)
        for h in hs:
            sout_ref[b, h] = s0_ref[b, h] * jnp.exp(gc_lasts[h]) + _mm(ks[h] * jnp.exp(gc_lasts[h] - gcs[h]), us[h],
                                                                       "tn", STEP_PASSES)
        for h in hs:
            o_ref[b, :, h * hd:(h + 1) * hd] = _rms(outs[h], nw_ref[...]) * _silu(z[:, h * hd:(h + 1) * hd])
        return carry

    lax.fori_loop(0, nb, per_sequence, 0, unroll=2 if nb % 2 == 0 else 1)


def _gdn_step(proj_a3, cols, hist, s0, conv_w, a_log, dt_bias, norm_w, *, seqs_per_step):
    n, t = s0.shape[0], proj_a3.shape[1]
    heads = a_log.shape[1]
    conv_width = conv_w.shape[1]
    v_w = heads * HEAD_DIM
    nb = seqs_per_step
    z_blk, ba_blk = cols["z"] // v_w, cols["ba"] // LANES
    state_spec = pl.BlockSpec((nb, heads, HEAD_DIM, HEAD_DIM), lambda i: (i, 0, 0, 0))
    return pl.pallas_call(
        functools.partial(_gdn_step_kernel, heads=heads),
        grid=(n // nb,),
        in_specs=[
            pl.BlockSpec((nb, t, conv_width), lambda i: (i, 0, 0)),
            pl.BlockSpec((nb, t, v_w), lambda i: (i, 0, z_blk)),
            pl.BlockSpec((nb, t, LANES), lambda i: (i, 0, ba_blk)),
            pl.BlockSpec((nb, HIST_ROWS, conv_width), lambda i: (i, 0, 0)),
            state_spec,
            pl.BlockSpec((DN_CONV, conv_width), lambda i: (0, 0)),
            pl.BlockSpec((1, heads), lambda i: (0, 0)),
            pl.BlockSpec((1, heads), lambda i: (0, 0)),
            pl.BlockSpec((1, HEAD_DIM), lambda i: (0, 0)),
        ],
        out_specs=[pl.BlockSpec((nb, t, v_w), lambda i: (i, 0, 0)), state_spec],
        out_shape=[jax.ShapeDtypeStruct((n, t, v_w), F32),
                   jax.ShapeDtypeStruct((n, heads, HEAD_DIM, HEAD_DIM), F32)],
        compiler_params=_params("parallel"),
        name="gdn_step",
    )(proj_a3, proj_a3, proj_a3, hist, s0, conv_w, a_log, dt_bias, norm_w)


def _alibi_slope(head, n_heads):
    return 2.0 ** (-8.0 * (head + 1) / n_heads)


def _penalty(dist, mask):
    return jnp.where(mask, jnp.minimum(dist, WINDOW).astype(F32), -NEG_BIG)


def _attend(jobs, n_rows):
    scale = HEAD_DIM ** -0.5
    hd = HEAD_DIM
    scores = [[_dot_nt(q4, k) * scale for k, _, _ in segs] for q4, segs, _, _ in jobs]
    v_ones = [[jnp.concatenate([v, jnp.ones_like(v)], axis=1) for _, v, _ in segs] for _, segs, _, _ in jobs]
    heads = [(ji, g) for ji in range(len(jobs)) for g in range(len(jobs[ji][3]))]
    logits, maxes = {}, {}
    for ji, g in heads:
        _, segs, _, slopes = jobs[ji]
        r = slice(g * n_rows, (g + 1) * n_rows)
        logits[ji, g] = [sc[r] - slopes[g] * pen for sc, (_, _, pen) in zip(scores[ji], segs)]
    for ji, g in heads:
        by_width = {}
        for lg in logits[ji, g]:
            w = lg.shape[1]
            by_width[w] = lg if w not in by_width else jnp.maximum(by_width[w], lg)
        m = jobs[ji][2][g]
        for lg in by_width.values():
            m = jnp.maximum(m, jnp.max(lg, axis=-1, keepdims=True))
        maxes[ji, g] = m
    accs = {}
    for ji, g in heads:
        acc = None
        for lg, v1 in zip(logits[ji, g], v_ones[ji]):
            pv = _dot(jnp.exp(lg - maxes[ji, g]).astype(BF16), v1)
            acc = pv if acc is None else acc + pv
        accs[ji, g] = acc
    outs = [[None] * len(job[3]) for job in jobs]
    for ji, g in heads:
        acc = accs[ji, g]
        outs[ji][g] = acc[:, :hd] / (acc[:, hd:] + jnp.exp(jobs[ji][2][g] - maxes[ji, g]))
    return outs


def _swa_prompt_kernel(q_ref, kv_ref, kvprev_ref, kvmeta_ref, sinks_ref, o_ref, *, n_meta, kv_heads):
    j = pl.program_id(1)
    tq = WINDOW
    n_sub = q_ref.shape[0] // tq
    hd = HEAD_DIM
    kv_w = kv_heads * hd
    n_heads = kv_heads * SWA_GROUP
    qi = lax.broadcasted_iota(jnp.int32, (tq, tq), 0)
    ki = lax.broadcasted_iota(jnp.int32, (tq, tq), 1)
    dist_own = qi - ki
    dist_prev = dist_own + tq
    pen_own = _penalty(dist_own, dist_own >= 0)
    pen_prev = _penalty(dist_prev, dist_prev <= WINDOW)
    pen_prev_first = _penalty(dist_prev, (dist_prev <= WINDOW) & (j > 0))
    jobs, job_dst = [], []
    for sub in range(n_sub):
        rows = slice(sub * tq, (sub + 1) * tq)
        prev_ref, prev_rows = (kvprev_ref, slice(0, tq)) if sub == 0 else (kv_ref, slice((sub - 1) * tq, sub * tq))
        qpos = n_meta + (j * n_sub + sub) * tq + lax.broadcasted_iota(jnp.int32, (tq, n_meta), 0)
        dist_meta = qpos - lax.broadcasted_iota(jnp.int32, (tq, n_meta), 1)
        pen_meta = _penalty(dist_meta, dist_meta >= 0)
        for kvh in range(kv_heads):
            ks = slice(kvh * hd, (kvh + 1) * hd)
            vs = slice(kv_w + kvh * hd, kv_w + (kvh + 1) * hd)
            segments = [
                (kvmeta_ref[:, ks].astype(BF16), kvmeta_ref[:, vs].astype(BF16), pen_meta),
                (prev_ref[prev_rows, ks].astype(BF16), prev_ref[prev_rows, vs].astype(BF16),
                 pen_prev_first if sub == 0 else pen_prev),
                (kv_ref[rows, ks].astype(BF16), kv_ref[rows, vs].astype(BF16), pen_own),
            ]
            heads = [kvh * SWA_GROUP + g for g in range(SWA_GROUP)]
            q4 = jnp.concatenate([q_ref[rows, hh * hd:(hh + 1) * hd] for hh in heads], axis=0).astype(BF16)
            jobs.append((q4, segments, [sinks_ref[:, hh:hh + 1] for hh in heads],
                         [_alibi_slope(hh, n_heads) for hh in heads]))
            job_dst.append((rows, heads))
    for (rows, heads), outs in zip(job_dst, _attend(jobs, tq)):
        for hh, o in zip(heads, outs):
            o_ref[rows, hh * hd:(hh + 1) * hd] = o


def _swa_prompt(proj_b, proj_b_meta, cols, sinks, *, n_seq, seq_len, n_meta):
    n_heads = sinks.shape[1]
    kv_heads = n_heads // SWA_GROUP
    q_w, kv_w2 = n_heads * HEAD_DIM, 2 * kv_heads * HEAD_DIM
    n_sub = SWA_SUBBLOCKS if seq_len % (SWA_SUBBLOCKS * WINDOW) == 0 else 1
    tq = n_sub * WINDOW
    nblk = seq_len // tq
    kv_blk = cols["skv"] // kv_w2
    return pl.pallas_call(
        functools.partial(_swa_prompt_kernel, n_meta=n_meta, kv_heads=kv_heads),
        grid=(n_seq, nblk),
        in_specs=[
            pl.BlockSpec((tq, q_w), lambda n, j: (n * nblk + j, 0)),
            pl.BlockSpec((tq, kv_w2), lambda n, j: (n * nblk + j, kv_blk)),
            pl.BlockSpec((WINDOW, kv_w2), lambda n, j: (jnp.maximum((n * nblk + j) * n_sub - 1, 0), kv_blk)),
            pl.BlockSpec((n_meta, kv_w2), lambda n, j: (0, kv_blk)),
            pl.BlockSpec((1, n_heads), lambda n, j: (0, 0)),
        ],
        out_specs=pl.BlockSpec((tq, q_w), lambda n, j: (n * nblk + j, 0)),
        out_shape=jax.ShapeDtypeStruct((n_seq * seq_len, q_w), F32),
        compiler_params=_params("parallel", "arbitrary"),
        name="swa_prompt",
    )(proj_b, proj_b, proj_b, proj_b_meta, sinks)


def _swa_step_kernel(q_ref, kvnew_ref, kmeta_ref, vmeta_ref, kbuf_ref, vbuf_ref, sinks_ref,
                     o_ref, kout_ref, vout_ref, *, kv_heads):
    nb, t, _ = q_ref.shape
    n_meta, w = kmeta_ref.shape[1], kbuf_ref.shape[1]
    hd = HEAD_DIM
    kv_w = kv_heads * hd
    n_heads = kv_heads * SWA_GROUP

    def grid2(rows, cols_):
        return (lax.broadcasted_iota(jnp.int32, (rows, cols_), 0), lax.broadcasted_iota(jnp.int32, (rows, cols_), 1))

    ti, mi = grid2(t, n_meta)
    dist_meta = PAST_LEN + ti - mi
    pen_meta = _penalty(dist_meta, dist_meta >= 0)
    ti, bi = grid2(t, w)
    dist_buf = w + ti - bi
    pen_buf = _penalty(dist_buf, (dist_buf <= WINDOW) & (PAST_LEN - w + bi >= n_meta))
    ti, si = grid2(t, t)
    dist_new = ti - si
    pen_new = _penalty(dist_new, dist_new >= 0)
    jobs, job_dst = [], []
    for b in range(nb):
        kv_new = kvnew_ref[b]
        for kvh in range(kv_heads):
            ks = slice(kvh * hd, (kvh + 1) * hd)
            k_old, v_old = kbuf_ref[b, :, kvh, :], vbuf_ref[b, :, kvh, :]
            k_new, v_new = kv_new[:, ks], kv_new[:, kv_w + kvh * hd:kv_w + (kvh + 1) * hd]
            kout_ref[b, 0:w - t, kvh, :] = k_old[t:w]
            kout_ref[b, w - t:w, kvh, :] = k_new
            vout_ref[b, 0:w - t, kvh, :] = v_old[t:w]
            vout_ref[b, w - t:w, kvh, :] = v_new
            segments = [
                (kmeta_ref[b, :, ks].astype(BF16), vmeta_ref[b, :, ks].astype(BF16), pen_meta),
                (k_old.astype(BF16), v_old.astype(BF16), pen_buf),
                (k_new.astype(BF16), v_new.astype(BF16), pen_new),
            ]
            heads = [kvh * SWA_GROUP + g for g in range(SWA_GROUP)]
            q4 = jnp.concatenate([q_ref[b, :, hh * hd:(hh + 1) * hd] for hh in heads], axis=0).astype(BF16)
            jobs.append((q4, segments, [sinks_ref[:, hh:hh + 1] for hh in heads],
                         [_alibi_slope(hh, n_heads) for hh in heads]))
            job_dst.append((b, heads))
    for (b, heads), outs in zip(job_dst, _attend(jobs, t)):
        for hh, o in zip(heads, outs):
            o_ref[b, :, hh * hd:(hh + 1) * hd] = o


def _swa_step(proj_b3, cols, k_meta, v_meta, k_buf, v_buf, sinks, *, seqs_per_step):
    n, t = k_buf.shape[0], proj_b3.shape[1]
    n_heads = sinks.shape[1]
    kv_heads = n_heads // SWA_GROUP
    q_w, kv_w = n_heads * HEAD_DIM, kv_heads * HEAD_DIM
    n_meta, w = k_meta.shape[1], k_buf.shape[1]
    nb = seqs_per_step
    kv_blk = cols["skv"] // (2 * kv_w)
    seq3 = lambda rows, width, blk=0: pl.BlockSpec((nb, rows, width), lambda i: (i, 0, blk))
    cache_spec = pl.BlockSpec((nb, w, kv_heads, HEAD_DIM), lambda i: (i, 0, 0, 0))
    return pl.pallas_call(
        functools.partial(_swa_step_kernel, kv_heads=kv_heads),
        grid=(n // nb,),
        in_specs=[seq3(t, q_w), seq3(t, 2 * kv_w, kv_blk), seq3(n_meta, kv_w), seq3(n_meta, kv_w),
                  cache_spec, cache_spec, pl.BlockSpec((1, n_heads), lambda i: (0, 0))],
        out_specs=[seq3(t, q_w), cache_spec, cache_spec],
        out_shape=[jax.ShapeDtypeStruct((n, t, q_w), F32),
                   jax.ShapeDtypeStruct((n, w, kv_heads, HEAD_DIM), F32),
                   jax.ShapeDtypeStruct((n, w, kv_heads, HEAD_DIM), F32)],
        compiler_params=_params("parallel"),
        name="swa_step",
    )(proj_b3, proj_b3, k_meta, v_meta, k_buf, v_buf, sinks)


def _split_w_in(w_in, heads, n_heads, kv_heads, d):
    conv_w = 3 * heads * HEAD_DIM
    v_w = heads * HEAD_DIM
    q_w, kv_w = n_heads * HEAD_DIM, kv_heads * HEAD_DIM
    sizes = (conv_w, v_w, heads, heads, q_w, kv_w, kv_w, d, d)
    offs = [0]
    for s in sizes:
        offs.append(offs[-1] + s)
    assert offs[2] % LANES == 0 and 2 * heads <= LANES and offs[2] + LANES <= w_in.shape[1]
    w_bf = w_in.astype(BF16)
    part = lambda i: w_bf[:, offs[i]:offs[i + 1]]
    w_b = jnp.concatenate([part(4), part(8), part(7), part(5), part(6)], axis=1)
    cols_a = {"qkv": 0, "z": conv_w, "ba": offs[2]}
    cols_b = {"q": 0, "g_swa": q_w, "g_dn": q_w + d, "skv": q_w + 2 * d}
    return w_bf, offs[2] + LANES, w_b, cols_a, cols_b


def kernel(x_prompt, x_sample, state_dn_conv, state_dn_ssm, cache_swa_meta_k, cache_swa_meta_v, cache_swa_k, cache_swa_v, meta_tokens, ffn1_norm_pre, ffn1_norm_post, ffn1_w_gate, ffn1_w_up, ffn1_w_down, mix_norm_pre, mix_norm_post, w_in, dn_conv_w, dn_a_log, dn_dt_bias, dn_norm_w, swa_sinks, w_out, ffn2_norm_pre, ffn2_norm_post, ffn2_w_gate, ffn2_w_up, ffn2_w_down):
    assert w_in.shape[0] == 1, "single-layer step"
    n_p, seq, d = x_prompt.shape
    n_s, t_s, _ = x_sample.shape
    n_meta = meta_tokens.shape[0]
    heads = dn_a_log.shape[1]
    n_heads = swa_sinks.shape[1]
    kv_heads = n_heads // SWA_GROUP
    conv_w = dn_conv_w.shape[2]
    kv_w = kv_heads * HEAD_DIM
    w_keep = cache_swa_k.shape[2]
    assert w_keep == WINDOW and seq % WINDOW == 0 and seq % DN_CHUNK == 0 and seq >= WINDOW

    w_bf, a_cols, w_b, cols_a, cols_b = _split_w_in(w_in[0], heads, n_heads, kv_heads, d)
    cols = {**cols_a, **cols_b}
    f1 = (ffn1_norm_pre, ffn1_norm_post, ffn1_w_gate[0].astype(BF16), ffn1_w_up[0].astype(BF16),
          ffn1_w_down[0].astype(BF16))
    f2 = (ffn2_norm_pre, ffn2_norm_post, ffn2_w_gate[0].astype(BF16), ffn2_w_up[0].astype(BF16),
          ffn2_w_down[0].astype(BF16))
    w_o = w_out[0].astype(BF16)
    gdn_w = (dn_conv_w[0], dn_a_log, dn_dt_bias, dn_norm_w)

    def front(h):
        h1 = _ffn(h, *f1)
        return h1, _proj(h1, mix_norm_pre, w_bf, a_cols), _proj(h1, mix_norm_pre, w_b)

    def hist_tile(rows3):
        return jnp.pad(rows3, ((0, 0), (HIST_ROWS - (DN_CONV - 1), 0), (0, 0)))

    rows_s = n_s * t_s
    assert n_meta % t_s == 0 and n_meta >= DN_CONV - 1
    h1_sm, pa_sm, pb_sm = front(jnp.concatenate([x_sample.reshape(rows_s, d), meta_tokens], axis=0))
    pa_m, pb_m = pa_sm[rows_s:], pb_sm[rows_s:]
    zero_hist = jnp.zeros((1, HIST_ROWS, conv_w), F32)
    zero_state = jnp.zeros((1, heads, HEAD_DIM, HEAD_DIM), F32)
    _, s_meta = _gdn_step(pa_m[None], cols, zero_hist, zero_state, *gdn_w, seqs_per_step=1)
    hist_meta = hist_tile(pa_m[None, n_meta - (DN_CONV - 1):, :conv_w])

    h1_p = _ffn(x_prompt.reshape(n_p * seq, d), *f1)
    pa_p, tail_p = _proj_conv(h1_p, mix_norm_pre, w_bf, a_cols, hist_meta, dn_conv_w[0], n_seq=n_p, seq_len=seq,
                              qk_width=heads * HEAD_DIM)
    pb_p = _proj(h1_p, mix_norm_pre, w_b)
    o_dn_p, s_p = _gdn_packed(pa_p, cols, s_meta, dn_a_log, dn_dt_bias, dn_norm_w, n_seq=n_p, seq_len=seq,
                              block=GDN_CHUNKS_PER_STEP * DN_CHUNK)
    o_sw_p = _swa_prompt(pb_p, pb_m, cols, swa_sinks, n_seq=n_p, seq_len=seq, n_meta=n_meta)
    y_prompt = _mixffn(h1_p, o_dn_p, o_sw_p, pb_p, cols, w_o, mix_norm_post, *f2).reshape(n_p, seq, d)

    pb_p3 = pb_p.reshape(n_p, seq, -1)
    p_conv = tail_p[:, HIST_ROWS - (DN_CONV - 1):]
    kv_meta = pb_m[:, cols["skv"]:]
    p_meta_k = jnp.broadcast_to(kv_meta[None, :, :kv_w], (n_p, n_meta, kv_w))
    p_meta_v = jnp.broadcast_to(kv_meta[None, :, kv_w:], (n_p, n_meta, kv_w))
    p_win_k = pb_p3[:, seq - w_keep:, cols["skv"]:cols["skv"] + kv_w]
    p_win_v = pb_p3[:, seq - w_keep:, cols["skv"] + kv_w:]

    pa_s3 = pa_sm.reshape(-1, t_s, pa_sm.shape[1])
    pb_s3 = pb_sm.reshape(-1, t_s, pb_sm.shape[1])
    o_dn_s, s_s = _gdn_step(pa_s3, cols, hist_tile(state_dn_conv[0]), state_dn_ssm[0], *gdn_w,
                            seqs_per_step=STEP_SEQS)
    o_sw_s, s_win_k, s_win_v = _swa_step(
        pb_s3, cols,
        cache_swa_meta_k[0].reshape(n_s, n_meta, kv_w), cache_swa_meta_v[0].reshape(n_s, n_meta, kv_w),
        cache_swa_k[0], cache_swa_v[0], swa_sinks, seqs_per_step=STEP_SEQS)
    y_sample = _mixffn(h1_sm, o_dn_s.reshape(rows_s, -1), o_sw_s.reshape(rows_s, -1), pb_sm, cols, w_o,
                       mix_norm_post, *f2, rows=rows_s).reshape(n_s, t_s, d)
    assert t_s >= DN_CONV - 1, "new conv state is taken from the new rows alone"
    s_conv = pa_s3[:n_s, t_s - (DN_CONV - 1):, :conv_w]

    def kv4(x):
        return x.reshape(1, x.shape[0], x.shape[1], kv_heads, HEAD_DIM)

    return (y_prompt, y_sample, p_conv[None], s_p[None], kv4(p_meta_k), kv4(p_meta_v), kv4(p_win_k),
            kv4(p_win_v), s_conv[None], s_s[None], s_win_k[None], s_win_v[None])
```

```python
import functools
import math

import jax
import jax.numpy as jnp
from jax import lax
from jax.experimental import pallas as pl
from jax.experimental.pallas import tpu as pltpu

F32 = jnp.float32
BF16 = jnp.bfloat16
HIGHEST = lax.Precision.HIGHEST

RMS_EPS = 1e-6
L2_EPS = 1e-6
WINDOW = 128
PAST_LEN = 8192
HEAD_DIM = 128
SWA_GROUP = 4
DN_CONV = 4
DN_CHUNK = 64
HIST_ROWS = 8

VMEM_LIMIT_BYTES = 56 * 1024 * 1024
ROW_TILE_MAX = 576
SUBLANES = 8
LANES = 128
MXU_WIDTH = 256
NEG_BIG = -1e30
SWA_SUBBLOCKS = 2
GDN_CHUNKS_PER_STEP = 4
STEP_SEQS = 4


def _rms(x, g):
    return x * lax.rsqrt(jnp.mean(x * x, axis=-1, keepdims=True) + RMS_EPS) * g


def _silu(x):
    return x * jax.nn.sigmoid(x)


def _dot(a, b):
    return jnp.dot(a, b, preferred_element_type=F32)


def _dot_hi(a, b):
    return jnp.dot(a, b, preferred_element_type=F32, precision=HIGHEST)


def _dot_tn_hi(a, b):
    return lax.dot_general(a, b, (((0,), (0,)), ((), ())), preferred_element_type=F32, precision=HIGHEST)


def _dot_nt(a, b):
    return lax.dot_general(a, b, (((1,), (1,)), ((), ())), preferred_element_type=F32)


def _dot_tn(a, b):
    return lax.dot_general(a, b, (((0,), (0,)), ((), ())), preferred_element_type=F32)


def _split_bf16(a):
    hi = a.astype(BF16)
    return hi, (a - hi.astype(F32)).astype(BF16)


_DOTS = {"nn": _dot, "nt": _dot_nt, "tn": _dot_tn}


def _dot3_split(a_pair, b_pair, kind="nn"):
    (a_hi, a_lo), (b_hi, b_lo) = a_pair, b_pair
    dot = _DOTS[kind]
    axis = 1 if kind == "tn" else 0
    m = a_hi.shape[axis]
    top = dot(jnp.concatenate([a_hi, a_lo], axis=axis), b_hi)
    return top[:m] + top[m:] + dot(a_hi, b_lo)


def _mm(a, b, kind="nn", passes=3):
    if passes == 1:
        return _DOTS[kind](a.astype(BF16), b.astype(BF16))
    return _dot3_split(_split_bf16(a), _split_bf16(b), kind)


def _row_tile(rows):
    start = min(ROW_TILE_MAX, rows) // SUBLANES * SUBLANES
    for tile in range(start, 0, -SUBLANES):
        if rows % tile == 0:
            return tile
    raise ValueError(f"no sublane-aligned row tile divides {rows}")


def _resident(shape):
    return pl.BlockSpec(shape, lambda *_: (0,) * len(shape), pipeline_mode=pl.Buffered(1))


def _params(*semantics):
    return pltpu.CompilerParams(dimension_semantics=semantics, vmem_limit_bytes=VMEM_LIMIT_BYTES)


def _ffn_tiles(hs, g_pre, g_post, wg_ref, wu_ref, wd_ref, ff_chunk):
    xns = [_rms(h, g_pre).astype(BF16) for h in hs]
    accs = [None] * len(hs)
    c0 = 0
    for width in ff_chunk:
        gates = [_dot(xn, wg_ref[:, c0:c0 + width]) for xn in xns]
        ups = [_dot(xn, wu_ref[:, c0:c0 + width]) for xn in xns]
        for i, (gate, up) in enumerate(zip(gates, ups)):
            part = _dot((_silu(gate) * up).astype(BF16), wd_ref[c0:c0 + width, :])
            accs[i] = part if accs[i] is None else accs[i] + part
        c0 += width
    return [h + 0.5 * _rms(acc, g_post) for h, acc in zip(hs, accs)]


def _row_halves(tm):
    return [slice(0, tm // 2), slice(tm // 2, tm)] if tm % (2 * SUBLANES) == 0 else [slice(0, tm)]


def _ff_chunk(d_ff):
    tiles, rem = divmod(d_ff, MXU_WIDTH)
    if rem or tiles < 2:
        return (d_ff,)
    first = (tiles + 1) // 2
    return (first * MXU_WIDTH, (tiles - first) * MXU_WIDTH)


def _ffn_kernel(h_ref, gpre_ref, gpost_ref, wg_ref, wu_ref, wd_ref, o_ref, *, ff_chunk):
    halves = _row_halves(h_ref.shape[0])
    outs = _ffn_tiles([h_ref[r, :] for r in halves], gpre_ref[...], gpost_ref[...], wg_ref, wu_ref, wd_ref, ff_chunk)
    for r, out in zip(halves, outs):
        o_ref[r, :] = out


def _ffn(h, g_pre, g_post, wg, wu, wd):
    rows, d = h.shape
    d_ff = wg.shape[1]
    tm = _row_tile(rows)
    row_spec = pl.BlockSpec((tm, d), lambda i: (i, 0))
    return pl.pallas_call(
        functools.partial(_ffn_kernel, ff_chunk=_ff_chunk(d_ff)),
        grid=(rows // tm,),
        in_specs=[row_spec, _resident((1, d)), _resident((1, d)),
                  _resident((d, d_ff)), _resident((d, d_ff)), _resident((d_ff, d))],
        out_specs=row_spec,
        out_shape=jax.ShapeDtypeStruct((rows, d), F32),
        compiler_params=_params("parallel"),
        name="ffn",
    )(h, g_pre, g_post, wg, wu, wd)


def _proj_kernel(h_ref, g_ref, w_ref, o_ref):
    o_ref[...] = _dot(_rms(h_ref[...], g_ref[...]).astype(BF16), w_ref[...])


def _proj(h, g, w, n_cols=None):
    rows, d = h.shape
    n = w.shape[1] if n_cols is None else n_cols
    tm = _row_tile(rows)
    return pl.pallas_call(
        _proj_kernel,
        grid=(rows // tm,),
        in_specs=[pl.BlockSpec((tm, d), lambda i: (i, 0)), _resident((1, d)), _resident((d, n))],
        out_specs=pl.BlockSpec((tm, n), lambda i: (i, 0)),
        out_shape=jax.ShapeDtypeStruct((rows, n), F32),
        compiler_params=_params("parallel"),
        name="proj",
    )(h, g, w)


def _proj_conv_kernel(h_ref, g_ref, w_ref, hist_ref, convw_ref, o_ref, tail_ref, ybuf_ref, *, conv_width, qk_width):
    j = pl.program_id(1)
    tm = h_ref.shape[0]
    hd = HEAD_DIM
    xn = _rms(h_ref[...], g_ref[...]).astype(BF16)

    @pl.when(j == 0)
    def _():
        ybuf_ref[0:HIST_ROWS, :] = hist_ref[0]

    @pl.when(j > 0)
    def _():
        ybuf_ref[0:HIST_ROWS, :] = tail_ref[0]

    def project_raw(c0, width):
        ybuf_ref[HIST_ROWS:HIST_ROWS + tm, c0:c0 + width] = _dot(xn, w_ref[:, c0:c0 + width])

    def conv_group(g0, width):
        tail_ref[0, :, g0:g0 + width] = ybuf_ref[tm:tm + HIST_ROWS, g0:g0 + width]
        for c0 in range(g0, g0 + width, hd):
            x = ybuf_ref[:, c0:c0 + hd]
            acc = x[HIST_ROWS:] * convw_ref[DN_CONV - 1:DN_CONV, c0:c0 + hd]
            for back in range(1, DN_CONV):
                shifted = pltpu.roll(x, back, 0)[HIST_ROWS:]
                acc = acc + shifted * convw_ref[DN_CONV - 1 - back:DN_CONV - back, c0:c0 + hd]
            y = _silu(acc)
            if c0 < 2 * qk_width:
                y = y * lax.rsqrt(jnp.sum(y * y, axis=-1, keepdims=True) + L2_EPS)
                if c0 < qk_width:
                    y = y * (hd ** -0.5)
            o_ref[:, c0:c0 + hd] = y

    group = 2 * MXU_WIDTH
    starts = list(range(0, conv_width, group))
    n_rest = w_ref.shape[1] - conv_width
    rest_cuts = ([conv_width + (n_rest * i // len(starts)) // MXU_WIDTH * MXU_WIDTH for i in range(len(starts))]
                 + [w_ref.shape[1]])
    project_raw(starts[0], group)
    for gi, g0 in enumerate(starts):
        if gi + 1 < len(starts):
            project_raw(starts[gi + 1], group)
        r0, r1 = rest_cuts[gi], rest_cuts[gi + 1]
        if r1 > r0:
            o_ref[:, r0:r1] = _dot(xn, w_ref[:, r0:r1])
        conv_group(g0, group)


def _proj_conv(h, g, w, n_cols, hist, conv_w, *, n_seq, seq_len, qk_width):
    rows, d = h.shape
    n = n_cols
    conv_width = conv_w.shape[1]
    tm = _row_tile(seq_len)
    tiles = seq_len // tm
    shared_hist = hist.shape[0] == 1
    return pl.pallas_call(
        functools.partial(_proj_conv_kernel, conv_width=conv_width, qk_width=qk_width),
        grid=(n_seq, tiles),
        in_specs=[pl.BlockSpec((tm, d), lambda s, j: (s * tiles + j, 0)), _resident((1, d)), _resident((d, n)),
                  pl.BlockSpec((1, HIST_ROWS, conv_width), lambda s, j: (0 if shared_hist else s, 0, 0)),
                  _resident((DN_CONV, conv_width))],
        out_specs=[pl.BlockSpec((tm, n), lambda s, j: (s * tiles + j, 0)),
                   pl.BlockSpec((1, HIST_ROWS, conv_width), lambda s, j: (s, 0, 0))],
        out_shape=[jax.ShapeDtypeStruct((rows, n), F32),
                   jax.ShapeDtypeStruct((n_seq, HIST_ROWS, conv_width), F32)],
        scratch_shapes=[pltpu.VMEM((HIST_ROWS + tm, conv_width), F32)],
        compiler_params=_params("parallel", "arbitrary"),
        name="proj_conv",
    )(h, g, w, hist, conv_w)


def _mixffn_kernel(h_ref, odn_ref, osw_ref, gdn_ref, gsw_ref, wout_ref, gmix_ref,
                   gpre_ref, gpost_ref, wg_ref, wu_ref, wd_ref, o_ref, *, ff_chunk):
    halves = _row_halves(h_ref.shape[0])
    ys = [(jax.nn.sigmoid(gdn_ref[r, :]) * odn_ref[r, :] + jax.nn.sigmoid(gsw_ref[r, :]) * osw_ref[r, :]).astype(BF16)
          for r in halves]
    mixed = [_dot(y, wout_ref[...]) for y in ys]
    h2s = [h_ref[r, :] + _rms(m, gmix_ref[...]) for r, m in zip(halves, mixed)]
    outs = _ffn_tiles(h2s, gpre_ref[...], gpost_ref[...], wg_ref, wu_ref, wd_ref, ff_chunk)
    for r, out in zip(halves, outs):
        o_ref[r, :] = out


def _mixffn(h, o_dn, o_sw, proj_b, cols, w_out, g_mix, g_pre, g_post, wg, wu, wd, rows=None):
    d = h.shape[1]
    rows = h.shape[0] if rows is None else rows
    d_ff = wg.shape[1]
    tm = _row_tile(rows)
    row_spec = pl.BlockSpec((tm, d), lambda i: (i, 0))
    gdn_blk, gsw_blk = cols["g_dn"] // d, cols["g_swa"] // d
    return pl.pallas_call(
        functools.partial(_mixffn_kernel, ff_chunk=_ff_chunk(d_ff)),
        grid=(rows // tm,),
        in_specs=[row_spec, row_spec, row_spec,
                  pl.BlockSpec((tm, d), lambda i: (i, gdn_blk)),
                  pl.BlockSpec((tm, d), lambda i: (i, gsw_blk)),
                  _resident((d, d)), _resident((1, d)), _resident((1, d)), _resident((1, d)),
                  _resident((d, d_ff)), _resident((d, d_ff)), _resident((d_ff, d))],
        out_specs=row_spec,
        out_shape=jax.ShapeDtypeStruct((rows, d), F32),
        compiler_params=_params("parallel"),
        name="mixffn",
    )(h, o_dn, o_sw, proj_b, proj_b, w_out, g_mix, g_pre, g_post, wg, wu, wd)


GROUP = 4
PASSES_QK = 1
PASSES_STATE = 1
PASSES_INV = 3
INV_BLOCK = 16
STEP_PASSES = 1


def _gdn_block_stages(qkv_ref, z_ref, ba_ref, alog_ref, dtb_ref, nw_ref, s_ref, store_out, *, chunk, heads):
    tb = qkv_ref.shape[0]
    hd = HEAD_DIM
    c = chunk
    gw = GROUP * c
    qk_w = heads * hd
    stages = []

    def iota(shape, dim):
        return lax.broadcasted_iota(jnp.int32, shape, dim)

    row = iota((c, gw), 0)
    col = iota((c, gw), 1) & (c - 1)
    causal4, strict4 = row >= col, row > col
    diag_blocks4 = (row // INV_BLOCK) == (col // INV_BLOCK)
    eye4 = (row == col).astype(F32)
    bd_mask = (iota((gw, gw), 0) // c) == (iota((gw, gw), 1) // c)
    lower_ones = (iota((c, c), 0) >= iota((c, c), 1)).astype(F32)
    upper_dup = (iota((c, 2 * c), 0) <= (iota((c, 2 * c), 1) & (c - 1))).astype(F32)
    first_half = iota((c, 2 * c), 1) < c
    zeros_head = jnp.zeros((c, hd), F32)

    def block_diag(x4):
        return jnp.where(bd_mask, jnp.concatenate([x4] * GROUP, axis=0), jnp.zeros((), x4.dtype))

    splits = {}

    def split(x):
        if id(x) not in splits:
            splits[id(x)] = (x, _split_bf16(x))
        return splits[id(x)][1]

    def times_block_diag(lhs, x4, passes):
        parts = lhs if isinstance(lhs, list) else [lhs]
        if passes == 1:
            return _dot(jnp.concatenate(parts, axis=0).astype(BF16), block_diag(x4.astype(BF16)))
        hi, lo = split(x4)
        lhs_split = tuple(jnp.concatenate(pieces, axis=0) for pieces in zip(*[split(p) for p in parts]))
        return _dot3_split(lhs_split, (block_diag(hi), block_diag(lo)))

    def block_diag_times(x4, rhs, passes):
        if passes == 1:
            return _dot(block_diag(x4.astype(BF16)), rhs.astype(BF16))
        hi, lo = _split_bf16(x4)
        return _dot3_split((block_diag(hi), block_diag(lo)), _split_bf16(rhs))

    def pack_cols(col_all, hs):
        halves = [jnp.where(first_half,
                            jnp.broadcast_to(col_all[:, hs[2 * p]:hs[2 * p] + 1], (c, 2 * c)),
                            jnp.broadcast_to(col_all[:, hs[2 * p + 1]:hs[2 * p + 1] + 1], (c, 2 * c)))
                  for p in range(GROUP // 2)]
        return jnp.concatenate(halves, axis=1)

    chunks = [dict(r0=r0) for r0 in range(0, tb, c)]
    items = [dict(ch=ch, hs=list(range(g0, g0 + GROUP))) for ch in chunks for g0 in range(0, heads, GROUP)]

    def gating():
        for ch in chunks:
            ba = ba_ref[ch["r0"]:ch["r0"] + c, :]
            a_in = ba[:, heads:2 * heads] + dtb_ref[...]
            softplus = jnp.maximum(a_in, 0.0) + jnp.log(1.0 + jnp.exp(-jnp.abs(a_in)))
            g_all = -jnp.exp(alog_ref[...]) * softplus
            ch["beta"] = jax.nn.sigmoid(ba[:, 0:heads])
            ch["gc"] = _dot_hi(lower_ones, g_all)
            ch["gc_row"] = _dot_tn_hi(g_all, upper_dup)
        for it in items:
            rows = slice(it["ch"]["r0"], it["ch"]["r0"] + c)
            it["qs"] = [qkv_ref[rows, h * hd:(h + 1) * hd] for h in it["hs"]]
            it["ks"] = [qkv_ref[rows, qk_w + h * hd:qk_w + (h + 1) * hd] for h in it["hs"]]
            it["vs"] = [qkv_ref[rows, 2 * qk_w + h * hd:2 * qk_w + (h + 1) * hd] for h in it["hs"]]

    def gram(it):
        ch, hs, qs, ks = it["ch"], it["hs"], it["qs"], it["ks"]
        beta4 = pack_cols(ch["beta"], hs)
        gc_row4 = jnp.concatenate(
            [jnp.where(first_half[0:1], ch["gc_row"][hs[2 * p]:hs[2 * p] + 1],
                       ch["gc_row"][hs[2 * p + 1]:hs[2 * p + 1] + 1])
             for p in range(GROUP // 2)], axis=1)
        diff4 = pack_cols(ch["gc"], hs) - gc_row4
        decay4 = jnp.where(causal4, jnp.exp(jnp.where(causal4, diff4, 0.0)), 0.0)
        lhs = jnp.concatenate([jnp.concatenate(qs, axis=1), jnp.concatenate(ks, axis=1)], axis=0)
        k_bd = jnp.concatenate(
            [jnp.concatenate([ks[i] if ii == i else zeros_head for ii in range(GROUP)], axis=1)
             for i in range(GROUP)], axis=0)
        qkkk = _mm(lhs, k_bd, "nt", PASSES_QK)
        it["qk4"] = qkkk[:c] * decay4
        a4 = jnp.where(strict4, beta4 * qkkk[c:] * decay4, 0.0)
        a_diag = jnp.where(diag_blocks4, a4, 0.0)
        it["a_diag"], it["a_off"] = a_diag, a4 - a_diag
        it["t4"] = eye4 - a_diag

    assert c // INV_BLOCK == 4
    n_sq = int(math.log2(INV_BLOCK)) - 1

    def inverse_start(it):
        it["p4"] = times_block_diag(it["a_diag"], it["a_diag"], PASSES_INV)

    def inverse_step(step, it):
        if step < n_sq - 1:
            both = times_block_diag([it["p4"], it["t4"]], it["p4"], PASSES_INV)
            it["p4"], it["t4"] = both[:c], it["t4"] + both[c:]
        else:
            it["t4"] = it["t4"] + times_block_diag(it["t4"], it["p4"], PASSES_INV)

    def off_diagonal(it):
        it["m4"] = times_block_diag(it["t4"], it["a_off"], PASSES_INV)

    def off_diagonal_square(it):
        it["m2"] = times_block_diag(it["m4"], it["m4"], PASSES_INV)

    def series(it):
        i_minus_m = eye4 - it["m4"]
        it["s4"] = i_minus_m + times_block_diag(i_minus_m, it["m2"], PASSES_INV)

    def inverse_finish(it):
        it["t4"] = times_block_diag(it["s4"], it["t4"], PASSES_INV)

    def solve(it):
        ch, hs, ks, vs = it["ch"], it["hs"], it["ks"], it["vs"]
        rhs = jnp.concatenate(
            [jnp.concatenate([vs[i] * ch["beta"][:, h:h + 1],
                              ks[i] * (ch["beta"][:, h:h + 1] * jnp.exp(ch["gc"][:, h:h + 1]))], axis=1)
             for i, h in enumerate(hs)], axis=0)
        it["sol"] = block_diag_times(it["t4"], rhs, PASSES_INV)

    def for_items(fn, subset=None):
        return lambda: [fn(it) for it in (items if subset is None else subset)]

    stages.append(gating)
    stages.append(for_items(gram))
    stages.append(for_items(inverse_start))
    for step in range(n_sq):
        stages.append(for_items(functools.partial(inverse_step, step)))
    stages += [for_items(off_diagonal), for_items(off_diagonal_square), for_items(series),
               for_items(inverse_finish), for_items(solve)]

    def read_state(it):
        ch = it["ch"]
        us, o_inter = [], []
        for i, h in enumerate(it["hs"]):
            sl = slice(i * c, (i + 1) * c)
            ws = _mm(jnp.concatenate([it["sol"][sl, hd:], it["qs"][i] * jnp.exp(ch["gc"][:, h:h + 1])], axis=0),
                     s_ref[h], "nn", PASSES_STATE)
            us.append(it["sol"][sl, :hd] - ws[:c])
            o_inter.append(ws[c:])
        it["us"], it["o_inter"] = us, o_inter

    def intra(it):
        it["o_intra"] = block_diag_times(it["qk4"], jnp.concatenate(it["us"], axis=0), PASSES_STATE)

    def update_state(it):
        ch = it["ch"]
        for i, h in enumerate(it["hs"]):
            gc = ch["gc"][:, h:h + 1]
            gc_last = ch["gc_row"][h:h + 1, c - 1:c]
            s_ref[h] = s_ref[h] * jnp.exp(gc_last) + _mm(it["ks"][i] * jnp.exp(gc_last - gc), it["us"][i],
                                                         "tn", PASSES_STATE)

    def emit(it):
        r0 = it["ch"]["r0"]
        for i, h in enumerate(it["hs"]):
            o = it["o_inter"][i] + it["o_intra"][i * c:(i + 1) * c]
            zh = z_ref[r0:r0 + c, h * hd:(h + 1) * hd]
            store_out(slice(r0, r0 + c), slice(h * hd, (h + 1) * hd), _rms(o, nw_ref[...]) * _silu(zh))

    for ch in chunks:
        group_items = [it for it in items if it["ch"] is ch]
        stages += [for_items(fn, group_items) for fn in (read_state, intra, update_state, emit)]
    return stages


def _gdn_packed_kernel(qkv_ref, z_ref, ba_ref, s0_ref, alog_ref, dtb_ref, nw_ref,
                       o_ref, sout_ref, s_ref, *, chunk, heads):
    j = pl.program_id(1)

    @pl.when(j == 0)
    def _():
        s_ref[...] = s0_ref[0]

    def store_out(rows, lanes, value):
        o_ref[rows, lanes] = value

    for stage in _gdn_block_stages(qkv_ref, z_ref, ba_ref, alog_ref, dtb_ref, nw_ref, s_ref, store_out,
                                   chunk=chunk, heads=heads):
        stage()

    @pl.when(j == pl.num_programs(1) - 1)
    def _():
        sout_ref[0] = s_ref[...]


def _gdn_packed(proj_a, cols, s0, a_log, dt_bias, norm_w, *, n_seq, seq_len, block):
    heads = a_log.shape[1]
    v_w = heads * HEAD_DIM
    conv_width = 3 * v_w
    nblk = seq_len // block
    z_blk, ba_blk = cols["z"] // v_w, cols["ba"] // LANES
    shared_s0 = s0.shape[0] == 1
    assert 2 * DN_CHUNK == HEAD_DIM and heads % GROUP == 0 and block % DN_CHUNK == 0
    return pl.pallas_call(
        functools.partial(_gdn_packed_kernel, chunk=DN_CHUNK, heads=heads),
        grid=(n_seq, nblk),
        in_specs=[
            pl.BlockSpec((block, conv_width), lambda n, j: (n * nblk + j, 0)),
            pl.BlockSpec((block, v_w), lambda n, j: (n * nblk + j, z_blk)),
            pl.BlockSpec((block, LANES), lambda n, j: (n * nblk + j, ba_blk)),
            pl.BlockSpec((1, heads, HEAD_DIM, HEAD_DIM), lambda n, j: (0 if shared_s0 else n, 0, 0, 0)),
            pl.BlockSpec((1, heads), lambda n, j: (0, 0)),
            pl.BlockSpec((1, heads), lambda n, j: (0, 0)),
            pl.BlockSpec((1, HEAD_DIM), lambda n, j: (0, 0)),
        ],
        out_specs=[
            pl.BlockSpec((block, v_w), lambda n, j: (n * nblk + j, 0)),
            pl.BlockSpec((1, heads, HEAD_DIM, HEAD_DIM), lambda n, j: (n, 0, 0, 0)),
        ],
        out_shape=[
            jax.ShapeDtypeStruct((n_seq * seq_len, v_w), F32),
            jax.ShapeDtypeStruct((n_seq, heads, HEAD_DIM, HEAD_DIM), F32),
        ],
        scratch_shapes=[pltpu.VMEM((heads, HEAD_DIM, HEAD_DIM), F32)],
        compiler_params=_params("parallel", "arbitrary"),
        name="gdn_packed",
    )(proj_a, proj_a, proj_a, s0, a_log, dt_bias, norm_w)


def _ff_pieces(d_ff, n):
    tiles, rem = divmod(d_ff, MXU_WIDTH)
    if rem:
        return (d_ff,)
    per = -(-tiles // n)
    return tuple(min(per, tiles - t0) * MXU_WIDTH for t0 in range(0, tiles, per))


def _gdn_mix_kernel(qkv_ref, z_ref, ba_ref, s0_ref, alog_ref, dtb_ref, nw_ref,
                    h_ref, osw_ref, gdn_ref, gsw_ref, wout_ref, gmix_ref, gpre_ref, gpost_ref, wg_ref, wu_ref, wd_ref,
                    y_ref, sout_ref, s_ref, odn_ref, *, chunk, heads, nblk, ff_pieces):
    s = pl.program_id(0)
    n_blocks = pl.num_programs(0) - 1
    j = s % nblk
    cur, prev = s % 2, (s + 1) % 2

    @pl.when(s == 0)
    def _():
        odn_ref[...] = jnp.zeros(odn_ref.shape, F32)

    @pl.when((j == 0) & (s < n_blocks))
    def _():
        s_ref[...] = s0_ref[0]

    def store_out(rows, lanes, value):
        odn_ref[cur, rows, lanes] = value

    gdn_stages = _gdn_block_stages(qkv_ref, z_ref, ba_ref, alog_ref, dtb_ref, nw_ref, s_ref, store_out,
                                   chunk=chunk, heads=heads)

    st = {}

    def merge():
        y = jax.nn.sigmoid(gdn_ref[...]) * odn_ref[prev] + jax.nn.sigmoid(gsw_ref[...]) * osw_ref[...]
        st["y"] = y.astype(BF16)

    def out_proj():
        st["mixed"] = _dot(st["y"], wout_ref[...])

    def norms():
        st["h2"] = h_ref[...] + _rms(st["mixed"], gmix_ref[...])
        st["xn"] = _rms(st["h2"], gpre_ref[...]).astype(BF16)
        st["acc"] = None

    def gate(c0, width):
        st["gate"] = _dot(st["xn"], wg_ref[:, c0:c0 + width])

    def up(c0, width):
        st["up"] = _dot(st["xn"], wu_ref[:, c0:c0 + width])

    def down(c0, width):
        part = _dot((_silu(st["gate"]) * st["up"]).astype(BF16), wd_ref[c0:c0 + width, :])
        st["acc"] = part if st["acc"] is None else st["acc"] + part

    def finish():
        y_ref[...] = st["h2"] + 0.5 * _rms(st["acc"], gpost_ref[...])

    mix_units = [merge, out_proj, norms]
    c0 = 0
    for width in ff_pieces:
        mix_units += [functools.partial(fn, c0, width) for fn in (gate, up, down)]
        c0 += width
    mix_units.append(finish)

    done = 0
    for i, stage in enumerate(gdn_stages):
        stage()
        upto = (i + 1) * len(mix_units) // len(gdn_stages)
        for unit in mix_units[done:upto]:
            unit()
        done = upto

    @pl.when((j == nblk - 1) & (s < n_blocks))
    def _():
        sout_ref[0] = s_ref[...]


def _gdn_mix(proj_a, proj_b, h, o_sw, cols, s0, a_log, dt_bias, norm_w, w_out, g_mix, g_pre, g_post, wg, wu, wd,
             *, n_seq, seq_len, block):
    heads = a_log.shape[1]
    v_w = heads * HEAD_DIM
    conv_width = 3 * v_w
    d = h.shape[1]
    d_ff = wg.shape[1]
    nblk = seq_len // block
    n_blocks = n_seq * nblk
    z_blk, ba_blk = cols["z"] // v_w, cols["ba"] // LANES
    gdn_blk, gsw_blk = cols["g_dn"] // d, cols["g_swa"] // d
    shared_s0 = s0.shape[0] == 1
    assert 2 * DN_CHUNK == HEAD_DIM and heads % GROUP == 0 and block % DN_CHUNK == 0 and v_w == d

    def cur(s):
        return jnp.minimum(s, n_blocks - 1)

    def prev(s):
        return jnp.maximum(s - 1, 0)

    return pl.pallas_call(
        functools.partial(_gdn_mix_kernel, chunk=DN_CHUNK, heads=heads, nblk=nblk, ff_pieces=_ff_pieces(d_ff, 4)),
        grid=(n_blocks + 1,),
        in_specs=[
            pl.BlockSpec((block, conv_width), lambda s: (cur(s), 0)),
            pl.BlockSpec((block, v_w), lambda s: (cur(s), z_blk)),
            pl.BlockSpec((block, LANES), lambda s: (cur(s), ba_blk)),
            pl.BlockSpec((1, heads, HEAD_DIM, HEAD_DIM), lambda s: (0 if shared_s0 else cur(s) // nblk, 0, 0, 0)),
            _resident((1, heads)), _resident((1, heads)), _resident((1, HEAD_DIM)),
            pl.BlockSpec((block, d), lambda s: (prev(s), 0)),
            pl.BlockSpec((block, d), lambda s: (prev(s), 0)),
            pl.BlockSpec((block, d), lambda s: (prev(s), gdn_blk)),
            pl.BlockSpec((block, d), lambda s: (prev(s), gsw_blk)),
            _resident((d, d)), _resident((1, d)), _resident((1, d)), _resident((1, d)),
            _resident((d, d_ff)), _resident((d, d_ff)), _resident((d_ff, d)),
        ],
        out_specs=[
            pl.BlockSpec((block, d), lambda s: (prev(s), 0)),
            pl.BlockSpec((1, heads, HEAD_DIM, HEAD_DIM), lambda s: (cur(s) // nblk, 0, 0, 0)),
        ],
        out_shape=[
            jax.ShapeDtypeStruct((n_seq * seq_len, d), F32),
            jax.ShapeDtypeStruct((n_seq, heads, HEAD_DIM, HEAD_DIM), F32),
        ],
        scratch_shapes=[pltpu.VMEM((heads, HEAD_DIM, HEAD_DIM), F32), pltpu.VMEM((2, block, v_w), F32)],
        compiler_params=_params("arbitrary"),
        name="gdn_mix",
    )(proj_a, proj_a, proj_a, s0, a_log, dt_bias, norm_w,
      h, o_sw, proj_b, proj_b, w_out, g_mix, g_pre, g_post, wg, wu, wd)


def _gdn_step_kernel(qkv_ref, z_ref, ba_ref, hist_ref, s0_ref, convw_ref, alog_ref, dtb_ref, nw_ref,
                     o_ref, sout_ref, *, heads):
    nb, t, _ = qkv_ref.shape
    hd = HEAD_DIM
    qk_w = heads * hd
    ti = lax.broadcasted_iota(jnp.int32, (t, t), 0)
    tj = lax.broadcasted_iota(jnp.int32, (t, t), 1)
    lower_ones = (ti >= tj).astype(F32)
    upper_ones = (ti <= tj).astype(F32)

    def per_sequence(b, carry):
        raw = jnp.concatenate([hist_ref[b], qkv_ref[b]], axis=0)
        acc = raw[HIST_ROWS:] * convw_ref[DN_CONV - 1:DN_CONV, :]
        for back in range(1, DN_CONV):
            acc = acc + pltpu.roll(raw, back, 0)[HIST_ROWS:] * convw_ref[DN_CONV - 1 - back:DN_CONV - back, :]
        x = _silu(acc)
        ba = ba_ref[b]
        beta_all = jax.nn.sigmoid(ba[:, 0:heads])
        a_in = ba[:, heads:2 * heads] + dtb_ref[...]
        softplus = jnp.maximum(a_in, 0.0) + jnp.log(1.0 + jnp.exp(-jnp.abs(a_in)))
        g_all = -jnp.exp(alog_ref[...]) * softplus
        gc_all = _dot_hi(lower_ones, g_all)
        gc_row_all = _dot_tn_hi(g_all, upper_ones)
        z = z_ref[b]
        hs = range(heads)
        qs, ks, vs = [], [], []
        for h in hs:
            q = x[:, h * hd:(h + 1) * hd]
            k = x[:, qk_w + h * hd:qk_w + (h + 1) * hd]
            qs.append(q * lax.rsqrt(jnp.sum(q * q, axis=-1, keepdims=True) + L2_EPS) * (hd ** -0.5))
            ks.append(k * lax.rsqrt(jnp.sum(k * k, axis=-1, keepdims=True) + L2_EPS))
            vs.append(x[:, 2 * qk_w + h * hd:2 * qk_w + (h + 1) * hd])
        gcs = [gc_all[:, h:h + 1] for h in hs]
        gc_lasts = [gc_row_all[h:h + 1, t - 1:t] for h in hs]
        kq = [jnp.concatenate([ks[h], qs[h]], axis=0) for h in hs]
        gram = [_mm(kq[h], ks[h], "nt", STEP_PASSES) for h in hs]
        from_state = [_mm(kq[h] * jnp.exp(jnp.concatenate([gcs[h], gcs[h]], axis=0)), s0_ref[b, h], "nn",
                          STEP_PASSES) for h in hs]
        us, outs = [], []
        for h in hs:
            decay = jnp.where(ti >= tj, jnp.exp(jnp.where(ti >= tj, gcs[h] - gc_row_all[h:h + 1, :], 0.0)), 0.0)
            beta = beta_all[:, h:h + 1]
            lower = jnp.where(ti > tj, beta * gram[h][:t] * decay, 0.0)
            u = beta * (vs[h] - from_state[h][:t])
            for j in range(t - 1):
                u = u - lower[:, j:j + 1] * u[j:j + 1, :]
            qk = gram[h][t:] * decay
            o = from_state[h][t:]
            for j in range(t):
                o = o + qk[:, j:j + 1] * u[j:j + 1, :]
            us.append(u)
            outs.append(o)
        for h in hs:
            sout_ref[b, h] = s0_ref[b, h] * jnp.exp(gc_lasts[h]) + _mm(ks[h] * jnp.exp(gc_lasts[h] - gcs[h]), us[h],
                                                                       "tn", STEP_PASSES)
        for h in hs:
            o_ref[b, :, h * hd:(h + 1) * hd] = _rms(outs[h], nw_ref[...]) * _silu(z[:, h * hd:(h + 1) * hd])
        return carry

    lax.fori_loop(0, nb, per_sequence, 0, unroll=2 if nb % 2 == 0 else 1)


def _gdn_step(proj_a3, cols, hist, s0, conv_w, a_log, dt_bias, norm_w, *, seqs_per_step):
    n, t = s0.shape[0], proj_a3.shape[1]
    heads = a_log.shape[1]
    conv_width = conv_w.shape[1]
    v_w = heads * HEAD_DIM
    nb = seqs_per_step
    z_blk, ba_blk = cols["z"] // v_w, cols["ba"] // LANES
    state_spec = pl.BlockSpec((nb, heads, HEAD_DIM, HEAD_DIM), lambda i: (i, 0, 0, 0))
    return pl.pallas_call(
        functools.partial(_gdn_step_kernel, heads=heads),
        grid=(n // nb,),
        in_specs=[
            pl.BlockSpec((nb, t, conv_width), lambda i: (i, 0, 0)),
            pl.BlockSpec((nb, t, v_w), lambda i: (i, 0, z_blk)),
            pl.BlockSpec((nb, t, LANES), lambda i: (i, 0, ba_blk)),
            pl.BlockSpec((nb, HIST_ROWS, conv_width), lambda i: (i, 0, 0)),
            state_spec,
            pl.BlockSpec((DN_CONV, conv_width), lambda i: (0, 0)),
            pl.BlockSpec((1, heads), lambda i: (0, 0)),
            pl.BlockSpec((1, heads), lambda i: (0, 0)),
            pl.BlockSpec((1, HEAD_DIM), lambda i: (0, 0)),
        ],
        out_specs=[pl.BlockSpec((nb, t, v_w), lambda i: (i, 0, 0)), state_spec],
        out_shape=[jax.ShapeDtypeStruct((n, t, v_w), F32),
                   jax.ShapeDtypeStruct((n, heads, HEAD_DIM, HEAD_DIM), F32)],
        compiler_params=_params("parallel"),
        name="gdn_step",
    )(proj_a3, proj_a3, proj_a3, hist, s0, conv_w, a_log, dt_bias, norm_w)


def _alibi_slope(head, n_heads):
    return 2.0 ** (-8.0 * (head + 1) / n_heads)


def _penalty(dist, mask):
    return jnp.where(mask, jnp.minimum(dist, WINDOW).astype(F32), -NEG_BIG)


def _attend(jobs, n_rows):
    scale = HEAD_DIM ** -0.5
    hd = HEAD_DIM
    scores = [[_dot_nt(q4, k) * scale for k, _, _ in segs] for q4, segs, _, _ in jobs]
    v_ones = [[jnp.concatenate([v, jnp.ones_like(v)], axis=1) for _, v, _ in segs] for _, segs, _, _ in jobs]
    heads = [(ji, g) for ji in range(len(jobs)) for g in range(len(jobs[ji][3]))]
    logits, maxes = {}, {}
    for ji, g in heads:
        _, segs, _, slopes = jobs[ji]
        r = slice(g * n_rows, (g + 1) * n_rows)
        logits[ji, g] = [sc[r] - slopes[g] * pen for sc, (_, _, pen) in zip(scores[ji], segs)]
    for ji, g in heads:
        by_width = {}
        for lg in logits[ji, g]:
            w = lg.shape[1]
            by_width[w] = lg if w not in by_width else jnp.maximum(by_width[w], lg)
        m = jobs[ji][2][g]
        for lg in by_width.values():
            m = jnp.maximum(m, jnp.max(lg, axis=-1, keepdims=True))
        maxes[ji, g] = m
    accs = {}
    for ji, g in heads:
        acc = None
        for lg, v1 in zip(logits[ji, g], v_ones[ji]):
            pv = _dot(jnp.exp(lg - maxes[ji, g]).astype(BF16), v1)
            acc = pv if acc is None else acc + pv
        accs[ji, g] = acc
    outs = [[None] * len(job[3]) for job in jobs]
    for ji, g in heads:
        acc = accs[ji, g]
        outs[ji][g] = acc[:, :hd] / (acc[:, hd:] + jnp.exp(jobs[ji][2][g] - maxes[ji, g]))
    return outs


def _swa_prompt_kernel(q_ref, kv_ref, kvprev_ref, kvmeta_ref, sinks_ref, o_ref, *, n_meta, kv_heads):
    j = pl.program_id(1)
    tq = WINDOW
    n_sub = q_ref.shape[0] // tq
    hd = HEAD_DIM
    kv_w = kv_heads * hd
    n_heads = kv_heads * SWA_GROUP
    qi = lax.broadcasted_iota(jnp.int32, (tq, tq), 0)
    ki = lax.broadcasted_iota(jnp.int32, (tq, tq), 1)
    dist_own = qi - ki
    dist_prev = dist_own + tq
    pen_own = _penalty(dist_own, dist_own >= 0)
    pen_prev = _penalty(dist_prev, dist_prev <= WINDOW)
    pen_prev_first = _penalty(dist_prev, (dist_prev <= WINDOW) & (j > 0))
    jobs, job_dst = [], []
    for sub in range(n_sub):
        rows = slice(sub * tq, (sub + 1) * tq)
        prev_ref, prev_rows = (kvprev_ref, slice(0, tq)) if sub == 0 else (kv_ref, slice((sub - 1) * tq, sub * tq))
        qpos = n_meta + (j * n_sub + sub) * tq + lax.broadcasted_iota(jnp.int32, (tq, n_meta), 0)
        dist_meta = qpos - lax.broadcasted_iota(jnp.int32, (tq, n_meta), 1)
        pen_meta = _penalty(dist_meta, dist_meta >= 0)
        for kvh in range(kv_heads):
            ks = slice(kvh * hd, (kvh + 1) * hd)
            vs = slice(kv_w + kvh * hd, kv_w + (kvh + 1) * hd)
            segments = [
                (kvmeta_ref[:, ks].astype(BF16), kvmeta_ref[:, vs].astype(BF16), pen_meta),
                (prev_ref[prev_rows, ks].astype(BF16), prev_ref[prev_rows, vs].astype(BF16),
                 pen_prev_first if sub == 0 else pen_prev),
                (kv_ref[rows, ks].astype(BF16), kv_ref[rows, vs].astype(BF16), pen_own),
            ]
            heads = [kvh * SWA_GROUP + g for g in range(SWA_GROUP)]
            q4 = jnp.concatenate([q_ref[rows, hh * hd:(hh + 1) * hd] for hh in heads], axis=0).astype(BF16)
            jobs.append((q4, segments, [sinks_ref[:, hh:hh + 1] for hh in heads],
                         [_alibi_slope(hh, n_heads) for hh in heads]))
            job_dst.append((rows, heads))
    for (rows, heads), outs in zip(job_dst, _attend(jobs, tq)):
        for hh, o in zip(heads, outs):
            o_ref[rows, hh * hd:(hh + 1) * hd] = o


def _swa_prompt(proj_b, proj_b_meta, cols, sinks, *, n_seq, seq_len, n_meta):
    n_heads = sinks.shape[1]
    kv_heads = n_heads // SWA_GROUP
    q_w, kv_w2 = n_heads * HEAD_DIM, 2 * kv_heads * HEAD_DIM
    n_sub = SWA_SUBBLOCKS if seq_len % (SWA_SUBBLOCKS * WINDOW) == 0 else 1
    tq = n_sub * WINDOW
    nblk = seq_len // tq
    kv_blk = cols["skv"] // kv_w2
    return pl.pallas_call(
        functools.partial(_swa_prompt_kernel, n_meta=n_meta, kv_heads=kv_heads),
        grid=(n_seq, nblk),
        in_specs=[
            pl.BlockSpec((tq, q_w), lambda n, j: (n * nblk + j, 0)),
            pl.BlockSpec((tq, kv_w2), lambda n, j: (n * nblk + j, kv_blk)),
            pl.BlockSpec((WINDOW, kv_w2), lambda n, j: (jnp.maximum((n * nblk + j) * n_sub - 1, 0), kv_blk)),
            pl.BlockSpec((n_meta, kv_w2), lambda n, j: (0, kv_blk)),
            pl.BlockSpec((1, n_heads), lambda n, j: (0, 0)),
        ],
        out_specs=pl.BlockSpec((tq, q_w), lambda n, j: (n * nblk + j, 0)),
        out_shape=jax.ShapeDtypeStruct((n_seq * seq_len, q_w), F32),
        compiler_params=_params("parallel", "arbitrary"),
        name="swa_prompt",
    )(proj_b, proj_b, proj_b, proj_b_meta, sinks)


def _swa_step_kernel(q_ref, kvnew_ref, kmeta_ref, vmeta_ref, kbuf_ref, vbuf_ref, sinks_ref,
                     o_ref, kout_ref, vout_ref, *, kv_heads):
    nb, t, _ = q_ref.shape
    n_meta, w = kmeta_ref.shape[1], kbuf_ref.shape[1]
    hd = HEAD_DIM
    kv_w = kv_heads * hd
    n_heads = kv_heads * SWA_GROUP

    def grid2(rows, cols_):
        return (lax.broadcasted_iota(jnp.int32, (rows, cols_), 0), lax.broadcasted_iota(jnp.int32, (rows, cols_), 1))

    ti, mi = grid2(t, n_meta)
    dist_meta = PAST_LEN + ti - mi
    pen_meta = _penalty(dist_meta, dist_meta >= 0)
    ti, bi = grid2(t, w)
    dist_buf = w + ti - bi
    pen_buf = _penalty(dist_buf, (dist_buf <= WINDOW) & (PAST_LEN - w + bi >= n_meta))
    ti, si = grid2(t, t)
    dist_new = ti - si
    pen_new = _penalty(dist_new, dist_new >= 0)
    jobs, job_dst = [], []
    for b in range(nb):
        kv_new = kvnew_ref[b]
        for kvh in range(kv_heads):
            ks = slice(kvh * hd, (kvh + 1) * hd)
            k_old, v_old = kbuf_ref[b, :, kvh, :], vbuf_ref[b, :, kvh, :]
            k_new, v_new = kv_new[:, ks], kv_new[:, kv_w + kvh * hd:kv_w + (kvh + 1) * hd]
            kout_ref[b, 0:w - t, kvh, :] = k_old[t:w]
            kout_ref[b, w - t:w, kvh, :] = k_new
            vout_ref[b, 0:w - t, kvh, :] = v_old[t:w]
            vout_ref[b, w - t:w, kvh, :] = v_new
            segments = [
                (kmeta_ref[b, :, ks].astype(BF16), vmeta_ref[b, :, ks].astype(BF16), pen_meta),
                (k_old.astype(BF16), v_old.astype(BF16), pen_buf),
                (k_new.astype(BF16), v_new.astype(BF16), pen_new),
            ]
            heads = [kvh * SWA_GROUP + g for g in range(SWA_GROUP)]
            q4 = jnp.concatenate([q_ref[b, :, hh * hd:(hh + 1) * hd] for hh in heads], axis=0).astype(BF16)
            jobs.append((q4, segments, [sinks_ref[:, hh:hh + 1] for hh in heads],
                         [_alibi_slope(hh, n_heads) for hh in heads]))
            job_dst.append((b, heads))
    for (b, heads), outs in zip(job_dst, _attend(jobs, t)):
        for hh, o in zip(heads, outs):
            o_ref[b, :, hh * hd:(hh + 1) * hd] = o


def _swa_step(proj_b3, cols, k_meta, v_meta, k_buf, v_buf, sinks, *, seqs_per_step):
    n, t = k_buf.shape[0], proj_b3.shape[1]
    n_heads = sinks.shape[1]
    kv_heads = n_heads // SWA_GROUP
    q_w, kv_w = n_heads * HEAD_DIM, kv_heads * HEAD_DIM
    n_meta, w = k_meta.shape[1], k_buf.shape[1]
    nb = seqs_per_step
    kv_blk = cols["skv"] // (2 * kv_w)
    seq3 = lambda rows, width, blk=0: pl.BlockSpec((nb, rows, width), lambda i: (i, 0, blk))
    cache_spec = pl.BlockSpec((nb, w, kv_heads, HEAD_DIM), lambda i: (i, 0, 0, 0))
    return pl.pallas_call(
        functools.partial(_swa_step_kernel, kv_heads=kv_heads),
        grid=(n // nb,),
        in_specs=[seq3(t, q_w), seq3(t, 2 * kv_w, kv_blk), seq3(n_meta, kv_w), seq3(n_meta, kv_w),
                  cache_spec, cache_spec, pl.BlockSpec((1, n_heads), lambda i: (0, 0))],
        out_specs=[seq3(t, q_w), cache_spec, cache_spec],
        out_shape=[jax.ShapeDtypeStruct((n, t, q_w), F32),
                   jax.ShapeDtypeStruct((n, w, kv_heads, HEAD_DIM), F32),
                   jax.ShapeDtypeStruct((n, w, kv_heads, HEAD_DIM), F32)],
        compiler_params=_params("parallel"),
        name="swa_step",
    )(proj_b3, proj_b3, k_meta, v_meta, k_buf, v_buf, sinks)


def _split_w_in(w_in, heads, n_heads, kv_heads, d):
    conv_w = 3 * heads * HEAD_DIM
    v_w = heads * HEAD_DIM
    q_w, kv_w = n_heads * HEAD_DIM, kv_heads * HEAD_DIM
    sizes = (conv_w, v_w, heads, heads, q_w, kv_w, kv_w, d, d)
    offs = [0]
    for s in sizes:
        offs.append(offs[-1] + s)
    assert offs[2] % LANES == 0 and 2 * heads <= LANES and offs[2] + LANES <= w_in.shape[1]
    w_bf = w_in.astype(BF16)
    part = lambda i: w_bf[:, offs[i]:offs[i + 1]]
    w_b = jnp.concatenate([part(4), part(8), part(7), part(5), part(6)], axis=1)
    cols_a = {"qkv": 0, "z": conv_w, "ba": offs[2]}
    cols_b = {"q": 0, "g_swa": q_w, "g_dn": q_w + d, "skv": q_w + 2 * d}
    return w_bf, offs[2] + LANES, w_b, cols_a, cols_b


def kernel(x_prompt, x_sample, state_dn_conv, state_dn_ssm, cache_swa_meta_k, cache_swa_meta_v, cache_swa_k, cache_swa_v, meta_tokens, ffn1_norm_pre, ffn1_norm_post, ffn1_w_gate, ffn1_w_up, ffn1_w_down, mix_norm_pre, mix_norm_post, w_in, dn_conv_w, dn_a_log, dn_dt_bias, dn_norm_w, swa_sinks, w_out, ffn2_norm_pre, ffn2_norm_post, ffn2_w_gate, ffn2_w_up, ffn2_w_down):
    assert w_in.shape[0] == 1, "single-layer step"
    n_p, seq, d = x_prompt.shape
    n_s, t_s, _ = x_sample.shape
    n_meta = meta_tokens.shape[0]
    heads = dn_a_log.shape[1]
    n_heads = swa_sinks.shape[1]
    kv_heads = n_heads // SWA_GROUP
    conv_w = dn_conv_w.shape[2]
    kv_w = kv_heads * HEAD_DIM
    w_keep = cache_swa_k.shape[2]
    assert w_keep == WINDOW and seq % WINDOW == 0 and seq % DN_CHUNK == 0 and seq >= WINDOW

    w_bf, a_cols, w_b, cols_a, cols_b = _split_w_in(w_in[0], heads, n_heads, kv_heads, d)
    cols = {**cols_a, **cols_b}
    f1 = (ffn1_norm_pre, ffn1_norm_post, ffn1_w_gate[0].astype(BF16), ffn1_w_up[0].astype(BF16),
          ffn1_w_down[0].astype(BF16))
    f2 = (ffn2_norm_pre, ffn2_norm_post, ffn2_w_gate[0].astype(BF16), ffn2_w_up[0].astype(BF16),
          ffn2_w_down[0].astype(BF16))
    w_o = w_out[0].astype(BF16)
    gdn_w = (dn_conv_w[0], dn_a_log, dn_dt_bias, dn_norm_w)

    def front(h):
        h1 = _ffn(h, *f1)
        return h1, _proj(h1, mix_norm_pre, w_bf, a_cols), _proj(h1, mix_norm_pre, w_b)

    def hist_tile(rows3):
        return jnp.pad(rows3, ((0, 0), (HIST_ROWS - (DN_CONV - 1), 0), (0, 0)))

    rows_s = n_s * t_s
    assert n_meta % t_s == 0 and n_meta >= DN_CONV - 1
    h1_sm, pa_sm, pb_sm = front(jnp.concatenate([x_sample.reshape(rows_s, d), meta_tokens], axis=0))
    pa_m, pb_m = pa_sm[rows_s:], pb_sm[rows_s:]
    zero_hist = jnp.zeros((1, HIST_ROWS, conv_w), F32)
    zero_state = jnp.zeros((1, heads, HEAD_DIM, HEAD_DIM), F32)
    _, s_meta = _gdn_step(pa_m[None], cols, zero_hist, zero_state, *gdn_w, seqs_per_step=1)
    hist_meta = hist_tile(pa_m[None, n_meta - (DN_CONV - 1):, :conv_w])

    h1_p = _ffn(x_prompt.reshape(n_p * seq, d), *f1)
    pa_p, tail_p = _proj_conv(h1_p, mix_norm_pre, w_bf, a_cols, hist_meta, dn_conv_w[0], n_seq=n_p, seq_len=seq,
                              qk_width=heads * HEAD_DIM)
    pb_p = _proj(h1_p, mix_norm_pre, w_b)
    o_sw_p = _swa_prompt(pb_p, pb_m, cols, swa_sinks, n_seq=n_p, seq_len=seq, n_meta=n_meta)
    y_prompt, s_p = _gdn_mix(pa_p, pb_p, h1_p, o_sw_p, cols, s_meta, dn_a_log, dn_dt_bias, dn_norm_w, w_o,
                             mix_norm_post, *f2, n_seq=n_p, seq_len=seq, block=GDN_CHUNKS_PER_STEP * DN_CHUNK)
    y_prompt = y_prompt.reshape(n_p, seq, d)

    pb_p3 = pb_p.reshape(n_p, seq, -1)
    p_conv = tail_p[:, HIST_ROWS - (DN_CONV - 1):]
    kv_meta = pb_m[:, cols["skv"]:]
    p_meta_k = jnp.broadcast_to(kv_meta[None, :, :kv_w], (n_p, n_meta, kv_w))
    p_meta_v = jnp.broadcast_to(kv_meta[None, :, kv_w:], (n_p, n_meta, kv_w))
    p_win_k = pb_p3[:, seq - w_keep:, cols["skv"]:cols["skv"] + kv_w]
    p_win_v = pb_p3[:, seq - w_keep:, cols["skv"] + kv_w:]

    pa_s3 = pa_sm.reshape(-1, t_s, pa_sm.shape[1])
    pb_s3 = pb_sm.reshape(-1, t_s, pb_sm.shape[1])
    o_dn_s, s_s = _gdn_step(pa_s3, cols, hist_tile(state_dn_conv[0]), state_dn_ssm[0], *gdn_w,
                            seqs_per_step=STEP_SEQS)
    o_sw_s, s_win_k, s_win_v = _swa_step(
        pb_s3, cols,
        cache_swa_meta_k[0].reshape(n_s, n_meta, kv_w), cache_swa_meta_v[0].reshape(n_s, n_meta, kv_w),
        cache_swa_k[0], cache_swa_v[0], swa_sinks, seqs_per_step=STEP_SEQS)
    y_sample = _mixffn(h1_sm, o_dn_s.reshape(rows_s, -1), o_sw_s.reshape(rows_s, -1), pb_sm, cols, w_o,
                       mix_norm_post, *f2, rows=rows_s).reshape(n_s, t_s, d)
    assert t_s >= DN_CONV - 1, "new conv state is taken from the new rows alone"
    s_conv = pa_s3[:n_s, t_s - (DN_CONV - 1):, :conv_w]

    def kv4(x):
        return x.reshape(1, x.shape[0], x.shape[1], kv_heads, HEAD_DIM)

    return (y_prompt, y_sample, p_conv[None], s_p[None], kv4(p_meta_k), kv4(p_meta_v), kv4(p_win_k),
            kv4(p_win_v), s_conv[None], s_s[None], s_win_k[None], s_win_v[None])
```

```python
import functools
import math

import jax
import jax.numpy as jnp
from jax import lax
from jax.experimental import pallas as pl
from jax.experimental.pallas import tpu as pltpu

F32 = jnp.float32
BF16 = jnp.bfloat16
HIGHEST = lax.Precision.HIGHEST

RMS_EPS = 1e-6
L2_EPS = 1e-6
WINDOW = 128
PAST_LEN = 8192
HEAD_DIM = 128
SWA_GROUP = 4
DN_CONV = 4
DN_CHUNK = 64
HIST_ROWS = 8

VMEM_LIMIT_BYTES = 56 * 1024 * 1024
ROW_TILE_MAX = 576
SUBLANES = 8
LANES = 128
MXU_WIDTH = 256
NEG_BIG = -1e30
SWA_SUBBLOCKS = 2
GDN_CHUNKS_PER_STEP = 4
STEP_SEQS = 4
FUSED_ROW_TILE = 256


def _rms(x, g):
    return x * lax.rsqrt(jnp.mean(x * x, axis=-1, keepdims=True) + RMS_EPS) * g


def _silu(x):
    return x * jax.nn.sigmoid(x)


def _dot(a, b):
    return jnp.dot(a, b, preferred_element_type=F32)


def _dot_hi(a, b):
    return jnp.dot(a, b, preferred_element_type=F32, precision=HIGHEST)


def _dot_tn_hi(a, b):
    return lax.dot_general(a, b, (((0,), (0,)), ((), ())), preferred_element_type=F32, precision=HIGHEST)


def _dot_nt(a, b):
    return lax.dot_general(a, b, (((1,), (1,)), ((), ())), preferred_element_type=F32)


def _dot_tn(a, b):
    return lax.dot_general(a, b, (((0,), (0,)), ((), ())), preferred_element_type=F32)


def _split_bf16(a):
    hi = a.astype(BF16)
    return hi, (a - hi.astype(F32)).astype(BF16)


_DOTS = {"nn": _dot, "nt": _dot_nt, "tn": _dot_tn}


def _dot3_split(a_pair, b_pair, kind="nn"):
    (a_hi, a_lo), (b_hi, b_lo) = a_pair, b_pair
    dot = _DOTS[kind]
    axis = 1 if kind == "tn" else 0
    m = a_hi.shape[axis]
    top = dot(jnp.concatenate([a_hi, a_lo], axis=axis), b_hi)
    return top[:m] + top[m:] + dot(a_hi, b_lo)


def _mm(a, b, kind="nn", passes=3):
    if passes == 1:
        return _DOTS[kind](a.astype(BF16), b.astype(BF16))
    return _dot3_split(_split_bf16(a), _split_bf16(b), kind)


def _row_tile(rows):
    start = min(ROW_TILE_MAX, rows) // SUBLANES * SUBLANES
    for tile in range(start, 0, -SUBLANES):
        if rows % tile == 0:
            return tile
    raise ValueError(f"no sublane-aligned row tile divides {rows}")


def _resident(shape):
    return pl.BlockSpec(shape, lambda *_: (0,) * len(shape), pipeline_mode=pl.Buffered(1))


def _params(*semantics):
    return pltpu.CompilerParams(dimension_semantics=semantics, vmem_limit_bytes=VMEM_LIMIT_BYTES)


def _ffn_tiles(hs, g_pre, g_post, wg_ref, wu_ref, wd_ref, ff_chunk):
    xns = [_rms(h, g_pre).astype(BF16) for h in hs]
    accs = [None] * len(hs)
    c0 = 0
    for width in ff_chunk:
        gates = [_dot(xn, wg_ref[:, c0:c0 + width]) for xn in xns]
        ups = [_dot(xn, wu_ref[:, c0:c0 + width]) for xn in xns]
        for i, (gate, up) in enumerate(zip(gates, ups)):
            part = _dot((_silu(gate) * up).astype(BF16), wd_ref[c0:c0 + width, :])
            accs[i] = part if accs[i] is None else accs[i] + part
        c0 += width
    return [h + 0.5 * _rms(acc, g_post) for h, acc in zip(hs, accs)]


def _row_halves(tm):
    return [slice(0, tm // 2), slice(tm // 2, tm)] if tm % (2 * SUBLANES) == 0 else [slice(0, tm)]


def _ff_chunk(d_ff):
    tiles, rem = divmod(d_ff, MXU_WIDTH)
    if rem or tiles < 2:
        return (d_ff,)
    first = (tiles + 1) // 2
    return (first * MXU_WIDTH, (tiles - first) * MXU_WIDTH)


def _ffn_kernel(h_ref, gpre_ref, gpost_ref, wg_ref, wu_ref, wd_ref, o_ref, *, ff_chunk):
    halves = _row_halves(h_ref.shape[0])
    outs = _ffn_tiles([h_ref[r, :] for r in halves], gpre_ref[...], gpost_ref[...], wg_ref, wu_ref, wd_ref, ff_chunk)
    for r, out in zip(halves, outs):
        o_ref[r, :] = out


def _ffn(h, g_pre, g_post, wg, wu, wd):
    rows, d = h.shape
    d_ff = wg.shape[1]
    tm = _row_tile(rows)
    row_spec = pl.BlockSpec((tm, d), lambda i: (i, 0))
    return pl.pallas_call(
        functools.partial(_ffn_kernel, ff_chunk=_ff_chunk(d_ff)),
        grid=(rows // tm,),
        in_specs=[row_spec, _resident((1, d)), _resident((1, d)),
                  _resident((d, d_ff)), _resident((d, d_ff)), _resident((d_ff, d))],
        out_specs=row_spec,
        out_shape=jax.ShapeDtypeStruct((rows, d), F32),
        compiler_params=_params("parallel"),
        name="ffn",
    )(h, g_pre, g_post, wg, wu, wd)


def _proj_kernel(h_ref, g_ref, w_ref, o_ref):
    o_ref[...] = _dot(_rms(h_ref[...], g_ref[...]).astype(BF16), w_ref[...])


def _proj(h, g, w, n_cols=None):
    rows, d = h.shape
    n = w.shape[1] if n_cols is None else n_cols
    tm = _row_tile(rows)
    return pl.pallas_call(
        _proj_kernel,
        grid=(rows // tm,),
        in_specs=[pl.BlockSpec((tm, d), lambda i: (i, 0)), _resident((1, d)), _resident((d, n))],
        out_specs=pl.BlockSpec((tm, n), lambda i: (i, 0)),
        out_shape=jax.ShapeDtypeStruct((rows, n), F32),
        compiler_params=_params("parallel"),
        name="proj",
    )(h, g, w)


def _proj_conv_kernel(h_ref, g_ref, w_ref, hist_ref, convw_ref, o_ref, tail_ref, ybuf_ref, *, conv_width, qk_width):
    j = pl.program_id(1)
    tm = h_ref.shape[0]
    hd = HEAD_DIM
    xn = _rms(h_ref[...], g_ref[...]).astype(BF16)

    @pl.when(j == 0)
    def _():
        ybuf_ref[0:HIST_ROWS, :] = hist_ref[0]

    @pl.when(j > 0)
    def _():
        ybuf_ref[0:HIST_ROWS, :] = tail_ref[0]

    def project_raw(c0, width):
        ybuf_ref[HIST_ROWS:HIST_ROWS + tm, c0:c0 + width] = _dot(xn, w_ref[:, c0:c0 + width])

    def conv_group(g0, width):
        tail_ref[0, :, g0:g0 + width] = ybuf_ref[tm:tm + HIST_ROWS, g0:g0 + width]
        for c0 in range(g0, g0 + width, hd):
            x = ybuf_ref[:, c0:c0 + hd]
            acc = x[HIST_ROWS:] * convw_ref[DN_CONV - 1:DN_CONV, c0:c0 + hd]
            for back in range(1, DN_CONV):
                shifted = pltpu.roll(x, back, 0)[HIST_ROWS:]
                acc = acc + shifted * convw_ref[DN_CONV - 1 - back:DN_CONV - back, c0:c0 + hd]
            y = _silu(acc)
            if c0 < 2 * qk_width:
                y = y * lax.rsqrt(jnp.sum(y * y, axis=-1, keepdims=True) + L2_EPS)
                if c0 < qk_width:
                    y = y * (hd ** -0.5)
            o_ref[:, c0:c0 + hd] = y

    group = 2 * MXU_WIDTH
    starts = list(range(0, conv_width, group))
    n_rest = w_ref.shape[1] - conv_width
    rest_cuts = ([conv_width + (n_rest * i // len(starts)) // MXU_WIDTH * MXU_WIDTH for i in range(len(starts))]
                 + [w_ref.shape[1]])
    project_raw(starts[0], group)
    for gi, g0 in enumerate(starts):
        if gi + 1 < len(starts):
            project_raw(starts[gi + 1], group)
        r0, r1 = rest_cuts[gi], rest_cuts[gi + 1]
        if r1 > r0:
            o_ref[:, r0:r1] = _dot(xn, w_ref[:, r0:r1])
        conv_group(g0, group)


def _proj_conv(h, g, w, n_cols, hist, conv_w, *, n_seq, seq_len, qk_width):
    rows, d = h.shape
    n = n_cols
    conv_width = conv_w.shape[1]
    tm = _row_tile(seq_len)
    tiles = seq_len // tm
    shared_hist = hist.shape[0] == 1
    return pl.pallas_call(
        functools.partial(_proj_conv_kernel, conv_width=conv_width, qk_width=qk_width),
        grid=(n_seq, tiles),
        in_specs=[pl.BlockSpec((tm, d), lambda s, j: (s * tiles + j, 0)), _resident((1, d)), _resident((d, n)),
                  pl.BlockSpec((1, HIST_ROWS, conv_width), lambda s, j: (0 if shared_hist else s, 0, 0)),
                  _resident((DN_CONV, conv_width))],
        out_specs=[pl.BlockSpec((tm, n), lambda s, j: (s * tiles + j, 0)),
                   pl.BlockSpec((1, HIST_ROWS, conv_width), lambda s, j: (s, 0, 0))],
        out_shape=[jax.ShapeDtypeStruct((rows, n), F32),
                   jax.ShapeDtypeStruct((n_seq, HIST_ROWS, conv_width), F32)],
        scratch_shapes=[pltpu.VMEM((HIST_ROWS + tm, conv_width), F32)],
        compiler_params=_params("parallel", "arbitrary"),
        name="proj_conv",
    )(h, g, w, hist, conv_w)


def _interleave(primary, secondary):
    done = 0
    for i, stage in enumerate(primary):
        stage()
        upto = (i + 1) * len(secondary) // len(primary)
        for unit in secondary[done:upto]:
            unit()
        done = upto


def _ffn_proj_conv_kernel(h_ref, gpre_ref, gpost_ref, wg_ref, wu_ref, wd_ref, gmix_ref, w_ref, hist_ref, convw_ref,
                          h1_ref, o_ref, tail_ref, h1prev_ref, ybuf_ref, *, tiles, conv_width, qk_width, ff_pieces):
    s = pl.program_id(0)
    j = jnp.maximum(s - 1, 0) % tiles
    cur, prev = s % 2, (s + 1) % 2
    tm = h_ref.shape[0]
    hd = HEAD_DIM

    @pl.when(s == 0)
    def _():
        h1prev_ref[...] = jnp.zeros(h1prev_ref.shape, F32)

    @pl.when(j == 0)
    def _():
        ybuf_ref[0:HIST_ROWS, :] = hist_ref[0]

    @pl.when(j > 0)
    def _():
        ybuf_ref[0:HIST_ROWS, :] = tail_ref[0]

    st = {}

    def ffn_norm():
        st["xn"] = _rms(h_ref[...], gpre_ref[...]).astype(BF16)
        st["acc"] = None

    def gate(c0, width):
        st["gate"] = _dot(st["xn"], wg_ref[:, c0:c0 + width])

    def up(c0, width):
        st["up"] = _dot(st["xn"], wu_ref[:, c0:c0 + width])

    def down(c0, width):
        part = _dot((_silu(st["gate"]) * st["up"]).astype(BF16), wd_ref[c0:c0 + width, :])
        st["acc"] = part if st["acc"] is None else st["acc"] + part

    def ffn_finish():
        h1 = h_ref[...] + 0.5 * _rms(st["acc"], gpost_ref[...])
        h1_ref[...] = h1
        h1prev_ref[cur] = h1

    ffn_units = [ffn_norm]
    c0 = 0
    for width in ff_pieces:
        ffn_units += [functools.partial(fn, c0, width) for fn in (gate, up, down)]
        c0 += width
    ffn_units.append(ffn_finish)

    def proj_norm():
        st["xp"] = _rms(h1prev_ref[prev], gmix_ref[...]).astype(BF16)

    def project_raw(c0, width):
        ybuf_ref[HIST_ROWS:HIST_ROWS + tm, c0:c0 + width] = _dot(st["xp"], w_ref[:, c0:c0 + width])

    def project_rest(r0, r1):
        o_ref[:, r0:r1] = _dot(st["xp"], w_ref[:, r0:r1])

    def conv_group(g0, width):
        tail_ref[0, :, g0:g0 + width] = ybuf_ref[tm:tm + HIST_ROWS, g0:g0 + width]
        for c0 in range(g0, g0 + width, hd):
            x = ybuf_ref[:, c0:c0 + hd]
            acc = x[HIST_ROWS:] * convw_ref[DN_CONV - 1:DN_CONV, c0:c0 + hd]
            for back in range(1, DN_CONV):
                shifted = pltpu.roll(x, back, 0)[HIST_ROWS:]
                acc = acc + shifted * convw_ref[DN_CONV - 1 - back:DN_CONV - back, c0:c0 + hd]
            y = _silu(acc)
            if c0 < 2 * qk_width:
                y = y * lax.rsqrt(jnp.sum(y * y, axis=-1, keepdims=True) + L2_EPS)
                if c0 < qk_width:
                    y = y * (hd ** -0.5)
            o_ref[:, c0:c0 + hd] = y

    group = 2 * MXU_WIDTH
    starts = list(range(0, conv_width, group))
    n_rest = w_ref.shape[1] - conv_width
    rest_cuts = ([conv_width + (n_rest * i // len(starts)) // MXU_WIDTH * MXU_WIDTH for i in range(len(starts))]
                 + [w_ref.shape[1]])
    proj_units = [proj_norm, functools.partial(project_raw, starts[0], group)]
    for gi, g0 in enumerate(starts):
        if gi + 1 < len(starts):
            proj_units.append(functools.partial(project_raw, starts[gi + 1], group))
        if rest_cuts[gi + 1] > rest_cuts[gi]:
            proj_units.append(functools.partial(project_rest, rest_cuts[gi], rest_cuts[gi + 1]))
        proj_units.append(functools.partial(conv_group, g0, group))

    _interleave(proj_units, ffn_units)


def _ffn_proj_conv(h, g_pre, g_post, wg, wu, wd, g_mix, w, n_cols, hist, conv_w, *, n_seq, seq_len, qk_width):
    rows, d = h.shape
    d_ff = wg.shape[1]
    conv_width = conv_w.shape[1]
    tm = FUSED_ROW_TILE
    assert seq_len % tm == 0
    tiles = seq_len // tm
    n_tiles = n_seq * tiles
    shared_hist = hist.shape[0] == 1

    def cur(s):
        return jnp.minimum(s, n_tiles - 1)

    def prev(s):
        return jnp.maximum(s - 1, 0)

    return pl.pallas_call(
        functools.partial(_ffn_proj_conv_kernel, tiles=tiles, conv_width=conv_width, qk_width=qk_width,
                          ff_pieces=_ff_pieces(d_ff, 4)),
        grid=(n_tiles + 1,),
        in_specs=[pl.BlockSpec((tm, d), lambda s: (cur(s), 0)), _resident((1, d)), _resident((1, d)),
                  _resident((d, d_ff)), _resident((d, d_ff)), _resident((d_ff, d)),
                  _resident((1, d)), _resident((d, n_cols)),
                  pl.BlockSpec((1, HIST_ROWS, conv_width), lambda s: (0 if shared_hist else prev(s) // tiles, 0, 0)),
                  _resident((DN_CONV, conv_width))],
        out_specs=[pl.BlockSpec((tm, d), lambda s: (cur(s), 0)),
                   pl.BlockSpec((tm, n_cols), lambda s: (prev(s), 0)),
                   pl.BlockSpec((1, HIST_ROWS, conv_width), lambda s: (prev(s) // tiles, 0, 0))],
        out_shape=[jax.ShapeDtypeStruct((rows, d), F32),
                   jax.ShapeDtypeStruct((rows, n_cols), F32),
                   jax.ShapeDtypeStruct((n_seq, HIST_ROWS, conv_width), F32)],
        scratch_shapes=[pltpu.VMEM((2, tm, d), F32), pltpu.VMEM((HIST_ROWS + tm, conv_width), F32)],
        compiler_params=_params("arbitrary"),
        name="ffn_proj_conv",
    )(h, g_pre, g_post, wg, wu, wd, g_mix, w, hist, conv_w)


def _mixffn_kernel(h_ref, odn_ref, osw_ref, gdn_ref, gsw_ref, wout_ref, gmix_ref,
                   gpre_ref, gpost_ref, wg_ref, wu_ref, wd_ref, o_ref, *, ff_chunk):
    halves = _row_halves(h_ref.shape[0])
    ys = [(jax.nn.sigmoid(gdn_ref[r, :]) * odn_ref[r, :] + jax.nn.sigmoid(gsw_ref[r, :]) * osw_ref[r, :]).astype(BF16)
          for r in halves]
    mixed = [_dot(y, wout_ref[...]) for y in ys]
    h2s = [h_ref[r, :] + _rms(m, gmix_ref[...]) for r, m in zip(halves, mixed)]
    outs = _ffn_tiles(h2s, gpre_ref[...], gpost_ref[...], wg_ref, wu_ref, wd_ref, ff_chunk)
    for r, out in zip(halves, outs):
        o_ref[r, :] = out


def _mixffn(h, o_dn, o_sw, proj_b, cols, w_out, g_mix, g_pre, g_post, wg, wu, wd, rows=None):
    d = h.shape[1]
    rows = h.shape[0] if rows is None else rows
    d_ff = wg.shape[1]
    tm = _row_tile(rows)
    row_spec = pl.BlockSpec((tm, d), lambda i: (i, 0))
    gdn_blk, gsw_blk = cols["g_dn"] // d, cols["g_swa"] // d
    return pl.pallas_call(
        functools.partial(_mixffn_kernel, ff_chunk=_ff_chunk(d_ff)),
        grid=(rows // tm,),
        in_specs=[row_spec, row_spec, row_spec,
                  pl.BlockSpec((tm, d), lambda i: (i, gdn_blk)),
                  pl.BlockSpec((tm, d), lambda i: (i, gsw_blk)),
                  _resident((d, d)), _resident((1, d)), _resident((1, d)), _resident((1, d)),
                  _resident((d, d_ff)), _resident((d, d_ff)), _resident((d_ff, d))],
        out_specs=row_spec,
        out_shape=jax.ShapeDtypeStruct((rows, d), F32),
        compiler_params=_params("parallel"),
        name="mixffn",
    )(h, o_dn, o_sw, proj_b, proj_b, w_out, g_mix, g_pre, g_post, wg, wu, wd)


GROUP = 4
PASSES_QK = 1
PASSES_STATE = 1
PASSES_INV = 3
INV_BLOCK = 16
STEP_PASSES = 1


def _gdn_block_stages(qkv_ref, z_ref, ba_ref, alog_ref, dtb_ref, nw_ref, s_ref, store_out, *, chunk, heads):
    tb = qkv_ref.shape[0]
    hd = HEAD_DIM
    c = chunk
    gw = GROUP * c
    qk_w = heads * hd
    stages = []

    def iota(shape, dim):
        return lax.broadcasted_iota(jnp.int32, shape, dim)

    row = iota((c, gw), 0)
    col = iota((c, gw), 1) & (c - 1)
    causal4, strict4 = row >= col, row > col
    diag_blocks4 = (row // INV_BLOCK) == (col // INV_BLOCK)
    eye4 = (row == col).astype(F32)
    bd_mask = (iota((gw, gw), 0) // c) == (iota((gw, gw), 1) // c)
    lower_ones = (iota((c, c), 0) >= iota((c, c), 1)).astype(F32)
    upper_dup = (iota((c, 2 * c), 0) <= (iota((c, 2 * c), 1) & (c - 1))).astype(F32)
    first_half = iota((c, 2 * c), 1) < c
    zeros_head = jnp.zeros((c, hd), F32)

    def block_diag(x4):
        return jnp.where(bd_mask, jnp.concatenate([x4] * GROUP, axis=0), jnp.zeros((), x4.dtype))

    splits = {}

    def split(x):
        if id(x) not in splits:
            splits[id(x)] = (x, _split_bf16(x))
        return splits[id(x)][1]

    def times_block_diag(lhs, x4, passes):
        parts = lhs if isinstance(lhs, list) else [lhs]
        if passes == 1:
            return _dot(jnp.concatenate(parts, axis=0).astype(BF16), block_diag(x4.astype(BF16)))
        hi, lo = split(x4)
        lhs_split = tuple(jnp.concatenate(pieces, axis=0) for pieces in zip(*[split(p) for p in parts]))
        return _dot3_split(lhs_split, (block_diag(hi), block_diag(lo)))

    def block_diag_times(x4, rhs, passes):
        if passes == 1:
            return _dot(block_diag(x4.astype(BF16)), rhs.astype(BF16))
        hi, lo = _split_bf16(x4)
        return _dot3_split((block_diag(hi), block_diag(lo)), _split_bf16(rhs))

    def pack_cols(col_all, hs):
        halves = [jnp.where(first_half,
                            jnp.broadcast_to(col_all[:, hs[2 * p]:hs[2 * p] + 1], (c, 2 * c)),
                            jnp.broadcast_to(col_all[:, hs[2 * p + 1]:hs[2 * p + 1] + 1], (c, 2 * c)))
                  for p in range(GROUP // 2)]
        return jnp.concatenate(halves, axis=1)

    chunks = [dict(r0=r0) for r0 in range(0, tb, c)]
    items = [dict(ch=ch, hs=list(range(g0, g0 + GROUP))) for ch in chunks for g0 in range(0, heads, GROUP)]

    def gating():
        for ch in chunks:
            ba = ba_ref[ch["r0"]:ch["r0"] + c, :]
            a_in = ba[:, heads:2 * heads] + dtb_ref[...]
            softplus = jnp.maximum(a_in, 0.0) + jnp.log(1.0 + jnp.exp(-jnp.abs(a_in)))
            g_all = -jnp.exp(alog_ref[...]) * softplus
            ch["beta"] = jax.nn.sigmoid(ba[:, 0:heads])
            ch["gc"] = _dot_hi(lower_ones, g_all)
            ch["gc_row"] = _dot_tn_hi(g_all, upper_dup)
        for it in items:
            rows = slice(it["ch"]["r0"], it["ch"]["r0"] + c)
            it["qs"] = [qkv_ref[rows, h * hd:(h + 1) * hd] for h in it["hs"]]
            it["ks"] = [qkv_ref[rows, qk_w + h * hd:qk_w + (h + 1) * hd] for h in it["hs"]]
            it["vs"] = [qkv_ref[rows, 2 * qk_w + h * hd:2 * qk_w + (h + 1) * hd] for h in it["hs"]]

    def gram(it):
        ch, hs, qs, ks = it["ch"], it["hs"], it["qs"], it["ks"]
        beta4 = pack_cols(ch["beta"], hs)
        gc_row4 = jnp.concatenate(
            [jnp.where(first_half[0:1], ch["gc_row"][hs[2 * p]:hs[2 * p] + 1],
                       ch["gc_row"][hs[2 * p + 1]:hs[2 * p + 1] + 1])
             for p in range(GROUP // 2)], axis=1)
        diff4 = pack_cols(ch["gc"], hs) - gc_row4
        decay4 = jnp.where(causal4, jnp.exp(jnp.where(causal4, diff4, 0.0)), 0.0)
        lhs = jnp.concatenate([jnp.concatenate(qs, axis=1), jnp.concatenate(ks, axis=1)], axis=0)
        k_bd = jnp.concatenate(
            [jnp.concatenate([ks[i] if ii == i else zeros_head for ii in range(GROUP)], axis=1)
             for i in range(GROUP)], axis=0)
        qkkk = _mm(lhs, k_bd, "nt", PASSES_QK)
        it["qk4"] = qkkk[:c] * decay4
        a4 = jnp.where(strict4, beta4 * qkkk[c:] * decay4, 0.0)
        a_diag = jnp.where(diag_blocks4, a4, 0.0)
        it["a_diag"], it["a_off"] = a_diag, a4 - a_diag
        it["t4"] = eye4 - a_diag

    assert c // INV_BLOCK == 4
    n_sq = int(math.log2(INV_BLOCK)) - 1

    def inverse_start(it):
        it["p4"] = times_block_diag(it["a_diag"], it["a_diag"], PASSES_INV)

    def inverse_step(step, it):
        if step < n_sq - 1:
            both = times_block_diag([it["p4"], it["t4"]], it["p4"], PASSES_INV)
            it["p4"], it["t4"] = both[:c], it["t4"] + both[c:]
        else:
            it["t4"] = it["t4"] + times_block_diag(it["t4"], it["p4"], PASSES_INV)

    def off_diagonal(it):
        it["m4"] = times_block_diag(it["t4"], it["a_off"], PASSES_INV)

    def off_diagonal_square(it):
        it["m2"] = times_block_diag(it["m4"], it["m4"], PASSES_INV)

    def series(it):
        i_minus_m = eye4 - it["m4"]
        it["s4"] = i_minus_m + times_block_diag(i_minus_m, it["m2"], PASSES_INV)

    def inverse_finish(it):
        it["t4"] = times_block_diag(it["s4"], it["t4"], PASSES_INV)

    def solve(it):
        ch, hs, ks, vs = it["ch"], it["hs"], it["ks"], it["vs"]
        rhs = jnp.concatenate(
            [jnp.concatenate([vs[i] * ch["beta"][:, h:h + 1],
                              ks[i] * (ch["beta"][:, h:h + 1] * jnp.exp(ch["gc"][:, h:h + 1]))], axis=1)
             for i, h in enumerate(hs)], axis=0)
        it["sol"] = block_diag_times(it["t4"], rhs, PASSES_INV)

    def for_items(fn, subset=None):
        return lambda: [fn(it) for it in (items if subset is None else subset)]

    stages.append(gating)
    stages.append(for_items(gram))
    stages.append(for_items(inverse_start))
    for step in range(n_sq):
        stages.append(for_items(functools.partial(inverse_step, step)))
    stages += [for_items(off_diagonal), for_items(off_diagonal_square), for_items(series),
               for_items(inverse_finish), for_items(solve)]

    def read_state(it):
        ch = it["ch"]
        us, o_inter = [], []
        for i, h in enumerate(it["hs"]):
            sl = slice(i * c, (i + 1) * c)
            ws = _mm(jnp.concatenate([it["sol"][sl, hd:], it["qs"][i] * jnp.exp(ch["gc"][:, h:h + 1])], axis=0),
                     s_ref[h], "nn", PASSES_STATE)
            us.append(it["sol"][sl, :hd] - ws[:c])
            o_inter.append(ws[c:])
        it["us"], it["o_inter"] = us, o_inter

    def intra(it):
        it["o_intra"] = block_diag_times(it["qk4"], jnp.concatenate(it["us"], axis=0), PASSES_STATE)

    def update_state(it):
        ch = it["ch"]
        for i, h in enumerate(it["hs"]):
            gc = ch["gc"][:, h:h + 1]
            gc_last = ch["gc_row"][h:h + 1, c - 1:c]
            s_ref[h] = s_ref[h] * jnp.exp(gc_last) + _mm(it["ks"][i] * jnp.exp(gc_last - gc), it["us"][i],
                                                         "tn", PASSES_STATE)

    def emit(it):
        r0 = it["ch"]["r0"]
        for i, h in enumerate(it["hs"]):
            o = it["o_inter"][i] + it["o_intra"][i * c:(i + 1) * c]
            zh = z_ref[r0:r0 + c, h * hd:(h + 1) * hd]
            store_out(slice(r0, r0 + c), slice(h * hd, (h + 1) * hd), _rms(o, nw_ref[...]) * _silu(zh))

    for ch in chunks:
        group_items = [it for it in items if it["ch"] is ch]
        stages += [for_items(fn, group_items) for fn in (read_state, intra, update_state, emit)]
    return stages


def _gdn_packed_kernel(qkv_ref, z_ref, ba_ref, s0_ref, alog_ref, dtb_ref, nw_ref,
                       o_ref, sout_ref, s_ref, *, chunk, heads):
    j = pl.program_id(1)

    @pl.when(j == 0)
    def _():
        s_ref[...] = s0_ref[0]

    def store_out(rows, lanes, value):
        o_ref[rows, lanes] = value

    for stage in _gdn_block_stages(qkv_ref, z_ref, ba_ref, alog_ref, dtb_ref, nw_ref, s_ref, store_out,
                                   chunk=chunk, heads=heads):
        stage()

    @pl.when(j == pl.num_programs(1) - 1)
    def _():
        sout_ref[0] = s_ref[...]


def _gdn_packed(proj_a, cols, s0, a_log, dt_bias, norm_w, *, n_seq, seq_len, block):
    heads = a_log.shape[1]
    v_w = heads * HEAD_DIM
    conv_width = 3 * v_w
    nblk = seq_len // block
    z_blk, ba_blk = cols["z"] // v_w, cols["ba"] // LANES
    shared_s0 = s0.shape[0] == 1
    assert 2 * DN_CHUNK == HEAD_DIM and heads % GROUP == 0 and block % DN_CHUNK == 0
    return pl.pallas_call(
        functools.partial(_gdn_packed_kernel, chunk=DN_CHUNK, heads=heads),
        grid=(n_seq, nblk),
        in_specs=[
            pl.BlockSpec((block, conv_width), lambda n, j: (n * nblk + j, 0)),
            pl.BlockSpec((block, v_w), lambda n, j: (n * nblk + j, z_blk)),
            pl.BlockSpec((block, LANES), lambda n, j: (n * nblk + j, ba_blk)),
            pl.BlockSpec((1, heads, HEAD_DIM, HEAD_DIM), lambda n, j: (0 if shared_s0 else n, 0, 0, 0)),
            pl.BlockSpec((1, heads), lambda n, j: (0, 0)),
            pl.BlockSpec((1, heads), lambda n, j: (0, 0)),
            pl.BlockSpec((1, HEAD_DIM), lambda n, j: (0, 0)),
        ],
        out_specs=[
            pl.BlockSpec((block, v_w), lambda n, j: (n * nblk + j, 0)),
            pl.BlockSpec((1, heads, HEAD_DIM, HEAD_DIM), lambda n, j: (n, 0, 0, 0)),
        ],
        out_shape=[
            jax.ShapeDtypeStruct((n_seq * seq_len, v_w), F32),
            jax.ShapeDtypeStruct((n_seq, heads, HEAD_DIM, HEAD_DIM), F32),
        ],
        scratch_shapes=[pltpu.VMEM((heads, HEAD_DIM, HEAD_DIM), F32)],
        compiler_params=_params("parallel", "arbitrary"),
        name="gdn_packed",
    )(proj_a, proj_a, proj_a, s0, a_log, dt_bias, norm_w)


def _ff_pieces(d_ff, n):
    tiles, rem = divmod(d_ff, MXU_WIDTH)
    if rem:
        return (d_ff,)
    per = -(-tiles // n)
    return tuple(min(per, tiles - t0) * MXU_WIDTH for t0 in range(0, tiles, per))


def _gdn_mix_kernel(qkv_ref, z_ref, ba_ref, s0_ref, alog_ref, dtb_ref, nw_ref,
                    h_ref, osw_ref, gdn_ref, gsw_ref, wout_ref, gmix_ref, gpre_ref, gpost_ref, wg_ref, wu_ref, wd_ref,
                    y_ref, sout_ref, s_ref, odn_ref, *, chunk, heads, nblk, ff_pieces):
    s = pl.program_id(0)
    n_blocks = pl.num_programs(0) - 1
    j = s % nblk
    cur, prev = s % 2, (s + 1) % 2

    @pl.when(s == 0)
    def _():
        odn_ref[...] = jnp.zeros(odn_ref.shape, F32)

    @pl.when((j == 0) & (s < n_blocks))
    def _():
        s_ref[...] = s0_ref[0]

    def store_out(rows, lanes, value):
        odn_ref[cur, rows, lanes] = value

    gdn_stages = _gdn_block_stages(qkv_ref, z_ref, ba_ref, alog_ref, dtb_ref, nw_ref, s_ref, store_out,
                                   chunk=chunk, heads=heads)

    st = {}

    def merge():
        y = jax.nn.sigmoid(gdn_ref[...]) * odn_ref[prev] + jax.nn.sigmoid(gsw_ref[...]) * osw_ref[...]
        st["y"] = y.astype(BF16)

    def out_proj():
        st["mixed"] = _dot(st["y"], wout_ref[...])

    def norms():
        st["h2"] = h_ref[...] + _rms(st["mixed"], gmix_ref[...])
        st["xn"] = _rms(st["h2"], gpre_ref[...]).astype(BF16)
        st["acc"] = None

    def gate(c0, width):
        st["gate"] = _dot(st["xn"], wg_ref[:, c0:c0 + width])

    def up(c0, width):
        st["up"] = _dot(st["xn"], wu_ref[:, c0:c0 + width])

    def down(c0, width):
        part = _dot((_silu(st["gate"]) * st["up"]).astype(BF16), wd_ref[c0:c0 + width, :])
        st["acc"] = part if st["acc"] is None else st["acc"] + part

    def finish():
        y_ref[...] = st["h2"] + 0.5 * _rms(st["acc"], gpost_ref[...])

    mix_units = [merge, out_proj, norms]
    c0 = 0
    for width in ff_pieces:
        mix_units += [functools.partial(fn, c0, width) for fn in (gate, up, down)]
        c0 += width
    mix_units.append(finish)

    _interleave(gdn_stages, mix_units)

    @pl.when((j == nblk - 1) & (s < n_blocks))
    def _():
        sout_ref[0] = s_ref[...]


def _gdn_mix(proj_a, proj_b, h, o_sw, cols, s0, a_log, dt_bias, norm_w, w_out, g_mix, g_pre, g_post, wg, wu, wd,
             *, n_seq, seq_len, block):
    heads = a_log.shape[1]
    v_w = heads * HEAD_DIM
    conv_width = 3 * v_w
    d = h.shape[1]
    d_ff = wg.shape[1]
    nblk = seq_len // block
    n_blocks = n_seq * nblk
    z_blk, ba_blk = cols["z"] // v_w, cols["ba"] // LANES
    gdn_blk, gsw_blk = cols["g_dn"] // d, cols["g_swa"] // d
    shared_s0 = s0.shape[0] == 1
    assert 2 * DN_CHUNK == HEAD_DIM and heads % GROUP == 0 and block % DN_CHUNK == 0 and v_w == d

    def cur(s):
        return jnp.minimum(s, n_blocks - 1)

    def prev(s):
        return jnp.maximum(s - 1, 0)

    return pl.pallas_call(
        functools.partial(_gdn_mix_kernel, chunk=DN_CHUNK, heads=heads, nblk=nblk, ff_pieces=_ff_pieces(d_ff, 4)),
        grid=(n_blocks + 1,),
        in_specs=[
            pl.BlockSpec((block, conv_width), lambda s: (cur(s), 0)),
            pl.BlockSpec((block, v_w), lambda s: (cur(s), z_blk)),
            pl.BlockSpec((block, LANES), lambda s: (cur(s), ba_blk)),
            pl.BlockSpec((1, heads, HEAD_DIM, HEAD_DIM), lambda s: (0 if shared_s0 else cur(s) // nblk, 0, 0, 0)),
            _resident((1, heads)), _resident((1, heads)), _resident((1, HEAD_DIM)),
            pl.BlockSpec((block, d), lambda s: (prev(s), 0)),
            pl.BlockSpec((block, d), lambda s: (prev(s), 0)),
            pl.BlockSpec((block, d), lambda s: (prev(s), gdn_blk)),
            pl.BlockSpec((block, d), lambda s: (prev(s), gsw_blk)),
            _resident((d, d)), _resident((1, d)), _resident((1, d)), _resident((1, d)),
            _resident((d, d_ff)), _resident((d, d_ff)), _resident((d_ff, d)),
        ],
        out_specs=[
            pl.BlockSpec((block, d), lambda s: (prev(s), 0)),
            pl.BlockSpec((1, heads, HEAD_DIM, HEAD_DIM), lambda s: (cur(s) // nblk, 0, 0, 0)),
        ],
        out_shape=[
            jax.ShapeDtypeStruct((n_seq * seq_len, d), F32),
            jax.ShapeDtypeStruct((n_seq, heads, HEAD_DIM, HEAD_DIM), F32),
        ],
        scratch_shapes=[pltpu.VMEM((heads, HEAD_DIM, HEAD_DIM), F32), pltpu.VMEM((2, block, v_w), F32)],
        compiler_params=_params("arbitrary"),
        name="gdn_mix",
    )(proj_a, proj_a, proj_a, s0, a_log, dt_bias, norm_w,
      h, o_sw, proj_b, proj_b, w_out, g_mix, g_pre, g_post, wg, wu, wd)


def _gdn_step_kernel(qkv_ref, z_ref, ba_ref, hist_ref, s0_ref, convw_ref, alog_ref, dtb_ref, nw_ref,
                     o_ref, sout_ref, *, heads):
    nb, t, _ = qkv_ref.shape
    hd = HEAD_DIM
    qk_w = heads * hd
    ti = lax.broadcasted_iota(jnp.int32, (t, t), 0)
    tj = lax.broadcasted_iota(jnp.int32, (t, t), 1)
    lower_ones = (ti >= tj).astype(F32)
    upper_ones = (ti <= tj).astype(F32)

    def per_sequence(b, carry):
        raw = jnp.concatenate([hist_ref[b], qkv_ref[b]], axis=0)
        acc = raw[HIST_ROWS:] * convw_ref[DN_CONV - 1:DN_CONV, :]
        for back in range(1, DN_CONV):
            acc = acc + pltpu.roll(raw, back, 0)[HIST_ROWS:] * convw_ref[DN_CONV - 1 - back:DN_CONV - back, :]
        x = _silu(acc)
        ba = ba_ref[b]
        beta_all = jax.nn.sigmoid(ba[:, 0:heads])
        a_in = ba[:, heads:2 * heads] + dtb_ref[...]
        softplus = jnp.maximum(a_in, 0.0) + jnp.log(1.0 + jnp.exp(-jnp.abs(a_in)))
        g_all = -jnp.exp(alog_ref[...]) * softplus
        gc_all = _dot_hi(lower_ones, g_all)
        gc_row_all = _dot_tn_hi(g_all, upper_ones)
        z = z_ref[b]
        hs = range(heads)
        qs, ks, vs = [], [], []
        for h in hs:
            q = x[:, h * hd:(h + 1) * hd]
            k = x[:, qk_w + h * hd:qk_w + (h + 1) * hd]
            qs.append(q * lax.rsqrt(jnp.sum(q * q, axis=-1, keepdims=True) + L2_EPS) * (hd ** -0.5))
            ks.append(k * lax.rsqrt(jnp.sum(k * k, axis=-1, keepdims=True) + L2_EPS))
            vs.append(x[:, 2 * qk_w + h * hd:2 * qk_w + (h + 1) * hd])
        gcs = [gc_all[:, h:h + 1] for h in hs]
        gc_lasts = [gc_row_all[h:h + 1, t - 1:t] for h in hs]
        kq = [jnp.concatenate([ks[h], qs[h]], axis=0) for h in hs]
        gram = [_mm(kq[h], ks[h], "nt", STEP_PASSES) for h in hs]
        from_state = [_mm(kq[h] * jnp.exp(jnp.concatenate([gcs[h], gcs[h]], axis=0)), s0_ref[b, h], "nn",
                          STEP_PASSES) for h in hs]
        us, outs = [], []
        for h in hs:
            decay = jnp.where(ti >= tj, jnp.exp(jnp.where(ti >= tj, gcs[h] - gc_row_all[h:h + 1, :], 0.0)), 0.0)
            beta = beta_all[:, h:h + 1]
            lower = jnp.where(ti > tj, beta * gram[h][:t] * decay, 0.0)
            u = beta * (vs[h] - from_state[h][:t])
            for j in range(t - 1):
                u = u - lower[:, j:j + 1] * u[j:j + 1, :]
            qk = gram[h][t:] * decay
            o = from_state[h][t:]
            for j in range(t):
                o = o + qk[:, j:j + 1] * u[j:j + 1, :]
            us.append(u)
            outs.append(o)
        for h in hs:
            sout_ref[b, h] = s0_ref[b, h] * jnp.exp(gc_lasts[h]) + _mm(ks[h] * jnp.exp(gc_lasts[h] - gcs[h]), us[h],
                                                                       "tn", STEP_PASSES)
        for h in hs:
            o_ref[b, :, h * hd:(h + 1) * hd] = _rms(outs[h], nw_ref[...]) * _silu(z[:, h * hd:(h + 1) * hd])
        return carry

    lax.fori_loop(0, nb, per_sequence, 0, unroll=2 if nb % 2 == 0 else 1)


def _gdn_step(proj_a3, cols, hist, s0, conv_w, a_log, dt_bias, norm_w, *, seqs_per_step):
    n, t = s0.shape[0], proj_a3.shape[1]
    heads = a_log.shape[1]
    conv_width = conv_w.shape[1]
    v_w = heads * HEAD_DIM
    nb = seqs_per_step
    z_blk, ba_blk = cols["z"] // v_w, cols["ba"] // LANES
    state_spec = pl.BlockSpec((nb, heads, HEAD_DIM, HEAD_DIM), lambda i: (i, 0, 0, 0))
    return pl.pallas_call(
        functools.partial(_gdn_step_kernel, heads=heads),
        grid=(n // nb,),
        in_specs=[
            pl.BlockSpec((nb, t, conv_width), lambda i: (i, 0, 0)),
            pl.BlockSpec((nb, t, v_w), lambda i: (i, 0, z_blk)),
            pl.BlockSpec((nb, t, LANES), lambda i: (i, 0, ba_blk)),
            pl.BlockSpec((nb, HIST_ROWS, conv_width), lambda i: (i, 0, 0)),
            state_spec,
            pl.BlockSpec((DN_CONV, conv_width), lambda i: (0, 0)),
            pl.BlockSpec((1, heads), lambda i: (0, 0)),
            pl.BlockSpec((1, heads), lambda i: (0, 0)),
            pl.BlockSpec((1, HEAD_DIM), lambda i: (0, 0)),
        ],
        out_specs=[pl.BlockSpec((nb, t, v_w), lambda i: (i, 0, 0)), state_spec],
        out_shape=[jax.ShapeDtypeStruct((n, t, v_w), F32),
                   jax.ShapeDtypeStruct((n, heads, HEAD_DIM, HEAD_DIM), F32)],
        compiler_params=_params("parallel"),
        name="gdn_step",
    )(proj_a3, proj_a3, proj_a3, hist, s0, conv_w, a_log, dt_bias, norm_w)


def _alibi_slope(head, n_heads):
    return 2.0 ** (-8.0 * (head + 1) / n_heads)


def _penalty(dist, mask):
    return jnp.where(mask, jnp.minimum(dist, WINDOW).astype(F32), -NEG_BIG)


def _attend(jobs, n_rows):
    scale = HEAD_DIM ** -0.5
    hd = HEAD_DIM
    scores = [[_dot_nt(q4, k) * scale for k, _, _ in segs] for q4, segs, _, _ in jobs]
    v_ones = [[jnp.concatenate([v, jnp.ones_like(v)], axis=1) for _, v, _ in segs] for _, segs, _, _ in jobs]
    heads = [(ji, g) for ji in range(len(jobs)) for g in range(len(jobs[ji][3]))]
    logits, maxes = {}, {}
    for ji, g in heads:
        _, segs, _, slopes = jobs[ji]
        r = slice(g * n_rows, (g + 1) * n_rows)
        logits[ji, g] = [sc[r] - slopes[g] * pen for sc, (_, _, pen) in zip(scores[ji], segs)]
    for ji, g in heads:
        by_width = {}
        for lg in logits[ji, g]:
            w = lg.shape[1]
            by_width[w] = lg if w not in by_width else jnp.maximum(by_width[w], lg)
        m = jobs[ji][2][g]
        for lg in by_width.values():
            m = jnp.maximum(m, jnp.max(lg, axis=-1, keepdims=True))
        maxes[ji, g] = m
    accs = {}
    for ji, g in heads:
        acc = None
        for lg, v1 in zip(logits[ji, g], v_ones[ji]):
            pv = _dot(jnp.exp(lg - maxes[ji, g]).astype(BF16), v1)
            acc = pv if acc is None else acc + pv
        accs[ji, g] = acc
    outs = [[None] * len(job[3]) for job in jobs]
    for ji, g in heads:
        acc = accs[ji, g]
        outs[ji][g] = acc[:, :hd] / (acc[:, hd:] + jnp.exp(jobs[ji][2][g] - maxes[ji, g]))
    return outs


def _swa_prompt_kernel(q_ref, kv_ref, kvprev_ref, kvmeta_ref, sinks_ref, o_ref, *, n_meta, kv_heads):
    j = pl.program_id(1)
    tq = WINDOW
    n_sub = q_ref.shape[0] // tq
    hd = HEAD_DIM
    kv_w = kv_heads * hd
    n_heads = kv_heads * SWA_GROUP
    qi = lax.broadcasted_iota(jnp.int32, (tq, tq), 0)
    ki = lax.broadcasted_iota(jnp.int32, (tq, tq), 1)
    dist_own = qi - ki
    dist_prev = dist_own + tq
    pen_own = _penalty(dist_own, dist_own >= 0)
    pen_prev = _penalty(dist_prev, dist_prev <= WINDOW)
    pen_prev_first = _penalty(dist_prev, (dist_prev <= WINDOW) & (j > 0))
    jobs, job_dst = [], []
    for sub in range(n_sub):
        rows = slice(sub * tq, (sub + 1) * tq)
        prev_ref, prev_rows = (kvprev_ref, slice(0, tq)) if sub == 0 else (kv_ref, slice((sub - 1) * tq, sub * tq))
        qpos = n_meta + (j * n_sub + sub) * tq + lax.broadcasted_iota(jnp.int32, (tq, n_meta), 0)
        dist_meta = qpos - lax.broadcasted_iota(jnp.int32, (tq, n_meta), 1)
        pen_meta = _penalty(dist_meta, dist_meta >= 0)
        for kvh in range(kv_heads):
            ks = slice(kvh * hd, (kvh + 1) * hd)
            vs = slice(kv_w + kvh * hd, kv_w + (kvh + 1) * hd)
            segments = [
                (kvmeta_ref[:, ks].astype(BF16), kvmeta_ref[:, vs].astype(BF16), pen_meta),
                (prev_ref[prev_rows, ks].astype(BF16), prev_ref[prev_rows, vs].astype(BF16),
                 pen_prev_first if sub == 0 else pen_prev),
                (kv_ref[rows, ks].astype(BF16), kv_ref[rows, vs].astype(BF16), pen_own),
            ]
            heads = [kvh * SWA_GROUP + g for g in range(SWA_GROUP)]
            q4 = jnp.concatenate([q_ref[rows, hh * hd:(hh + 1) * hd] for hh in heads], axis=0).astype(BF16)
            jobs.append((q4, segments, [sinks_ref[:, hh:hh + 1] for hh in heads],
                         [_alibi_slope(hh, n_heads) for hh in heads]))
            job_dst.append((rows, heads))
    for (rows, heads), outs in zip(job_dst, _attend(jobs, tq)):
        for hh, o in zip(heads, outs):
            o_ref[rows, hh * hd:(hh + 1) * hd] = o


def _swa_prompt(proj_b, proj_b_meta, cols, sinks, *, n_seq, seq_len, n_meta):
    n_heads = sinks.shape[1]
    kv_heads = n_heads // SWA_GROUP
    q_w, kv_w2 = n_heads * HEAD_DIM, 2 * kv_heads * HEAD_DIM
    n_sub = SWA_SUBBLOCKS if seq_len % (SWA_SUBBLOCKS * WINDOW) == 0 else 1
    tq = n_sub * WINDOW
    nblk = seq_len // tq
    kv_blk = cols["skv"] // kv_w2
    return pl.pallas_call(
        functools.partial(_swa_prompt_kernel, n_meta=n_meta, kv_heads=kv_heads),
        grid=(n_seq, nblk),
        in_specs=[
            pl.BlockSpec((tq, q_w), lambda n, j: (n * nblk + j, 0)),
            pl.BlockSpec((tq, kv_w2), lambda n, j: (n * nblk + j, kv_blk)),
            pl.BlockSpec((WINDOW, kv_w2), lambda n, j: (jnp.maximum((n * nblk + j) * n_sub - 1, 0), kv_blk)),
            pl.BlockSpec((n_meta, kv_w2), lambda n, j: (0, kv_blk)),
            pl.BlockSpec((1, n_heads), lambda n, j: (0, 0)),
        ],
        out_specs=pl.BlockSpec((tq, q_w), lambda n, j: (n * nblk + j, 0)),
        out_shape=jax.ShapeDtypeStruct((n_seq * seq_len, q_w), F32),
        compiler_params=_params("parallel", "arbitrary"),
        name="swa_prompt",
    )(proj_b, proj_b, proj_b, proj_b_meta, sinks)


def _swa_step_kernel(q_ref, kvnew_ref, kmeta_ref, vmeta_ref, kbuf_ref, vbuf_ref, sinks_ref,
                     o_ref, kout_ref, vout_ref, *, kv_heads):
    nb, t, _ = q_ref.shape
    n_meta, w = kmeta_ref.shape[1], kbuf_ref.shape[1]
    hd = HEAD_DIM
    kv_w = kv_heads * hd
    n_heads = kv_heads * SWA_GROUP

    def grid2(rows, cols_):
        return (lax.broadcasted_iota(jnp.int32, (rows, cols_), 0), lax.broadcasted_iota(jnp.int32, (rows, cols_), 1))

    ti, mi = grid2(t, n_meta)
    dist_meta = PAST_LEN + ti - mi
    pen_meta = _penalty(dist_meta, dist_meta >= 0)
    ti, bi = grid2(t, w)
    dist_buf = w + ti - bi
    pen_buf = _penalty(dist_buf, (dist_buf <= WINDOW) & (PAST_LEN - w + bi >= n_meta))
    ti, si = grid2(t, t)
    dist_new = ti - si
    pen_new = _penalty(dist_new, dist_new >= 0)
    jobs, job_dst = [], []
    for b in range(nb):
        kv_new = kvnew_ref[b]
        for kvh in range(kv_heads):
            ks = slice(kvh * hd, (kvh + 1) * hd)
            k_old, v_old = kbuf_ref[b, :, kvh, :], vbuf_ref[b, :, kvh, :]
            k_new, v_new = kv_new[:, ks], kv_new[:, kv_w + kvh * hd:kv_w + (kvh + 1) * hd]
            kout_ref[b, 0:w - t, kvh, :] = k_old[t:w]
            kout_ref[b, w - t:w, kvh, :] = k_new
            vout_ref[b, 0:w - t, kvh, :] = v_old[t:w]
            vout_ref[b, w - t:w, kvh, :] = v_new
            segments = [
                (kmeta_ref[b, :, ks].astype(BF16), vmeta_ref[b, :, ks].astype(BF16), pen_meta),
                (k_old.astype(BF16), v_old.astype(BF16), pen_buf),
                (k_new.astype(BF16), v_new.astype(BF16), pen_new),
            ]
            heads = [kvh * SWA_GROUP + g for g in range(SWA_GROUP)]
            q4 = jnp.concatenate([q_ref[b, :, hh * hd:(hh + 1) * hd] for hh in heads], axis=0).astype(BF16)
            jobs.append((q4, segments, [sinks_ref[:, hh:hh + 1] for hh in heads],
                         [_alibi_slope(hh, n_heads) for hh in heads]))
            job_dst.append((b, heads))
    for (b, heads), outs in zip(job_dst, _attend(jobs, t)):
        for hh, o in zip(heads, outs):
            o_ref[b, :, hh * hd:(hh + 1) * hd] = o


def _swa_step(proj_b3, cols, k_meta, v_meta, k_buf, v_buf, sinks, *, seqs_per_step):
    n, t = k_buf.shape[0], proj_b3.shape[1]
    n_heads = sinks.shape[1]
    kv_heads = n_heads // SWA_GROUP
    q_w, kv_w = n_heads * HEAD_DIM, kv_heads * HEAD_DIM
    n_meta, w = k_meta.shape[1], k_buf.shape[1]
    nb = seqs_per_step
    kv_blk = cols["skv"] // (2 * kv_w)
    seq3 = lambda rows, width, blk=0: pl.BlockSpec((nb, rows, width), lambda i: (i, 0, blk))
    cache_spec = pl.BlockSpec((nb, w, kv_heads, HEAD_DIM), lambda i: (i, 0, 0, 0))
    return pl.pallas_call(
        functools.partial(_swa_step_kernel, kv_heads=kv_heads),
        grid=(n // nb,),
        in_specs=[seq3(t, q_w), seq3(t, 2 * kv_w, kv_blk), seq3(n_meta, kv_w), seq3(n_meta, kv_w),
                  cache_spec, cache_spec, pl.BlockSpec((1, n_heads), lambda i: (0, 0))],
        out_specs=[seq3(t, q_w), cache_spec, cache_spec],
        out_shape=[jax.ShapeDtypeStruct((n, t, q_w), F32),
                   jax.ShapeDtypeStruct((n, w, kv_heads, HEAD_DIM), F32),
                   jax.ShapeDtypeStruct((n, w, kv_heads, HEAD_DIM), F32)],
        compiler_params=_params("parallel"),
        name="swa_step",
    )(proj_b3, proj_b3, k_meta, v_meta, k_buf, v_buf, sinks)


def _split_w_in(w_in, heads, n_heads, kv_heads, d):
    conv_w = 3 * heads * HEAD_DIM
    v_w = heads * HEAD_DIM
    q_w, kv_w = n_heads * HEAD_DIM, kv_heads * HEAD_DIM
    sizes = (conv_w, v_w, heads, heads, q_w, kv_w, kv_w, d, d)
    offs = [0]
    for s in sizes:
        offs.append(offs[-1] + s)
    assert offs[2] % LANES == 0 and 2 * heads <= LANES and offs[2] + LANES <= w_in.shape[1]
    w_bf = w_in.astype(BF16)
    part = lambda i: w_bf[:, offs[i]:offs[i + 1]]
    w_b = jnp.concatenate([part(4), part(8), part(7), part(5), part(6)], axis=1)
    cols_a = {"qkv": 0, "z": conv_w, "ba": offs[2]}
    cols_b = {"q": 0, "g_swa": q_w, "g_dn": q_w + d, "skv": q_w + 2 * d}
    return w_bf, offs[2] + LANES, w_b, cols_a, cols_b


def kernel(x_prompt, x_sample, state_dn_conv, state_dn_ssm, cache_swa_meta_k, cache_swa_meta_v, cache_swa_k, cache_swa_v, meta_tokens, ffn1_norm_pre, ffn1_norm_post, ffn1_w_gate, ffn1_w_up, ffn1_w_down, mix_norm_pre, mix_norm_post, w_in, dn_conv_w, dn_a_log, dn_dt_bias, dn_norm_w, swa_sinks, w_out, ffn2_norm_pre, ffn2_norm_post, ffn2_w_gate, ffn2_w_up, ffn2_w_down):
    assert w_in.shape[0] == 1, "single-layer step"
    n_p, seq, d = x_prompt.shape
    n_s, t_s, _ = x_sample.shape
    n_meta = meta_tokens.shape[0]
    heads = dn_a_log.shape[1]
    n_heads = swa_sinks.shape[1]
    kv_heads = n_heads // SWA_GROUP
    conv_w = dn_conv_w.shape[2]
    kv_w = kv_heads * HEAD_DIM
    w_keep = cache_swa_k.shape[2]
    assert w_keep == WINDOW and seq % WINDOW == 0 and seq % DN_CHUNK == 0 and seq >= WINDOW

    w_bf, a_cols, w_b, cols_a, cols_b = _split_w_in(w_in[0], heads, n_heads, kv_heads, d)
    cols = {**cols_a, **cols_b}
    f1 = (ffn1_norm_pre, ffn1_norm_post, ffn1_w_gate[0].astype(BF16), ffn1_w_up[0].astype(BF16),
          ffn1_w_down[0].astype(BF16))
    f2 = (ffn2_norm_pre, ffn2_norm_post, ffn2_w_gate[0].astype(BF16), ffn2_w_up[0].astype(BF16),
          ffn2_w_down[0].astype(BF16))
    w_o = w_out[0].astype(BF16)
    gdn_w = (dn_conv_w[0], dn_a_log, dn_dt_bias, dn_norm_w)

    def front(h):
        h1 = _ffn(h, *f1)
        return h1, _proj(h1, mix_norm_pre, w_bf, a_cols), _proj(h1, mix_norm_pre, w_b)

    def hist_tile(rows3):
        return jnp.pad(rows3, ((0, 0), (HIST_ROWS - (DN_CONV - 1), 0), (0, 0)))

    rows_s = n_s * t_s
    assert n_meta % t_s == 0 and n_meta >= DN_CONV - 1
    h1_sm, pa_sm, pb_sm = front(jnp.concatenate([x_sample.reshape(rows_s, d), meta_tokens], axis=0))
    pa_m, pb_m = pa_sm[rows_s:], pb_sm[rows_s:]
    zero_hist = jnp.zeros((1, HIST_ROWS, conv_w), F32)
    zero_state = jnp.zeros((1, heads, HEAD_DIM, HEAD_DIM), F32)
    _, s_meta = _gdn_step(pa_m[None], cols, zero_hist, zero_state, *gdn_w, seqs_per_step=1)
    hist_meta = hist_tile(pa_m[None, n_meta - (DN_CONV - 1):, :conv_w])

    h1_p, pa_p, tail_p = _ffn_proj_conv(x_prompt.reshape(n_p * seq, d), *f1, mix_norm_pre, w_bf, a_cols, hist_meta,
                                        dn_conv_w[0], n_seq=n_p, seq_len=seq, qk_width=heads * HEAD_DIM)
    pb_p = _proj(h1_p, mix_norm_pre, w_b)
    o_sw_p = _swa_prompt(pb_p, pb_m, cols, swa_sinks, n_seq=n_p, seq_len=seq, n_meta=n_meta)
    y_prompt, s_p = _gdn_mix(pa_p, pb_p, h1_p, o_sw_p, cols, s_meta, dn_a_log, dn_dt_bias, dn_norm_w, w_o,
                             mix_norm_post, *f2, n_seq=n_p, seq_len=seq, block=GDN_CHUNKS_PER_STEP * DN_CHUNK)
    y_prompt = y_prompt.reshape(n_p, seq, d)

    pb_p3 = pb_p.reshape(n_p, seq, -1)
    p_conv = tail_p[:, HIST_ROWS - (DN_CONV - 1):]
    kv_meta = pb_m[:, cols["skv"]:]
    p_meta_k = jnp.broadcast_to(kv_meta[None, :, :kv_w], (n_p, n_meta, kv_w))
    p_meta_v = jnp.broadcast_to(kv_meta[None, :, kv_w:], (n_p, n_meta, kv_w))
    p_win_k = pb_p3[:, seq - w_keep:, cols["skv"]:cols["skv"] + kv_w]
    p_win_v = pb_p3[:, seq - w_keep:, cols["skv"] + kv_w:]

    pa_s3 = pa_sm.reshape(-1, t_s, pa_sm.shape[1])
    pb_s3 = pb_sm.reshape(-1, t_s, pb_sm.shape[1])
    o_dn_s, s_s = _gdn_step(pa_s3, cols, hist_tile(state_dn_conv[0]), state_dn_ssm[0], *gdn_w,
                            seqs_per_step=STEP_SEQS)
    o_sw_s, s_win_k, s_win_v = _swa_step(
        pb_s3, cols,
        cache_swa_meta_k[0].reshape(n_s, n_meta, kv_w), cache_swa_meta_v[0].reshape(n_s, n_meta, kv_w),
        cache_swa_k[0], cache_swa_v[0], swa_sinks, seqs_per_step=STEP_SEQS)
    y_sample = _mixffn(h1_sm, o_dn_s.reshape(rows_s, -1), o_sw_s.reshape(rows_s, -1), pb_sm, cols, w_o,
                       mix_norm_post, *f2, rows=rows_s).reshape(n_s, t_s, d)
    assert t_s >= DN_CONV - 1, "new conv state is taken from the new rows alone"
    s_conv = pa_s3[:n_s, t_s - (DN_CONV - 1):, :conv_w]

    def kv4(x):
        return x.reshape(1, x.shape[0], x.shape[1], kv_heads, HEAD_DIM)

    return (y_prompt, y_sample, p_conv[None], s_p[None], kv4(p_meta_k), kv4(p_meta_v), kv4(p_win_k),
            kv4(p_win_v), s_conv[None], s_s[None], s_win_k[None], s_win_v[None])
```

```python
import functools
import math

import jax
import jax.numpy as jnp
from jax import lax
from jax.experimental import pallas as pl
from jax.experimental.pallas import tpu as pltpu

F32 = jnp.float32
BF16 = jnp.bfloat16
HIGHEST = lax.Precision.HIGHEST

RMS_EPS = 1e-6
L2_EPS = 1e-6
WINDOW = 128
PAST_LEN = 8192
HEAD_DIM = 128
SWA_GROUP = 4
DN_CONV = 4
DN_CHUNK = 64
HIST_ROWS = 8

VMEM_LIMIT_BYTES = 56 * 1024 * 1024
ROW_TILE_MAX = 576
SUBLANES = 8
LANES = 128
MXU_WIDTH = 256
NEG_BIG = -1e30
SWA_SUBBLOCKS = 4
GDN_CHUNKS_PER_STEP = 4
STEP_SEQS = 4

def _rms(x, g):
    return x * lax.rsqrt(jnp.mean(x * x, axis=-1, keepdims=True) + RMS_EPS) * g


def _silu(x):
    return x * jax.nn.sigmoid(x)


def _dot(a, b):
    return jnp.dot(a, b, preferred_element_type=F32)


def _dot_hi(a, b):
    return jnp.dot(a, b, preferred_element_type=F32, precision=HIGHEST)


def _dot_tn_hi(a, b):
    return lax.dot_general(a, b, (((0,), (0,)), ((), ())), preferred_element_type=F32, precision=HIGHEST)


def _dot_nt(a, b):
    return lax.dot_general(a, b, (((1,), (1,)), ((), ())), preferred_element_type=F32)


def _dot_tn(a, b):
    return lax.dot_general(a, b, (((0,), (0,)), ((), ())), preferred_element_type=F32)


def _split_bf16(a):
    hi = a.astype(BF16)
    return hi, (a - hi.astype(F32)).astype(BF16)


_DOTS = {"nn": _dot, "nt": _dot_nt, "tn": _dot_tn}


def _dot3_split(a_pair, b_pair, kind="nn"):
    (a_hi, a_lo), (b_hi, b_lo) = a_pair, b_pair
    dot = _DOTS[kind]
    axis = 1 if kind == "tn" else 0
    m = a_hi.shape[axis]
    top = dot(jnp.concatenate([a_hi, a_lo], axis=axis), b_hi)
    return top[:m] + top[m:] + dot(a_hi, b_lo)


def _mm(a, b, kind="nn", passes=3):
    if passes == 1:
        return _DOTS[kind](a.astype(BF16), b.astype(BF16))
    return _dot3_split(_split_bf16(a), _split_bf16(b), kind)


def _row_tile(rows):
    start = min(ROW_TILE_MAX, rows) // SUBLANES * SUBLANES
    for tile in range(start, 0, -SUBLANES):
        if rows % tile == 0:
            return tile
    raise ValueError(f"no sublane-aligned row tile divides {rows}")


def _resident(shape):
    return pl.BlockSpec(shape, lambda *_: (0,) * len(shape), pipeline_mode=pl.Buffered(1))


def _params(*semantics):
    return pltpu.CompilerParams(dimension_semantics=semantics, vmem_limit_bytes=VMEM_LIMIT_BYTES)


def _ffn_tiles(hs, g_pre, g_post, wg_ref, wu_ref, wd_ref, ff_chunk):
    xns = [_rms(h, g_pre).astype(BF16) for h in hs]
    accs = [None] * len(hs)
    c0 = 0
    for width in ff_chunk:
        gates = [_dot(xn, wg_ref[:, c0:c0 + width]) for xn in xns]
        ups = [_dot(xn, wu_ref[:, c0:c0 + width]) for xn in xns]
        for i, (gate, up) in enumerate(zip(gates, ups)):
            part = _dot((_silu(gate) * up).astype(BF16), wd_ref[c0:c0 + width, :])
            accs[i] = part if accs[i] is None else accs[i] + part
        c0 += width
    return [h + 0.5 * _rms(acc, g_post) for h, acc in zip(hs, accs)]


def _row_halves(tm):
    return [slice(0, tm // 2), slice(tm // 2, tm)] if tm % (2 * SUBLANES) == 0 else [slice(0, tm)]


def _ff_chunk(d_ff):
    tiles, rem = divmod(d_ff, MXU_WIDTH)
    if rem or tiles < 2:
        return (d_ff,)
    first = (tiles + 1) // 2
    return (first * MXU_WIDTH, (tiles - first) * MXU_WIDTH)


def _ffn_kernel(h_ref, gpre_ref, gpost_ref, wg_ref, wu_ref, wd_ref, o_ref, *, ff_chunk):
    halves = _row_halves(h_ref.shape[0])
    outs = _ffn_tiles([h_ref[r, :] for r in halves], gpre_ref[...], gpost_ref[...], wg_ref, wu_ref, wd_ref, ff_chunk)
    for r, out in zip(halves, outs):
        o_ref[r, :] = out


def _ffn(h, g_pre, g_post, wg, wu, wd):
    rows, d = h.shape
    d_ff = wg.shape[1]
    tm = _row_tile(rows)
    row_spec = pl.BlockSpec((tm, d), lambda i: (i, 0))
    return pl.pallas_call(
        functools.partial(_ffn_kernel, ff_chunk=_ff_chunk(d_ff)),
        grid=(rows // tm,),
        in_specs=[row_spec, _resident((1, d)), _resident((1, d)),
                  _resident((d, d_ff)), _resident((d, d_ff)), _resident((d_ff, d))],
        out_specs=row_spec,
        out_shape=jax.ShapeDtypeStruct((rows, d), F32),
        compiler_params=_params("parallel"),
        name="ffn",
    )(h, g_pre, g_post, wg, wu, wd)


def _proj_kernel(h_ref, g_ref, w_ref, o_ref):
    o_ref[...] = _dot(_rms(h_ref[...], g_ref[...]).astype(BF16), w_ref[...])


def _proj(h, g, w, n_cols=None):
    rows, d = h.shape
    n = w.shape[1] if n_cols is None else n_cols
    tm = _row_tile(rows)
    return pl.pallas_call(
        _proj_kernel,
        grid=(rows // tm,),
        in_specs=[pl.BlockSpec((tm, d), lambda i: (i, 0)), _resident((1, d)), _resident((d, n))],
        out_specs=pl.BlockSpec((tm, n), lambda i: (i, 0)),
        out_shape=jax.ShapeDtypeStruct((rows, n), F32),
        compiler_params=_params("parallel"),
        name="proj",
    )(h, g, w)


def _proj_conv_kernel(h_ref, g_ref, w_ref, wb_ref, hist_ref, convw_ref, o_ref, ob_ref, tail_ref, before_ref, *,
                      conv_width, qk_width):
    j = pl.program_id(1)
    tm = h_ref.shape[0]
    hd = HEAD_DIM
    xn = _rms(h_ref[...], g_ref[...]).astype(BF16)

    @pl.when(j == 0)
    def _():
        before_ref[...] = hist_ref[0]

    @pl.when(j > 0)
    def _():
        before_ref[...] = tail_ref[0]

    before = before_ref[...]
    raw = {}

    def project_raw(c0, width):
        raw[c0] = _dot(xn, w_ref[:, c0:c0 + width])

    def conv_column(g0, c0):
        y_group = raw[g0]
        if c0 == g0:
            tail_ref[0, :, g0:g0 + y_group.shape[1]] = y_group[tm - HIST_ROWS:, :]
        x = jnp.concatenate([before[:, c0:c0 + hd], y_group[:, c0 - g0:c0 - g0 + hd]], axis=0)
        acc = x[HIST_ROWS:] * convw_ref[DN_CONV - 1:DN_CONV, c0:c0 + hd]
        for back in range(1, DN_CONV):
            shifted = pltpu.roll(x, back, 0)[HIST_ROWS:]
            acc = acc + shifted * convw_ref[DN_CONV - 1 - back:DN_CONV - back, c0:c0 + hd]
        y = _silu(acc)
        if c0 < 2 * qk_width:
            y = y * lax.rsqrt(jnp.sum(y * y, axis=-1, keepdims=True) + L2_EPS)
            if c0 < qk_width:
                y = y * (hd ** -0.5)
        o_ref[:, c0:c0 + hd] = y

    def plain_a(c0, c1):
        o_ref[:, c0:c1] = _dot(xn, w_ref[:, c0:c1])

    def plain_b(c0, c1):
        ob_ref[:, c0:c1] = _dot(xn, wb_ref[:, c0:c1])

    group = 2 * MXU_WIDTH
    starts = list(range(0, conv_width, group))
    plain = ([functools.partial(plain_a, c0, min(c0 + MXU_WIDTH, w_ref.shape[1]))
              for c0 in range(conv_width, w_ref.shape[1], MXU_WIDTH)]
             + [functools.partial(plain_b, c0, min(c0 + MXU_WIDTH, wb_ref.shape[1]))
                for c0 in range(0, wb_ref.shape[1], MXU_WIDTH)])
    columns = [(g0, c0) for g0 in starts for c0 in range(g0, g0 + group, hd)]
    free_slots = len(columns) - (len(starts) - 1)
    project_raw(starts[0], group)
    used = slot = 0
    for g0, c0 in columns:
        conv_column(g0, c0)
        if c0 == g0 and g0 + group < conv_width:
            project_raw(g0 + group, group)
        else:
            slot += 1
            upto = len(plain) * slot // free_slots
            for piece in plain[used:upto]:
                piece()
            used = upto


def _proj_conv(h, g, w, n_cols, w_b, hist, conv_w, *, n_seq, seq_len, qk_width):
    rows, d = h.shape
    n, n_b = n_cols, w_b.shape[1]
    conv_width = conv_w.shape[1]
    tm = _row_tile(seq_len)
    tiles = seq_len // tm
    shared_hist = hist.shape[0] == 1
    return pl.pallas_call(
        functools.partial(_proj_conv_kernel, conv_width=conv_width, qk_width=qk_width),
        grid=(n_seq, tiles),
        in_specs=[pl.BlockSpec((tm, d), lambda s, j: (s * tiles + j, 0)), _resident((1, d)), _resident((d, n)),
                  _resident((d, n_b)),
                  pl.BlockSpec((1, HIST_ROWS, conv_width), lambda s, j: (0 if shared_hist else s, 0, 0)),
                  _resident((DN_CONV, conv_width))],
        out_specs=[pl.BlockSpec((tm, n), lambda s, j: (s * tiles + j, 0)),
                   pl.BlockSpec((tm, n_b), lambda s, j: (s * tiles + j, 0)),
                   pl.BlockSpec((1, HIST_ROWS, conv_width), lambda s, j: (s, 0, 0))],
        out_shape=[jax.ShapeDtypeStruct((rows, n), F32),
                   jax.ShapeDtypeStruct((rows, n_b), F32),
                   jax.ShapeDtypeStruct((n_seq, HIST_ROWS, conv_width), F32)],
        scratch_shapes=[pltpu.VMEM((HIST_ROWS, conv_width), F32)],
        compiler_params=_params("parallel", "arbitrary"),
        name="proj_conv",
    )(h, g, w, w_b, hist, conv_w)


def _interleave(primary, secondary):
    done = 0
    for i, stage in enumerate(primary):
        stage()
        upto = (i + 1) * len(secondary) // len(primary)
        for unit in secondary[done:upto]:
            unit()
        done = upto


def _mixffn_kernel(h_ref, odn_ref, osw_ref, gdn_ref, gsw_ref, wout_ref, gmix_ref,
                   gpre_ref, gpost_ref, wg_ref, wu_ref, wd_ref, o_ref, *, ff_chunk):
    halves = _row_halves(h_ref.shape[0])
    ys = [(jax.nn.sigmoid(gdn_ref[r, :]) * odn_ref[r, :] + jax.nn.sigmoid(gsw_ref[r, :]) * osw_ref[r, :]).astype(BF16)
          for r in halves]
    mixed = [_dot(y, wout_ref[...]) for y in ys]
    h2s = [h_ref[r, :] + _rms(m, gmix_ref[...]) for r, m in zip(halves, mixed)]
    outs = _ffn_tiles(h2s, gpre_ref[...], gpost_ref[...], wg_ref, wu_ref, wd_ref, ff_chunk)
    for r, out in zip(halves, outs):
        o_ref[r, :] = out


def _mixffn(h, o_dn, o_sw, proj_b, cols, w_out, g_mix, g_pre, g_post, wg, wu, wd, rows=None):
    d = h.shape[1]
    rows = h.shape[0] if rows is None else rows
    d_ff = wg.shape[1]
    tm = _row_tile(rows)
    row_spec = pl.BlockSpec((tm, d), lambda i: (i, 0))
    gdn_blk, gsw_blk = cols["g_dn"] // d, cols["g_swa"] // d
    return pl.pallas_call(
        functools.partial(_mixffn_kernel, ff_chunk=_ff_chunk(d_ff)),
        grid=(rows // tm,),
        in_specs=[row_spec, row_spec, row_spec,
                  pl.BlockSpec((tm, d), lambda i: (i, gdn_blk)),
                  pl.BlockSpec((tm, d), lambda i: (i, gsw_blk)),
                  _resident((d, d)), _resident((1, d)), _resident((1, d)), _resident((1, d)),
                  _resident((d, d_ff)), _resident((d, d_ff)), _resident((d_ff, d))],
        out_specs=row_spec,
        out_shape=jax.ShapeDtypeStruct((rows, d), F32),
        compiler_params=_params("parallel"),
        name="mixffn",
    )(h, o_dn, o_sw, proj_b, proj_b, w_out, g_mix, g_pre, g_post, wg, wu, wd)


GROUP = 4
PASSES_QK = 1
PASSES_STATE = 1
PASSES_INV = 3
INV_BLOCK = 16
STEP_PASSES = 1


def _gdn_block_stages(qkv_ref, z_ref, ba_ref, alog_ref, dtb_ref, nw_ref, s_ref, store_out, *, chunk, heads):
    tb = qkv_ref.shape[0]
    hd = HEAD_DIM
    c = chunk
    gw = GROUP * c
    qk_w = heads * hd
    stages = []

    def iota(shape, dim):
        return lax.broadcasted_iota(jnp.int32, shape, dim)

    row = iota((c, gw), 0)
    col = iota((c, gw), 1) & (c - 1)
    causal4, strict4 = row >= col, row > col
    diag_blocks4 = (row // INV_BLOCK) == (col // INV_BLOCK)
    eye4 = (row == col).astype(F32)
    bd_mask = (iota((gw, gw), 0) // c) == (iota((gw, gw), 1) // c)
    lower_ones = (iota((c, c), 0) >= iota((c, c), 1)).astype(F32)
    upper_dup = (iota((c, 2 * c), 0) <= (iota((c, 2 * c), 1) & (c - 1))).astype(F32)
    first_half = iota((c, 2 * c), 1) < c
    zeros_head = jnp.zeros((c, hd), F32)

    def block_diag(x4):
        return jnp.where(bd_mask, jnp.concatenate([x4] * GROUP, axis=0), jnp.zeros((), x4.dtype))

    splits = {}

    def split(x):
        if id(x) not in splits:
            splits[id(x)] = (x, _split_bf16(x))
        return splits[id(x)][1]

    def times_block_diag(lhs, x4, passes):
        parts = lhs if isinstance(lhs, list) else [lhs]
        if passes == 1:
            return _dot(jnp.concatenate(parts, axis=0).astype(BF16), block_diag(x4.astype(BF16)))
        hi, lo = split(x4)
        lhs_split = tuple(jnp.concatenate(pieces, axis=0) for pieces in zip(*[split(p) for p in parts]))
        return _dot3_split(lhs_split, (block_diag(hi), block_diag(lo)))

    def block_diag_times(x4, rhs, passes):
        if passes == 1:
            return _dot(block_diag(x4.astype(BF16)), rhs.astype(BF16))
        hi, lo = _split_bf16(x4)
        return _dot3_split((block_diag(hi), block_diag(lo)), _split_bf16(rhs))

    def pack_cols(col_all, hs):
        halves = [jnp.where(first_half,
                            jnp.broadcast_to(col_all[:, hs[2 * p]:hs[2 * p] + 1], (c, 2 * c)),
                            jnp.broadcast_to(col_all[:, hs[2 * p + 1]:hs[2 * p + 1] + 1], (c, 2 * c)))
                  for p in range(GROUP // 2)]
        return jnp.concatenate(halves, axis=1)

    chunks = [dict(r0=r0) for r0 in range(0, tb, c)]
    items = [dict(ch=ch, hs=list(range(g0, g0 + GROUP))) for ch in chunks for g0 in range(0, heads, GROUP)]

    def gating():
        for ch in chunks:
            ba = ba_ref[ch["r0"]:ch["r0"] + c, :]
            a_in = ba[:, heads:2 * heads] + dtb_ref[...]
            softplus = jnp.maximum(a_in, 0.0) + jnp.log(1.0 + jnp.exp(-jnp.abs(a_in)))
            g_all = -jnp.exp(alog_ref[...]) * softplus
            ch["beta"] = jax.nn.sigmoid(ba[:, 0:heads])
            ch["gc"] = _dot_hi(lower_ones, g_all)
            ch["gc_row"] = _dot_tn_hi(g_all, upper_dup)
        for it in items:
            rows = slice(it["ch"]["r0"], it["ch"]["r0"] + c)
            it["qs"] = [qkv_ref[rows, h * hd:(h + 1) * hd] for h in it["hs"]]
            it["ks"] = [qkv_ref[rows, qk_w + h * hd:qk_w + (h + 1) * hd] for h in it["hs"]]
            it["vs"] = [qkv_ref[rows, 2 * qk_w + h * hd:2 * qk_w + (h + 1) * hd] for h in it["hs"]]

    def gram(it):
        ch, hs, qs, ks = it["ch"], it["hs"], it["qs"], it["ks"]
        beta4 = pack_cols(ch["beta"], hs)
        gc_row4 = jnp.concatenate(
            [jnp.where(first_half[0:1], ch["gc_row"][hs[2 * p]:hs[2 * p] + 1],
                       ch["gc_row"][hs[2 * p + 1]:hs[2 * p + 1] + 1])
             for p in range(GROUP // 2)], axis=1)
        diff4 = pack_cols(ch["gc"], hs) - gc_row4
        decay4 = jnp.where(causal4, jnp.exp(jnp.where(causal4, diff4, 0.0)), 0.0)
        lhs = jnp.concatenate([jnp.concatenate(qs, axis=1), jnp.concatenate(ks, axis=1)], axis=0)
        k_bd = jnp.concatenate(
            [jnp.concatenate([ks[i] if ii == i else zeros_head for ii in range(GROUP)], axis=1)
             for i in range(GROUP)], axis=0)
        qkkk = _mm(lhs, k_bd, "nt", PASSES_QK)
        it["qk4"] = qkkk[:c] * decay4
        a4 = jnp.where(strict4, beta4 * qkkk[c:] * decay4, 0.0)
        a_diag = jnp.where(diag_blocks4, a4, 0.0)
        it["a_diag"], it["a_off"] = a_diag, a4 - a_diag
        it["t4"] = eye4 - a_diag

    assert c // INV_BLOCK == 4
    n_sq = int(math.log2(INV_BLOCK)) - 1

    def inverse_start(it):
        it["p4"] = times_block_diag(it["a_diag"], it["a_diag"], PASSES_INV)

    def inverse_step(step, it):
        if step < n_sq - 1:
            both = times_block_diag([it["p4"], it["t4"]], it["p4"], PASSES_INV)
            it["p4"], it["t4"] = both[:c], it["t4"] + both[c:]
        else:
            it["t4"] = it["t4"] + times_block_diag(it["t4"], it["p4"], PASSES_INV)

    def off_diagonal(it):
        it["m4"] = times_block_diag(it["t4"], it["a_off"], PASSES_INV)

    def off_diagonal_square(it):
        it["m2"] = times_block_diag(it["m4"], it["m4"], PASSES_INV)

    def series(it):
        i_minus_m = eye4 - it["m4"]
        it["s4"] = i_minus_m + times_block_diag(i_minus_m, it["m2"], PASSES_INV)

    def inverse_finish(it):
        it["t4"] = times_block_diag(it["s4"], it["t4"], PASSES_INV)

    def solve(it):
        ch, hs, ks, vs = it["ch"], it["hs"], it["ks"], it["vs"]
        rhs = jnp.concatenate(
            [jnp.concatenate([vs[i] * ch["beta"][:, h:h + 1],
                              ks[i] * (ch["beta"][:, h:h + 1] * jnp.exp(ch["gc"][:, h:h + 1]))], axis=1)
             for i, h in enumerate(hs)], axis=0)
        it["sol"] = block_diag_times(it["t4"], rhs, PASSES_INV)

    def for_items(fn, subset=None):
        return lambda: [fn(it) for it in (items if subset is None else subset)]

    stages.append(gating)
    stages.append(for_items(gram))
    stages.append(for_items(inverse_start))
    for step in range(n_sq):
        stages.append(for_items(functools.partial(inverse_step, step)))
    stages += [for_items(off_diagonal), for_items(off_diagonal_square), for_items(series),
               for_items(inverse_finish), for_items(solve)]

    def read_state(it):
        ch = it["ch"]
        us, o_inter = [], []
        for i, h in enumerate(it["hs"]):
            sl = slice(i * c, (i + 1) * c)
            ws = _mm(jnp.concatenate([it["sol"][sl, hd:], it["qs"][i] * jnp.exp(ch["gc"][:, h:h + 1])], axis=0),
                     s_ref[h], "nn", PASSES_STATE)
            us.append(it["sol"][sl, :hd] - ws[:c])
            o_inter.append(ws[c:])
        it["us"], it["o_inter"] = us, o_inter

    def intra(it):
        it["o_intra"] = block_diag_times(it["qk4"], jnp.concatenate(it["us"], axis=0), PASSES_STATE)

    def update_state(it):
        ch = it["ch"]
        for i, h in enumerate(it["hs"]):
            gc = ch["gc"][:, h:h + 1]
            gc_last = ch["gc_row"][h:h + 1, c - 1:c]
            s_ref[h] = s_ref[h] * jnp.exp(gc_last) + _mm(it["ks"][i] * jnp.exp(gc_last - gc), it["us"][i],
                                                         "tn", PASSES_STATE)

    def emit(it):
        r0 = it["ch"]["r0"]
        for i, h in enumerate(it["hs"]):
            o = it["o_inter"][i] + it["o_intra"][i * c:(i + 1) * c]
            zh = z_ref[r0:r0 + c, h * hd:(h + 1) * hd]
            store_out(slice(r0, r0 + c), slice(h * hd, (h + 1) * hd), _rms(o, nw_ref[...]) * _silu(zh))

    for ch in chunks:
        group_items = [it for it in items if it["ch"] is ch]
        stages += [for_items(fn, group_items) for fn in (read_state, intra, update_state, emit)]
    return stages


def _ff_pieces(d_ff, n):
    tiles, rem = divmod(d_ff, MXU_WIDTH)
    if rem:
        return (d_ff,)
    per = -(-tiles // n)
    return tuple(min(per, tiles - t0) * MXU_WIDTH for t0 in range(0, tiles, per))


def _gdn_mix_kernel(qkv_ref, z_ref, ba_ref, s0_ref, alog_ref, dtb_ref, nw_ref,
                    h_ref, osw_ref, gdn_ref, gsw_ref, wout_ref, gmix_ref, gpre_ref, gpost_ref, wg_ref, wu_ref, wd_ref,
                    y_ref, sout_ref, s_ref, odn_ref, *, chunk, heads, nblk, ff_pieces):
    s = pl.program_id(0)
    n_blocks = pl.num_programs(0) - 1
    j = s % nblk
    cur, prev = s % 2, (s + 1) % 2

    @pl.when(s == 0)
    def _():
        odn_ref[...] = jnp.zeros(odn_ref.shape, F32)

    @pl.when((j == 0) & (s < n_blocks))
    def _():
        s_ref[...] = s0_ref[0]

    def store_out(rows, lanes, value):
        odn_ref[cur, rows, lanes] = value

    gdn_stages = _gdn_block_stages(qkv_ref, z_ref, ba_ref, alog_ref, dtb_ref, nw_ref, s_ref, store_out,
                                   chunk=chunk, heads=heads)

    st = {}

    def merge():
        y = jax.nn.sigmoid(gdn_ref[...]) * odn_ref[prev] + jax.nn.sigmoid(gsw_ref[...]) * osw_ref[...]
        st["y"] = y.astype(BF16)

    def out_proj():
        st["mixed"] = _dot(st["y"], wout_ref[...])

    def norms():
        st["h2"] = h_ref[...] + _rms(st["mixed"], gmix_ref[...])
        st["xn"] = _rms(st["h2"], gpre_ref[...]).astype(BF16)
        st["acc"] = None

    def gate(c0, width):
        st["gate"] = _dot(st["xn"], wg_ref[:, c0:c0 + width])

    def up(c0, width):
        st["up"] = _dot(st["xn"], wu_ref[:, c0:c0 + width])

    def down(c0, width):
        part = _dot((_silu(st["gate"]) * st["up"]).astype(BF16), wd_ref[c0:c0 + width, :])
        st["acc"] = part if st["acc"] is None else st["acc"] + part

    def finish():
        y_ref[...] = st["h2"] + 0.5 * _rms(st["acc"], gpost_ref[...])

    mix_units = [merge, out_proj, norms]
    c0 = 0
    for width in ff_pieces:
        mix_units += [functools.partial(fn, c0, width) for fn in (gate, up, down)]
        c0 += width
    mix_units.append(finish)

    _interleave(gdn_stages, mix_units)

    @pl.when((j == nblk - 1) & (s < n_blocks))
    def _():
        sout_ref[0] = s_ref[...]


def _gdn_mix(proj_a, proj_b, h, o_sw, cols, s0, a_log, dt_bias, norm_w, w_out, g_mix, g_pre, g_post, wg, wu, wd,
             *, n_seq, seq_len, block):
    heads = a_log.shape[1]
    v_w = heads * HEAD_DIM
    conv_width = 3 * v_w
    d = h.shape[1]
    d_ff = wg.shape[1]
    nblk = seq_len // block
    n_blocks = n_seq * nblk
    z_blk, ba_blk = cols["z"] // v_w, cols["ba"] // LANES
    gdn_blk, gsw_blk = cols["g_dn"] // d, cols["g_swa"] // d
    shared_s0 = s0.shape[0] == 1
    assert 2 * DN_CHUNK == HEAD_DIM and heads % GROUP == 0 and block % DN_CHUNK == 0 and v_w == d

    def cur(s):
        return jnp.minimum(s, n_blocks - 1)

    def prev(s):
        return jnp.maximum(s - 1, 0)

    return pl.pallas_call(
        functools.partial(_gdn_mix_kernel, chunk=DN_CHUNK, heads=heads, nblk=nblk, ff_pieces=_ff_pieces(d_ff, 4)),
        grid=(n_blocks + 1,),
        in_specs=[
            pl.BlockSpec((block, conv_width), lambda s: (cur(s), 0)),
            pl.BlockSpec((block, v_w), lambda s: (cur(s), z_blk)),
            pl.BlockSpec((block, LANES), lambda s: (cur(s), ba_blk)),
            pl.BlockSpec((1, heads, HEAD_DIM, HEAD_DIM), lambda s: (0 if shared_s0 else cur(s) // nblk, 0, 0, 0)),
            _resident((1, heads)), _resident((1, heads)), _resident((1, HEAD_DIM)),
            pl.BlockSpec((block, d), lambda s: (prev(s), 0)),
            pl.BlockSpec((block, d), lambda s: (prev(s), 0)),
            pl.BlockSpec((block, d), lambda s: (prev(s), gdn_blk)),
            pl.BlockSpec((block, d), lambda s: (prev(s), gsw_blk)),
            _resident((d, d)), _resident((1, d)), _resident((1, d)), _resident((1, d)),
            _resident((d, d_ff)), _resident((d, d_ff)), _resident((d_ff, d)),
        ],
        out_specs=[
            pl.BlockSpec((block, d), lambda s: (prev(s), 0)),
            pl.BlockSpec((1, heads, HEAD_DIM, HEAD_DIM), lambda s: (cur(s) // nblk, 0, 0, 0)),
        ],
        out_shape=[
            jax.ShapeDtypeStruct((n_seq * seq_len, d), F32),
            jax.ShapeDtypeStruct((n_seq, heads, HEAD_DIM, HEAD_DIM), F32),
        ],
        scratch_shapes=[pltpu.VMEM((heads, HEAD_DIM, HEAD_DIM), F32), pltpu.VMEM((2, block, v_w), F32)],
        compiler_params=_params("arbitrary"),
        name="gdn_mix",
    )(proj_a, proj_a, proj_a, s0, a_log, dt_bias, norm_w,
      h, o_sw, proj_b, proj_b, w_out, g_mix, g_pre, g_post, wg, wu, wd)


def _gdn_step_kernel(qkv_ref, z_ref, ba_ref, hist_ref, s0_ref, convw_ref, alog_ref, dtb_ref, nw_ref,
                     o_ref, sout_ref, *, heads):
    nb, t, _ = qkv_ref.shape
    hd = HEAD_DIM
    qk_w = heads * hd
    ti = lax.broadcasted_iota(jnp.int32, (t, t), 0)
    tj = lax.broadcasted_iota(jnp.int32, (t, t), 1)
    lower_ones = (ti >= tj).astype(F32)
    upper_ones = (ti <= tj).astype(F32)

    def per_sequence(b, carry):
        raw = jnp.concatenate([hist_ref[b], qkv_ref[b]], axis=0)
        acc = raw[HIST_ROWS:] * convw_ref[DN_CONV - 1:DN_CONV, :]
        for back in range(1, DN_CONV):
            acc = acc + pltpu.roll(raw, back, 0)[HIST_ROWS:] * convw_ref[DN_CONV - 1 - back:DN_CONV - back, :]
        x = _silu(acc)
        ba = ba_ref[b]
        beta_all = jax.nn.sigmoid(ba[:, 0:heads])
        a_in = ba[:, heads:2 * heads] + dtb_ref[...]
        softplus = jnp.maximum(a_in, 0.0) + jnp.log(1.0 + jnp.exp(-jnp.abs(a_in)))
        g_all = -jnp.exp(alog_ref[...]) * softplus
        gc_all = _dot_hi(lower_ones, g_all)
        gc_row_all = _dot_tn_hi(g_all, upper_ones)
        z = z_ref[b]
        hs = range(heads)
        qs, ks, vs = [], [], []
        for h in hs:
            q = x[:, h * hd:(h + 1) * hd]
            k = x[:, qk_w + h * hd:qk_w + (h + 1) * hd]
            qs.append(q * lax.rsqrt(jnp.sum(q * q, axis=-1, keepdims=True) + L2_EPS) * (hd ** -0.5))
            ks.append(k * lax.rsqrt(jnp.sum(k * k, axis=-1, keepdims=True) + L2_EPS))
            vs.append(x[:, 2 * qk_w + h * hd:2 * qk_w + (h + 1) * hd])
        gcs = [gc_all[:, h:h + 1] for h in hs]
        gc_lasts = [gc_row_all[h:h + 1, t - 1:t] for h in hs]
        kq = [jnp.concatenate([ks[h], qs[h]], axis=0) for h in hs]
        gram = [_mm(kq[h], ks[h], "nt", STEP_PASSES) for h in hs]
        from_state = [_mm(kq[h] * jnp.exp(jnp.concatenate([gcs[h], gcs[h]], axis=0)), s0_ref[b, h], "nn",
                          STEP_PASSES) for h in hs]
        us, outs = [], []
        for h in hs:
            decay = jnp.where(ti >= tj, jnp.exp(jnp.where(ti >= tj, gcs[h] - gc_row_all[h:h + 1, :], 0.0)), 0.0)
            beta = beta_all[:, h:h + 1]
            lower = jnp.where(ti > tj, beta * gram[h][:t] * decay, 0.0)
            u = beta * (vs[h] - from_state[h][:t])
            for j in range(t - 1):
                u = u - lower[:, j:j + 1] * u[j:j + 1, :]
            qk = gram[h][t:] * decay
            o = from_state[h][t:]
            for j in range(t):
                o = o + qk[:, j:j + 1] * u[j:j + 1, :]
            us.append(u)
            outs.append(o)
        for h in hs:
            sout_ref[b, h] = s0_ref[b, h] * jnp.exp(gc_lasts[h]) + _mm(ks[h] * jnp.exp(gc_lasts[h] - gcs[h]), us[h],
                                                                       "tn", STEP_PASSES)
        for h in hs:
            o_ref[b, :, h * hd:(h + 1) * hd] = _rms(outs[h], nw_ref[...]) * _silu(z[:, h * hd:(h + 1) * hd])
        return carry

    lax.fori_loop(0, nb, per_sequence, 0, unroll=2 if nb % 2 == 0 else 1)


def _gdn_step(proj_a3, cols, hist, s0, conv_w, a_log, dt_bias, norm_w, *, seqs_per_step):
    n, t = s0.shape[0], proj_a3.shape[1]
    heads = a_log.shape[1]
    conv_width = conv_w.shape[1]
    v_w = heads * HEAD_DIM
    nb = seqs_per_step
    z_blk, ba_blk = cols["z"] // v_w, cols["ba"] // LANES
    state_spec = pl.BlockSpec((nb, heads, HEAD_DIM, HEAD_DIM), lambda i: (i, 0, 0, 0))
    return pl.pallas_call(
        functools.partial(_gdn_step_kernel, heads=heads),
        grid=(n // nb,),
        in_specs=[
            pl.BlockSpec((nb, t, conv_width), lambda i: (i, 0, 0)),
            pl.BlockSpec((nb, t, v_w), lambda i: (i, 0, z_blk)),
            pl.BlockSpec((nb, t, LANES), lambda i: (i, 0, ba_blk)),
            pl.BlockSpec((nb, HIST_ROWS, conv_width), lambda i: (i, 0, 0)),
            state_spec,
            pl.BlockSpec((DN_CONV, conv_width), lambda i: (0, 0)),
            pl.BlockSpec((1, heads), lambda i: (0, 0)),
            pl.BlockSpec((1, heads), lambda i: (0, 0)),
            pl.BlockSpec((1, HEAD_DIM), lambda i: (0, 0)),
        ],
        out_specs=[pl.BlockSpec((nb, t, v_w), lambda i: (i, 0, 0)), state_spec],
        out_shape=[jax.ShapeDtypeStruct((n, t, v_w), F32),
                   jax.ShapeDtypeStruct((n, heads, HEAD_DIM, HEAD_DIM), F32)],
        compiler_params=_params("parallel"),
        name="gdn_step",
    )(proj_a3, proj_a3, proj_a3, hist, s0, conv_w, a_log, dt_bias, norm_w)


def _alibi_slope(head, n_heads):
    return 2.0 ** (-8.0 * (head + 1) / n_heads)


def _penalty(dist, mask):
    return jnp.where(mask, jnp.minimum(dist, WINDOW).astype(F32), -NEG_BIG)


def _attend(jobs, n_rows):
    scale = HEAD_DIM ** -0.5
    hd = HEAD_DIM
    scores = [[_dot_nt(q4, k) * scale for k, _, _ in segs] for q4, segs, _, _ in jobs]
    v_ones = [[jnp.concatenate([v, jnp.ones_like(v)], axis=1) for _, v, _ in segs] for _, segs, _, _ in jobs]
    heads = [(ji, g) for ji in range(len(jobs)) for g in range(len(jobs[ji][3]))]
    logits, maxes = {}, {}
    for ji, g in heads:
        _, segs, _, slopes = jobs[ji]
        r = slice(g * n_rows, (g + 1) * n_rows)
        logits[ji, g] = [sc[r] - slopes[g] * pen for sc, (_, _, pen) in zip(scores[ji], segs)]
    for ji, g in heads:
        by_width = {}
        for lg in logits[ji, g]:
            w = lg.shape[1]
            by_width[w] = lg if w not in by_width else jnp.maximum(by_width[w], lg)
        m = jobs[ji][2][g]
        for lg in by_width.values():
            m = jnp.maximum(m, jnp.max(lg, axis=-1, keepdims=True))
        maxes[ji, g] = m
    accs = {}
    for ji, g in heads:
        acc = None
        for lg, v1 in zip(logits[ji, g], v_ones[ji]):
            pv = _dot(jnp.exp(lg - maxes[ji, g]).astype(BF16), v1)
            acc = pv if acc is None else acc + pv
        accs[ji, g] = acc
    outs = [[None] * len(job[3]) for job in jobs]
    for ji, g in heads:
        acc = accs[ji, g]
        outs[ji][g] = acc[:, :hd] / (acc[:, hd:] + jnp.exp(jobs[ji][2][g] - maxes[ji, g]))
    return outs


def _swa_prompt_kernel(q_ref, kv_ref, kvprev_ref, kvmeta_ref, sinks_ref, o_ref, *, n_meta, kv_heads):
    j = pl.program_id(1)
    tq = WINDOW
    n_sub = q_ref.shape[0] // tq
    hd = HEAD_DIM
    kv_w = kv_heads * hd
    n_heads = kv_heads * SWA_GROUP
    qi = lax.broadcasted_iota(jnp.int32, (tq, tq), 0)
    ki = lax.broadcasted_iota(jnp.int32, (tq, tq), 1)
    dist_own = qi - ki
    dist_prev = dist_own + tq
    pen_own = _penalty(dist_own, dist_own >= 0)
    pen_prev = _penalty(dist_prev, dist_prev <= WINDOW)
    pen_prev_first = _penalty(dist_prev, (dist_prev <= WINDOW) & (j > 0))
    jobs, job_dst = [], []
    for sub in range(n_sub):
        rows = slice(sub * tq, (sub + 1) * tq)
        prev_ref, prev_rows = (kvprev_ref, slice(0, tq)) if sub == 0 else (kv_ref, slice((sub - 1) * tq, sub * tq))
        qpos = n_meta + (j * n_sub + sub) * tq + lax.broadcasted_iota(jnp.int32, (tq, n_meta), 0)
        dist_meta = qpos - lax.broadcasted_iota(jnp.int32, (tq, n_meta), 1)
        pen_meta = _penalty(dist_meta, dist_meta >= 0)
        for kvh in range(kv_heads):
            ks = slice(kvh * hd, (kvh + 1) * hd)
            vs = slice(kv_w + kvh * hd, kv_w + (kvh + 1) * hd)
            segments = [
                (kvmeta_ref[:, ks].astype(BF16), kvmeta_ref[:, vs].astype(BF16), pen_meta),
                (prev_ref[prev_rows, ks].astype(BF16), prev_ref[prev_rows, vs].astype(BF16),
                 pen_prev_first if sub == 0 else pen_prev),
                (kv_ref[rows, ks].astype(BF16), kv_ref[rows, vs].astype(BF16), pen_own),
            ]
            heads = [kvh * SWA_GROUP + g for g in range(SWA_GROUP)]
            q4 = jnp.concatenate([q_ref[rows, hh * hd:(hh + 1) * hd] for hh in heads], axis=0).astype(BF16)
            jobs.append((q4, segments, [sinks_ref[:, hh:hh + 1] for hh in heads],
                         [_alibi_slope(hh, n_heads) for hh in heads]))
            job_dst.append((rows, heads))
    for (rows, heads), outs in zip(job_dst, _attend(jobs, tq)):
        for hh, o in zip(heads, outs):
            o_ref[rows, hh * hd:(hh + 1) * hd] = o


def _swa_prompt(proj_b, proj_b_meta, cols, sinks, *, n_seq, seq_len, n_meta):
    n_heads = sinks.shape[1]
    kv_heads = n_heads // SWA_GROUP
    q_w, kv_w2 = n_heads * HEAD_DIM, 2 * kv_heads * HEAD_DIM
    n_sub = SWA_SUBBLOCKS if seq_len % (SWA_SUBBLOCKS * WINDOW) == 0 else 1
    tq = n_sub * WINDOW
    nblk = seq_len // tq
    kv_blk = cols["skv"] // kv_w2
    return pl.pallas_call(
        functools.partial(_swa_prompt_kernel, n_meta=n_meta, kv_heads=kv_heads),
        grid=(n_seq, nblk),
        in_specs=[
            pl.BlockSpec((tq, q_w), lambda n, j: (n * nblk + j, 0)),
            pl.BlockSpec((tq, kv_w2), lambda n, j: (n * nblk + j, kv_blk)),
            pl.BlockSpec((WINDOW, kv_w2), lambda n, j: (jnp.maximum((n * nblk + j) * n_sub - 1, 0), kv_blk)),
            pl.BlockSpec((n_meta, kv_w2), lambda n, j: (0, kv_blk)),
            pl.BlockSpec((1, n_heads), lambda n, j: (0, 0)),
        ],
        out_specs=pl.BlockSpec((tq, q_w), lambda n, j: (n * nblk + j, 0)),
        out_shape=jax.ShapeDtypeStruct((n_seq * seq_len, q_w), F32),
        compiler_params=_params("parallel", "arbitrary"),
        name="swa_prompt",
    )(proj_b, proj_b, proj_b, proj_b_meta, sinks)


def _swa_step_kernel(q_ref, kvnew_ref, kmeta_ref, vmeta_ref, kbuf_ref, vbuf_ref, sinks_ref,
                     o_ref, kout_ref, vout_ref, *, kv_heads):
    nb, t, _ = q_ref.shape
    n_meta, w = kmeta_ref.shape[1], kbuf_ref.shape[1]
    hd = HEAD_DIM
    kv_w = kv_heads * hd
    n_heads = kv_heads * SWA_GROUP

    def grid2(rows, cols_):
        return (lax.broadcasted_iota(jnp.int32, (rows, cols_), 0), lax.broadcasted_iota(jnp.int32, (rows, cols_), 1))

    ti, mi = grid2(t, n_meta)
    dist_meta = PAST_LEN + ti - mi
    pen_meta = _penalty(dist_meta, dist_meta >= 0)
    ti, bi = grid2(t, w)
    dist_buf = w + ti - bi
    pen_buf = _penalty(dist_buf, (dist_buf <= WINDOW) & (PAST_LEN - w + bi >= n_meta))
    ti, si = grid2(t, t)
    dist_new = ti - si
    pen_new = _penalty(dist_new, dist_new >= 0)
    jobs, job_dst = [], []
    for b in range(nb):
        kv_new = kvnew_ref[b]
        for kvh in range(kv_heads):
            ks = slice(kvh * hd, (kvh + 1) * hd)
            k_old, v_old = kbuf_ref[b, :, kvh, :], vbuf_ref[b, :, kvh, :]
            k_new, v_new = kv_new[:, ks], kv_new[:, kv_w + kvh * hd:kv_w + (kvh + 1) * hd]
            kout_ref[b, 0:w - t, kvh, :] = k_old[t:w]
            kout_ref[b, w - t:w, kvh, :] = k_new
            vout_ref[b, 0:w - t, kvh, :] = v_old[t:w]
            vout_ref[b, w - t:w, kvh, :] = v_new
            segments = [
                (kmeta_ref[b, :, ks].astype(BF16), vmeta_ref[b, :, ks].astype(BF16), pen_meta),
                (k_old.astype(BF16), v_old.astype(BF16), pen_buf),
                (k_new.astype(BF16), v_new.astype(BF16), pen_new),
            ]
            heads = [kvh * SWA_GROUP + g for g in range(SWA_GROUP)]
            q4 = jnp.concatenate([q_ref[b, :, hh * hd:(hh + 1) * hd] for hh in heads], axis=0).astype(BF16)
            jobs.append((q4, segments, [sinks_ref[:, hh:hh + 1] for hh in heads],
                         [_alibi_slope(hh, n_heads) for hh in heads]))
            job_dst.append((b, heads))
    for (b, heads), outs in zip(job_dst, _attend(jobs, t)):
        for hh, o in zip(heads, outs):
            o_ref[b, :, hh * hd:(hh + 1) * hd] = o


def _swa_step(proj_b3, cols, k_meta, v_meta, k_buf, v_buf, sinks, *, seqs_per_step):
    n, t = k_buf.shape[0], proj_b3.shape[1]
    n_heads = sinks.shape[1]
    kv_heads = n_heads // SWA_GROUP
    q_w, kv_w = n_heads * HEAD_DIM, kv_heads * HEAD_DIM
    n_meta, w = k_meta.shape[1], k_buf.shape[1]
    nb = seqs_per_step
    kv_blk = cols["skv"] // (2 * kv_w)
    seq3 = lambda rows, width, blk=0: pl.BlockSpec((nb, rows, width), lambda i: (i, 0, blk))
    cache_spec = pl.BlockSpec((nb, w, kv_heads, HEAD_DIM), lambda i: (i, 0, 0, 0))
    return pl.pallas_call(
        functools.partial(_swa_step_kernel, kv_heads=kv_heads),
        grid=(n // nb,),
        in_specs=[seq3(t, q_w), seq3(t, 2 * kv_w, kv_blk), seq3(n_meta, kv_w), seq3(n_meta, kv_w),
                  cache_spec, cache_spec, pl.BlockSpec((1, n_heads), lambda i: (0, 0))],
        out_specs=[seq3(t, q_w), cache_spec, cache_spec],
        out_shape=[jax.ShapeDtypeStruct((n, t, q_w), F32),
                   jax.ShapeDtypeStruct((n, w, kv_heads, HEAD_DIM), F32),
                   jax.ShapeDtypeStruct((n, w, kv_heads, HEAD_DIM), F32)],
        compiler_params=_params("parallel"),
        name="swa_step",
    )(proj_b3, proj_b3, k_meta, v_meta, k_buf, v_buf, sinks)


def _split_w_in(w_in, heads, n_heads, kv_heads, d):
    conv_w = 3 * heads * HEAD_DIM
    v_w = heads * HEAD_DIM
    q_w, kv_w = n_heads * HEAD_DIM, kv_heads * HEAD_DIM
    sizes = (conv_w, v_w, heads, heads, q_w, kv_w, kv_w, d, d)
    offs = [0]
    for s in sizes:
        offs.append(offs[-1] + s)
    assert offs[2] % LANES == 0 and 2 * heads <= LANES and offs[2] + LANES <= w_in.shape[1]
    w_bf = w_in.astype(BF16)
    part = lambda i: w_bf[:, offs[i]:offs[i + 1]]
    w_b = jnp.concatenate([part(4), part(8), part(7), part(5), part(6)], axis=1)
    cols_a = {"qkv": 0, "z": conv_w, "ba": offs[2]}
    cols_b = {"q": 0, "g_swa": q_w, "g_dn": q_w + d, "skv": q_w + 2 * d}
    return w_bf, offs[2] + LANES, w_b, cols_a, cols_b


def kernel(x_prompt, x_sample, state_dn_conv, state_dn_ssm, cache_swa_meta_k, cache_swa_meta_v, cache_swa_k, cache_swa_v, meta_tokens, ffn1_norm_pre, ffn1_norm_post, ffn1_w_gate, ffn1_w_up, ffn1_w_down, mix_norm_pre, mix_norm_post, w_in, dn_conv_w, dn_a_log, dn_dt_bias, dn_norm_w, swa_sinks, w_out, ffn2_norm_pre, ffn2_norm_post, ffn2_w_gate, ffn2_w_up, ffn2_w_down):
    assert w_in.shape[0] == 1, "single-layer step"
    n_p, seq, d = x_prompt.shape
    n_s, t_s, _ = x_sample.shape
    n_meta = meta_tokens.shape[0]
    heads = dn_a_log.shape[1]
    n_heads = swa_sinks.shape[1]
    kv_heads = n_heads // SWA_GROUP
    conv_w = dn_conv_w.shape[2]
    kv_w = kv_heads * HEAD_DIM
    w_keep = cache_swa_k.shape[2]
    assert w_keep == WINDOW and seq % WINDOW == 0 and seq % DN_CHUNK == 0 and seq >= WINDOW

    w_bf, a_cols, w_b, cols_a, cols_b = _split_w_in(w_in[0], heads, n_heads, kv_heads, d)
    cols = {**cols_a, **cols_b}
    f1 = (ffn1_norm_pre, ffn1_norm_post, ffn1_w_gate[0].astype(BF16), ffn1_w_up[0].astype(BF16),
          ffn1_w_down[0].astype(BF16))
    f2 = (ffn2_norm_pre, ffn2_norm_post, ffn2_w_gate[0].astype(BF16), ffn2_w_up[0].astype(BF16),
          ffn2_w_down[0].astype(BF16))
    w_o = w_out[0].astype(BF16)
    gdn_w = (dn_conv_w[0], dn_a_log, dn_dt_bias, dn_norm_w)

    def front(h):
        h1 = _ffn(h, *f1)
        return h1, _proj(h1, mix_norm_pre, w_bf, a_cols), _proj(h1, mix_norm_pre, w_b)

    def hist_tile(rows3):
        return jnp.pad(rows3, ((0, 0), (HIST_ROWS - (DN_CONV - 1), 0), (0, 0)))

    rows_s = n_s * t_s
    assert n_meta % t_s == 0 and n_meta >= DN_CONV - 1
    h1_sm, pa_sm, pb_sm = front(jnp.concatenate([x_sample.reshape(rows_s, d), meta_tokens], axis=0))
    pa_m, pb_m = pa_sm[rows_s:], pb_sm[rows_s:]
    zero_hist = jnp.zeros((1, HIST_ROWS, conv_w), F32)
    zero_state = jnp.zeros((1, heads, HEAD_DIM, HEAD_DIM), F32)
    _, s_meta = _gdn_step(pa_m[None], cols, zero_hist, zero_state, *gdn_w, seqs_per_step=1)
    hist_meta = hist_tile(pa_m[None, n_meta - (DN_CONV - 1):, :conv_w])

    h1_p = _ffn(x_prompt.reshape(n_p * seq, d), *f1)
    pa_p, pb_p, tail_p = _proj_conv(h1_p, mix_norm_pre, w_bf, a_cols, w_b, hist_meta, dn_conv_w[0], n_seq=n_p,
                                    seq_len=seq, qk_width=heads * HEAD_DIM)
    o_sw_p = _swa_prompt(pb_p, pb_m, cols, swa_sinks, n_seq=n_p, seq_len=seq, n_meta=n_meta)
    y_prompt, s_p = _gdn_mix(pa_p, pb_p, h1_p, o_sw_p, cols, s_meta, dn_a_log, dn_dt_bias, dn_norm_w, w_o,
                             mix_norm_post, *f2, n_seq=n_p, seq_len=seq, block=GDN_CHUNKS_PER_STEP * DN_CHUNK)
    y_prompt = y_prompt.reshape(n_p, seq, d)

    pb_p3 = pb_p.reshape(n_p, seq, -1)
    p_conv = tail_p[:, HIST_ROWS - (DN_CONV - 1):]
    kv_meta = pb_m[:, cols["skv"]:]
    p_meta_k = jnp.broadcast_to(kv_meta[None, :, :kv_w], (n_p, n_meta, kv_w))
    p_meta_v = jnp.broadcast_to(kv_meta[None, :, kv_w:], (n_p, n_meta, kv_w))
    p_win_k = pb_p3[:, seq - w_keep:, cols["skv"]:cols["skv"] + kv_w]
    p_win_v = pb_p3[:, seq - w_keep:, cols["skv"] + kv_w:]

    pa_s3 = pa_sm.reshape(-1, t_s, pa_sm.shape[1])
    pb_s3 = pb_sm.reshape(-1, t_s, pb_sm.shape[1])
    o_dn_s, s_s = _gdn_step(pa_s3, cols, hist_tile(state_dn_conv[0]), state_dn_ssm[0], *gdn_w,
                            seqs_per_step=STEP_SEQS)
    o_sw_s, s_win_k, s_win_v = _swa_step(
        pb_s3, cols,
        cache_swa_meta_k[0].reshape(n_s, n_meta, kv_w), cache_swa_meta_v[0].reshape(n_s, n_meta, kv_w),
        cache_swa_k[0], cache_swa_v[0], swa_sinks, seqs_per_step=STEP_SEQS)
    y_sample = _mixffn(h1_sm, o_dn_s.reshape(rows_s, -1), o_sw_s.reshape(rows_s, -1), pb_sm, cols, w_o,
                       mix_norm_post, *f2, rows=rows_s).reshape(n_s, t_s, d)
    assert t_s >= DN_CONV - 1, "new conv state is taken from the new rows alone"
    s_conv = pa_s3[:n_s, t_s - (DN_CONV - 1):, :conv_w]

    def kv4(x):
        return x.reshape(1, x.shape[0], x.shape[1], kv_heads, HEAD_DIM)

    return (y_prompt, y_sample, p_conv[None], s_p[None], kv4(p_meta_k), kv4(p_meta_v), kv4(p_win_k),
            kv4(p_win_v), s_conv[None], s_s[None], s_win_k[None], s_win_v[None])
```

```python
import functools
import math

import jax
import jax.numpy as jnp
from jax import lax
from jax.experimental import pallas as pl
from jax.experimental.pallas import tpu as pltpu

F32 = jnp.float32
BF16 = jnp.bfloat16
HIGHEST = lax.Precision.HIGHEST

RMS_EPS = 1e-6
L2_EPS = 1e-6
WINDOW = 128
PAST_LEN = 8192
HEAD_DIM = 128
SWA_GROUP = 4
DN_CONV = 4
DN_CHUNK = 64
HIST_ROWS = 8

VMEM_LIMIT_BYTES = 56 * 1024 * 1024
ROW_TILE_MAX = 576
SUBLANES = 8
LANES = 128
MXU_WIDTH = 256
NEG_BIG = -1e30
SWA_SUBBLOCKS = 4
GDN_CHUNKS_PER_STEP = 4
STEP_SEQS = 4

def _rms(x, g):
    return x * lax.rsqrt(jnp.mean(x * x, axis=-1, keepdims=True) + RMS_EPS) * g


def _silu(x):
    return x * jax.nn.sigmoid(x)


def _dot(a, b):
    return jnp.dot(a, b, preferred_element_type=F32)


def _dot_hi(a, b):
    return jnp.dot(a, b, preferred_element_type=F32, precision=HIGHEST)


def _dot_tn_hi(a, b):
    return lax.dot_general(a, b, (((0,), (0,)), ((), ())), preferred_element_type=F32, precision=HIGHEST)


def _dot_nt(a, b):
    return lax.dot_general(a, b, (((1,), (1,)), ((), ())), preferred_element_type=F32)


def _dot_tn(a, b):
    return lax.dot_general(a, b, (((0,), (0,)), ((), ())), preferred_element_type=F32)


def _split_bf16(a):
    hi = a.astype(BF16)
    return hi, (a - hi.astype(F32)).astype(BF16)


_DOTS = {"nn": _dot, "nt": _dot_nt, "tn": _dot_tn}


def _dot3_split(a_pair, b_pair, kind="nn"):
    (a_hi, a_lo), (b_hi, b_lo) = a_pair, b_pair
    dot = _DOTS[kind]
    axis = 1 if kind == "tn" else 0
    m = a_hi.shape[axis]
    top = dot(jnp.concatenate([a_hi, a_lo], axis=axis), b_hi)
    return top[:m] + top[m:] + dot(a_hi, b_lo)


def _mm(a, b, kind="nn", passes=3):
    if passes == 1:
        return _DOTS[kind](a.astype(BF16), b.astype(BF16))
    return _dot3_split(_split_bf16(a), _split_bf16(b), kind)


def _row_tile(rows):
    start = min(ROW_TILE_MAX, rows) // SUBLANES * SUBLANES
    for tile in range(start, 0, -SUBLANES):
        if rows % tile == 0:
            return tile
    raise ValueError(f"no sublane-aligned row tile divides {rows}")


def _resident(shape):
    return pl.BlockSpec(shape, lambda *_: (0,) * len(shape), pipeline_mode=pl.Buffered(1))


def _params(*semantics):
    return pltpu.CompilerParams(dimension_semantics=semantics, vmem_limit_bytes=VMEM_LIMIT_BYTES)


def _ffn_tiles(hs, g_pre, g_post, wg_ref, wu_ref, wd_ref, ff_chunk):
    xns = [_rms(h, g_pre).astype(BF16) for h in hs]
    accs = [None] * len(hs)
    c0 = 0
    for width in ff_chunk:
        gates = [_dot(xn, wg_ref[:, c0:c0 + width]) for xn in xns]
        ups = [_dot(xn, wu_ref[:, c0:c0 + width]) for xn in xns]
        for i, (gate, up) in enumerate(zip(gates, ups)):
            part = _dot((_silu(gate) * up).astype(BF16), wd_ref[c0:c0 + width, :])
            accs[i] = part if accs[i] is None else accs[i] + part
        c0 += width
    return [h + 0.5 * _rms(acc, g_post) for h, acc in zip(hs, accs)]


def _row_halves(tm):
    return [slice(0, tm // 2), slice(tm // 2, tm)] if tm % (2 * SUBLANES) == 0 else [slice(0, tm)]


def _ff_chunk(d_ff):
    tiles, rem = divmod(d_ff, MXU_WIDTH)
    if rem or tiles < 2:
        return (d_ff,)
    first = (tiles + 1) // 2
    return (first * MXU_WIDTH, (tiles - first) * MXU_WIDTH)


def _ffn_kernel(h_ref, gpre_ref, gpost_ref, wg_ref, wu_ref, wd_ref, o_ref, *, ff_chunk):
    halves = _row_halves(h_ref.shape[0])
    outs = _ffn_tiles([h_ref[r, :] for r in halves], gpre_ref[...], gpost_ref[...], wg_ref, wu_ref, wd_ref, ff_chunk)
    for r, out in zip(halves, outs):
        o_ref[r, :] = out


def _ffn(h, g_pre, g_post, wg, wu, wd):
    rows, d = h.shape
    d_ff = wg.shape[1]
    tm = _row_tile(rows)
    row_spec = pl.BlockSpec((tm, d), lambda i: (i, 0))
    return pl.pallas_call(
        functools.partial(_ffn_kernel, ff_chunk=_ff_chunk(d_ff)),
        grid=(rows // tm,),
        in_specs=[row_spec, _resident((1, d)), _resident((1, d)),
                  _resident((d, d_ff)), _resident((d, d_ff)), _resident((d_ff, d))],
        out_specs=row_spec,
        out_shape=jax.ShapeDtypeStruct((rows, d), F32),
        compiler_params=_params("parallel"),
        name="ffn",
    )(h, g_pre, g_post, wg, wu, wd)


def _proj_kernel(h_ref, g_ref, w_ref, o_ref):
    o_ref[...] = _dot(_rms(h_ref[...], g_ref[...]).astype(BF16), w_ref[...])


def _proj(h, g, w, n_cols=None):
    rows, d = h.shape
    n = w.shape[1] if n_cols is None else n_cols
    tm = _row_tile(rows)
    return pl.pallas_call(
        _proj_kernel,
        grid=(rows // tm,),
        in_specs=[pl.BlockSpec((tm, d), lambda i: (i, 0)), _resident((1, d)), _resident((d, n))],
        out_specs=pl.BlockSpec((tm, n), lambda i: (i, 0)),
        out_shape=jax.ShapeDtypeStruct((rows, n), F32),
        compiler_params=_params("parallel"),
        name="proj",
    )(h, g, w)


def _proj_conv_kernel(h_ref, g_ref, w_ref, wb_ref, hist_ref, convw_ref, o_ref, ob_ref, tail_ref, before_ref, *,
                      conv_width, qk_width):
    j = pl.program_id(1)
    tm = h_ref.shape[0]
    hd = HEAD_DIM
    xn = _rms(h_ref[...], g_ref[...]).astype(BF16)

    @pl.when(j == 0)
    def _():
        before_ref[...] = hist_ref[0]

    @pl.when(j > 0)
    def _():
        before_ref[...] = tail_ref[0]

    before = before_ref[...]
    raw = {}

    def project_raw(c0, width):
        raw[c0] = _dot(xn, w_ref[:, c0:c0 + width])

    def conv_column(g0, c0):
        y_group = raw[g0]
        if c0 == g0:
            tail_ref[0, :, g0:g0 + y_group.shape[1]] = y_group[tm - HIST_ROWS:, :]
        x = jnp.concatenate([before[:, c0:c0 + hd], y_group[:, c0 - g0:c0 - g0 + hd]], axis=0)
        acc = x[HIST_ROWS:] * convw_ref[DN_CONV - 1:DN_CONV, c0:c0 + hd]
        for back in range(1, DN_CONV):
            shifted = pltpu.roll(x, back, 0)[HIST_ROWS:]
            acc = acc + shifted * convw_ref[DN_CONV - 1 - back:DN_CONV - back, c0:c0 + hd]
        y = _silu(acc)
        if c0 < 2 * qk_width:
            y = y * lax.rsqrt(jnp.sum(y * y, axis=-1, keepdims=True) + L2_EPS)
            if c0 < qk_width:
                y = y * (hd ** -0.5)
        o_ref[:, c0:c0 + hd] = y

    def plain_a(c0, c1):
        o_ref[:, c0:c1] = _dot(xn, w_ref[:, c0:c1])

    def plain_b(c0, c1):
        ob_ref[:, c0:c1] = _dot(xn, wb_ref[:, c0:c1])

    group = 2 * MXU_WIDTH
    starts = list(range(0, conv_width, group))
    plain = ([functools.partial(plain_a, c0, min(c0 + MXU_WIDTH, w_ref.shape[1]))
              for c0 in range(conv_width, w_ref.shape[1], MXU_WIDTH)]
             + [functools.partial(plain_b, c0, min(c0 + MXU_WIDTH, wb_ref.shape[1]))
                for c0 in range(0, wb_ref.shape[1], MXU_WIDTH)])
    columns = [(g0, c0) for g0 in starts for c0 in range(g0, g0 + group, hd)]
    free_slots = len(columns) - (len(starts) - 1)
    project_raw(starts[0], group)
    used = slot = 0
    for g0, c0 in columns:
        conv_column(g0, c0)
        if c0 == g0 and g0 + group < conv_width:
            project_raw(g0 + group, group)
        else:
            slot += 1
            upto = len(plain) * slot // free_slots
            for piece in plain[used:upto]:
                piece()
            used = upto


def _proj_conv(h, g, w, n_cols, w_b, hist, conv_w, *, n_seq, seq_len, qk_width):
    rows, d = h.shape
    n, n_b = n_cols, w_b.shape[1]
    conv_width = conv_w.shape[1]
    tm = _row_tile(seq_len)
    tiles = seq_len // tm
    shared_hist = hist.shape[0] == 1
    return pl.pallas_call(
        functools.partial(_proj_conv_kernel, conv_width=conv_width, qk_width=qk_width),
        grid=(n_seq, tiles),
        in_specs=[pl.BlockSpec((tm, d), lambda s, j: (s * tiles + j, 0)), _resident((1, d)), _resident((d, n)),
                  _resident((d, n_b)),
                  pl.BlockSpec((1, HIST_ROWS, conv_width), lambda s, j: (0 if shared_hist else s, 0, 0)),
                  _resident((DN_CONV, conv_width))],
        out_specs=[pl.BlockSpec((tm, n), lambda s, j: (s * tiles + j, 0)),
                   pl.BlockSpec((tm, n_b), lambda s, j: (s * tiles + j, 0)),
                   pl.BlockSpec((1, HIST_ROWS, conv_width), lambda s, j: (s, 0, 0))],
        out_shape=[jax.ShapeDtypeStruct((rows, n), F32),
                   jax.ShapeDtypeStruct((rows, n_b), F32),
                   jax.ShapeDtypeStruct((n_seq, HIST_ROWS, conv_width), F32)],
        scratch_shapes=[pltpu.VMEM((HIST_ROWS, conv_width), F32)],
        compiler_params=_params("parallel", "arbitrary"),
        name="proj_conv",
    )(h, g, w, w_b, hist, conv_w)


def _interleave(primary, secondary):
    done = 0
    for i, stage in enumerate(primary):
        stage()
        upto = (i + 1) * len(secondary) // len(primary)
        for unit in secondary[done:upto]:
            unit()
        done = upto


def _mixffn_kernel(h_ref, odn_ref, osw_ref, gdn_ref, gsw_ref, wout_ref, gmix_ref,
                   gpre_ref, gpost_ref, wg_ref, wu_ref, wd_ref, o_ref, *, ff_chunk):
    halves = _row_halves(h_ref.shape[0])
    ys = [(jax.nn.sigmoid(gdn_ref[r, :]) * odn_ref[r, :] + jax.nn.sigmoid(gsw_ref[r, :]) * osw_ref[r, :]).astype(BF16)
          for r in halves]
    mixed = [_dot(y, wout_ref[...]) for y in ys]
    h2s = [h_ref[r, :] + _rms(m, gmix_ref[...]) for r, m in zip(halves, mixed)]
    outs = _ffn_tiles(h2s, gpre_ref[...], gpost_ref[...], wg_ref, wu_ref, wd_ref, ff_chunk)
    for r, out in zip(halves, outs):
        o_ref[r, :] = out


def _mixffn(h, o_dn, o_sw, proj_b, cols, w_out, g_mix, g_pre, g_post, wg, wu, wd, rows=None):
    d = h.shape[1]
    rows = h.shape[0] if rows is None else rows
    d_ff = wg.shape[1]
    tm = _row_tile(rows)
    row_spec = pl.BlockSpec((tm, d), lambda i: (i, 0))
    gdn_blk, gsw_blk = cols["g_dn"] // d, cols["g_swa"] // d
    return pl.pallas_call(
        functools.partial(_mixffn_kernel, ff_chunk=_ff_chunk(d_ff)),
        grid=(rows // tm,),
        in_specs=[row_spec, row_spec, row_spec,
                  pl.BlockSpec((tm, d), lambda i: (i, gdn_blk)),
                  pl.BlockSpec((tm, d), lambda i: (i, gsw_blk)),
                  _resident((d, d)), _resident((1, d)), _resident((1, d)), _resident((1, d)),
                  _resident((d, d_ff)), _resident((d, d_ff)), _resident((d_ff, d))],
        out_specs=row_spec,
        out_shape=jax.ShapeDtypeStruct((rows, d), F32),
        compiler_params=_params("parallel"),
        name="mixffn",
    )(h, o_dn, o_sw, proj_b, proj_b, w_out, g_mix, g_pre, g_post, wg, wu, wd)


GROUP = 4
PASSES_QK = 1
PASSES_STATE = 1
PASSES_INV = 3
INV_BLOCK = 16
STEP_PASSES = 1


def _gdn_block_stages(qkv_ref, z_ref, ba_ref, alog_ref, dtb_ref, nw_ref, s_ref, store_out, *, chunk, heads):
    tb = qkv_ref.shape[0]
    hd = HEAD_DIM
    c = chunk
    gw = GROUP * c
    qk_w = heads * hd
    stages = []

    def iota(shape, dim):
        return lax.broadcasted_iota(jnp.int32, shape, dim)

    row = iota((c, gw), 0)
    col = iota((c, gw), 1) & (c - 1)
    causal4, strict4 = row >= col, row > col
    diag_blocks4 = (row // INV_BLOCK) == (col // INV_BLOCK)
    eye4 = (row == col).astype(F32)
    bd_mask = (iota((gw, gw), 0) // c) == (iota((gw, gw), 1) // c)
    lower_ones = (iota((c, c), 0) >= iota((c, c), 1)).astype(F32)
    upper_dup = (iota((c, 2 * c), 0) <= (iota((c, 2 * c), 1) & (c - 1))).astype(F32)
    first_half = iota((c, 2 * c), 1) < c
    zeros_head = jnp.zeros((c, hd), F32)

    def block_diag(x4):
        return jnp.where(bd_mask, jnp.concatenate([x4] * GROUP, axis=0), jnp.zeros((), x4.dtype))

    splits = {}

    def split(x):
        if id(x) not in splits:
            splits[id(x)] = (x, _split_bf16(x))
        return splits[id(x)][1]

    def times_block_diag(lhs, x4, passes):
        parts = lhs if isinstance(lhs, list) else [lhs]
        if passes == 1:
            return _dot(jnp.concatenate(parts, axis=0).astype(BF16), block_diag(x4.astype(BF16)))
        hi, lo = split(x4)
        lhs_split = tuple(jnp.concatenate(pieces, axis=0) for pieces in zip(*[split(p) for p in parts]))
        return _dot3_split(lhs_split, (block_diag(hi), block_diag(lo)))

    def block_diag_times(x4, rhs, passes):
        if passes == 1:
            return _dot(block_diag(x4.astype(BF16)), rhs.astype(BF16))
        hi, lo = _split_bf16(x4)
        return _dot3_split((block_diag(hi), block_diag(lo)), _split_bf16(rhs))

    def pack_cols(col_all, hs):
        halves = [jnp.where(first_half,
                            jnp.broadcast_to(col_all[:, hs[2 * p]:hs[2 * p] + 1], (c, 2 * c)),
                            jnp.broadcast_to(col_all[:, hs[2 * p + 1]:hs[2 * p + 1] + 1], (c, 2 * c)))
                  for p in range(GROUP // 2)]
        return jnp.concatenate(halves, axis=1)

    chunks = [dict(r0=r0) for r0 in range(0, tb, c)]
    items = [dict(ch=ch, hs=list(range(g0, g0 + GROUP))) for ch in chunks for g0 in range(0, heads, GROUP)]

    def gating():
        for ch in chunks:
            ba = ba_ref[ch["r0"]:ch["r0"] + c, :]
            a_in = ba[:, heads:2 * heads] + dtb_ref[...]
            softplus = jnp.maximum(a_in, 0.0) + jnp.log(1.0 + jnp.exp(-jnp.abs(a_in)))
            g_all = -jnp.exp(alog_ref[...]) * softplus
            ch["beta"] = jax.nn.sigmoid(ba[:, 0:heads])
            ch["gc"] = _dot_hi(lower_ones, g_all)
            ch["gc_row"] = _dot_tn_hi(g_all, upper_dup)
        for it in items:
            rows = slice(it["ch"]["r0"], it["ch"]["r0"] + c)
            it["qs"] = [qkv_ref[rows, h * hd:(h + 1) * hd] for h in it["hs"]]
            it["ks"] = [qkv_ref[rows, qk_w + h * hd:qk_w + (h + 1) * hd] for h in it["hs"]]
            it["vs"] = [qkv_ref[rows, 2 * qk_w + h * hd:2 * qk_w + (h + 1) * hd] for h in it["hs"]]

    def gram(it):
        ch, hs, qs, ks = it["ch"], it["hs"], it["qs"], it["ks"]
        beta4 = pack_cols(ch["beta"], hs)
        gc_row4 = jnp.concatenate(
            [jnp.where(first_half[0:1], ch["gc_row"][hs[2 * p]:hs[2 * p] + 1],
                       ch["gc_row"][hs[2 * p + 1]:hs[2 * p + 1] + 1])
             for p in range(GROUP // 2)], axis=1)
        diff4 = pack_cols(ch["gc"], hs) - gc_row4
        decay4 = jnp.where(causal4, jnp.exp(jnp.where(causal4, diff4, 0.0)), 0.0)
        lhs = jnp.concatenate([jnp.concatenate(qs, axis=1), jnp.concatenate(ks, axis=1)], axis=0)
        k_bd = jnp.concatenate(
            [jnp.concatenate([ks[i] if ii == i else zeros_head for ii in range(GROUP)], axis=1)
             for i in range(GROUP)], axis=0)
        qkkk = _mm(lhs, k_bd, "nt", PASSES_QK)
        it["qk4"] = qkkk[:c] * decay4
        a4 = jnp.where(strict4, beta4 * qkkk[c:] * decay4, 0.0)
        a_diag = jnp.where(diag_blocks4, a4, 0.0)
        it["a_diag"], it["a_off"] = a_diag, a4 - a_diag
        it["t4"] = eye4 - a_diag

    assert c // INV_BLOCK == 4
    n_sq = int(math.log2(INV_BLOCK)) - 1

    def inverse_start(it):
        it["p4"] = times_block_diag(it["a_diag"], it["a_diag"], PASSES_INV)

    def inverse_step(step, it):
        if step < n_sq - 1:
            both = times_block_diag([it["p4"], it["t4"]], it["p4"], PASSES_INV)
            it["p4"], it["t4"] = both[:c], it["t4"] + both[c:]
        else:
            it["t4"] = it["t4"] + times_block_diag(it["t4"], it["p4"], PASSES_INV)

    def off_diagonal(it):
        it["m4"] = times_block_diag(it["t4"], it["a_off"], PASSES_INV)

    def off_diagonal_square(it):
        it["m2"] = times_block_diag(it["m4"], it["m4"], PASSES_INV)

    def series(it):
        i_minus_m = eye4 - it["m4"]
        it["s4"] = i_minus_m + times_block_diag(i_minus_m, it["m2"], PASSES_INV)

    def inverse_finish(it):
        it["t4"] = times_block_diag(it["s4"], it["t4"], PASSES_INV)

    def solve(it):
        ch, hs, ks, vs = it["ch"], it["hs"], it["ks"], it["vs"]
        rhs = jnp.concatenate(
            [jnp.concatenate([vs[i] * ch["beta"][:, h:h + 1],
                              ks[i] * (ch["beta"][:, h:h + 1] * jnp.exp(ch["gc"][:, h:h + 1]))], axis=1)
             for i, h in enumerate(hs)], axis=0)
        it["sol"] = block_diag_times(it["t4"], rhs, PASSES_STATE)

    def for_items(fn, subset=None):
        return lambda: [fn(it) for it in (items if subset is None else subset)]

    stages.append(gating)
    stages.append(for_items(gram))
    stages.append(for_items(inverse_start))
    for step in range(n_sq):
        stages.append(for_items(functools.partial(inverse_step, step)))
    stages += [for_items(off_diagonal), for_items(off_diagonal_square), for_items(series),
               for_items(inverse_finish), for_items(solve)]

    def read_state(it):
        ch = it["ch"]
        us, o_inter = [], []
        for i, h in enumerate(it["hs"]):
            sl = slice(i * c, (i + 1) * c)
            ws = _mm(jnp.concatenate([it["sol"][sl, hd:], it["qs"][i] * jnp.exp(ch["gc"][:, h:h + 1])], axis=0),
                     s_ref[h], "nn", PASSES_STATE)
            us.append(it["sol"][sl, :hd] - ws[:c])
            o_inter.append(ws[c:])
        it["us"], it["o_inter"] = us, o_inter

    def intra(it):
        it["o_intra"] = block_diag_times(it["qk4"], jnp.concatenate(it["us"], axis=0), PASSES_STATE)

    def update_state(it):
        ch = it["ch"]
        for i, h in enumerate(it["hs"]):
            gc = ch["gc"][:, h:h + 1]
            gc_last = ch["gc_row"][h:h + 1, c - 1:c]
            s_ref[h] = s_ref[h] * jnp.exp(gc_last) + _mm(it["ks"][i] * jnp.exp(gc_last - gc), it["us"][i],
                                                         "tn", PASSES_STATE)

    def emit(it):
        r0 = it["ch"]["r0"]
        for i, h in enumerate(it["hs"]):
            o = it["o_inter"][i] + it["o_intra"][i * c:(i + 1) * c]
            zh = z_ref[r0:r0 + c, h * hd:(h + 1) * hd]
            store_out(slice(r0, r0 + c), slice(h * hd, (h + 1) * hd), _rms(o, nw_ref[...]) * _silu(zh))

    for ch in chunks:
        group_items = [it for it in items if it["ch"] is ch]
        stages += [for_items(fn, group_items) for fn in (read_state, intra, update_state, emit)]
    return stages


def _ff_pieces(d_ff, n):
    tiles, rem = divmod(d_ff, MXU_WIDTH)
    if rem:
        return (d_ff,)
    per = -(-tiles // n)
    return tuple(min(per, tiles - t0) * MXU_WIDTH for t0 in range(0, tiles, per))


def _gdn_mix_kernel(qkv_ref, z_ref, ba_ref, s0_ref, alog_ref, dtb_ref, nw_ref,
                    h_ref, osw_ref, gdn_ref, gsw_ref, wout_ref, gmix_ref, gpre_ref, gpost_ref, wg_ref, wu_ref, wd_ref,
                    y_ref, sout_ref, s_ref, odn_ref, *, chunk, heads, nblk, ff_pieces):
    s = pl.program_id(0)
    n_blocks = pl.num_programs(0) - 1
    j = s % nblk
    cur, prev = s % 2, (s + 1) % 2

    @pl.when(s == 0)
    def _():
        odn_ref[...] = jnp.zeros(odn_ref.shape, F32)

    @pl.when((j == 0) & (s < n_blocks))
    def _():
        s_ref[...] = s0_ref[0]

    def store_out(rows, lanes, value):
        odn_ref[cur, rows, lanes] = value

    gdn_stages = _gdn_block_stages(qkv_ref, z_ref, ba_ref, alog_ref, dtb_ref, nw_ref, s_ref, store_out,
                                   chunk=chunk, heads=heads)

    st = {}

    def merge():
        y = jax.nn.sigmoid(gdn_ref[...]) * odn_ref[prev] + jax.nn.sigmoid(gsw_ref[...]) * osw_ref[...]
        st["y"] = y.astype(BF16)

    def out_proj():
        st["mixed"] = _dot(st["y"], wout_ref[...])

    def norms():
        st["h2"] = h_ref[...] + _rms(st["mixed"], gmix_ref[...])
        st["xn"] = _rms(st["h2"], gpre_ref[...]).astype(BF16)
        st["acc"] = None

    def gate(c0, width):
        st["gate"] = _dot(st["xn"], wg_ref[:, c0:c0 + width])

    def up(c0, width):
        st["up"] = _dot(st["xn"], wu_ref[:, c0:c0 + width])

    def down(c0, width):
        part = _dot((_silu(st["gate"]) * st["up"]).astype(BF16), wd_ref[c0:c0 + width, :])
        st["acc"] = part if st["acc"] is None else st["acc"] + part

    def finish():
        y_ref[...] = st["h2"] + 0.5 * _rms(st["acc"], gpost_ref[...])

    mix_units = [merge, out_proj, norms]
    c0 = 0
    for width in ff_pieces:
        mix_units += [functools.partial(fn, c0, width) for fn in (gate, up, down)]
        c0 += width
    mix_units.append(finish)

    _interleave(gdn_stages, mix_units)

    @pl.when((j == nblk - 1) & (s < n_blocks))
    def _():
        sout_ref[0] = s_ref[...]


def _gdn_mix(proj_a, proj_b, h, o_sw, cols, s0, a_log, dt_bias, norm_w, w_out, g_mix, g_pre, g_post, wg, wu, wd,
             *, n_seq, seq_len, block):
    heads = a_log.shape[1]
    v_w = heads * HEAD_DIM
    conv_width = 3 * v_w
    d = h.shape[1]
    d_ff = wg.shape[1]
    nblk = seq_len // block
    n_blocks = n_seq * nblk
    z_blk, ba_blk = cols["z"] // v_w, cols["ba"] // LANES
    gdn_blk, gsw_blk = cols["g_dn"] // d, cols["g_swa"] // d
    shared_s0 = s0.shape[0] == 1
    assert 2 * DN_CHUNK == HEAD_DIM and heads % GROUP == 0 and block % DN_CHUNK == 0 and v_w == d

    def cur(s):
        return jnp.minimum(s, n_blocks - 1)

    def prev(s):
        return jnp.maximum(s - 1, 0)

    return pl.pallas_call(
        functools.partial(_gdn_mix_kernel, chunk=DN_CHUNK, heads=heads, nblk=nblk, ff_pieces=_ff_pieces(d_ff, 4)),
        grid=(n_blocks + 1,),
        in_specs=[
            pl.BlockSpec((block, conv_width), lambda s: (cur(s), 0)),
            pl.BlockSpec((block, v_w), lambda s: (cur(s), z_blk)),
            pl.BlockSpec((block, LANES), lambda s: (cur(s), ba_blk)),
            pl.BlockSpec((1, heads, HEAD_DIM, HEAD_DIM), lambda s: (0 if shared_s0 else cur(s) // nblk, 0, 0, 0)),
            _resident((1, heads)), _resident((1, heads)), _resident((1, HEAD_DIM)),
            pl.BlockSpec((block, d), lambda s: (prev(s), 0)),
            pl.BlockSpec((block, d), lambda s: (prev(s), 0)),
            pl.BlockSpec((block, d), lambda s: (prev(s), gdn_blk)),
            pl.BlockSpec((block, d), lambda s: (prev(s), gsw_blk)),
            _resident((d, d)), _resident((1, d)), _resident((1, d)), _resident((1, d)),
            _resident((d, d_ff)), _resident((d, d_ff)), _resident((d_ff, d)),
        ],
        out_specs=[
            pl.BlockSpec((block, d), lambda s: (prev(s), 0)),
            pl.BlockSpec((1, heads, HEAD_DIM, HEAD_DIM), lambda s: (cur(s) // nblk, 0, 0, 0)),
        ],
        out_shape=[
            jax.ShapeDtypeStruct((n_seq * seq_len, d), F32),
            jax.ShapeDtypeStruct((n_seq, heads, HEAD_DIM, HEAD_DIM), F32),
        ],
        scratch_shapes=[pltpu.VMEM((heads, HEAD_DIM, HEAD_DIM), F32), pltpu.VMEM((2, block, v_w), F32)],
        compiler_params=_params("arbitrary"),
        name="gdn_mix",
    )(proj_a, proj_a, proj_a, s0, a_log, dt_bias, norm_w,
      h, o_sw, proj_b, proj_b, w_out, g_mix, g_pre, g_post, wg, wu, wd)


def _gdn_step_kernel(qkv_ref, z_ref, ba_ref, hist_ref, s0_ref, convw_ref, alog_ref, dtb_ref, nw_ref,
                     o_ref, sout_ref, *, heads):
    nb, t, _ = qkv_ref.shape
    hd = HEAD_DIM
    qk_w = heads * hd
    ti = lax.broadcasted_iota(jnp.int32, (t, t), 0)
    tj = lax.broadcasted_iota(jnp.int32, (t, t), 1)
    lower_ones = (ti >= tj).astype(F32)
    upper_ones = (ti <= tj).astype(F32)

    def per_sequence(b, carry):
        raw = jnp.concatenate([hist_ref[b], qkv_ref[b]], axis=0)
        acc = raw[HIST_ROWS:] * convw_ref[DN_CONV - 1:DN_CONV, :]
        for back in range(1, DN_CONV):
            acc = acc + pltpu.roll(raw, back, 0)[HIST_ROWS:] * convw_ref[DN_CONV - 1 - back:DN_CONV - back, :]
        x = _silu(acc)
        ba = ba_ref[b]
        beta_all = jax.nn.sigmoid(ba[:, 0:heads])
        a_in = ba[:, heads:2 * heads] + dtb_ref[...]
        softplus = jnp.maximum(a_in, 0.0) + jnp.log(1.0 + jnp.exp(-jnp.abs(a_in)))
        g_all = -jnp.exp(alog_ref[...]) * softplus
        gc_all = _dot_hi(lower_ones, g_all)
        gc_row_all = _dot_tn_hi(g_all, upper_ones)
        z = z_ref[b]
        hs = range(heads)
        qs, ks, vs = [], [], []
        for h in hs:
            q = x[:, h * hd:(h + 1) * hd]
            k = x[:, qk_w + h * hd:qk_w + (h + 1) * hd]
            qs.append(q * lax.rsqrt(jnp.sum(q * q, axis=-1, keepdims=True) + L2_EPS) * (hd ** -0.5))
            ks.append(k * lax.rsqrt(jnp.sum(k * k, axis=-1, keepdims=True) + L2_EPS))
            vs.append(x[:, 2 * qk_w + h * hd:2 * qk_w + (h + 1) * hd])
        gcs = [gc_all[:, h:h + 1] for h in hs]
        gc_lasts = [gc_row_all[h:h + 1, t - 1:t] for h in hs]
        kq = [jnp.concatenate([ks[h], qs[h]], axis=0) for h in hs]
        gram = [_mm(kq[h], ks[h], "nt", STEP_PASSES) for h in hs]
        from_state = [_mm(kq[h] * jnp.exp(jnp.concatenate([gcs[h], gcs[h]], axis=0)), s0_ref[b, h], "nn",
                          STEP_PASSES) for h in hs]
        us, outs = [], []
        for h in hs:
            decay = jnp.where(ti >= tj, jnp.exp(jnp.where(ti >= tj, gcs[h] - gc_row_all[h:h + 1, :], 0.0)), 0.0)
            beta = beta_all[:, h:h + 1]
            lower = jnp.where(ti > tj, beta * gram[h][:t] * decay, 0.0)
            u = beta * (vs[h] - from_state[h][:t])
            for j in range(t - 1):
                u = u - lower[:, j:j + 1] * u[j:j + 1, :]
            qk = gram[h][t:] * decay
            o = from_state[h][t:]
            for j in range(t):
                o = o + qk[:, j:j + 1] * u[j:j + 1, :]
            us.append(u)
            outs.append(o)
        for h in hs:
            sout_ref[b, h] = s0_ref[b, h] * jnp.exp(gc_lasts[h]) + _mm(ks[h] * jnp.exp(gc_lasts[h] - gcs[h]), us[h],
                                                                       "tn", STEP_PASSES)
        for h in hs:
            o_ref[b, :, h * hd:(h + 1) * hd] = _rms(outs[h], nw_ref[...]) * _silu(z[:, h * hd:(h + 1) * hd])
        return carry

    lax.fori_loop(0, nb, per_sequence, 0, unroll=2 if nb % 2 == 0 else 1)


def _gdn_step(proj_a3, cols, hist, s0, conv_w, a_log, dt_bias, norm_w, *, seqs_per_step):
    n, t = s0.shape[0], proj_a3.shape[1]
    heads = a_log.shape[1]
    conv_width = conv_w.shape[1]
    v_w = heads * HEAD_DIM
    nb = seqs_per_step
    z_blk, ba_blk = cols["z"] // v_w, cols["ba"] // LANES
    state_spec = pl.BlockSpec((nb, heads, HEAD_DIM, HEAD_DIM), lambda i: (i, 0, 0, 0))
    return pl.pallas_call(
        functools.partial(_gdn_step_kernel, heads=heads),
        grid=(n // nb,),
        in_specs=[
            pl.BlockSpec((nb, t, conv_width), lambda i: (i, 0, 0)),
            pl.BlockSpec((nb, t, v_w), lambda i: (i, 0, z_blk)),
            pl.BlockSpec((nb, t, LANES), lambda i: (i, 0, ba_blk)),
            pl.BlockSpec((nb, HIST_ROWS, conv_width), lambda i: (i, 0, 0)),
            state_spec,
            pl.BlockSpec((DN_CONV, conv_width), lambda i: (0, 0)),
            pl.BlockSpec((1, heads), lambda i: (0, 0)),
            pl.BlockSpec((1, heads), lambda i: (0, 0)),
            pl.BlockSpec((1, HEAD_DIM), lambda i: (0, 0)),
        ],
        out_specs=[pl.BlockSpec((nb, t, v_w), lambda i: (i, 0, 0)), state_spec],
        out_shape=[jax.ShapeDtypeStruct((n, t, v_w), F32),
                   jax.ShapeDtypeStruct((n, heads, HEAD_DIM, HEAD_DIM), F32)],
        compiler_params=_params("parallel"),
        name="gdn_step",
    )(proj_a3, proj_a3, proj_a3, hist, s0, conv_w, a_log, dt_bias, norm_w)


def _alibi_slope(head, n_heads):
    return 2.0 ** (-8.0 * (head + 1) / n_heads)


def _penalty(dist, mask):
    return jnp.where(mask, jnp.minimum(dist, WINDOW).astype(F32), -NEG_BIG)


def _attend(jobs, n_rows):
    scale = HEAD_DIM ** -0.5
    hd = HEAD_DIM
    scores = [[_dot_nt(q4, k) * scale for k, _, _ in segs] for q4, segs, _, _ in jobs]
    v_ones = [[jnp.concatenate([v, jnp.ones_like(v)], axis=1) for _, v, _ in segs] for _, segs, _, _ in jobs]
    heads = [(ji, g) for ji in range(len(jobs)) for g in range(len(jobs[ji][3]))]
    logits, maxes = {}, {}
    for ji, g in heads:
        _, segs, _, slopes = jobs[ji]
        r = slice(g * n_rows, (g + 1) * n_rows)
        logits[ji, g] = [sc[r] - slopes[g] * pen for sc, (_, _, pen) in zip(scores[ji], segs)]
    for ji, g in heads:
        by_width = {}
        for lg in logits[ji, g]:
            w = lg.shape[1]
            by_width[w] = lg if w not in by_width else jnp.maximum(by_width[w], lg)
        m = jobs[ji][2][g]
        for lg in by_width.values():
            m = jnp.maximum(m, jnp.max(lg, axis=-1, keepdims=True))
        maxes[ji, g] = m
    accs = {}
    for ji, g in heads:
        acc = None
        for lg, v1 in zip(logits[ji, g], v_ones[ji]):
            pv = _dot(jnp.exp(lg - maxes[ji, g]).astype(BF16), v1)
            acc = pv if acc is None else acc + pv
        accs[ji, g] = acc
    outs = [[None] * len(job[3]) for job in jobs]
    for ji, g in heads:
        acc = accs[ji, g]
        outs[ji][g] = acc[:, :hd] / (acc[:, hd:] + jnp.exp(jobs[ji][2][g] - maxes[ji, g]))
    return outs


def _swa_prompt_kernel(q_ref, kv_ref, kvprev_ref, kvmeta_ref, sinks_ref, o_ref, *, n_meta, kv_heads):
    j = pl.program_id(1)
    tq = WINDOW
    n_sub = q_ref.shape[0] // tq
    hd = HEAD_DIM
    kv_w = kv_heads * hd
    n_heads = kv_heads * SWA_GROUP
    qi = lax.broadcasted_iota(jnp.int32, (tq, tq), 0)
    ki = lax.broadcasted_iota(jnp.int32, (tq, tq), 1)
    dist_own = qi - ki
    dist_prev = dist_own + tq
    pen_own = _penalty(dist_own, dist_own >= 0)
    pen_prev = _penalty(dist_prev, dist_prev <= WINDOW)
    pen_prev_first = _penalty(dist_prev, (dist_prev <= WINDOW) & (j > 0))
    jobs, job_dst = [], []
    for sub in range(n_sub):
        rows = slice(sub * tq, (sub + 1) * tq)
        prev_ref, prev_rows = (kvprev_ref, slice(0, tq)) if sub == 0 else (kv_ref, slice((sub - 1) * tq, sub * tq))
        qpos = n_meta + (j * n_sub + sub) * tq + lax.broadcasted_iota(jnp.int32, (tq, n_meta), 0)
        dist_meta = qpos - lax.broadcasted_iota(jnp.int32, (tq, n_meta), 1)
        pen_meta = _penalty(dist_meta, dist_meta >= 0)
        for kvh in range(kv_heads):
            ks = slice(kvh * hd, (kvh + 1) * hd)
            vs = slice(kv_w + kvh * hd, kv_w + (kvh + 1) * hd)
            segments = [
                (kvmeta_ref[:, ks].astype(BF16), kvmeta_ref[:, vs].astype(BF16), pen_meta),
                (prev_ref[prev_rows, ks].astype(BF16), prev_ref[prev_rows, vs].astype(BF16),
                 pen_prev_first if sub == 0 else pen_prev),
                (kv_ref[rows, ks].astype(BF16), kv_ref[rows, vs].astype(BF16), pen_own),
            ]
            heads = [kvh * SWA_GROUP + g for g in range(SWA_GROUP)]
            q4 = jnp.concatenate([q_ref[rows, hh * hd:(hh + 1) * hd] for hh in heads], axis=0).astype(BF16)
            jobs.append((q4, segments, [sinks_ref[:, hh:hh + 1] for hh in heads],
                         [_alibi_slope(hh, n_heads) for hh in heads]))
            job_dst.append((rows, heads))
    for (rows, heads), outs in zip(job_dst, _attend(jobs, tq)):
        for hh, o in zip(heads, outs):
            o_ref[rows, hh * hd:(hh + 1) * hd] = o


def _swa_prompt(proj_b, proj_b_meta, cols, sinks, *, n_seq, seq_len, n_meta):
    n_heads = sinks.shape[1]
    kv_heads = n_heads // SWA_GROUP
    q_w, kv_w2 = n_heads * HEAD_DIM, 2 * kv_heads * HEAD_DIM
    n_sub = SWA_SUBBLOCKS if seq_len % (SWA_SUBBLOCKS * WINDOW) == 0 else 1
    tq = n_sub * WINDOW
    nblk = seq_len // tq
    kv_blk = cols["skv"] // kv_w2
    return pl.pallas_call(
        functools.partial(_swa_prompt_kernel, n_meta=n_meta, kv_heads=kv_heads),
        grid=(n_seq, nblk),
        in_specs=[
            pl.BlockSpec((tq, q_w), lambda n, j: (n * nblk + j, 0)),
            pl.BlockSpec((tq, kv_w2), lambda n, j: (n * nblk + j, kv_blk)),
            pl.BlockSpec((WINDOW, kv_w2), lambda n, j: (jnp.maximum((n * nblk + j) * n_sub - 1, 0), kv_blk)),
            pl.BlockSpec((n_meta, kv_w2), lambda n, j: (0, kv_blk)),
            pl.BlockSpec((1, n_heads), lambda n, j: (0, 0)),
        ],
        out_specs=pl.BlockSpec((tq, q_w), lambda n, j: (n * nblk + j, 0)),
        out_shape=jax.ShapeDtypeStruct((n_seq * seq_len, q_w), F32),
        compiler_params=_params("parallel", "arbitrary"),
        name="swa_prompt",
    )(proj_b, proj_b, proj_b, proj_b_meta, sinks)


def _swa_step_kernel(q_ref, kvnew_ref, kmeta_ref, vmeta_ref, kbuf_ref, vbuf_ref, sinks_ref,
                     o_ref, kout_ref, vout_ref, *, kv_heads):
    nb, t, _ = q_ref.shape
    n_meta, w = kmeta_ref.shape[1], kbuf_ref.shape[1]
    hd = HEAD_DIM
    kv_w = kv_heads * hd
    n_heads = kv_heads * SWA_GROUP

    def grid2(rows, cols_):
        return (lax.broadcasted_iota(jnp.int32, (rows, cols_), 0), lax.broadcasted_iota(jnp.int32, (rows, cols_), 1))

    ti, mi = grid2(t, n_meta)
    dist_meta = PAST_LEN + ti - mi
    pen_meta = _penalty(dist_meta, dist_meta >= 0)
    ti, bi = grid2(t, w)
    dist_buf = w + ti - bi
    pen_buf = _penalty(dist_buf, (dist_buf <= WINDOW) & (PAST_LEN - w + bi >= n_meta))
    ti, si = grid2(t, t)
    dist_new = ti - si
    pen_new = _penalty(dist_new, dist_new >= 0)
    jobs, job_dst = [], []
    for b in range(nb):
        kv_new = kvnew_ref[b]
        for kvh in range(kv_heads):
            ks = slice(kvh * hd, (kvh + 1) * hd)
            k_old, v_old = kbuf_ref[b, :, kvh, :], vbuf_ref[b, :, kvh, :]
            k_new, v_new = kv_new[:, ks], kv_new[:, kv_w + kvh * hd:kv_w + (kvh + 1) * hd]
            kout_ref[b, 0:w - t, kvh, :] = k_old[t:w]
            kout_ref[b, w - t:w, kvh, :] = k_new
            vout_ref[b, 0:w - t, kvh, :] = v_old[t:w]
            vout_ref[b, w - t:w, kvh, :] = v_new
            segments = [
                (kmeta_ref[b, :, ks].astype(BF16), vmeta_ref[b, :, ks].astype(BF16), pen_meta),
                (k_old.astype(BF16), v_old.astype(BF16), pen_buf),
                (k_new.astype(BF16), v_new.astype(BF16), pen_new),
            ]
            heads = [kvh * SWA_GROUP + g for g in range(SWA_GROUP)]
            q4 = jnp.concatenate([q_ref[b, :, hh * hd:(hh + 1) * hd] for hh in heads], axis=0).astype(BF16)
            jobs.append((q4, segments, [sinks_ref[:, hh:hh + 1] for hh in heads],
                         [_alibi_slope(hh, n_heads) for hh in heads]))
            job_dst.append((b, heads))
    for (b, heads), outs in zip(job_dst, _attend(jobs, t)):
        for hh, o in zip(heads, outs):
            o_ref[b, :, hh * hd:(hh + 1) * hd] = o


def _swa_step(proj_b3, cols, k_meta, v_meta, k_buf, v_buf, sinks, *, seqs_per_step):
    n, t = k_buf.shape[0], proj_b3.shape[1]
    n_heads = sinks.shape[1]
    kv_heads = n_heads // SWA_GROUP
    q_w, kv_w = n_heads * HEAD_DIM, kv_heads * HEAD_DIM
    n_meta, w = k_meta.shape[1], k_buf.shape[1]
    nb = seqs_per_step
    kv_blk = cols["skv"] // (2 * kv_w)
    seq3 = lambda rows, width, blk=0: pl.BlockSpec((nb, rows, width), lambda i: (i, 0, blk))
    cache_spec = pl.BlockSpec((nb, w, kv_heads, HEAD_DIM), lambda i: (i, 0, 0, 0))
    return pl.pallas_call(
        functools.partial(_swa_step_kernel, kv_heads=kv_heads),
        grid=(n // nb,),
        in_specs=[seq3(t, q_w), seq3(t, 2 * kv_w, kv_blk), seq3(n_meta, kv_w), seq3(n_meta, kv_w),
                  cache_spec, cache_spec, pl.BlockSpec((1, n_heads), lambda i: (0, 0))],
        out_specs=[seq3(t, q_w), cache_spec, cache_spec],
        out_shape=[jax.ShapeDtypeStruct((n, t, q_w), F32),
                   jax.ShapeDtypeStruct((n, w, kv_heads, HEAD_DIM), F32),
                   jax.ShapeDtypeStruct((n, w, kv_heads, HEAD_DIM), F32)],
        compiler_params=_params("parallel"),
        name="swa_step",
    )(proj_b3, proj_b3, k_meta, v_meta, k_buf, v_buf, sinks)


def _split_w_in(w_in, heads, n_heads, kv_heads, d):
    conv_w = 3 * heads * HEAD_DIM
    v_w = heads * HEAD_DIM
    q_w, kv_w = n_heads * HEAD_DIM, kv_heads * HEAD_DIM
    sizes = (conv_w, v_w, heads, heads, q_w, kv_w, kv_w, d, d)
    offs = [0]
    for s in sizes:
        offs.append(offs[-1] + s)
    assert offs[2] % LANES == 0 and 2 * heads <= LANES and offs[2] + LANES <= w_in.shape[1]
    w_bf = w_in.astype(BF16)
    part = lambda i: w_bf[:, offs[i]:offs[i + 1]]
    w_b = jnp.concatenate([part(4), part(8), part(7), part(5), part(6)], axis=1)
    cols_a = {"qkv": 0, "z": conv_w, "ba": offs[2]}
    cols_b = {"q": 0, "g_swa": q_w, "g_dn": q_w + d, "skv": q_w + 2 * d}
    return w_bf, offs[2] + LANES, w_b, cols_a, cols_b


def kernel(x_prompt, x_sample, state_dn_conv, state_dn_ssm, cache_swa_meta_k, cache_swa_meta_v, cache_swa_k, cache_swa_v, meta_tokens, ffn1_norm_pre, ffn1_norm_post, ffn1_w_gate, ffn1_w_up, ffn1_w_down, mix_norm_pre, mix_norm_post, w_in, dn_conv_w, dn_a_log, dn_dt_bias, dn_norm_w, swa_sinks, w_out, ffn2_norm_pre, ffn2_norm_post, ffn2_w_gate, ffn2_w_up, ffn2_w_down):
    assert w_in.shape[0] == 1, "single-layer step"
    n_p, seq, d = x_prompt.shape
    n_s, t_s, _ = x_sample.shape
    n_meta = meta_tokens.shape[0]
    heads = dn_a_log.shape[1]
    n_heads = swa_sinks.shape[1]
    kv_heads = n_heads // SWA_GROUP
    conv_w = dn_conv_w.shape[2]
    kv_w = kv_heads * HEAD_DIM
    w_keep = cache_swa_k.shape[2]
    assert w_keep == WINDOW and seq % WINDOW == 0 and seq % DN_CHUNK == 0 and seq >= WINDOW

    w_bf, a_cols, w_b, cols_a, cols_b = _split_w_in(w_in[0], heads, n_heads, kv_heads, d)
    cols = {**cols_a, **cols_b}
    f1 = (ffn1_norm_pre, ffn1_norm_post, ffn1_w_gate[0].astype(BF16), ffn1_w_up[0].astype(BF16),
          ffn1_w_down[0].astype(BF16))
    f2 = (ffn2_norm_pre, ffn2_norm_post, ffn2_w_gate[0].astype(BF16), ffn2_w_up[0].astype(BF16),
          ffn2_w_down[0].astype(BF16))
    w_o = w_out[0].astype(BF16)
    gdn_w = (dn_conv_w[0], dn_a_log, dn_dt_bias, dn_norm_w)

    def front(h):
        h1 = _ffn(h, *f1)
        return h1, _proj(h1, mix_norm_pre, w_bf, a_cols), _proj(h1, mix_norm_pre, w_b)

    def hist_tile(rows3):
        return jnp.pad(rows3, ((0, 0), (HIST_ROWS - (DN_CONV - 1), 0), (0, 0)))

    rows_s = n_s * t_s
    assert n_meta % t_s == 0 and n_meta >= DN_CONV - 1
    h1_sm, pa_sm, pb_sm = front(jnp.concatenate([x_sample.reshape(rows_s, d), meta_tokens], axis=0))
    pa_m, pb_m = pa_sm[rows_s:], pb_sm[rows_s:]
    zero_hist = jnp.zeros((1, HIST_ROWS, conv_w), F32)
    zero_state = jnp.zeros((1, heads, HEAD_DIM, HEAD_DIM), F32)
    _, s_meta = _gdn_step(pa_m[None], cols, zero_hist, zero_state, *gdn_w, seqs_per_step=1)
    hist_meta = hist_tile(pa_m[None, n_meta - (DN_CONV - 1):, :conv_w])

    h1_p = _ffn(x_prompt.reshape(n_p * seq, d), *f1)
    pa_p, pb_p, tail_p = _proj_conv(h1_p, mix_norm_pre, w_bf, a_cols, w_b, hist_meta, dn_conv_w[0], n_seq=n_p,
                                    seq_len=seq, qk_width=heads * HEAD_DIM)
    o_sw_p = _swa_prompt(pb_p, pb_m, cols, swa_sinks, n_seq=n_p, seq_len=seq, n_meta=n_meta)
    y_prompt, s_p = _gdn_mix(pa_p, pb_p, h1_p, o_sw_p, cols, s_meta, dn_a_log, dn_dt_bias, dn_norm_w, w_o,
                             mix_norm_post, *f2, n_seq=n_p, seq_len=seq, block=GDN_CHUNKS_PER_STEP * DN_CHUNK)
    y_prompt = y_prompt.reshape(n_p, seq, d)

    pb_p3 = pb_p.reshape(n_p, seq, -1)
    p_conv = tail_p[:, HIST_ROWS - (DN_CONV - 1):]
    kv_meta = pb_m[:, cols["skv"]:]
    p_meta_k = jnp.broadcast_to(kv_meta[None, :, :kv_w], (n_p, n_meta, kv_w))
    p_meta_v = jnp.broadcast_to(kv_meta[None, :, kv_w:], (n_p, n_meta, kv_w))
    p_win_k = pb_p3[:, seq - w_keep:, cols["skv"]:cols["skv"] + kv_w]
    p_win_v = pb_p3[:, seq - w_keep:, cols["skv"] + kv_w:]

    pa_s3 = pa_sm.reshape(-1, t_s, pa_sm.shape[1])
    pb_s3 = pb_sm.reshape(-1, t_s, pb_sm.shape[1])
    o_dn_s, s_s = _gdn_step(pa_s3, cols, hist_tile(state_dn_conv[0]), state_dn_ssm[0], *gdn_w,
                            seqs_per_step=STEP_SEQS)
    o_sw_s, s_win_k, s_win_v = _swa_step(
        pb_s3, cols,
        cache_swa_meta_k[0].reshape(n_s, n_meta, kv_w), cache_swa_meta_v[0].reshape(n_s, n_meta, kv_w),
        cache_swa_k[0], cache_swa_v[0], swa_sinks, seqs_per_step=STEP_SEQS)
    y_sample = _mixffn(h1_sm, o_dn_s.reshape(rows_s, -1), o_sw_s.reshape(rows_s, -1), pb_sm, cols, w_o,
                       mix_norm_post, *f2, rows=rows_s).reshape(n_s, t_s, d)
    assert t_s >= DN_CONV - 1, "new conv state is taken from the new rows alone"
    s_conv = pa_s3[:n_s, t_s - (DN_CONV - 1):, :conv_w]

    def kv4(x):
        return x.reshape(1, x.shape[0], x.shape[1], kv_heads, HEAD_DIM)

    return (y_prompt, y_sample, p_conv[None], s_p[None], kv4(p_meta_k), kv4(p_meta_v), kv4(p_win_k),
            kv4(p_win_v), s_conv[None], s_s[None], s_win_k[None], s_win_v[None])
```
